```python
import jax, jax.numpy as jnp
from jax import lax
import numpy as np

D_MODEL = 1024
BATCH = 1
SEQ = 16384
DEPTH = 2
DEC_BATCH = 8
DEC_SEQ = 32
PAST_LEN = 2048

CHUNK = 64
D_MIX = D_MODEL
HEAD_DIM = 64
W_A = D_MIX // 4
W_B = 3 * D_MIX // 8
W_C = D_MIX - W_A - W_B
H_A = W_A // HEAD_DIM
H_B = W_B // HEAD_DIM
H_C = W_C // HEAD_DIM
GMLP_CHUNK = 128
SB_QBLOCK = 128
ROPE_BASE = 10000.0
N_EXPERTS = 16
N_GROUPS = 4
EXPERTS_PER_GROUP = N_EXPERTS // N_GROUPS
TOP_K = 2
D_EXPERT = D_MODEL // 2
ALPHA = (2 * DEPTH) ** 0.25
BETA = (8 * DEPTH) ** -0.25
LN_EPS = 1e-5
D_IN = 2 * W_A + 4 * W_B + 3 * W_C
SPLITS = [W_A, 2 * W_A, 2 * W_A + W_B, 2 * W_A + 2 * W_B, 2 * W_A + 3 * W_B,
          2 * W_A + 4 * W_B, 2 * W_A + 4 * W_B + W_C, 2 * W_A + 4 * W_B + 2 * W_C]

kernel_name = 'hybrid_stream_encoder_step'


def layer_norm(x, g=None, b=None):
    xf = x.astype(jnp.float32)
    xc = xf - jnp.mean(xf, axis=-1, keepdims=True)
    y = xc * lax.rsqrt(jnp.mean(xc * xc, axis=-1, keepdims=True) + LN_EPS)
    if g is not None:
        y = y * g + b
    return y.astype(x.dtype)


def modulate(h, shift, scale):
    return h * (1 + scale) + shift


def adaln(c, w, b):
    m = jax.nn.silu(c) @ w + b
    return [t[:, None, :] for t in jnp.split(m, 6, axis=-1)]


def rotary(x, pos):
    half = HEAD_DIM // 2
    inv = ROPE_BASE ** (-jnp.arange(half, dtype=jnp.float32) / half)
    ang = pos.astype(jnp.float32)[:, None] * inv[None, :]
    cos = jnp.cos(ang)[None, :, None, :]
    sin = jnp.sin(ang)[None, :, None, :]
    x1, x2 = jnp.split(x.astype(jnp.float32), 2, axis=-1)
    return jnp.concatenate([x1 * cos - x2 * sin, x1 * sin + x2 * cos], axis=-1)


def chunk_mlp(u, v, ln_g, ln_b, w_sp, b_sp):
    bsz, T, _ = u.shape
    L = min(T, GMLP_CHUNK)
    N = T // L
    u = jax.nn.gelu(u)
    v = layer_norm(jax.nn.gelu(v), ln_g, ln_b)
    w = jnp.tril(w_sp[:, :L, :L])
    vb = v.reshape(bsz, N, L, H_A, HEAD_DIM)
    mixed = jnp.einsum('hts,bnshc->bnthc', w, vb) + b_sp[:, :L].T[None, None, :, :, None]
    return u * mixed.reshape(bsz, T, W_A).astype(u.dtype), v


def retention(q, k, v, s0):
    bsz, T, H, _ = q.shape
    dv = v.shape[-1]
    L = min(T, CHUNK)
    N = T // L
    log_g = jnp.log1p(-jnp.exp2(-5.0 - jnp.arange(H, dtype=jnp.float32)))
    idx = jnp.arange(L, dtype=jnp.float32)
    diff = idx[:, None] - idx[None, :]
    decay = jnp.where(diff >= 0, jnp.exp(diff[None] * log_g[:, None, None]), 0.0)
    q_decay = jnp.exp((idx[None, :] + 1.0) * log_g[:, None])
    k_decay = jnp.exp((L - 1.0 - idx[None, :]) * log_g[:, None])
    blk_decay = jnp.exp(L * log_g)

    def to_blocks(t):
        return jnp.moveaxis(t.astype(jnp.float32).reshape(bsz, N, L, H, t.shape[-1]), 1, 0)

    def step(s, blk):
        qb, kb, vb = blk
        scores = jnp.einsum('bihd,bjhd->bhij', qb, kb) * decay[None]
        o = (jnp.einsum('bhij,bjhe->bihe', scores, vb)
             + jnp.einsum('bihd,bhde->bihe', qb, s) * q_decay.T[None, :, :, None])
        s = s * blk_decay[None, :, None, None] + jnp.einsum('bjhd,bjhe,hj->bhde', kb, vb, k_decay)
        return s, o

    s, o = lax.scan(step, s0.astype(jnp.float32), (to_blocks(q), to_blocks(k), to_blocks(v)))
    return jnp.moveaxis(o, 0, 1).reshape(bsz, T, H, dv), s


def stick_breaking(q, k, v, q_pos0):
    bsz, T, H, d = q.shape
    Lq = min(T, SB_QBLOCK)
    N = T // Lq
    k_pos = jnp.arange(k.shape[1])
    q_pos = (q_pos0 + jnp.arange(T)).reshape(N, Lq)
    q_blocks = jnp.moveaxis(q.reshape(bsz, N, Lq, H, d), 1, 0)

    def block(args):
        qq, qp = args
        z = jnp.einsum('bqhd,bkhd->bhqk', qq, k, preferred_element_type=jnp.float32) * d ** -0.5
        causal = k_pos[None, :] < qp[:, None]
        log_stay = jnp.where(causal, jax.nn.log_sigmoid(-z), 0.0)
        between = lax.cumsum(log_stay, axis=3, reverse=True) - log_stay
        att = jnp.where(causal, jnp.exp(jax.nn.log_sigmoid(z) + between), 0.0)
        return jnp.einsum('bhqk,bkhd->bqhd', att.astype(v.dtype), v)

    o = lax.map(block, (q_blocks, q_pos))
    return jnp.moveaxis(o, 0, 1).reshape(bsz, T, H * d)


def moe(h, w_router, b_router, w1, w3, w2):
    bsz, T, D = h.shape
    t = h.reshape(bsz * T, D)
    s = jax.nn.sigmoid(jnp.einsum('td,de->te', t, w_router, preferred_element_type=jnp.float32))
    sel = (s + b_router).reshape(-1, N_GROUPS, EXPERTS_PER_GROUP)
    g_score = jnp.sum(lax.top_k(sel, TOP_K)[0], axis=-1)
    g_idx = jnp.argmax(g_score, axis=-1)
    in_group = jnp.einsum('tg,tge->te', jax.nn.one_hot(g_idx, N_GROUPS, dtype=jnp.float32), sel)
    _, e_local = lax.top_k(in_group, TOP_K)
    e_idx = g_idx[:, None] * EXPERTS_PER_GROUP + e_local
    wts = jnp.take_along_axis(s, e_idx, axis=-1)
    wts = wts / jnp.sum(wts, axis=-1, keepdims=True)
    comb = jnp.sum(jax.nn.one_hot(e_idx, N_EXPERTS, dtype=jnp.float32) * wts[..., None], axis=1)
    a = jnp.einsum('td,edf->tef', t, w1)
    g = jnp.einsum('td,edf->tef', t, w3)
    act = jax.nn.silu(a) * g * comb[:, :, None].astype(t.dtype)
    return jnp.einsum('tef,efd->td', act, w2).reshape(bsz, T, D)


def trunk_layer(x, mods, pos0, ret_s0, k_past, v_past, w_in, w_out, ln_v_g, ln_v_b, w_sp, b_sp,
                gn_g, gn_b, ln1_g, ln1_b, ln2_g, ln2_b, w_router, b_router, w1, w3, w2):
    bsz, T, _ = x.shape
    shift1, scale1, gate1, shift2, scale2, gate2 = mods
    h = modulate(layer_norm(x), shift1, scale1)
    u_a, v_a, q_b, k_b, v_b, g_b, q_c, k_c, v_c = jnp.split(h @ w_in, SPLITS, axis=-1)

    def heads(t):
        return t.reshape(bsz, T, -1, HEAD_DIM)

    a_out, v_a_rows = chunk_mlp(u_a, v_a, ln_v_g, ln_v_b, w_sp, b_sp)
    pos = pos0 + jnp.arange(T)
    qr = rotary(heads(q_b), pos)
    kr = rotary(heads(k_b), pos) * HEAD_DIM ** -0.5
    s0 = jnp.zeros((bsz, H_B, HEAD_DIM, HEAD_DIM), jnp.float32) if ret_s0 is None else ret_s0
    ret, s_new = retention(qr, kr, heads(v_b), s0)
    b_out = (layer_norm(ret).reshape(bsz, T, W_B) * gn_g + gn_b) * jax.nn.silu(g_b.astype(jnp.float32))
    kc, vc = heads(k_c), heads(v_c)
    k_all = kc if k_past is None else jnp.concatenate([k_past.astype(kc.dtype), kc], axis=1)
    v_all = vc if v_past is None else jnp.concatenate([v_past.astype(vc.dtype), vc], axis=1)
    c_out = stick_breaking(heads(q_c), k_all, v_all, pos0)
    mix = jnp.concatenate([a_out.astype(x.dtype), b_out.astype(x.dtype), c_out.astype(x.dtype)], axis=-1)
    x = layer_norm(ALPHA * x + gate1 * (mix @ w_out), ln1_g, ln1_b)
    h2 = modulate(layer_norm(x), shift2, scale2)
    x = layer_norm(ALPHA * x + gate2 * moe(h2, w_router, b_router, w1, w3, w2), ln2_g, ln2_b)
    return x, v_a_rows, s_new, kc, vc


def setup_inputs(seed: int = 0) -> dict:
    key = jax.random.key(seed)
    ks = jax.random.split(key, 26)
    f32 = jnp.float32

    def nrm(k, shape, s):
        return jax.random.normal(k, shape, f32) * s

    D = D_MODEL
    gate_one = jnp.zeros((6 * D,), f32).at[2 * D:3 * D].set(1.0).at[5 * D:].set(1.0)
    col_scale = jnp.concatenate([
        jnp.full((W_A,), BETA, f32), jnp.ones((W_A,), f32),
        jnp.ones((2 * W_B,), f32), jnp.full((W_B,), BETA, f32), jnp.ones((W_B,), f32),
        jnp.ones((2 * W_C,), f32), jnp.full((W_C,), BETA, f32)])
    return {
        'x_prompt': nrm(ks[0], (BATCH, SEQ, D), 1.0),
        'x_sample': nrm(ks[1], (DEC_BATCH, DEC_SEQ, D), 1.0),
        'cache_sb_k': nrm(ks[2], (DEPTH, DEC_BATCH, PAST_LEN, H_C, HEAD_DIM), 1.0),
        'cache_sb_v': nrm(ks[3], (DEPTH, DEC_BATCH, PAST_LEN, H_C, HEAD_DIM), 1.0),
        'state_ret': nrm(ks[4], (DEPTH, DEC_BATCH, H_B, HEAD_DIM, HEAD_DIM), 1.0),
        'c_prompt': nrm(ks[5], (BATCH, D), 1.0),
        'c_sample': nrm(ks[6], (DEC_BATCH, D), 1.0),
        'w_ada': nrm(ks[7], (DEPTH, D, 6 * D), 0.1 * D ** -0.5),
        'b_ada': nrm(ks[8], (DEPTH, 6 * D), 0.02) + gate_one,
        'w_in': nrm(ks[9], (DEPTH, D, D_IN), D ** -0.5) * col_scale,
        'w_out': nrm(ks[10], (DEPTH, D_MIX, D), BETA * D_MIX ** -0.5),
        'ln_v_g': 1.0 + nrm(ks[11], (DEPTH, W_A), 0.02),
        'ln_v_b': nrm(ks[12], (DEPTH, W_A), 0.02),
        'w_spatial': nrm(ks[13], (DEPTH, H_A, GMLP_CHUNK, GMLP_CHUNK), GMLP_CHUNK ** -0.5),
        'b_spatial': 1.0 + nrm(ks[14], (DEPTH, H_A, GMLP_CHUNK), 0.02),
        'gn_g': 1.0 + nrm(ks[15], (DEPTH, W_B), 0.02),
        'gn_b': nrm(ks[16], (DEPTH, W_B), 0.02),
        'ln1_g': 1.0 + nrm(ks[17], (DEPTH, D), 0.02),
        'ln1_b': nrm(ks[18], (DEPTH, D), 0.02),
        'ln2_g': 1.0 + nrm(ks[19], (DEPTH, D), 0.02),
        'ln2_b': nrm(ks[20], (DEPTH, D), 0.02),
        'w_router': nrm(ks[21], (D, N_EXPERTS), D ** -0.5),
        'b_router': nrm(ks[22], (N_EXPERTS,), 0.01),
        'w1': nrm(ks[23], (DEPTH, N_EXPERTS, D, D_EXPERT), BETA * D ** -0.5),
        'w3': nrm(ks[24], (DEPTH, N_EXPERTS, D, D_EXPERT), BETA * D ** -0.5),
        'w2': nrm(ks[25], (DEPTH, N_EXPERTS, D_EXPERT, D), BETA * D_EXPERT ** -0.5),
    }


def reference(x_prompt, x_sample, cache_sb_k, cache_sb_v, state_ret, c_prompt, c_sample,
              w_ada, b_ada, w_in, w_out, ln_v_g, ln_v_b, w_spatial, b_spatial, gn_g, gn_b,
              ln1_g, ln1_b, ln2_g, ln2_b, w_router, b_router, w1, w3, w2):
    past_len = cache_sb_k.shape[2]
    y_p, y_s = x_prompt, x_sample
    ret_p, k_p_rows, v_p_rows = [], [], []
    ret_s, k_s_rows, v_s_rows, gmlp_s_rows = [], [], [], []
    for l in range(DEPTH):
        shared = (w_in[l], w_out[l], ln_v_g[l], ln_v_b[l], w_spatial[l], b_spatial[l], gn_g[l], gn_b[l],
                  ln1_g[l], ln1_b[l], ln2_g[l], ln2_b[l], w_router, b_router, w1[l], w3[l], w2[l])
        y_p, _, s_p, k_p, v_p = trunk_layer(
            y_p, adaln(c_prompt, w_ada[l], b_ada[l]), 0, None, None, None, *shared)
        y_s, g_s, s_s, k_s, v_s = trunk_layer(
            y_s, adaln(c_sample, w_ada[l], b_ada[l]), past_len, state_ret[l],
            cache_sb_k[l], cache_sb_v[l], *shared)
        ret_p.append(s_p)
        k_p_rows.append(k_p)
        v_p_rows.append(v_p)
        ret_s.append(s_s)
        k_s_rows.append(k_s)
        v_s_rows.append(v_s)
        gmlp_s_rows.append(g_s)
    return (y_p, y_s, jnp.stack(ret_p), jnp.stack(k_p_rows), jnp.stack(v_p_rows),
            jnp.stack(ret_s), jnp.stack(k_s_rows), jnp.stack(v_s_rows), jnp.stack(gmlp_s_rows))
```

```python
import functools

import numpy as np
import jax
import jax.numpy as jnp
from jax import lax
from jax.experimental import pallas as pl
from jax.experimental.pallas import tpu as pltpu

F32 = jnp.float32
BF16 = jnp.bfloat16
HIGHEST = lax.Precision.HIGHEST

D_MODEL = 1024
DEPTH = 2
HEAD_DIM = 64
W_A = D_MODEL // 4
W_B = 3 * D_MODEL // 8
W_C = D_MODEL - W_A - W_B
H_A = W_A // HEAD_DIM
H_B = W_B // HEAD_DIM
H_C = W_C // HEAD_DIM
CHUNK = 64
GMLP_CHUNK = 128
ROPE_BASE = 10000.0
N_EXPERTS = 16
N_GROUPS = 4
EXPERTS_PER_GROUP = N_EXPERTS // N_GROUPS
D_EXPERT = D_MODEL // 2
ALPHA = (2 * DEPTH) ** 0.25
LN_EPS = 1e-5
D_IN = 2 * W_A + 4 * W_B + 3 * W_C
LANES = 128
VMEM_LIMIT = 48 * 1024 * 1024

NT_DIMS = (((1,), (1,)), ((), ()))
TN_DIMS = (((0,), (0,)), ((), ()))


def _ln(x):
    mu = jnp.mean(x, axis=-1, keepdims=True)
    xc = x - mu
    var = jnp.mean(xc * xc, axis=-1, keepdims=True)
    return xc * lax.rsqrt(var + LN_EPS)


def _silu(x):
    return x * jax.nn.sigmoid(x)


def _params(*sem):
    return pltpu.CompilerParams(dimension_semantics=sem, vmem_limit_bytes=VMEM_LIMIT)


def _row_spec(rows, tm, width, total_rows):
    if rows == 1:
        return pl.BlockSpec((1, width), lambda i: (0, 0))
    assert rows == total_rows
    return pl.BlockSpec((tm, width), lambda i: (i, 0))


def _adaln_kernel(c_ref, w_ref, b_ref, o_ref):
    sc = _silu(c_ref[...])
    o_ref[0] = jnp.dot(sc, w_ref[0], preferred_element_type=F32, precision=HIGHEST) + b_ref[0]


def _adaln(c_all, w_ada, b_ada):
    rows = c_all.shape[0]
    tn = 1536
    return pl.pallas_call(
        _adaln_kernel,
        grid=(DEPTH, 6 * D_MODEL // tn),
        in_specs=[pl.BlockSpec((rows, D_MODEL), lambda l, j: (0, 0)),
                  pl.BlockSpec((1, D_MODEL, tn), lambda l, j: (l, 0, j)),
                  pl.BlockSpec((1, 1, tn), lambda l, j: (l, 0, j))],
        out_specs=pl.BlockSpec((1, rows, tn), lambda l, j: (l, 0, j)),
        out_shape=jax.ShapeDtypeStruct((DEPTH, rows, 6 * D_MODEL), F32),
        compiler_params=_params("parallel", "parallel"),
        name="adaln",
    )(c_all, w_ada, b_ada.reshape(DEPTH, 1, 6 * D_MODEL))


def _inproj_kernel(x_ref, shift_ref, scale_ref, w_ref, uv_ref, ret_ref, q_ref, k_ref, v_ref):
    h = _ln(x_ref[...]) * (1.0 + scale_ref[...]) + shift_ref[...]
    r = jnp.dot(h.astype(BF16), w_ref[...], preferred_element_type=F32)
    c0 = 2 * W_A
    c1 = c0 + 4 * W_B
    uv_ref[...] = r[:, :c0]
    ret_ref[...] = r[:, c0:c1]
    q_ref[...] = r[:, c1:c1 + W_C]
    k_ref[...] = r[:, c1 + W_C:c1 + 2 * W_C]
    v_ref[...] = r[:, c1 + 2 * W_C:]


def _inproj(x, shift, scale, w_bf16, tm):
    rows = x.shape[0]
    widths = (2 * W_A, 4 * W_B, W_C, W_C, W_C)
    return pl.pallas_call(
        _inproj_kernel,
        grid=(rows // tm,),
        in_specs=[pl.BlockSpec((tm, D_MODEL), lambda i: (i, 0)),
                  _row_spec(shift.shape[0], tm, D_MODEL, rows),
                  _row_spec(scale.shape[0], tm, D_MODEL, rows),
                  pl.BlockSpec((D_MODEL, D_IN), lambda i: (0, 0))],
        out_specs=[pl.BlockSpec((tm, w), lambda i: (i, 0)) for w in widths],
        out_shape=[jax.ShapeDtypeStruct((rows, w), F32) for w in widths],
        compiler_params=_params("parallel"),
        name="inproj",
    )(x, shift, scale, w_bf16)


def _gmlp_kernel(uv_ref, wsp_ref, bias_ref, g_ref, b_ref, a_ref, vn_ref, *, chunk):
    uv = uv_ref[...]
    u = jax.nn.gelu(uv[:, :W_A])
    v = _ln(jax.nn.gelu(uv[:, W_A:])) * g_ref[...] + b_ref[...]
    vn_ref[...] = v
    row = lax.broadcasted_iota(jnp.int32, (chunk, chunk), 0)
    col = lax.broadcasted_iota(jnp.int32, (chunk, chunk), 1)
    lane_head = lax.broadcasted_iota(jnp.int32, (chunk, W_A), 1) // HEAD_DIM
    mixed = bias_ref[...]
    for h in range(H_A):
        w = jnp.where(col <= row, wsp_ref[h], 0.0).astype(BF16)
        vh = jnp.where(lane_head == h, v, 0.0).astype(BF16)
        mixed = mixed + jnp.dot(w, vh, preferred_element_type=F32)
    a_ref[...] = u * mixed


def _gmlp(uv, w_sp, b_sp, ln_g, ln_b, chunk):
    rows = uv.shape[0]
    wsp = w_sp[:, :chunk, :chunk]
    bias = jnp.repeat(b_sp[:, :chunk].T, HEAD_DIM, axis=1)
    return pl.pallas_call(
        functools.partial(_gmlp_kernel, chunk=chunk),
        grid=(rows // chunk,),
        in_specs=[pl.BlockSpec((chunk, 2 * W_A), lambda i: (i, 0)),
                  pl.BlockSpec((H_A, chunk, chunk), lambda i: (0, 0, 0)),
                  pl.BlockSpec((chunk, W_A), lambda i: (0, 0)),
                  pl.BlockSpec((1, W_A), lambda i: (0, 0)),
                  pl.BlockSpec((1, W_A), lambda i: (0, 0))],
        out_specs=[pl.BlockSpec((chunk, W_A), lambda i: (i, 0)),
                   pl.BlockSpec((chunk, W_A), lambda i: (i, 0))],
        out_shape=[jax.ShapeDtypeStruct((rows, W_A), F32),
                   jax.ShapeDtypeStruct((rows, W_A), F32)],
        compiler_params=_params("parallel"),
        name="gmlp",
    )(uv, wsp, bias, ln_g.reshape(1, W_A), ln_b.reshape(1, W_A))


def _rope(x, cos, sin):
    lane = lax.broadcasted_iota(jnp.int32, (x.shape[0], LANES), 1)
    first_half = (lane & (HEAD_DIM // 2)) == 0
    parts = []
    for c in range(x.shape[1] // LANES):
        xc = x[:, c * LANES:(c + 1) * LANES]
        rot = jnp.where(first_half,
                        pltpu.roll(xc, LANES - HEAD_DIM // 2, 1),
                        pltpu.roll(xc, HEAD_DIM // 2, 1))
        parts.append(xc * cos + rot * sin)
    return jnp.concatenate(parts, axis=1)


def _ret_kernel(r_ref, cos_ref, sin_ref, qdec_ref, kdec_ref, dec_ref, blk_ref, s0_ref,
                gng_ref, gnb_ref, o_ref, sout_ref, s_scr, o_scr, *, n_chunks):
    n = pl.program_id(1)

    @pl.when(n == 0)
    def _():
        s_scr[...] = s0_ref[0]

    r = r_ref[...]
    cos = cos_ref[...]
    sin = sin_ref[...]
    qr = _rope(r[:, :W_B], cos, sin)
    kr = _rope(r[:, W_B:2 * W_B], cos, sin) * (HEAD_DIM ** -0.5)
    vb = r[:, 2 * W_B:3 * W_B].astype(BF16)
    gate = r[:, 3 * W_B:]
    qb = qr.astype(BF16)
    kb = kr.astype(BF16)
    qdb = (qr * qdec_ref[...]).astype(BF16)
    kdb = (kr * kdec_ref[...]).astype(BF16)
    for h in range(H_B):
        sl = slice(h * HEAD_DIM, (h + 1) * HEAD_DIM)
        scores = lax.dot_general(qb[:, sl], kb[:, sl], NT_DIMS, preferred_element_type=F32) * dec_ref[h]
        s_h = s_scr[h]
        o_h = (jnp.dot(scores.astype(BF16), vb[:, sl], preferred_element_type=F32)
               + jnp.dot(qdb[:, sl], s_h.astype(BF16), preferred_element_type=F32))
        s_scr[h] = s_h * blk_ref[h] + lax.dot_general(kdb[:, sl], vb[:, sl], TN_DIMS,
                                                      preferred_element_type=F32)
        o_scr[:, sl] = _ln(o_h)
    o_ref[...] = (o_scr[...] * gng_ref[...] + gnb_ref[...]) * _silu(gate)

    @pl.when(n == n_chunks - 1)
    def _():
        sout_ref[0] = s_scr[...]


def _retention(ret, s0, pos0, bsz, seq, gn_g, gn_b):
    L = min(seq, CHUNK)
    n_chunks = seq // L
    half = HEAD_DIM // 2
    inv = ROPE_BASE ** (-jnp.arange(half, dtype=F32) / half)
    ang = (pos0 + jnp.arange(seq)).astype(F32)[:, None] * inv[None, :]
    cos, sin = jnp.cos(ang), jnp.sin(ang)
    cos_t = jnp.tile(jnp.concatenate([cos, cos], axis=1), (1, LANES // HEAD_DIM))
    sin_t = jnp.tile(jnp.concatenate([-sin, sin], axis=1), (1, LANES // HEAD_DIM))
    log_g = jnp.log1p(-jnp.exp2(-5.0 - jnp.arange(H_B, dtype=F32)))
    idx = jnp.arange(L, dtype=F32)
    diff = idx[:, None] - idx[None, :]
    decay = jnp.where(diff >= 0, jnp.exp(diff[None] * log_g[:, None, None]), 0.0)
    q_decay = jnp.exp((idx[None, :] + 1.0) * log_g[:, None])
    k_decay = jnp.exp((L - 1.0 - idx[None, :]) * log_g[:, None])
    blk_decay = jnp.exp(L * log_g)
    qdec = jnp.repeat(q_decay.T, HEAD_DIM, axis=1)
    kdec = jnp.repeat(k_decay.T, HEAD_DIM, axis=1)
    blk = jnp.broadcast_to(blk_decay[:, None, None], (H_B, HEAD_DIM, HEAD_DIM))
    const2 = lambda b, n: (0, 0)
    const3 = lambda b, n: (0, 0, 0)
    return pl.pallas_call(
        functools.partial(_ret_kernel, n_chunks=n_chunks),
        grid=(bsz, n_chunks),
        in_specs=[pl.BlockSpec((L, 4 * W_B), lambda b, n: (b * n_chunks + n, 0)),
                  pl.BlockSpec((L, LANES), lambda b, n: (n, 0)),
                  pl.BlockSpec((L, LANES), lambda b, n: (n, 0)),
                  pl.BlockSpec((L, W_B), const2),
                  pl.BlockSpec((L, W_B), const2),
                  pl.BlockSpec((H_B, L, L), const3),
                  pl.BlockSpec((H_B, HEAD_DIM, HEAD_DIM), const3),
                  pl.BlockSpec((1, H_B, HEAD_DIM, HEAD_DIM), lambda b, n: (b, 0, 0, 0)),
                  pl.BlockSpec((1, W_B), const2),
                  pl.BlockSpec((1, W_B), const2)],
        out_specs=[pl.BlockSpec((L, W_B), lambda b, n: (b * n_chunks + n, 0)),
                   pl.BlockSpec((1, H_B, HEAD_DIM, HEAD_DIM), lambda b, n: (b, 0, 0, 0))],
        out_shape=[jax.ShapeDtypeStruct((bsz * seq, W_B), F32),
                   jax.ShapeDtypeStruct((bsz, H_B, HEAD_DIM, HEAD_DIM), F32)],
        scratch_shapes=[pltpu.VMEM((H_B, HEAD_DIM, HEAD_DIM), F32),
                        pltpu.VMEM((L, W_B), F32)],
        compiler_params=_params("parallel", "arbitrary"),
        name="retention",
    )(ret, cos_t, sin_t, qdec, kdec, decay, blk, s0, gn_g.reshape(1, W_B), gn_b.reshape(1, W_B))


def _sb_last_block(i, tq, tk, q_pos0):
    return (q_pos0 + (i + 1) * tq - 2) // tk


def _sb_kernel(q_ref, k_ref, v_ref, u_ref, o_ref, acc_ref, carry_ref, *, tq, tk, qs, q_pos0):
    i = pl.program_id(1)
    j = pl.program_id(2)
    last = _sb_last_block(i, tq, tk, q_pos0)

    @pl.when(j == 0)
    def _():
        acc_ref[...] = jnp.zeros_like(acc_ref)
        carry_ref[...] = jnp.zeros_like(carry_ref)

    @pl.when(j <= last)
    def _():
        k0 = (last - j) * tk
        kall = k_ref[...].astype(BF16)
        vall = v_ref[...].astype(BF16)
        u = u_ref[...]
        kpos = k0 + lax.broadcasted_iota(jnp.int32, (qs, tk), 1)
        row = lax.broadcasted_iota(jnp.int32, (qs, tk), 0)
        for h in range(H_C):
            sl = slice(h * HEAD_DIM, (h + 1) * HEAD_DIM)
            kh = kall[:, sl]
            vh = vall[:, sl]

            def body(s, _, h=h, sl=sl, kh=kh, vh=vh):
                r0 = pl.multiple_of(s * qs, qs)
                q = (q_ref[pl.ds(r0, qs), sl] * (HEAD_DIM ** -0.5)).astype(BF16)
                z = lax.dot_general(q, kh, NT_DIMS, preferred_element_type=F32)
                causal = kpos < (q_pos0 + i * tq + r0) + row
                ls_pos = jnp.minimum(z, 0.0) - jnp.log1p(jnp.exp(-jnp.abs(z)))
                log_stay = jnp.where(causal, ls_pos - z, 0.0)
                hi = log_stay.astype(BF16)
                lo = (log_stay - hi.astype(F32)).astype(BF16)
                excl = (jnp.dot(hi, u, preferred_element_type=F32)
                        + jnp.dot(lo, u, preferred_element_type=F32))
                carry = carry_ref[h, pl.ds(r0, qs), :]
                att = jnp.where(causal, jnp.exp(ls_pos + excl + carry), 0.0)
                acc_ref[pl.ds(r0, qs), sl] += jnp.dot(att.astype(BF16), vh, preferred_element_type=F32)
                carry_ref[h, pl.ds(r0, qs), :] = carry + jnp.sum(log_stay, axis=-1, keepdims=True)
                return 0

            lax.fori_loop(0, tq // qs, body, 0)

    @pl.when(j == last)
    def _():
        o_ref[...] = acc_ref[...]


def _stick_breaking(q, k, v, bsz, seq, kv_len, q_pos0, tq, tk, qs):
    nq = seq // tq
    nk = (q_pos0 + seq - 2) // tk + 1
    nkb = kv_len // tk
    assert nk <= nkb
    tri = np.tril(np.ones((tk, tk), np.float32), -1)
    u = jnp.asarray(tri, dtype=BF16)

    def kv_map(b, i, j):
        return (b * nkb + jnp.maximum(_sb_last_block(i, tq, tk, q_pos0) - j, 0), 0)

    return pl.pallas_call(
        functools.partial(_sb_kernel, tq=tq, tk=tk, qs=qs, q_pos0=q_pos0),
        grid=(bsz, nq, nk),
        in_specs=[pl.BlockSpec((tq, W_C), lambda b, i, j: (b * nq + i, 0)),
                  pl.BlockSpec((tk, W_C), kv_map),
                  pl.BlockSpec((tk, W_C), kv_map),
                  pl.BlockSpec((tk, tk), lambda b, i, j: (0, 0))],
        out_specs=pl.BlockSpec((tq, W_C), lambda b, i, j: (b * nq + i, 0)),
        out_shape=jax.ShapeDtypeStruct((bsz * seq, W_C), F32),
        scratch_shapes=[pltpu.VMEM((tq, W_C), F32),
                        pltpu.VMEM((H_C, tq, 1), F32)],
        compiler_params=_params("parallel", "parallel", "arbitrary"),
        name="stick_breaking",
    )(q, k, v, u)


def _route(sel, s, tm):
    g_scores = []
    for g in range(N_GROUPS):
        a, b, c, d = sel[EXPERTS_PER_GROUP * g:EXPERTS_PER_GROUP * (g + 1)]
        ab_hi, ab_lo = jnp.maximum(a, b), jnp.minimum(a, b)
        cd_hi, cd_lo = jnp.maximum(c, d), jnp.minimum(c, d)
        top1 = jnp.maximum(ab_hi, cd_hi)
        top2 = jnp.maximum(jnp.minimum(ab_hi, cd_hi), jnp.maximum(ab_lo, cd_lo))
        g_scores.append(top1 + top2)
    best = g_scores[0]
    gi = jnp.zeros((tm, 1), jnp.int32)
    for g in range(1, N_GROUPS):
        upd = g_scores[g] > best
        gi = jnp.where(upd, g, gi)
        best = jnp.where(upd, g_scores[g], best)

    def pick_group(cols, l):
        out = cols[(N_GROUPS - 1) * EXPERTS_PER_GROUP + l]
        for g in range(N_GROUPS - 2, -1, -1):
            out = jnp.where(gi == g, cols[g * EXPERTS_PER_GROUP + l], out)
        return out

    ig = [pick_group(sel, l) for l in range(EXPERTS_PER_GROUP)]
    sg = [pick_group(s, l) for l in range(EXPERTS_PER_GROUP)]
    b1 = ig[0]
    i1 = jnp.zeros((tm, 1), jnp.int32)
    for l in range(1, EXPERTS_PER_GROUP):
        upd = ig[l] > b1
        i1 = jnp.where(upd, l, i1)
        b1 = jnp.where(upd, ig[l], b1)
    b2 = jnp.full((tm, 1), -jnp.inf, F32)
    i2 = jnp.zeros((tm, 1), jnp.int32)
    for l in range(EXPERTS_PER_GROUP):
        upd = jnp.logical_and(i1 != l, ig[l] > b2)
        i2 = jnp.where(upd, l, i2)
        b2 = jnp.where(upd, ig[l], b2)

    def pick_local(idx):
        out = sg[EXPERTS_PER_GROUP - 1]
        for l in range(EXPERTS_PER_GROUP - 2, -1, -1):
            out = jnp.where(idx == l, sg[l], out)
        return out

    w1 = pick_local(i1)
    w2 = pick_local(i2)
    tot = w1 + w2
    return gi * EXPERTS_PER_GROUP + i1, gi * EXPERTS_PER_GROUP + i2, w1 / tot, w2 / tot


def _post_kernel(a_ref, b_ref, c_ref, x_ref, wo_ref, gate_ref, g1_ref, b1_ref, sh2_ref, sc2_ref,
                 wr_ref, br_ref, x1_ref, h2_ref, comb_ref, *, tm):
    proj = (jnp.dot(a_ref[...].astype(BF16), wo_ref[:W_A], preferred_element_type=F32)
            + jnp.dot(b_ref[...].astype(BF16), wo_ref[W_A:W_A + W_B], preferred_element_type=F32)
            + jnp.dot(c_ref[...].astype(BF16), wo_ref[W_A + W_B:], preferred_element_type=F32))
    x1 = _ln(ALPHA * x_ref[...] + gate_ref[...] * proj) * g1_ref[...] + b1_ref[...]
    x1_ref[...] = x1
    h2 = _ln(x1) * (1.0 + sc2_ref[...]) + sh2_ref[...]
    h2_ref[...] = h2.astype(BF16)
    logits = jnp.dot(h2, wr_ref[...], preferred_element_type=F32, precision=HIGHEST)
    s_all = jax.nn.sigmoid(logits)
    sel_all = s_all + br_ref[...]
    s = [s_all[:, e:e + 1] for e in range(N_EXPERTS)]
    sel = [sel_all[:, e:e + 1] for e in range(N_EXPERTS)]
    e1, e2, w1, w2 = _route(sel, s, tm)
    lane = lax.broadcasted_iota(jnp.int32, (tm, LANES), 1)
    comb_ref[...] = jnp.where(lane == e1, w1, jnp.where(lane == e2, w2, 0.0))


def _post(a, b, c, x, w_out_bf16, gate1, ln_g, ln_b, shift2, scale2, wr_pad, br_pad, tm):
    rows = x.shape[0]
    row = lambda w: pl.BlockSpec((tm, w), lambda i: (i, 0))
    const = lambda r, w: pl.BlockSpec((r, w), lambda i: (0, 0))
    return pl.pallas_call(
        functools.partial(_post_kernel, tm=tm),
        grid=(rows // tm,),
        in_specs=[row(W_A), row(W_B), row(W_C), row(D_MODEL),
                  const(D_MODEL, D_MODEL),
                  _row_spec(gate1.shape[0], tm, D_MODEL, rows),
                  const(1, D_MODEL), const(1, D_MODEL),
                  _row_spec(shift2.shape[0], tm, D_MODEL, rows),
                  _row_spec(scale2.shape[0], tm, D_MODEL, rows),
                  const(D_MODEL, LANES), const(1, LANES)],
        out_specs=[row(D_MODEL), row(D_MODEL), row(LANES)],
        out_shape=[jax.ShapeDtypeStruct((rows, D_MODEL), F32),
                   jax.ShapeDtypeStruct((rows, D_MODEL), BF16),
                   jax.ShapeDtypeStruct((rows, LANES), F32)],
        compiler_params=_params("parallel"),
        name="post_mix",
    )(a, b, c, x, w_out_bf16, gate1, ln_g.reshape(1, D_MODEL), ln_b.reshape(1, D_MODEL),
      shift2, scale2, wr_pad, br_pad)


def _moe_kernel(h_ref, comb_ref, x_ref, gate_ref, g2_ref, b2_ref, w1_ref, w3_ref, w2_ref,
                o_ref, acc_ref, *, tm):
    e = pl.program_id(1)

    @pl.when(e == 0)
    def _():
        acc_ref[...] = jnp.zeros_like(acc_ref)

    h = h_ref[...]
    a = jnp.dot(h, w1_ref[0], preferred_element_type=F32)
    g = jnp.dot(h, w3_ref[0], preferred_element_type=F32)
    lane = lax.broadcasted_iota(jnp.int32, (tm, LANES), 1)
    ce = jnp.sum(jnp.where(lane == e, comb_ref[...], 0.0), axis=-1, keepdims=True)
    act = _silu(a) * g * ce
    acc_ref[...] += jnp.dot(act.astype(BF16), w2_ref[0], preferred_element_type=F32)

    @pl.when(e == N_EXPERTS - 1)
    def _():
        y = ALPHA * x_ref[...] + gate_ref[...] * acc_ref[...]
        o_ref[...] = _ln(y) * g2_ref[...] + b2_ref[...]


def _moe(h2, comb, x1, gate2, ln_g, ln_b, w1, w3, w2, tm):
    rows = x1.shape[0]
    row = lambda w: pl.BlockSpec((tm, w), lambda i, e: (i, 0))
    const = pl.BlockSpec((1, D_MODEL), lambda i, e: (0, 0))
    gate_spec = (pl.BlockSpec((1, D_MODEL), lambda i, e: (0, 0)) if gate2.shape[0] == 1
                 else row(D_MODEL))
    return pl.pallas_call(
        functools.partial(_moe_kernel, tm=tm),
        grid=(rows // tm, N_EXPERTS),
        in_specs=[row(D_MODEL), row(LANES), row(D_MODEL), gate_spec, const, const,
                  pl.BlockSpec((1, D_MODEL, D_EXPERT), lambda i, e: (e, 0, 0)),
                  pl.BlockSpec((1, D_MODEL, D_EXPERT), lambda i, e: (e, 0, 0)),
                  pl.BlockSpec((1, D_EXPERT, D_MODEL), lambda i, e: (e, 0, 0))],
        out_specs=row(D_MODEL),
        out_shape=jax.ShapeDtypeStruct((rows, D_MODEL), F32),
        scratch_shapes=[pltpu.VMEM((tm, D_MODEL), F32)],
        compiler_params=_params("parallel", "arbitrary"),
        name="experts",
    )(h2, comb, x1, gate2, ln_g.reshape(1, D_MODEL), ln_b.reshape(1, D_MODEL), w1, w3, w2)


def _trunk_layer(x, mods, bsz, seq, pos0, s0, k_past, v_past, p, tiles):
    shift1, scale1, gate1, shift2, scale2, gate2 = mods
    uv, ret, q_c, k_c, v_c = _inproj(x, shift1, scale1, p["w_in"], tiles["tm_in"])
    a_out, v_rows = _gmlp(uv, p["w_sp"], p["b_sp"], p["ln_v_g"], p["ln_v_b"], min(seq, GMLP_CHUNK))
    b_out, s_new = _retention(ret, s0, pos0, bsz, seq, p["gn_g"], p["gn_b"])
    tq, tk, qs = tiles["sb"]
    if k_past is None:
        k_all, v_all, kv_len = k_c, v_c, seq
    else:
        past = k_past.shape[1]
        kv_len = -(-(past + seq) // tk) * tk
        pad = jnp.zeros((bsz, kv_len - past - seq, W_C), F32)
        k_all = jnp.concatenate([k_past, k_c.reshape(bsz, seq, W_C), pad], axis=1).reshape(bsz * kv_len, W_C)
        v_all = jnp.concatenate([v_past, v_c.reshape(bsz, seq, W_C), pad], axis=1).reshape(bsz * kv_len, W_C)
    c_out = _stick_breaking(q_c, k_all, v_all, bsz, seq, kv_len, pos0, tq, tk, qs)
    x1, h2, comb = _post(a_out, b_out, c_out, x, p["w_out"], gate1, p["ln1_g"], p["ln1_b"],
                         shift2, scale2, p["wr"], p["br"], tiles["tm_post"])
    y = _moe(h2, comb, x1, gate2, p["ln2_g"], p["ln2_b"], p["w1"], p["w3"], p["w2"], tiles["tm_moe"])
    return y, v_rows, s_new, k_c, v_c


def kernel(x_prompt, x_sample, cache_sb_k, cache_sb_v, state_ret, c_prompt, c_sample, w_ada, b_ada, w_in, w_out, ln_v_g, ln_v_b, w_spatial, b_spatial, gn_g, gn_b, ln1_g, ln1_b, ln2_g, ln2_b, w_router, b_router, w1, w3, w2):
    bp, tp, _ = x_prompt.shape
    bs, ts, _ = x_sample.shape
    past_len = cache_sb_k.shape[2]
    assert bp == 1

    n_c = bp + bs
    c_rows = -(-n_c // 8) * 8
    c_all = jnp.concatenate([c_prompt, c_sample, jnp.zeros((c_rows - n_c, D_MODEL), F32)], axis=0)
    mods = _adaln(c_all, w_ada, b_ada)

    wr_pad = jnp.pad(w_router, ((0, 0), (0, LANES - N_EXPERTS)))
    br_pad = jnp.pad(b_router, (0, LANES - N_EXPERTS)).reshape(1, LANES)

    tiles_p = dict(tm_in=256, sb=(512, 256, 128), tm_post=256, tm_moe=512)
    tiles_s = dict(tm_in=bs * ts, sb=(ts, 256, ts), tm_post=bs * ts, tm_moe=bs * ts)

    y_p = x_prompt.reshape(bp * tp, D_MODEL)
    y_s = x_sample.reshape(bs * ts, D_MODEL)
    zero_state = jnp.zeros((bp, H_B, HEAD_DIM, HEAD_DIM), F32)
    outs = [[] for _ in range(7)]
    for l in range(DEPTH):
        p = dict(w_in=w_in[l].astype(BF16), w_out=w_out[l].astype(BF16),
                 w_sp=w_spatial[l], b_sp=b_spatial[l], ln_v_g=ln_v_g[l], ln_v_b=ln_v_b[l],
                 gn_g=gn_g[l], gn_b=gn_b[l], ln1_g=ln1_g[l], ln1_b=ln1_b[l],
                 ln2_g=ln2_g[l], ln2_b=ln2_b[l], wr=wr_pad, br=br_pad,
                 w1=w1[l].astype(BF16), w3=w3[l].astype(BF16), w2=w2[l].astype(BF16))
        m = mods[l]
        mods_p = [m[0:1, i * D_MODEL:(i + 1) * D_MODEL] for i in range(6)]
        mods_s = [jnp.repeat(m[bp:bp + bs, i * D_MODEL:(i + 1) * D_MODEL], ts, axis=0) for i in range(6)]
        y_p, _, s_p, k_p, v_p = _trunk_layer(y_p, mods_p, bp, tp, 0, zero_state, None, None, p, tiles_p)
        y_s, g_s, s_s, k_s, v_s = _trunk_layer(
            y_s, mods_s, bs, ts, past_len, state_ret[l],
            cache_sb_k[l].reshape(bs, past_len, W_C), cache_sb_v[l].reshape(bs, past_len, W_C), p, tiles_s)
        outs[0].append(s_p)
        outs[1].append(k_p.reshape(bp, tp, H_C, HEAD_DIM))
        outs[2].append(v_p.reshape(bp, tp, H_C, HEAD_DIM))
        outs[3].append(s_s)
        outs[4].append(k_s.reshape(bs, ts, H_C, HEAD_DIM))
        outs[5].append(v_s.reshape(bs, ts, H_C, HEAD_DIM))
        outs[6].append(g_s.reshape(bs, ts, W_A))
    return (y_p.reshape(bp, tp, D_MODEL), y_s.reshape(bs, ts, D_MODEL)) + tuple(jnp.stack(o) for o in outs)
```

```python
import functools

import numpy as np
import jax
import jax.numpy as jnp
from jax import lax
from jax.experimental import pallas as pl
from jax.experimental.pallas import tpu as pltpu

F32 = jnp.float32
BF16 = jnp.bfloat16
HIGHEST = lax.Precision.HIGHEST

D_MODEL = 1024
DEPTH = 2
HEAD_DIM = 64
W_A = D_MODEL // 4
W_B = 3 * D_MODEL // 8
W_C = D_MODEL - W_A - W_B
H_A = W_A // HEAD_DIM
H_B = W_B // HEAD_DIM
H_C = W_C // HEAD_DIM
CHUNK = 64
GMLP_CHUNK = 128
ROPE_BASE = 10000.0
N_EXPERTS = 16
N_GROUPS = 4
EXPERTS_PER_GROUP = N_EXPERTS // N_GROUPS
D_EXPERT = D_MODEL // 2
ALPHA = (2 * DEPTH) ** 0.25
LN_EPS = 1e-5
D_IN = 2 * W_A + 4 * W_B + 3 * W_C
LANES = 128
VMEM_LIMIT = 48 * 1024 * 1024

NT_DIMS = (((1,), (1,)), ((), ()))
TN_DIMS = (((0,), (0,)), ((), ()))
INV_LN2 = 1.4426950408889634
SB_QSCALE = HEAD_DIM ** -0.5 * INV_LN2


def _ln(x):
    mu = jnp.mean(x, axis=-1, keepdims=True)
    xc = x - mu
    var = jnp.mean(xc * xc, axis=-1, keepdims=True)
    return xc * lax.rsqrt(var + LN_EPS)


def _silu(x):
    return x * jax.nn.sigmoid(x)


def _params(*sem):
    return pltpu.CompilerParams(dimension_semantics=sem, vmem_limit_bytes=VMEM_LIMIT)


def _row_spec(rows, tm, width, total_rows):
    if rows == 1:
        return pl.BlockSpec((1, width), lambda i: (0, 0))
    assert rows == total_rows
    return pl.BlockSpec((tm, width), lambda i: (i, 0))


def _adaln_kernel(c_ref, w_ref, b_ref, o_ref):
    sc = _silu(c_ref[...])
    o_ref[0] = jnp.dot(sc, w_ref[0], preferred_element_type=F32, precision=HIGHEST) + b_ref[0]


def _adaln(c_all, w_ada, b_ada):
    rows = c_all.shape[0]
    tn = 1536
    return pl.pallas_call(
        _adaln_kernel,
        grid=(DEPTH, 6 * D_MODEL // tn),
        in_specs=[pl.BlockSpec((rows, D_MODEL), lambda l, j: (0, 0)),
                  pl.BlockSpec((1, D_MODEL, tn), lambda l, j: (l, 0, j)),
                  pl.BlockSpec((1, 1, tn), lambda l, j: (l, 0, j))],
        out_specs=pl.BlockSpec((1, rows, tn), lambda l, j: (l, 0, j)),
        out_shape=jax.ShapeDtypeStruct((DEPTH, rows, 6 * D_MODEL), F32),
        compiler_params=_params("parallel", "parallel"),
        name="adaln",
    )(c_all, w_ada, b_ada.reshape(DEPTH, 1, 6 * D_MODEL))


def _inproj_kernel(x_ref, shift_ref, scale_ref, w_ref, uv_ref, ret_ref, q_ref, k_ref, v_ref):
    h = _ln(x_ref[...]) * (1.0 + scale_ref[...]) + shift_ref[...]
    r = jnp.dot(h.astype(BF16), w_ref[...], preferred_element_type=F32)
    c0 = 2 * W_A
    c1 = c0 + 4 * W_B
    uv_ref[...] = r[:, :c0]
    ret_ref[...] = r[:, c0:c1]
    q_ref[...] = r[:, c1:c1 + W_C]
    k_ref[...] = r[:, c1 + W_C:c1 + 2 * W_C]
    v_ref[...] = r[:, c1 + 2 * W_C:]


def _inproj(x, shift, scale, w_bf16, tm):
    rows = x.shape[0]
    widths = (2 * W_A, 4 * W_B, W_C, W_C, W_C)
    return pl.pallas_call(
        _inproj_kernel,
        grid=(rows // tm,),
        in_specs=[pl.BlockSpec((tm, D_MODEL), lambda i: (i, 0)),
                  _row_spec(shift.shape[0], tm, D_MODEL, rows),
                  _row_spec(scale.shape[0], tm, D_MODEL, rows),
                  pl.BlockSpec((D_MODEL, D_IN), lambda i: (0, 0))],
        out_specs=[pl.BlockSpec((tm, w), lambda i: (i, 0)) for w in widths],
        out_shape=[jax.ShapeDtypeStruct((rows, w), F32) for w in widths],
        compiler_params=_params("parallel"),
        name="inproj",
    )(x, shift, scale, w_bf16)


def _gmlp_kernel(uv_ref, wsp_ref, bias_ref, g_ref, b_ref, a_ref, vn_ref, *, chunk):
    uv = uv_ref[...]
    u = jax.nn.gelu(uv[:, :W_A])
    v = _ln(jax.nn.gelu(uv[:, W_A:])) * g_ref[...] + b_ref[...]
    vn_ref[...] = v
    row = lax.broadcasted_iota(jnp.int32, (chunk, chunk), 0)
    col = lax.broadcasted_iota(jnp.int32, (chunk, chunk), 1)
    lane_head = lax.broadcasted_iota(jnp.int32, (chunk, W_A), 1) // HEAD_DIM
    mixed = bias_ref[...]
    for h in range(H_A):
        w = jnp.where(col <= row, wsp_ref[h], 0.0).astype(BF16)
        vh = jnp.where(lane_head == h, v, 0.0).astype(BF16)
        mixed = mixed + jnp.dot(w, vh, preferred_element_type=F32)
    a_ref[...] = u * mixed


def _gmlp(uv, w_sp, b_sp, ln_g, ln_b, chunk):
    rows = uv.shape[0]
    wsp = w_sp[:, :chunk, :chunk]
    bias = jnp.repeat(b_sp[:, :chunk].T, HEAD_DIM, axis=1)
    return pl.pallas_call(
        functools.partial(_gmlp_kernel, chunk=chunk),
        grid=(rows // chunk,),
        in_specs=[pl.BlockSpec((chunk, 2 * W_A), lambda i: (i, 0)),
                  pl.BlockSpec((H_A, chunk, chunk), lambda i: (0, 0, 0)),
                  pl.BlockSpec((chunk, W_A), lambda i: (0, 0)),
                  pl.BlockSpec((1, W_A), lambda i: (0, 0)),
                  pl.BlockSpec((1, W_A), lambda i: (0, 0))],
        out_specs=[pl.BlockSpec((chunk, W_A), lambda i: (i, 0)),
                   pl.BlockSpec((chunk, W_A), lambda i: (i, 0))],
        out_shape=[jax.ShapeDtypeStruct((rows, W_A), F32),
                   jax.ShapeDtypeStruct((rows, W_A), F32)],
        compiler_params=_params("parallel"),
        name="gmlp",
    )(uv, wsp, bias, ln_g.reshape(1, W_A), ln_b.reshape(1, W_A))


def _rope(x, cos, sin):
    lane = lax.broadcasted_iota(jnp.int32, (x.shape[0], LANES), 1)
    first_half = (lane & (HEAD_DIM // 2)) == 0
    parts = []
    for c in range(x.shape[1] // LANES):
        xc = x[:, c * LANES:(c + 1) * LANES]
        rot = jnp.where(first_half,
                        pltpu.roll(xc, LANES - HEAD_DIM // 2, 1),
                        pltpu.roll(xc, HEAD_DIM // 2, 1))
        parts.append(xc * cos + rot * sin)
    return jnp.concatenate(parts, axis=1)


def _ret_kernel(r_ref, cos_ref, sin_ref, qdec_ref, kdec_ref, dec_ref, blk_ref, s0_ref,
                gng_ref, gnb_ref, o_ref, sout_ref, s_scr, o_scr, *, n_chunks):
    n = pl.program_id(1)

    @pl.when(n == 0)
    def _():
        s_scr[...] = s0_ref[0]

    r = r_ref[...]
    cos = cos_ref[...]
    sin = sin_ref[...]
    qr = _rope(r[:, :W_B], cos, sin)
    kr = _rope(r[:, W_B:2 * W_B], cos, sin) * (HEAD_DIM ** -0.5)
    vb = r[:, 2 * W_B:3 * W_B].astype(BF16)
    gate = r[:, 3 * W_B:]
    qb = qr.astype(BF16)
    kb = kr.astype(BF16)
    qdb = (qr * qdec_ref[...]).astype(BF16)
    kdb = (kr * kdec_ref[...]).astype(BF16)
    for h in range(H_B):
        sl = slice(h * HEAD_DIM, (h + 1) * HEAD_DIM)
        scores = lax.dot_general(qb[:, sl], kb[:, sl], NT_DIMS, preferred_element_type=F32) * dec_ref[h]
        s_h = s_scr[h]
        o_h = (jnp.dot(scores.astype(BF16), vb[:, sl], preferred_element_type=F32)
               + jnp.dot(qdb[:, sl], s_h.astype(BF16), preferred_element_type=F32))
        s_scr[h] = s_h * blk_ref[h] + lax.dot_general(kdb[:, sl], vb[:, sl], TN_DIMS,
                                                      preferred_element_type=F32)
        o_scr[:, sl] = _ln(o_h)
    o_ref[...] = (o_scr[...] * gng_ref[...] + gnb_ref[...]) * _silu(gate)

    @pl.when(n == n_chunks - 1)
    def _():
        sout_ref[0] = s_scr[...]


def _retention(ret, s0, pos0, bsz, seq, gn_g, gn_b):
    L = min(seq, CHUNK)
    n_chunks = seq // L
    half = HEAD_DIM // 2
    inv = ROPE_BASE ** (-jnp.arange(half, dtype=F32) / half)
    ang = (pos0 + jnp.arange(seq)).astype(F32)[:, None] * inv[None, :]
    cos, sin = jnp.cos(ang), jnp.sin(ang)
    cos_t = jnp.tile(jnp.concatenate([cos, cos], axis=1), (1, LANES // HEAD_DIM))
    sin_t = jnp.tile(jnp.concatenate([-sin, sin], axis=1), (1, LANES // HEAD_DIM))
    log_g = jnp.log1p(-jnp.exp2(-5.0 - jnp.arange(H_B, dtype=F32)))
    idx = jnp.arange(L, dtype=F32)
    diff = idx[:, None] - idx[None, :]
    decay = jnp.where(diff >= 0, jnp.exp(diff[None] * log_g[:, None, None]), 0.0)
    q_decay = jnp.exp((idx[None, :] + 1.0) * log_g[:, None])
    k_decay = jnp.exp((L - 1.0 - idx[None, :]) * log_g[:, None])
    blk_decay = jnp.exp(L * log_g)
    qdec = jnp.repeat(q_decay.T, HEAD_DIM, axis=1)
    kdec = jnp.repeat(k_decay.T, HEAD_DIM, axis=1)
    blk = jnp.broadcast_to(blk_decay[:, None, None], (H_B, HEAD_DIM, HEAD_DIM))
    const2 = lambda b, n: (0, 0)
    const3 = lambda b, n: (0, 0, 0)
    return pl.pallas_call(
        functools.partial(_ret_kernel, n_chunks=n_chunks),
        grid=(bsz, n_chunks),
        in_specs=[pl.BlockSpec((L, 4 * W_B), lambda b, n: (b * n_chunks + n, 0)),
                  pl.BlockSpec((L, LANES), lambda b, n: (n, 0)),
                  pl.BlockSpec((L, LANES), lambda b, n: (n, 0)),
                  pl.BlockSpec((L, W_B), const2),
                  pl.BlockSpec((L, W_B), const2),
                  pl.BlockSpec((H_B, L, L), const3),
                  pl.BlockSpec((H_B, HEAD_DIM, HEAD_DIM), const3),
                  pl.BlockSpec((1, H_B, HEAD_DIM, HEAD_DIM), lambda b, n: (b, 0, 0, 0)),
                  pl.BlockSpec((1, W_B), const2),
                  pl.BlockSpec((1, W_B), const2)],
        out_specs=[pl.BlockSpec((L, W_B), lambda b, n: (b * n_chunks + n, 0)),
                   pl.BlockSpec((1, H_B, HEAD_DIM, HEAD_DIM), lambda b, n: (b, 0, 0, 0))],
        out_shape=[jax.ShapeDtypeStruct((bsz * seq, W_B), F32),
                   jax.ShapeDtypeStruct((bsz, H_B, HEAD_DIM, HEAD_DIM), F32)],
        scratch_shapes=[pltpu.VMEM((H_B, HEAD_DIM, HEAD_DIM), F32),
                        pltpu.VMEM((L, W_B), F32)],
        compiler_params=_params("parallel", "arbitrary"),
        name="retention",
    )(ret, cos_t, sin_t, qdec, kdec, decay, blk, s0, gn_g.reshape(1, W_B), gn_b.reshape(1, W_B))


def _sb_last_block(i, tq, tk, q_pos0):
    return (q_pos0 + (i + 1) * tq - 2) // tk


def _sb_block(qm_ref, k_ref, v_ref, u2_ref, acc_ref, carry_ref, causal):
    tk = k_ref.shape[0]
    half = lax.broadcasted_iota(jnp.int32, (tk, LANES), 1) // HEAD_DIM
    u2 = u2_ref[...]
    kp = [k_ref[:, p * LANES:(p + 1) * LANES].astype(BF16) for p in range(H_C // 2)]

    def scores(h):
        z = lax.dot_general(qm_ref[h], kp[h // 2], NT_DIMS, preferred_element_type=F32)
        neg_abs = pltpu.bitcast(pltpu.bitcast(z, jnp.uint32) | jnp.uint32(0x80000000), F32)
        ls_pos = jnp.minimum(z, 0.0) - jnp.log(1.0 + jnp.exp2(neg_abs)) * INV_LN2
        log_stay = ls_pos - z
        if causal is not None:
            log_stay = jnp.where(causal, log_stay, 0.0)
        hi = log_stay.astype(BF16)
        lo = (log_stay - hi.astype(F32)).astype(BF16)
        return ls_pos, log_stay[:, :1], jnp.concatenate([hi, lo], axis=1)

    def cumsum(hi_lo):
        return jnp.dot(hi_lo, u2, preferred_element_type=F32)

    def weigh(h, ls_pos, first_col, excl):
        carry = carry_ref[h]
        att = jnp.exp2(ls_pos + excl + carry)
        if causal is not None:
            att = jnp.where(causal, att, 0.0)
        vp = jnp.where(half == h % 2, v_ref[:, (h // 2) * LANES:(h // 2 + 1) * LANES], 0.0).astype(BF16)
        carry_ref[h] = carry + (excl[:, :1] + first_col)
        return jnp.dot(att.astype(BF16), vp, preferred_element_type=F32)

    stage_a, stage_b, outs = {}, {}, {}
    for step in range(H_C + 2):
        if step - 2 >= 0:
            h = step - 2
            outs[h] = weigh(h, stage_a[h][0], stage_a[h][1], stage_b.pop(h))
            del stage_a[h]
            if h % 2 == 1:
                p = h // 2
                acc_ref[:, p * LANES:(p + 1) * LANES] += outs.pop(h - 1) + outs.pop(h)
        if 0 <= step - 1 < H_C:
            stage_b[step - 1] = cumsum(stage_a[step - 1][2])
        if step < H_C:
            stage_a[step] = scores(step)


def _sb_kernel(q_ref, k_ref, v_ref, u2_ref, o_ref, qm_ref, acc_ref, carry_ref, *, tq, tk, q_pos0):
    i = pl.program_id(1)
    j = pl.program_id(2)
    last = _sb_last_block(i, tq, tk, q_pos0)
    k0 = (last - j) * tk
    q0 = q_pos0 + i * tq
    on_diagonal = k0 + tk > q0

    @pl.when(j == 0)
    def _():
        acc_ref[...] = jnp.zeros_like(acc_ref)
        carry_ref[...] = jnp.zeros_like(carry_ref)
        half = lax.broadcasted_iota(jnp.int32, (tq, LANES), 1) // HEAD_DIM
        for h in range(H_C):
            qp = q_ref[:, (h // 2) * LANES:(h // 2 + 1) * LANES] * SB_QSCALE
            qm_ref[h] = jnp.where(half == h % 2, qp, 0.0).astype(BF16)

    @pl.when(jnp.logical_and(j <= last, on_diagonal))
    def _():
        kpos = k0 + lax.broadcasted_iota(jnp.int32, (tq, tk), 1)
        qpos = q0 + lax.broadcasted_iota(jnp.int32, (tq, tk), 0)
        _sb_block(qm_ref, k_ref, v_ref, u2_ref, acc_ref, carry_ref, kpos < qpos)

    @pl.when(jnp.logical_and(j <= last, jnp.logical_not(on_diagonal)))
    def _():
        _sb_block(qm_ref, k_ref, v_ref, u2_ref, acc_ref, carry_ref, None)

    @pl.when(j == last)
    def _():
        o_ref[...] = acc_ref[...]


def _stick_breaking(q, k, v, bsz, seq, kv_len, q_pos0, tq, tk):
    nq = seq // tq
    nk = (q_pos0 + seq - 2) // tk + 1
    nkb = kv_len // tk
    assert nk <= nkb
    tri = np.tril(np.ones((tk, tk), np.float32), -1)
    u2 = jnp.asarray(np.concatenate([tri, tri], axis=0), dtype=BF16)

    def kv_map(b, i, j):
        return (b * nkb + jnp.maximum(_sb_last_block(i, tq, tk, q_pos0) - j, 0), 0)

    return pl.pallas_call(
        functools.partial(_sb_kernel, tq=tq, tk=tk, q_pos0=q_pos0),
        grid=(bsz, nq, nk),
        in_specs=[pl.BlockSpec((tq, W_C), lambda b, i, j: (b * nq + i, 0)),
                  pl.BlockSpec((tk, W_C), kv_map),
                  pl.BlockSpec((tk, W_C), kv_map),
                  pl.BlockSpec((2 * tk, tk), lambda b, i, j: (0, 0))],
        out_specs=pl.BlockSpec((tq, W_C), lambda b, i, j: (b * nq + i, 0)),
        out_shape=jax.ShapeDtypeStruct((bsz * seq, W_C), F32),
        scratch_shapes=[pltpu.VMEM((H_C, tq, LANES), BF16),
                        pltpu.VMEM((tq, W_C), F32),
                        pltpu.VMEM((H_C, tq, 1), F32)],
        compiler_params=_params("parallel", "parallel", "arbitrary"),
        name="stick_breaking",
    )(q, k, v, u2)


def _route(sel, s, tm):
    g_scores = []
    for g in range(N_GROUPS):
        a, b, c, d = sel[EXPERTS_PER_GROUP * g:EXPERTS_PER_GROUP * (g + 1)]
        ab_hi, ab_lo = jnp.maximum(a, b), jnp.minimum(a, b)
        cd_hi, cd_lo = jnp.maximum(c, d), jnp.minimum(c, d)
        top1 = jnp.maximum(ab_hi, cd_hi)
        top2 = jnp.maximum(jnp.minimum(ab_hi, cd_hi), jnp.maximum(ab_lo, cd_lo))
        g_scores.append(top1 + top2)
    best = g_scores[0]
    gi = jnp.zeros((tm, 1), jnp.int32)
    for g in range(1, N_GROUPS):
        upd = g_scores[g] > best
        gi = jnp.where(upd, g, gi)
        best = jnp.where(upd, g_scores[g], best)

    def pick_group(cols, l):
        out = cols[(N_GROUPS - 1) * EXPERTS_PER_GROUP + l]
        for g in range(N_GROUPS - 2, -1, -1):
            out = jnp.where(gi == g, cols[g * EXPERTS_PER_GROUP + l], out)
        return out

    ig = [pick_group(sel, l) for l in range(EXPERTS_PER_GROUP)]
    sg = [pick_group(s, l) for l in range(EXPERTS_PER_GROUP)]
    b1 = ig[0]
    i1 = jnp.zeros((tm, 1), jnp.int32)
    for l in range(1, EXPERTS_PER_GROUP):
        upd = ig[l] > b1
        i1 = jnp.where(upd, l, i1)
        b1 = jnp.where(upd, ig[l], b1)
    b2 = jnp.full((tm, 1), -jnp.inf, F32)
    i2 = jnp.zeros((tm, 1), jnp.int32)
    for l in range(EXPERTS_PER_GROUP):
        upd = jnp.logical_and(i1 != l, ig[l] > b2)
        i2 = jnp.where(upd, l, i2)
        b2 = jnp.where(upd, ig[l], b2)

    def pick_local(idx):
        out = sg[EXPERTS_PER_GROUP - 1]
        for l in range(EXPERTS_PER_GROUP - 2, -1, -1):
            out = jnp.where(idx == l, sg[l], out)
        return out

    w1 = pick_local(i1)
    w2 = pick_local(i2)
    tot = w1 + w2
    return gi * EXPERTS_PER_GROUP + i1, gi * EXPERTS_PER_GROUP + i2, w1 / tot, w2 / tot


def _post_kernel(a_ref, b_ref, c_ref, x_ref, wo_ref, gate_ref, g1_ref, b1_ref, sh2_ref, sc2_ref,
                 wr_ref, br_ref, x1_ref, h2_ref, comb_ref, *, tm):
    proj = (jnp.dot(a_ref[...].astype(BF16), wo_ref[:W_A], preferred_element_type=F32)
            + jnp.dot(b_ref[...].astype(BF16), wo_ref[W_A:W_A + W_B], preferred_element_type=F32)
            + jnp.dot(c_ref[...].astype(BF16), wo_ref[W_A + W_B:], preferred_element_type=F32))
    x1 = _ln(ALPHA * x_ref[...] + gate_ref[...] * proj) * g1_ref[...] + b1_ref[...]
    x1_ref[...] = x1
    h2 = _ln(x1) * (1.0 + sc2_ref[...]) + sh2_ref[...]
    h2_ref[...] = h2.astype(BF16)
    logits = jnp.dot(h2, wr_ref[...], preferred_element_type=F32, precision=HIGHEST)
    s_all = jax.nn.sigmoid(logits)
    sel_all = s_all + br_ref[...]
    s = [s_all[:, e:e + 1] for e in range(N_EXPERTS)]
    sel = [sel_all[:, e:e + 1] for e in range(N_EXPERTS)]
    e1, e2, w1, w2 = _route(sel, s, tm)
    lane = lax.broadcasted_iota(jnp.int32, (tm, LANES), 1)
    comb_ref[...] = jnp.where(lane == e1, w1, jnp.where(lane == e2, w2, 0.0))


def _post(a, b, c, x, w_out_bf16, gate1, ln_g, ln_b, shift2, scale2, wr_pad, br_pad, tm):
    rows = x.shape[0]
    row = lambda w: pl.BlockSpec((tm, w), lambda i: (i, 0))
    const = lambda r, w: pl.BlockSpec((r, w), lambda i: (0, 0))
    return pl.pallas_call(
        functools.partial(_post_kernel, tm=tm),
        grid=(rows // tm,),
        in_specs=[row(W_A), row(W_B), row(W_C), row(D_MODEL),
                  const(D_MODEL, D_MODEL),
                  _row_spec(gate1.shape[0], tm, D_MODEL, rows),
                  const(1, D_MODEL), const(1, D_MODEL),
                  _row_spec(shift2.shape[0], tm, D_MODEL, rows),
                  _row_spec(scale2.shape[0], tm, D_MODEL, rows),
                  const(D_MODEL, LANES), const(1, LANES)],
        out_specs=[row(D_MODEL), row(D_MODEL), row(LANES)],
        out_shape=[jax.ShapeDtypeStruct((rows, D_MODEL), F32),
                   jax.ShapeDtypeStruct((rows, D_MODEL), BF16),
                   jax.ShapeDtypeStruct((rows, LANES), F32)],
        compiler_params=_params("parallel"),
        name="post_mix",
    )(a, b, c, x, w_out_bf16, gate1, ln_g.reshape(1, D_MODEL), ln_b.reshape(1, D_MODEL),
      shift2, scale2, wr_pad, br_pad)


def _moe_kernel(h_ref, comb_ref, x_ref, gate_ref, g2_ref, b2_ref, w1_ref, w3_ref, w2_ref,
                o_ref, acc_ref, *, tm):
    e = pl.program_id(1)

    @pl.when(e == 0)
    def _():
        acc_ref[...] = jnp.zeros_like(acc_ref)

    h = h_ref[...]
    a = jnp.dot(h, w1_ref[0], preferred_element_type=F32)
    g = jnp.dot(h, w3_ref[0], preferred_element_type=F32)
    lane = lax.broadcasted_iota(jnp.int32, (tm, LANES), 1)
    ce = jnp.sum(jnp.where(lane == e, comb_ref[...], 0.0), axis=-1, keepdims=True)
    act = _silu(a) * g * ce
    acc_ref[...] += jnp.dot(act.astype(BF16), w2_ref[0], preferred_element_type=F32)

    @pl.when(e == N_EXPERTS - 1)
    def _():
        y = ALPHA * x_ref[...] + gate_ref[...] * acc_ref[...]
        o_ref[...] = _ln(y) * g2_ref[...] + b2_ref[...]


def _moe(h2, comb, x1, gate2, ln_g, ln_b, w1, w3, w2, tm):
    rows = x1.shape[0]
    row = lambda w: pl.BlockSpec((tm, w), lambda i, e: (i, 0))
    const = pl.BlockSpec((1, D_MODEL), lambda i, e: (0, 0))
    gate_spec = (pl.BlockSpec((1, D_MODEL), lambda i, e: (0, 0)) if gate2.shape[0] == 1
                 else row(D_MODEL))
    return pl.pallas_call(
        functools.partial(_moe_kernel, tm=tm),
        grid=(rows // tm, N_EXPERTS),
        in_specs=[row(D_MODEL), row(LANES), row(D_MODEL), gate_spec, const, const,
                  pl.BlockSpec((1, D_MODEL, D_EXPERT), lambda i, e: (e, 0, 0)),
                  pl.BlockSpec((1, D_MODEL, D_EXPERT), lambda i, e: (e, 0, 0)),
                  pl.BlockSpec((1, D_EXPERT, D_MODEL), lambda i, e: (e, 0, 0))],
        out_specs=row(D_MODEL),
        out_shape=jax.ShapeDtypeStruct((rows, D_MODEL), F32),
        scratch_shapes=[pltpu.VMEM((tm, D_MODEL), F32)],
        compiler_params=_params("parallel", "arbitrary"),
        name="experts",
    )(h2, comb, x1, gate2, ln_g.reshape(1, D_MODEL), ln_b.reshape(1, D_MODEL), w1, w3, w2)


def _trunk_layer(x, mods, bsz, seq, pos0, s0, k_past, v_past, p, tiles):
    shift1, scale1, gate1, shift2, scale2, gate2 = mods
    uv, ret, q_c, k_c, v_c = _inproj(x, shift1, scale1, p["w_in"], tiles["tm_in"])
    a_out, v_rows = _gmlp(uv, p["w_sp"], p["b_sp"], p["ln_v_g"], p["ln_v_b"], min(seq, GMLP_CHUNK))
    b_out, s_new = _retention(ret, s0, pos0, bsz, seq, p["gn_g"], p["gn_b"])
    tq, tk = tiles["sb"]
    if k_past is None:
        k_all, v_all, kv_len = k_c, v_c, seq
    else:
        past = k_past.shape[1]
        kv_len = -(-(past + seq) // tk) * tk
        pad = jnp.zeros((bsz, kv_len - past - seq, W_C), F32)
        k_all = jnp.concatenate([k_past, k_c.reshape(bsz, seq, W_C), pad], axis=1).reshape(bsz * kv_len, W_C)
        v_all = jnp.concatenate([v_past, v_c.reshape(bsz, seq, W_C), pad], axis=1).reshape(bsz * kv_len, W_C)
    c_out = _stick_breaking(q_c, k_all, v_all, bsz, seq, kv_len, pos0, tq, tk)
    x1, h2, comb = _post(a_out, b_out, c_out, x, p["w_out"], gate1, p["ln1_g"], p["ln1_b"],
                         shift2, scale2, p["wr"], p["br"], tiles["tm_post"])
    y = _moe(h2, comb, x1, gate2, p["ln2_g"], p["ln2_b"], p["w1"], p["w3"], p["w2"], tiles["tm_moe"])
    return y, v_rows, s_new, k_c, v_c


def kernel(x_prompt, x_sample, cache_sb_k, cache_sb_v, state_ret, c_prompt, c_sample, w_ada, b_ada, w_in, w_out, ln_v_g, ln_v_b, w_spatial, b_spatial, gn_g, gn_b, ln1_g, ln1_b, ln2_g, ln2_b, w_router, b_router, w1, w3, w2):
    bp, tp, _ = x_prompt.shape
    bs, ts, _ = x_sample.shape
    past_len = cache_sb_k.shape[2]
    assert bp == 1

    n_c = bp + bs
    c_rows = -(-n_c // 8) * 8
    c_all = jnp.concatenate([c_prompt, c_sample, jnp.zeros((c_rows - n_c, D_MODEL), F32)], axis=0)
    mods = _adaln(c_all, w_ada, b_ada)

    wr_pad = jnp.pad(w_router, ((0, 0), (0, LANES - N_EXPERTS)))
    br_pad = jnp.pad(b_router, (0, LANES - N_EXPERTS)).reshape(1, LANES)

    tiles_p = dict(tm_in=256, sb=(512, 256), tm_post=256, tm_moe=512)
    tiles_s = dict(tm_in=bs * ts, sb=(ts, 256), tm_post=bs * ts, tm_moe=bs * ts)

    y_p = x_prompt.reshape(bp * tp, D_MODEL)
    y_s = x_sample.reshape(bs * ts, D_MODEL)
    zero_state = jnp.zeros((bp, H_B, HEAD_DIM, HEAD_DIM), F32)
    outs = [[] for _ in range(7)]
    for l in range(DEPTH):
        p = dict(w_in=w_in[l].astype(BF16), w_out=w_out[l].astype(BF16),
                 w_sp=w_spatial[l], b_sp=b_spatial[l], ln_v_g=ln_v_g[l], ln_v_b=ln_v_b[l],
                 gn_g=gn_g[l], gn_b=gn_b[l], ln1_g=ln1_g[l], ln1_b=ln1_b[l],
                 ln2_g=ln2_g[l], ln2_b=ln2_b[l], wr=wr_pad, br=br_pad,
                 w1=w1[l].astype(BF16), w3=w3[l].astype(BF16), w2=w2[l].astype(BF16))
        m = mods[l]
        mods_p = [m[0:1, i * D_MODEL:(i + 1) * D_MODEL] for i in range(6)]
        mods_s = [jnp.repeat(m[bp:bp + bs, i * D_MODEL:(i + 1) * D_MODEL], ts, axis=0) for i in range(6)]
        y_p, _, s_p, k_p, v_p = _trunk_layer(y_p, mods_p, bp, tp, 0, zero_state, None, None, p, tiles_p)
        y_s, g_s, s_s, k_s, v_s = _trunk_layer(
            y_s, mods_s, bs, ts, past_len, state_ret[l],
            cache_sb_k[l].reshape(bs, past_len, W_C), cache_sb_v[l].reshape(bs, past_len, W_C), p, tiles_s)
        outs[0].append(s_p)
        outs[1].append(k_p.reshape(bp, tp, H_C, HEAD_DIM))
        outs[2].append(v_p.reshape(bp, tp, H_C, HEAD_DIM))
        outs[3].append(s_s)
        outs[4].append(k_s.reshape(bs, ts, H_C, HEAD_DIM))
        outs[5].append(v_s.reshape(bs, ts, H_C, HEAD_DIM))
        outs[6].append(g_s.reshape(bs, ts, W_A))
    return (y_p.reshape(bp, tp, D_MODEL), y_s.reshape(bs, ts, D_MODEL)) + tuple(jnp.stack(o) for o in outs)
```

```python
import functools

import numpy as np
import jax
import jax.numpy as jnp
from jax import lax
from jax.experimental import pallas as pl
from jax.experimental.pallas import tpu as pltpu

F32 = jnp.float32
BF16 = jnp.bfloat16
HIGHEST = lax.Precision.HIGHEST

D_MODEL = 1024
DEPTH = 2
HEAD_DIM = 64
W_A = D_MODEL // 4
W_B = 3 * D_MODEL // 8
W_C = D_MODEL - W_A - W_B
H_A = W_A // HEAD_DIM
H_B = W_B // HEAD_DIM
H_C = W_C // HEAD_DIM
GMLP_CHUNK = 128
ROPE_BASE = 10000.0
N_EXPERTS = 16
N_GROUPS = 4
EXPERTS_PER_GROUP = N_EXPERTS // N_GROUPS
D_EXPERT = D_MODEL // 2
ALPHA = (2 * DEPTH) ** 0.25
LN_EPS = 1e-5
D_IN = 2 * W_A + 4 * W_B + 3 * W_C
LANES = 128
VMEM_LIMIT = 48 * 1024 * 1024

NT_DIMS = (((1,), (1,)), ((), ()))
TN_DIMS = (((0,), (0,)), ((), ()))
INV_LN2 = 1.4426950408889634
SB_QSCALE = HEAD_DIM ** -0.5 * INV_LN2
SB_DEAD = -152.0
GMLP_SUB = 4
RET_BLOCK = 256


def _ln(x):
    mu = jnp.mean(x, axis=-1, keepdims=True)
    xc = x - mu
    var = jnp.mean(xc * xc, axis=-1, keepdims=True)
    return xc * lax.rsqrt(var + LN_EPS)


def _silu(x):
    return x * jax.nn.sigmoid(x)


def _params(*sem):
    return pltpu.CompilerParams(dimension_semantics=sem, vmem_limit_bytes=VMEM_LIMIT)


def _row_spec(rows, tm, width, total_rows):
    if rows == 1:
        return pl.BlockSpec((1, width), lambda i: (0, 0))
    assert rows == total_rows
    return pl.BlockSpec((tm, width), lambda i: (i, 0))


def _adaln_kernel(c_ref, w_ref, b_ref, o_ref):
    sc = _silu(c_ref[...])
    o_ref[0] = jnp.dot(sc, w_ref[0], preferred_element_type=F32, precision=HIGHEST) + b_ref[0]


def _adaln(c_all, w_ada, b_ada):
    rows = c_all.shape[0]
    tn = 1536
    return pl.pallas_call(
        _adaln_kernel,
        grid=(DEPTH, 6 * D_MODEL // tn),
        in_specs=[pl.BlockSpec((rows, D_MODEL), lambda l, j: (0, 0)),
                  pl.BlockSpec((1, D_MODEL, tn), lambda l, j: (l, 0, j)),
                  pl.BlockSpec((1, 1, tn), lambda l, j: (l, 0, j))],
        out_specs=pl.BlockSpec((1, rows, tn), lambda l, j: (l, 0, j)),
        out_shape=jax.ShapeDtypeStruct((DEPTH, rows, 6 * D_MODEL), F32),
        compiler_params=_params("parallel", "parallel"),
        name="adaln",
    )(c_all, w_ada, b_ada.reshape(DEPTH, 1, 6 * D_MODEL))


def _inproj_kernel(x_ref, shift_ref, scale_ref, w_ref, uv_ref, ret_ref, q_ref, k_ref, v_ref, kb_ref, vb_ref, wb_ref):
    @pl.when(pl.program_id(0) == 0)
    def _():
        wb_ref[...] = w_ref[0].astype(BF16)

    h = _ln(x_ref[...]) * (1.0 + scale_ref[...]) + shift_ref[...]
    r = jnp.dot(h.astype(BF16), wb_ref[...], preferred_element_type=F32)
    c0 = 2 * W_A
    c1 = c0 + 4 * W_B
    uv_ref[...] = r[:, :c0]
    ret_ref[...] = r[:, c0:c1]
    q_ref[...] = r[:, c1:c1 + W_C]
    k = r[:, c1 + W_C:c1 + 2 * W_C]
    v = r[:, c1 + 2 * W_C:]
    k_ref[...] = k
    v_ref[...] = v
    kb_ref[...] = k.astype(BF16)
    vb_ref[...] = v.astype(BF16)


def _inproj(x, shift, scale, w_in, layer, tm):
    rows = x.shape[0]
    widths = (2 * W_A, 4 * W_B, W_C, W_C, W_C, W_C, W_C)
    dtypes = (F32,) * 5 + (BF16,) * 2
    return pl.pallas_call(
        _inproj_kernel,
        grid=(rows // tm,),
        in_specs=[pl.BlockSpec((tm, D_MODEL), lambda i: (i, 0)),
                  _row_spec(shift.shape[0], tm, D_MODEL, rows),
                  _row_spec(scale.shape[0], tm, D_MODEL, rows),
                  pl.BlockSpec((1, D_MODEL, D_IN), lambda i: (layer, 0, 0), pipeline_mode=pl.Buffered(1))],
        out_specs=[pl.BlockSpec((tm, w), lambda i: (i, 0)) for w in widths],
        out_shape=[jax.ShapeDtypeStruct((rows, w), dt) for w, dt in zip(widths, dtypes)],
        scratch_shapes=[pltpu.VMEM((D_MODEL, D_IN), BF16)],
        compiler_params=_params("arbitrary"),
        name="inproj",
    )(x, shift, scale, w_in)


def _gmlp_kernel(uv_ref, wsp_ref, bias_ref, g_ref, b_ref, a_ref, vn_ref, *, chunk, n_sub):
    uv = uv_ref[...]
    u = jax.nn.gelu(uv[:, :W_A])
    v = _ln(jax.nn.gelu(uv[:, W_A:])) * g_ref[...] + b_ref[...]
    vn_ref[...] = v
    row = lax.broadcasted_iota(jnp.int32, (chunk, chunk), 0)
    col = lax.broadcasted_iota(jnp.int32, (chunk, chunk), 1)
    lane_head = lax.broadcasted_iota(jnp.int32, (chunk, W_A), 1) // HEAD_DIM
    w = [jnp.where(col <= row, wsp_ref[h], 0.0).astype(BF16) for h in range(H_A)]
    for c in range(n_sub):
        rows = slice(c * chunk, (c + 1) * chunk)
        vc = v[rows]
        mixed = bias_ref[...]
        for h in range(H_A):
            vh = jnp.where(lane_head == h, vc, 0.0).astype(BF16)
            mixed = mixed + jnp.dot(w[h], vh, preferred_element_type=F32)
        a_ref[rows, :] = u[rows] * mixed


def _gmlp(uv, w_sp, b_sp, ln_g, ln_b, chunk):
    rows = uv.shape[0]
    n_sub = min(GMLP_SUB, rows // chunk)
    tm = n_sub * chunk
    wsp = w_sp[:, :chunk, :chunk]
    bias = jnp.repeat(b_sp[:, :chunk].T, HEAD_DIM, axis=1)
    return pl.pallas_call(
        functools.partial(_gmlp_kernel, chunk=chunk, n_sub=n_sub),
        grid=(rows // tm,),
        in_specs=[pl.BlockSpec((tm, 2 * W_A), lambda i: (i, 0)),
                  pl.BlockSpec((H_A, chunk, chunk), lambda i: (0, 0, 0)),
                  pl.BlockSpec((chunk, W_A), lambda i: (0, 0)),
                  pl.BlockSpec((1, W_A), lambda i: (0, 0)),
                  pl.BlockSpec((1, W_A), lambda i: (0, 0))],
        out_specs=[pl.BlockSpec((tm, W_A), lambda i: (i, 0)),
                   pl.BlockSpec((tm, W_A), lambda i: (i, 0))],
        out_shape=[jax.ShapeDtypeStruct((rows, W_A), F32),
                   jax.ShapeDtypeStruct((rows, W_A), F32)],
        compiler_params=_params("parallel"),
        name="gmlp",
    )(uv, wsp, bias, ln_g.reshape(1, W_A), ln_b.reshape(1, W_A))


def _rope(x, cos, sin):
    lane = lax.broadcasted_iota(jnp.int32, (x.shape[0], LANES), 1)
    first_half = (lane & (HEAD_DIM // 2)) == 0
    parts = []
    for c in range(x.shape[1] // LANES):
        xc = x[:, c * LANES:(c + 1) * LANES]
        rot = jnp.where(first_half,
                        pltpu.roll(xc, LANES - HEAD_DIM // 2, 1),
                        pltpu.roll(xc, HEAD_DIM // 2, 1))
        parts.append(xc * cos + rot * sin)
    return jnp.concatenate(parts, axis=1)


def _ret_kernel(r_ref, cos_ref, sin_ref, qdec_ref, kdec_ref, dec_ref, blk_ref, s0_ref,
                gng_ref, gnb_ref, o_ref, sout_ref, s_scr, o_scr, *, n_blocks):
    n = pl.program_id(1)

    @pl.when(n == 0)
    def _():
        s_scr[...] = s0_ref[0]

    r = r_ref[...]
    cos = cos_ref[...]
    sin = sin_ref[...]
    qr = _rope(r[:, :W_B], cos, sin)
    kr = _rope(r[:, W_B:2 * W_B], cos, sin) * (HEAD_DIM ** -0.5)
    vb = r[:, 2 * W_B:3 * W_B].astype(BF16)
    gate = r[:, 3 * W_B:]
    qb = qr.astype(BF16)
    kb = kr.astype(BF16)
    qdb = (qr * qdec_ref[...]).astype(BF16)
    kdb = (kr * kdec_ref[...]).astype(BF16)
    for h in range(H_B):
        sl = slice(h * HEAD_DIM, (h + 1) * HEAD_DIM)
        scores = lax.dot_general(qb[:, sl], kb[:, sl], NT_DIMS, preferred_element_type=F32) * dec_ref[h]
        s_h = s_scr[h]
        o_h = (jnp.dot(scores.astype(BF16), vb[:, sl], preferred_element_type=F32)
               + jnp.dot(qdb[:, sl], s_h.astype(BF16), preferred_element_type=F32))
        s_scr[h] = s_h * blk_ref[h] + lax.dot_general(kdb[:, sl], vb[:, sl], TN_DIMS,
                                                      preferred_element_type=F32)
        o_scr[:, sl] = _ln(o_h)
    o_ref[...] = (o_scr[...] * gng_ref[...] + gnb_ref[...]) * _silu(gate)

    @pl.when(n == n_blocks - 1)
    def _():
        sout_ref[0] = s_scr[...]


def _retention(ret, s0, pos0, bsz, seq, gn_g, gn_b):
    L = min(seq, RET_BLOCK)
    n_blocks = seq // L
    half = HEAD_DIM // 2
    inv = ROPE_BASE ** (-jnp.arange(half, dtype=F32) / half)
    ang = (pos0 + jnp.arange(seq)).astype(F32)[:, None] * inv[None, :]
    cos, sin = jnp.cos(ang), jnp.sin(ang)
    cos_t = jnp.tile(jnp.concatenate([cos, cos], axis=1), (1, LANES // HEAD_DIM))
    sin_t = jnp.tile(jnp.concatenate([-sin, sin], axis=1), (1, LANES // HEAD_DIM))
    log_g = jnp.log1p(-jnp.exp2(-5.0 - jnp.arange(H_B, dtype=F32)))
    idx = jnp.arange(L, dtype=F32)
    diff = idx[:, None] - idx[None, :]
    decay = jnp.where(diff >= 0, jnp.exp(diff[None] * log_g[:, None, None]), 0.0)
    q_decay = jnp.exp((idx[None, :] + 1.0) * log_g[:, None])
    k_decay = jnp.exp((L - 1.0 - idx[None, :]) * log_g[:, None])
    blk_decay = jnp.exp(L * log_g)
    qdec = jnp.repeat(q_decay.T, HEAD_DIM, axis=1)
    kdec = jnp.repeat(k_decay.T, HEAD_DIM, axis=1)
    blk = jnp.broadcast_to(blk_decay[:, None, None], (H_B, HEAD_DIM, HEAD_DIM))
    const2 = lambda b, n: (0, 0)
    const3 = lambda b, n: (0, 0, 0)
    return pl.pallas_call(
        functools.partial(_ret_kernel, n_blocks=n_blocks),
        grid=(bsz, n_blocks),
        in_specs=[pl.BlockSpec((L, 4 * W_B), lambda b, n: (b * n_blocks + n, 0)),
                  pl.BlockSpec((L, LANES), lambda b, n: (n, 0)),
                  pl.BlockSpec((L, LANES), lambda b, n: (n, 0)),
                  pl.BlockSpec((L, W_B), const2),
                  pl.BlockSpec((L, W_B), const2),
                  pl.BlockSpec((H_B, L, L), const3),
                  pl.BlockSpec((H_B, HEAD_DIM, HEAD_DIM), const3),
                  pl.BlockSpec((1, H_B, HEAD_DIM, HEAD_DIM), lambda b, n: (b, 0, 0, 0)),
                  pl.BlockSpec((1, W_B), const2),
                  pl.BlockSpec((1, W_B), const2)],
        out_specs=[pl.BlockSpec((L, W_B), lambda b, n: (b * n_blocks + n, 0)),
                   pl.BlockSpec((1, H_B, HEAD_DIM, HEAD_DIM), lambda b, n: (b, 0, 0, 0))],
        out_shape=[jax.ShapeDtypeStruct((bsz * seq, W_B), F32),
                   jax.ShapeDtypeStruct((bsz, H_B, HEAD_DIM, HEAD_DIM), F32)],
        scratch_shapes=[pltpu.VMEM((H_B, HEAD_DIM, HEAD_DIM), F32),
                        pltpu.VMEM((L, W_B), F32)],
        compiler_params=_params("parallel", "arbitrary"),
        name="retention",
    )(ret, cos_t, sin_t, qdec, kdec, decay, blk, s0, gn_g.reshape(1, W_B), gn_b.reshape(1, W_B))


def _sb_last_block(i, tq, tk, q_pos0):
    return (q_pos0 + (i + 1) * tq - 2) // tk


def _sb_block(qm_ref, k_ref, v_ref, u2_ref, acc_ref, carry_ref, causal):
    tk = k_ref.shape[0]
    half = lax.broadcasted_iota(jnp.int32, (tk, LANES), 1) // HEAD_DIM
    u2 = u2_ref[...]
    kp = [k_ref[:, p * LANES:(p + 1) * LANES] for p in range(H_C // 2)]

    def scores(h):
        z = lax.dot_general(qm_ref[h], kp[h // 2], NT_DIMS, preferred_element_type=F32)
        neg_abs = pltpu.bitcast(pltpu.bitcast(z, jnp.uint32) | jnp.uint32(0x80000000), F32)
        ls_pos = jnp.minimum(z, 0.0) - jnp.log(1.0 + jnp.exp2(neg_abs)) * INV_LN2
        log_stay = ls_pos - z
        if causal is not None:
            log_stay = jnp.where(causal, log_stay, 0.0)
        hi = log_stay.astype(BF16)
        lo = (log_stay - hi.astype(F32)).astype(BF16)
        return ls_pos, log_stay[:, :1], jnp.concatenate([hi, lo], axis=1)

    def cumsum(hi_lo):
        return jnp.dot(hi_lo, u2, preferred_element_type=F32)

    def weigh(h, ls_pos, first_col, excl):
        carry = carry_ref[h]
        att = jnp.exp2(ls_pos + excl + carry)
        if causal is not None:
            att = jnp.where(causal, att, 0.0)
        vp = v_ref[:, (h // 2) * LANES:(h // 2 + 1) * LANES]
        vp = jnp.where(half == h % 2, vp, jnp.zeros_like(vp))
        carry_ref[h] = carry + (excl[:, :1] + first_col)
        return jnp.dot(att.astype(BF16), vp, preferred_element_type=F32)

    stage_a, stage_b, outs = {}, {}, {}
    for step in range(H_C + 2):
        if step - 2 >= 0:
            h = step - 2
            outs[h] = weigh(h, stage_a[h][0], stage_a[h][1], stage_b.pop(h))
            del stage_a[h]
            if h % 2 == 1:
                p = h // 2
                acc_ref[:, p * LANES:(p + 1) * LANES] += outs.pop(h - 1) + outs.pop(h)
        if 0 <= step - 1 < H_C:
            stage_b[step - 1] = cumsum(stage_a[step - 1][2])
        if step < H_C:
            stage_a[step] = scores(step)


def _sb_kernel(q_ref, k_ref, v_ref, u2_ref, o_ref, qm_ref, acc_ref, carry_ref, *, tq, tk, q_pos0):
    i = pl.program_id(1)
    last = _sb_last_block(i, tq, tk, q_pos0)
    q0 = q_pos0 + i * tq

    acc_ref[...] = jnp.zeros_like(acc_ref)
    carry_ref[...] = jnp.zeros_like(carry_ref)
    half = lax.broadcasted_iota(jnp.int32, (tq, LANES), 1) // HEAD_DIM
    for h in range(H_C):
        qp = q_ref[:, (h // 2) * LANES:(h // 2 + 1) * LANES] * SB_QSCALE
        qm_ref[h] = jnp.where(half == h % 2, qp, 0.0).astype(BF16)

    def key_block(state):
        j, _ = state
        k0 = pl.multiple_of((last - j) * tk, tk)
        k_blk = k_ref.at[pl.ds(k0, tk), :]
        v_blk = v_ref.at[pl.ds(k0, tk), :]
        on_diagonal = k0 + tk > q0

        @pl.when(on_diagonal)
        def _():
            kpos = k0 + lax.broadcasted_iota(jnp.int32, (tq, tk), 1)
            qpos = q0 + lax.broadcasted_iota(jnp.int32, (tq, tk), 0)
            _sb_block(qm_ref, k_blk, v_blk, u2_ref, acc_ref, carry_ref, kpos < qpos)

        @pl.when(jnp.logical_not(on_diagonal))
        def _():
            _sb_block(qm_ref, k_blk, v_blk, u2_ref, acc_ref, carry_ref, None)

        dead = jnp.max(carry_ref[...]) < SB_DEAD
        return j + 1, dead.astype(jnp.int32)

    lax.while_loop(lambda state: jnp.logical_and(state[0] <= last, state[1] == 0),
                   key_block, (jnp.int32(0), jnp.int32(0)))
    o_ref[...] = acc_ref[...]


def _stick_breaking(q, k, v, bsz, seq, kv_len, q_pos0, tq, tk):
    nq = seq // tq
    assert (q_pos0 + seq - 2) // tk + 1 <= kv_len // tk
    tri = np.tril(np.ones((tk, tk), np.float32), -1)
    u2 = jnp.asarray(np.concatenate([tri, tri], axis=0), dtype=BF16)
    resident = dict(pipeline_mode=pl.Buffered(1))
    return pl.pallas_call(
        functools.partial(_sb_kernel, tq=tq, tk=tk, q_pos0=q_pos0),
        grid=(bsz, nq),
        in_specs=[pl.BlockSpec((tq, W_C), lambda b, i: (b * nq + i, 0)),
                  pl.BlockSpec((kv_len, W_C), lambda b, i: (b, 0), **resident),
                  pl.BlockSpec((kv_len, W_C), lambda b, i: (b, 0), **resident),
                  pl.BlockSpec((2 * tk, tk), lambda b, i: (0, 0), **resident)],
        out_specs=pl.BlockSpec((tq, W_C), lambda b, i: (b * nq + i, 0)),
        out_shape=jax.ShapeDtypeStruct((bsz * seq, W_C), F32),
        scratch_shapes=[pltpu.VMEM((H_C, tq, LANES), BF16),
                        pltpu.VMEM((tq, W_C), F32),
                        pltpu.VMEM((H_C, tq, 1), F32)],
        compiler_params=_params("parallel", "arbitrary"),
        name="stick_breaking",
    )(q, k, v, u2)


def _route(sel, s):
    g_scores = []
    for g in range(N_GROUPS):
        a, b, c, d = sel[EXPERTS_PER_GROUP * g:EXPERTS_PER_GROUP * (g + 1)]
        ab_hi, ab_lo = jnp.maximum(a, b), jnp.minimum(a, b)
        cd_hi, cd_lo = jnp.maximum(c, d), jnp.minimum(c, d)
        top1 = jnp.maximum(ab_hi, cd_hi)
        top2 = jnp.maximum(jnp.minimum(ab_hi, cd_hi), jnp.maximum(ab_lo, cd_lo))
        g_scores.append(top1 + top2)
    best = g_scores[0]
    gi = jnp.zeros(best.shape, jnp.int32)
    for g in range(1, N_GROUPS):
        upd = g_scores[g] > best
        gi = jnp.where(upd, g, gi)
        best = jnp.where(upd, g_scores[g], best)

    def pick_group(rows, l):
        out = rows[(N_GROUPS - 1) * EXPERTS_PER_GROUP + l]
        for g in range(N_GROUPS - 2, -1, -1):
            out = jnp.where(gi == g, rows[g * EXPERTS_PER_GROUP + l], out)
        return out

    ig = [pick_group(sel, l) for l in range(EXPERTS_PER_GROUP)]
    sg = [pick_group(s, l) for l in range(EXPERTS_PER_GROUP)]
    b1 = ig[0]
    i1 = jnp.zeros(best.shape, jnp.int32)
    for l in range(1, EXPERTS_PER_GROUP):
        upd = ig[l] > b1
        i1 = jnp.where(upd, l, i1)
        b1 = jnp.where(upd, ig[l], b1)
    b2 = jnp.full(best.shape, -jnp.inf, F32)
    i2 = jnp.zeros(best.shape, jnp.int32)
    for l in range(EXPERTS_PER_GROUP):
        upd = jnp.logical_and(i1 != l, ig[l] > b2)
        i2 = jnp.where(upd, l, i2)
        b2 = jnp.where(upd, ig[l], b2)

    def pick_local(idx):
        out = sg[EXPERTS_PER_GROUP - 1]
        for l in range(EXPERTS_PER_GROUP - 2, -1, -1):
            out = jnp.where(idx == l, sg[l], out)
        return out

    w1 = pick_local(i1)
    w2 = pick_local(i2)
    tot = w1 + w2
    return gi * EXPERTS_PER_GROUP + i1, gi * EXPERTS_PER_GROUP + i2, w1 / tot, w2 / tot


def _post_kernel(a_ref, b_ref, c_ref, x_ref, wo_ref, gate_ref, g1_ref, b1_ref, sh2_ref, sc2_ref,
                 wrt_ref, br_ref, x1_ref, h2_ref, comb_ref, wob_ref, *, tm):
    @pl.when(pl.program_id(0) == 0)
    def _():
        wob_ref[...] = wo_ref[0].astype(BF16)

    proj = (jnp.dot(a_ref[...].astype(BF16), wob_ref[:W_A], preferred_element_type=F32)
            + jnp.dot(b_ref[...].astype(BF16), wob_ref[W_A:W_A + W_B], preferred_element_type=F32)
            + jnp.dot(c_ref[...].astype(BF16), wob_ref[W_A + W_B:], preferred_element_type=F32))
    x1 = _ln(ALPHA * x_ref[...] + gate_ref[...] * proj) * g1_ref[...] + b1_ref[...]
    x1_ref[...] = x1
    h2 = _ln(x1) * (1.0 + sc2_ref[...]) + sh2_ref[...]
    h2_ref[...] = h2.astype(BF16)
    logits_t = lax.dot_general(wrt_ref[...], h2, NT_DIMS, preferred_element_type=F32, precision=HIGHEST)
    s_t = jax.nn.sigmoid(logits_t)
    sel_t = s_t + br_ref[...]
    s = [s_t[e:e + 1, :] for e in range(N_EXPERTS)]
    sel = [sel_t[e:e + 1, :] for e in range(N_EXPERTS)]
    e1, e2, w1, w2 = _route(sel, s)
    expert = lax.broadcasted_iota(jnp.int32, (LANES, tm), 0)
    comb_t = jnp.where(expert == e1, w1, jnp.where(expert == e2, w2, 0.0))
    comb_ref[...] = comb_t.T


def _post(a, b, c, x, w_out, layer, gate1, ln_g, ln_b, shift2, scale2, wr_t, br_col, tm):
    rows = x.shape[0]
    row = lambda w: pl.BlockSpec((tm, w), lambda i: (i, 0))
    const = lambda r, w: pl.BlockSpec((r, w), lambda i: (0, 0))
    return pl.pallas_call(
        functools.partial(_post_kernel, tm=tm),
        grid=(rows // tm,),
        in_specs=[row(W_A), row(W_B), row(W_C), row(D_MODEL),
                  pl.BlockSpec((1, D_MODEL, D_MODEL), lambda i: (layer, 0, 0), pipeline_mode=pl.Buffered(1)),
                  _row_spec(gate1.shape[0], tm, D_MODEL, rows),
                  const(1, D_MODEL), const(1, D_MODEL),
                  _row_spec(shift2.shape[0], tm, D_MODEL, rows),
                  _row_spec(scale2.shape[0], tm, D_MODEL, rows),
                  const(N_EXPERTS, D_MODEL), const(N_EXPERTS, 1)],
        out_specs=[row(D_MODEL), row(D_MODEL), row(LANES)],
        out_shape=[jax.ShapeDtypeStruct((rows, D_MODEL), F32),
                   jax.ShapeDtypeStruct((rows, D_MODEL), BF16),
                   jax.ShapeDtypeStruct((rows, LANES), F32)],
        scratch_shapes=[pltpu.VMEM((D_MODEL, D_MODEL), BF16)],
        compiler_params=_params("arbitrary"),
        name="post_mix",
    )(a, b, c, x, w_out, gate1, ln_g.reshape(1, D_MODEL), ln_b.reshape(1, D_MODEL),
      shift2, scale2, wr_t, br_col)


def _moe_kernel(h_ref, comb_ref, x_ref, gate_ref, g2_ref, b2_ref, w1_ref, w3_ref, w2_ref,
                o_ref, acc_ref, *, tm):
    e = pl.program_id(1)

    @pl.when(e == 0)
    def _():
        acc_ref[...] = jnp.zeros_like(acc_ref)

    h = h_ref[...]
    a = jnp.dot(h, w1_ref[0], preferred_element_type=F32)
    g = jnp.dot(h, w3_ref[0], preferred_element_type=F32)
    lane = lax.broadcasted_iota(jnp.int32, (tm, LANES), 1)
    ce = jnp.sum(jnp.where(lane == e, comb_ref[...], 0.0), axis=-1, keepdims=True)
    act = _silu(a) * g * ce
    acc_ref[...] += jnp.dot(act.astype(BF16), w2_ref[0], preferred_element_type=F32)

    @pl.when(e == N_EXPERTS - 1)
    def _():
        y = ALPHA * x_ref[...] + gate_ref[...] * acc_ref[...]
        o_ref[...] = _ln(y) * g2_ref[...] + b2_ref[...]


def _moe(h2, comb, x1, gate2, ln_g, ln_b, w1, w3, w2, tm):
    rows = x1.shape[0]
    row = lambda w: pl.BlockSpec((tm, w), lambda i, e: (i, 0))
    const = pl.BlockSpec((1, D_MODEL), lambda i, e: (0, 0))
    gate_spec = (pl.BlockSpec((1, D_MODEL), lambda i, e: (0, 0)) if gate2.shape[0] == 1
                 else row(D_MODEL))
    return pl.pallas_call(
        functools.partial(_moe_kernel, tm=tm),
        grid=(rows // tm, N_EXPERTS),
        in_specs=[row(D_MODEL), row(LANES), row(D_MODEL), gate_spec, const, const,
                  pl.BlockSpec((1, D_MODEL, D_EXPERT), lambda i, e: (e, 0, 0)),
                  pl.BlockSpec((1, D_MODEL, D_EXPERT), lambda i, e: (e, 0, 0)),
                  pl.BlockSpec((1, D_EXPERT, D_MODEL), lambda i, e: (e, 0, 0))],
        out_specs=row(D_MODEL),
        out_shape=jax.ShapeDtypeStruct((rows, D_MODEL), F32),
        scratch_shapes=[pltpu.VMEM((tm, D_MODEL), F32)],
        compiler_params=_params("parallel", "arbitrary"),
        name="experts",
    )(h2, comb, x1, gate2, ln_g.reshape(1, D_MODEL), ln_b.reshape(1, D_MODEL), w1, w3, w2)


def _trunk_layer(x, mods, bsz, seq, pos0, s0, k_past, v_past, p, tiles):
    shift1, scale1, gate1, shift2, scale2, gate2 = mods
    uv, ret, q_c, k_c, v_c, k_bf, v_bf = _inproj(x, shift1, scale1, p["w_in"], p["layer"], tiles["tm_in"])
    a_out, v_rows = _gmlp(uv, p["w_sp"], p["b_sp"], p["ln_v_g"], p["ln_v_b"], min(seq, GMLP_CHUNK))
    b_out, s_new = _retention(ret, s0, pos0, bsz, seq, p["gn_g"], p["gn_b"])
    tq, tk = tiles["sb"]
    if k_past is None:
        k_all, v_all, kv_len = k_bf, v_bf, seq
    else:
        past = k_past.shape[1]
        kv_len = -(-(past + seq) // tk) * tk
        pad = jnp.zeros((bsz, kv_len - past - seq, W_C), BF16)
        k_all = jnp.concatenate([k_past.astype(BF16), k_bf.reshape(bsz, seq, W_C), pad], axis=1).reshape(bsz * kv_len, W_C)
        v_all = jnp.concatenate([v_past.astype(BF16), v_bf.reshape(bsz, seq, W_C), pad], axis=1).reshape(bsz * kv_len, W_C)
    c_out = _stick_breaking(q_c, k_all, v_all, bsz, seq, kv_len, pos0, tq, tk)
    x1, h2, comb = _post(a_out, b_out, c_out, x, p["w_out"], p["layer"], gate1, p["ln1_g"], p["ln1_b"],
                         shift2, scale2, p["wr"], p["br"], tiles["tm_post"])
    y = _moe(h2, comb, x1, gate2, p["ln2_g"], p["ln2_b"], p["w1"], p["w3"], p["w2"], tiles["tm_moe"])
    return y, v_rows, s_new, k_c, v_c


def kernel(x_prompt, x_sample, cache_sb_k, cache_sb_v, state_ret, c_prompt, c_sample, w_ada, b_ada, w_in, w_out, ln_v_g, ln_v_b, w_spatial, b_spatial, gn_g, gn_b, ln1_g, ln1_b, ln2_g, ln2_b, w_router, b_router, w1, w3, w2):
    bp, tp, _ = x_prompt.shape
    bs, ts, _ = x_sample.shape
    past_len = cache_sb_k.shape[2]
    assert bp == 1

    n_c = bp + bs
    c_rows = -(-n_c // 8) * 8
    c_all = jnp.concatenate([c_prompt, c_sample, jnp.zeros((c_rows - n_c, D_MODEL), F32)], axis=0)
    mods = _adaln(c_all, w_ada, b_ada)

    wr_t = w_router.T
    br_col = b_router.reshape(N_EXPERTS, 1)

    tiles_p = dict(tm_in=256, sb=(256, 256), tm_post=512, tm_moe=512)
    tiles_s = dict(tm_in=bs * ts, sb=(ts, 256), tm_post=bs * ts, tm_moe=bs * ts)

    y_p = x_prompt.reshape(bp * tp, D_MODEL)
    y_s = x_sample.reshape(bs * ts, D_MODEL)
    zero_state = jnp.zeros((bp, H_B, HEAD_DIM, HEAD_DIM), F32)
    outs = [[] for _ in range(7)]
    for l in range(DEPTH):
        p = dict(w_in=w_in, w_out=w_out,
                 w_sp=w_spatial[l], b_sp=b_spatial[l], ln_v_g=ln_v_g[l], ln_v_b=ln_v_b[l],
                 gn_g=gn_g[l], gn_b=gn_b[l], ln1_g=ln1_g[l], ln1_b=ln1_b[l],
                 ln2_g=ln2_g[l], ln2_b=ln2_b[l], wr=wr_t, br=br_col,
                 w1=w1[l].astype(BF16), w3=w3[l].astype(BF16), w2=w2[l].astype(BF16), layer=l)
        m = mods[l]
        mods_p = [m[0:1, i * D_MODEL:(i + 1) * D_MODEL] for i in range(6)]
        mods_s = [jnp.repeat(m[bp:bp + bs, i * D_MODEL:(i + 1) * D_MODEL], ts, axis=0) for i in range(6)]
        y_p, _, s_p, k_p, v_p = _trunk_layer(y_p, mods_p, bp, tp, 0, zero_state, None, None, p, tiles_p)
        y_s, g_s, s_s, k_s, v_s = _trunk_layer(
            y_s, mods_s, bs, ts, past_len, state_ret[l],
            cache_sb_k[l].reshape(bs, past_len, W_C), cache_sb_v[l].reshape(bs, past_len, W_C), p, tiles_s)
        outs[0].append(s_p)
        outs[1].append(k_p.reshape(bp, tp, H_C, HEAD_DIM))
        outs[2].append(v_p.reshape(bp, tp, H_C, HEAD_DIM))
        outs[3].append(s_s)
        outs[4].append(k_s.reshape(bs, ts, H_C, HEAD_DIM))
        outs[5].append(v_s.reshape(bs, ts, H_C, HEAD_DIM))
        outs[6].append(g_s.reshape(bs, ts, W_A))
    return (y_p.reshape(bp, tp, D_MODEL), y_s.reshape(bs, ts, D_MODEL)) + tuple(jnp.stack(o) for o in outs)
```

```python
import functools

import numpy as np
import jax
import jax.numpy as jnp
from jax import lax
from jax.experimental import pallas as pl
from jax.experimental.pallas import tpu as pltpu

F32 = jnp.float32
BF16 = jnp.bfloat16
HIGHEST = lax.Precision.HIGHEST

D_MODEL = 1024
DEPTH = 2
HEAD_DIM = 64
W_A = D_MODEL // 4
W_B = 3 * D_MODEL // 8
W_C = D_MODEL - W_A - W_B
H_A = W_A // HEAD_DIM
H_B = W_B // HEAD_DIM
H_C = W_C // HEAD_DIM
GMLP_CHUNK = 128
ROPE_BASE = 10000.0
N_EXPERTS = 16
N_GROUPS = 4
EXPERTS_PER_GROUP = N_EXPERTS // N_GROUPS
D_EXPERT = D_MODEL // 2
ALPHA = (2 * DEPTH) ** 0.25
LN_EPS = 1e-5
D_IN = 2 * W_A + 4 * W_B + 3 * W_C
LANES = 128
VMEM_LIMIT = 48 * 1024 * 1024

NT_DIMS = (((1,), (1,)), ((), ()))
TN_DIMS = (((0,), (0,)), ((), ()))
INV_LN2 = 1.4426950408889634
SB_QSCALE = HEAD_DIM ** -0.5 * INV_LN2
SB_DEAD = -152.0
GMLP_SUB = 4
RET_BLOCK = 256


def _ln(x):
    mu = jnp.mean(x, axis=-1, keepdims=True)
    xc = x - mu
    var = jnp.mean(xc * xc, axis=-1, keepdims=True)
    return xc * lax.rsqrt(var + LN_EPS)


def _silu(x):
    return x * jax.nn.sigmoid(x)


def _params(*sem):
    return pltpu.CompilerParams(dimension_semantics=sem, vmem_limit_bytes=VMEM_LIMIT)


def _row_spec(rows, tm, width, total_rows):
    if rows == 1:
        return pl.BlockSpec((1, width), lambda i: (0, 0))
    assert rows == total_rows
    return pl.BlockSpec((tm, width), lambda i: (i, 0))


def _adaln_kernel(c_ref, w_ref, b_ref, o_ref):
    sc = _silu(c_ref[...])
    o_ref[0] = jnp.dot(sc, w_ref[0], preferred_element_type=F32, precision=HIGHEST) + b_ref[0]


def _adaln(c_all, w_ada, b_ada):
    rows = c_all.shape[0]
    tn = 1536
    return pl.pallas_call(
        _adaln_kernel,
        grid=(DEPTH, 6 * D_MODEL // tn),
        in_specs=[pl.BlockSpec((rows, D_MODEL), lambda l, j: (0, 0)),
                  pl.BlockSpec((1, D_MODEL, tn), lambda l, j: (l, 0, j)),
                  pl.BlockSpec((1, 1, tn), lambda l, j: (l, 0, j))],
        out_specs=pl.BlockSpec((1, rows, tn), lambda l, j: (l, 0, j)),
        out_shape=jax.ShapeDtypeStruct((DEPTH, rows, 6 * D_MODEL), F32),
        compiler_params=_params("parallel", "parallel"),
        name="adaln",
    )(c_all, w_ada, b_ada.reshape(DEPTH, 1, 6 * D_MODEL))


def _inproj_kernel(x_ref, shift_ref, scale_ref, w_ref, uv_ref, ret_ref, q_ref, k_ref, v_ref, kb_ref, vb_ref, wb_ref):
    @pl.when(pl.program_id(0) == 0)
    def _():
        wb_ref[...] = w_ref[0].astype(BF16)

    h = _ln(x_ref[...]) * (1.0 + scale_ref[...]) + shift_ref[...]
    r = jnp.dot(h.astype(BF16), wb_ref[...], preferred_element_type=F32)
    c0 = 2 * W_A
    c1 = c0 + 4 * W_B
    uv_ref[...] = r[:, :c0]
    ret_ref[...] = r[:, c0:c1]
    q_ref[...] = r[:, c1:c1 + W_C]
    k = r[:, c1 + W_C:c1 + 2 * W_C]
    v = r[:, c1 + 2 * W_C:]
    k_ref[...] = k
    v_ref[...] = v
    kb_ref[...] = k.astype(BF16)
    vb_ref[...] = v.astype(BF16)


def _inproj(x, shift, scale, w_in, layer, tm):
    rows = x.shape[0]
    widths = (2 * W_A, 4 * W_B, W_C, W_C, W_C, W_C, W_C)
    dtypes = (F32,) * 5 + (BF16,) * 2
    return pl.pallas_call(
        _inproj_kernel,
        grid=(rows // tm,),
        in_specs=[pl.BlockSpec((tm, D_MODEL), lambda i: (i, 0)),
                  _row_spec(shift.shape[0], tm, D_MODEL, rows),
                  _row_spec(scale.shape[0], tm, D_MODEL, rows),
                  pl.BlockSpec((1, D_MODEL, D_IN), lambda i: (layer, 0, 0), pipeline_mode=pl.Buffered(1))],
        out_specs=[pl.BlockSpec((tm, w), lambda i: (i, 0)) for w in widths],
        out_shape=[jax.ShapeDtypeStruct((rows, w), dt) for w, dt in zip(widths, dtypes)],
        scratch_shapes=[pltpu.VMEM((D_MODEL, D_IN), BF16)],
        compiler_params=_params("arbitrary"),
        name="inproj",
    )(x, shift, scale, w_in)


def _gmlp_kernel(uv_ref, wsp_ref, bias_ref, g_ref, b_ref, a_ref, vn_ref, *, chunk, n_sub):
    uv = uv_ref[...]
    u = jax.nn.gelu(uv[:, :W_A])
    v = _ln(jax.nn.gelu(uv[:, W_A:])) * g_ref[...] + b_ref[...]
    vn_ref[...] = v
    row = lax.broadcasted_iota(jnp.int32, (chunk, chunk), 0)
    col = lax.broadcasted_iota(jnp.int32, (chunk, chunk), 1)
    lane_head = lax.broadcasted_iota(jnp.int32, (chunk, W_A), 1) // HEAD_DIM
    w = [jnp.where(col <= row, wsp_ref[h], 0.0).astype(BF16) for h in range(H_A)]
    for c in range(n_sub):
        rows = slice(c * chunk, (c + 1) * chunk)
        vc = v[rows]
        mixed = bias_ref[...]
        for h in range(H_A):
            vh = jnp.where(lane_head == h, vc, 0.0).astype(BF16)
            mixed = mixed + jnp.dot(w[h], vh, preferred_element_type=F32)
        a_ref[rows, :] = u[rows] * mixed


def _gmlp(uv, w_sp, b_sp, ln_g, ln_b, chunk):
    rows = uv.shape[0]
    n_sub = min(GMLP_SUB, rows // chunk)
    tm = n_sub * chunk
    wsp = w_sp[:, :chunk, :chunk]
    bias = jnp.repeat(b_sp[:, :chunk].T, HEAD_DIM, axis=1)
    return pl.pallas_call(
        functools.partial(_gmlp_kernel, chunk=chunk, n_sub=n_sub),
        grid=(rows // tm,),
        in_specs=[pl.BlockSpec((tm, 2 * W_A), lambda i: (i, 0)),
                  pl.BlockSpec((H_A, chunk, chunk), lambda i: (0, 0, 0)),
                  pl.BlockSpec((chunk, W_A), lambda i: (0, 0)),
                  pl.BlockSpec((1, W_A), lambda i: (0, 0)),
                  pl.BlockSpec((1, W_A), lambda i: (0, 0))],
        out_specs=[pl.BlockSpec((tm, W_A), lambda i: (i, 0)),
                   pl.BlockSpec((tm, W_A), lambda i: (i, 0))],
        out_shape=[jax.ShapeDtypeStruct((rows, W_A), F32),
                   jax.ShapeDtypeStruct((rows, W_A), F32)],
        compiler_params=_params("parallel"),
        name="gmlp",
    )(uv, wsp, bias, ln_g.reshape(1, W_A), ln_b.reshape(1, W_A))


def _rope(x, cos, sin):
    lane = lax.broadcasted_iota(jnp.int32, (x.shape[0], LANES), 1)
    first_half = (lane & (HEAD_DIM // 2)) == 0
    parts = []
    for c in range(x.shape[1] // LANES):
        xc = x[:, c * LANES:(c + 1) * LANES]
        rot = jnp.where(first_half,
                        pltpu.roll(xc, LANES - HEAD_DIM // 2, 1),
                        pltpu.roll(xc, HEAD_DIM // 2, 1))
        parts.append(xc * cos + rot * sin)
    return jnp.concatenate(parts, axis=1)


def _ret_kernel(r_ref, cos_ref, sin_ref, qdec_ref, kdec_ref, dec_ref, blk_ref, s0_ref,
                gng_ref, gnb_ref, o_ref, sout_ref, s_scr, o_scr, *, n_blocks):
    n = pl.program_id(1)

    @pl.when(n == 0)
    def _():
        s_scr[...] = s0_ref[0]

    r = r_ref[...]
    cos = cos_ref[...]
    sin = sin_ref[...]
    qr = _rope(r[:, :W_B], cos, sin)
    kr = _rope(r[:, W_B:2 * W_B], cos, sin) * (HEAD_DIM ** -0.5)
    vb = r[:, 2 * W_B:3 * W_B].astype(BF16)
    gate = r[:, 3 * W_B:]
    qb = qr.astype(BF16)
    kb = kr.astype(BF16)
    qdb = (qr * qdec_ref[...]).astype(BF16)
    kdb = (kr * kdec_ref[...]).astype(BF16)
    for h in range(H_B):
        sl = slice(h * HEAD_DIM, (h + 1) * HEAD_DIM)
        scores = lax.dot_general(qb[:, sl], kb[:, sl], NT_DIMS, preferred_element_type=F32) * dec_ref[h]
        s_h = s_scr[h]
        o_h = (jnp.dot(scores.astype(BF16), vb[:, sl], preferred_element_type=F32)
               + jnp.dot(qdb[:, sl], s_h.astype(BF16), preferred_element_type=F32))
        s_scr[h] = s_h * blk_ref[h] + lax.dot_general(kdb[:, sl], vb[:, sl], TN_DIMS,
                                                      preferred_element_type=F32)
        o_scr[:, sl] = _ln(o_h)
    o_ref[...] = (o_scr[...] * gng_ref[...] + gnb_ref[...]) * _silu(gate)

    @pl.when(n == n_blocks - 1)
    def _():
        sout_ref[0] = s_scr[...]


def _retention(ret, s0, pos0, bsz, seq, gn_g, gn_b):
    L = min(seq, RET_BLOCK)
    n_blocks = seq // L
    half = HEAD_DIM // 2
    inv = ROPE_BASE ** (-jnp.arange(half, dtype=F32) / half)
    ang = (pos0 + jnp.arange(seq)).astype(F32)[:, None] * inv[None, :]
    cos, sin = jnp.cos(ang), jnp.sin(ang)
    cos_t = jnp.tile(jnp.concatenate([cos, cos], axis=1), (1, LANES // HEAD_DIM))
    sin_t = jnp.tile(jnp.concatenate([-sin, sin], axis=1), (1, LANES // HEAD_DIM))
    log_g = jnp.log1p(-jnp.exp2(-5.0 - jnp.arange(H_B, dtype=F32)))
    idx = jnp.arange(L, dtype=F32)
    diff = idx[:, None] - idx[None, :]
    decay = jnp.where(diff >= 0, jnp.exp(diff[None] * log_g[:, None, None]), 0.0)
    q_decay = jnp.exp((idx[None, :] + 1.0) * log_g[:, None])
    k_decay = jnp.exp((L - 1.0 - idx[None, :]) * log_g[:, None])
    blk_decay = jnp.exp(L * log_g)
    qdec = jnp.repeat(q_decay.T, HEAD_DIM, axis=1)
    kdec = jnp.repeat(k_decay.T, HEAD_DIM, axis=1)
    blk = jnp.broadcast_to(blk_decay[:, None, None], (H_B, HEAD_DIM, HEAD_DIM))
    const2 = lambda b, n: (0, 0)
    const3 = lambda b, n: (0, 0, 0)
    return pl.pallas_call(
        functools.partial(_ret_kernel, n_blocks=n_blocks),
        grid=(bsz, n_blocks),
        in_specs=[pl.BlockSpec((L, 4 * W_B), lambda b, n: (b * n_blocks + n, 0)),
                  pl.BlockSpec((L, LANES), lambda b, n: (n, 0)),
                  pl.BlockSpec((L, LANES), lambda b, n: (n, 0)),
                  pl.BlockSpec((L, W_B), const2),
                  pl.BlockSpec((L, W_B), const2),
                  pl.BlockSpec((H_B, L, L), const3),
                  pl.BlockSpec((H_B, HEAD_DIM, HEAD_DIM), const3),
                  pl.BlockSpec((1, H_B, HEAD_DIM, HEAD_DIM), lambda b, n: (b, 0, 0, 0)),
                  pl.BlockSpec((1, W_B), const2),
                  pl.BlockSpec((1, W_B), const2)],
        out_specs=[pl.BlockSpec((L, W_B), lambda b, n: (b * n_blocks + n, 0)),
                   pl.BlockSpec((1, H_B, HEAD_DIM, HEAD_DIM), lambda b, n: (b, 0, 0, 0))],
        out_shape=[jax.ShapeDtypeStruct((bsz * seq, W_B), F32),
                   jax.ShapeDtypeStruct((bsz, H_B, HEAD_DIM, HEAD_DIM), F32)],
        scratch_shapes=[pltpu.VMEM((H_B, HEAD_DIM, HEAD_DIM), F32),
                        pltpu.VMEM((L, W_B), F32)],
        compiler_params=_params("parallel", "arbitrary"),
        name="retention",
    )(ret, cos_t, sin_t, qdec, kdec, decay, blk, s0, gn_g.reshape(1, W_B), gn_b.reshape(1, W_B))


def _sb_last_block(i, tq, tk, q_pos0):
    return (q_pos0 + (i + 1) * tq - 2) // tk


def _sb_block(qm_ref, k_ref, v_ref, u2_ref, acc_ref, carry_ref, causal):
    tk = k_ref.shape[0]
    half = lax.broadcasted_iota(jnp.int32, (tk, LANES), 1) // HEAD_DIM
    u2 = u2_ref[...]
    kp = [k_ref[:, p * LANES:(p + 1) * LANES] for p in range(H_C // 2)]

    def scores(h):
        z = lax.dot_general(qm_ref[h], kp[h // 2], NT_DIMS, preferred_element_type=F32)
        neg_abs = pltpu.bitcast(pltpu.bitcast(z, jnp.uint32) | jnp.uint32(0x80000000), F32)
        ls_pos = jnp.minimum(z, 0.0) - jnp.log(1.0 + jnp.exp2(neg_abs)) * INV_LN2
        log_stay = ls_pos - z
        if causal is not None:
            log_stay = jnp.where(causal, log_stay, 0.0)
        hi = log_stay.astype(BF16)
        lo = (log_stay - hi.astype(F32)).astype(BF16)
        return ls_pos, log_stay[:, :1], jnp.concatenate([hi, lo], axis=1)

    def cumsum(hi_lo):
        return jnp.dot(hi_lo, u2, preferred_element_type=F32)

    def weigh(h, ls_pos, first_col, excl):
        carry = carry_ref[h]
        att = jnp.exp2(ls_pos + excl + carry)
        if causal is not None:
            att = jnp.where(causal, att, 0.0)
        vp = v_ref[:, (h // 2) * LANES:(h // 2 + 1) * LANES]
        vp = jnp.where(half == h % 2, vp, jnp.zeros_like(vp))
        carry_ref[h] = carry + (excl[:, :1] + first_col)
        return jnp.dot(att.astype(BF16), vp, preferred_element_type=F32)

    stage_a, stage_b, outs = {}, {}, {}
    for step in range(H_C + 2):
        if step - 2 >= 0:
            h = step - 2
            outs[h] = weigh(h, stage_a[h][0], stage_a[h][1], stage_b.pop(h))
            del stage_a[h]
            if h % 2 == 1:
                p = h // 2
                acc_ref[:, p * LANES:(p + 1) * LANES] += outs.pop(h - 1) + outs.pop(h)
        if 0 <= step - 1 < H_C:
            stage_b[step - 1] = cumsum(stage_a[step - 1][2])
        if step < H_C:
            stage_a[step] = scores(step)


def _sb_kernel(q_ref, k_ref, v_ref, u2_ref, o_ref, qm_ref, acc_ref, carry_ref, *, tq, tk, q_pos0):
    i = pl.program_id(1)
    last = _sb_last_block(i, tq, tk, q_pos0)
    q0 = q_pos0 + i * tq

    acc_ref[...] = jnp.zeros_like(acc_ref)
    carry_ref[...] = jnp.zeros_like(carry_ref)
    half = lax.broadcasted_iota(jnp.int32, (tq, LANES), 1) // HEAD_DIM
    for h in range(H_C):
        qp = q_ref[:, (h // 2) * LANES:(h // 2 + 1) * LANES] * SB_QSCALE
        qm_ref[h] = jnp.where(half == h % 2, qp, 0.0).astype(BF16)

    def key_block(state):
        j, _ = state
        k0 = pl.multiple_of((last - j) * tk, tk)
        k_blk = k_ref.at[pl.ds(k0, tk), :]
        v_blk = v_ref.at[pl.ds(k0, tk), :]
        on_diagonal = k0 + tk > q0

        @pl.when(on_diagonal)
        def _():
            kpos = k0 + lax.broadcasted_iota(jnp.int32, (tq, tk), 1)
            qpos = q0 + lax.broadcasted_iota(jnp.int32, (tq, tk), 0)
            _sb_block(qm_ref, k_blk, v_blk, u2_ref, acc_ref, carry_ref, kpos < qpos)

        @pl.when(jnp.logical_not(on_diagonal))
        def _():
            _sb_block(qm_ref, k_blk, v_blk, u2_ref, acc_ref, carry_ref, None)

        dead = jnp.max(carry_ref[...]) < SB_DEAD
        return j + 1, dead.astype(jnp.int32)

    lax.while_loop(lambda state: jnp.logical_and(state[0] <= last, state[1] == 0),
                   key_block, (jnp.int32(0), jnp.int32(0)))
    o_ref[...] = acc_ref[...]


def _stick_breaking(q, k, v, bsz, seq, kv_len, q_pos0, tq, tk):
    nq = seq // tq
    assert (q_pos0 + seq - 2) // tk + 1 <= kv_len // tk
    tri = np.tril(np.ones((tk, tk), np.float32), -1)
    u2 = jnp.asarray(np.concatenate([tri, tri], axis=0), dtype=BF16)
    resident = dict(pipeline_mode=pl.Buffered(1))
    return pl.pallas_call(
        functools.partial(_sb_kernel, tq=tq, tk=tk, q_pos0=q_pos0),
        grid=(bsz, nq),
        in_specs=[pl.BlockSpec((tq, W_C), lambda b, i: (b * nq + i, 0)),
                  pl.BlockSpec((kv_len, W_C), lambda b, i: (b, 0), **resident),
                  pl.BlockSpec((kv_len, W_C), lambda b, i: (b, 0), **resident),
                  pl.BlockSpec((2 * tk, tk), lambda b, i: (0, 0), **resident)],
        out_specs=pl.BlockSpec((tq, W_C), lambda b, i: (b * nq + i, 0)),
        out_shape=jax.ShapeDtypeStruct((bsz * seq, W_C), F32),
        scratch_shapes=[pltpu.VMEM((H_C, tq, LANES), BF16),
                        pltpu.VMEM((tq, W_C), F32),
                        pltpu.VMEM((H_C, tq, 1), F32)],
        compiler_params=_params("parallel", "arbitrary"),
        name="stick_breaking",
    )(q, k, v, u2)


def _route(sel, s):
    g_scores = []
    for g in range(N_GROUPS):
        a, b, c, d = sel[EXPERTS_PER_GROUP * g:EXPERTS_PER_GROUP * (g + 1)]
        ab_hi, ab_lo = jnp.maximum(a, b), jnp.minimum(a, b)
        cd_hi, cd_lo = jnp.maximum(c, d), jnp.minimum(c, d)
        top1 = jnp.maximum(ab_hi, cd_hi)
        top2 = jnp.maximum(jnp.minimum(ab_hi, cd_hi), jnp.maximum(ab_lo, cd_lo))
        g_scores.append(top1 + top2)
    best = g_scores[0]
    gi = jnp.zeros(best.shape, jnp.int32)
    for g in range(1, N_GROUPS):
        upd = g_scores[g] > best
        gi = jnp.where(upd, g, gi)
        best = jnp.where(upd, g_scores[g], best)

    def pick_group(rows, l):
        out = rows[(N_GROUPS - 1) * EXPERTS_PER_GROUP + l]
        for g in range(N_GROUPS - 2, -1, -1):
            out = jnp.where(gi == g, rows[g * EXPERTS_PER_GROUP + l], out)
        return out

    ig = [pick_group(sel, l) for l in range(EXPERTS_PER_GROUP)]
    sg = [pick_group(s, l) for l in range(EXPERTS_PER_GROUP)]
    b1 = ig[0]
    i1 = jnp.zeros(best.shape, jnp.int32)
    for l in range(1, EXPERTS_PER_GROUP):
        upd = ig[l] > b1
        i1 = jnp.where(upd, l, i1)
        b1 = jnp.where(upd, ig[l], b1)
    b2 = jnp.full(best.shape, -jnp.inf, F32)
    i2 = jnp.zeros(best.shape, jnp.int32)
    for l in range(EXPERTS_PER_GROUP):
        upd = jnp.logical_and(i1 != l, ig[l] > b2)
        i2 = jnp.where(upd, l, i2)
        b2 = jnp.where(upd, ig[l], b2)

    def pick_local(idx):
        out = sg[EXPERTS_PER_GROUP - 1]
        for l in range(EXPERTS_PER_GROUP - 2, -1, -1):
            out = jnp.where(idx == l, sg[l], out)
        return out

    w1 = pick_local(i1)
    w2 = pick_local(i2)
    tot = w1 + w2
    return gi * EXPERTS_PER_GROUP + i1, gi * EXPERTS_PER_GROUP + i2, w1 / tot, w2 / tot


def _post_kernel(a_ref, b_ref, c_ref, x_ref, wo_ref, gate_ref, g1_ref, b1_ref, sh2_ref, sc2_ref,
                 wrt_ref, br_ref, x1_ref, h2_ref, comb_ref, *rest, tm, with_hx):
    hx_ref, wob_ref = rest if with_hx else (None,) + rest
    @pl.when(pl.program_id(0) == 0)
    def _():
        wob_ref[...] = wo_ref[0].astype(BF16)

    proj = (jnp.dot(a_ref[...].astype(BF16), wob_ref[:W_A], preferred_element_type=F32)
            + jnp.dot(b_ref[...].astype(BF16), wob_ref[W_A:W_A + W_B], preferred_element_type=F32)
            + jnp.dot(c_ref[...].astype(BF16), wob_ref[W_A + W_B:], preferred_element_type=F32))
    x1 = _ln(ALPHA * x_ref[...] + gate_ref[...] * proj) * g1_ref[...] + b1_ref[...]
    x1_ref[...] = x1
    h2 = _ln(x1) * (1.0 + sc2_ref[...]) + sh2_ref[...]
    h2_ref[...] = h2.astype(BF16)
    logits_t = lax.dot_general(wrt_ref[...], h2, NT_DIMS, preferred_element_type=F32, precision=HIGHEST)
    s_t = jax.nn.sigmoid(logits_t)
    sel_t = s_t + br_ref[...]
    s = [s_t[e:e + 1, :] for e in range(N_EXPERTS)]
    sel = [sel_t[e:e + 1, :] for e in range(N_EXPERTS)]
    e1, e2, w1, w2 = _route(sel, s)
    expert = lax.broadcasted_iota(jnp.int32, (LANES, tm), 0)
    comb_t = jnp.where(expert == e1, w1, jnp.where(expert == e2, w2, 0.0))
    group = (e1 // EXPERTS_PER_GROUP).astype(F32)
    comb_t = jnp.where(expert == N_EXPERTS, group, comb_t)
    comb_ref[...] = comb_t.T
    if hx_ref is not None:
        hx_ref[:, :D_MODEL] = h2
        hx_ref[:, D_MODEL:] = comb_t.T


def _post(a, b, c, x, w_out, layer, gate1, ln_g, ln_b, shift2, scale2, wr_t, br_col, tm, with_hx):
    rows = x.shape[0]
    extra_spec = [pl.BlockSpec((tm, D_MODEL + LANES), lambda i: (i, 0))] if with_hx else []
    extra_shape = [jax.ShapeDtypeStruct((rows, D_MODEL + LANES), F32)] if with_hx else []
    row = lambda w: pl.BlockSpec((tm, w), lambda i: (i, 0))
    const = lambda r, w: pl.BlockSpec((r, w), lambda i: (0, 0))
    return pl.pallas_call(
        functools.partial(_post_kernel, tm=tm, with_hx=with_hx),
        grid=(rows // tm,),
        in_specs=[row(W_A), row(W_B), row(W_C), row(D_MODEL),
                  pl.BlockSpec((1, D_MODEL, D_MODEL), lambda i: (layer, 0, 0), pipeline_mode=pl.Buffered(1)),
                  _row_spec(gate1.shape[0], tm, D_MODEL, rows),
                  const(1, D_MODEL), const(1, D_MODEL),
                  _row_spec(shift2.shape[0], tm, D_MODEL, rows),
                  _row_spec(scale2.shape[0], tm, D_MODEL, rows),
                  const(N_EXPERTS, D_MODEL), const(N_EXPERTS, 1)],
        out_specs=[row(D_MODEL), row(D_MODEL), row(LANES)] + extra_spec,
        out_shape=[jax.ShapeDtypeStruct((rows, D_MODEL), F32),
                   jax.ShapeDtypeStruct((rows, D_MODEL), BF16),
                   jax.ShapeDtypeStruct((rows, LANES), F32)] + extra_shape,
        scratch_shapes=[pltpu.VMEM((D_MODEL, D_MODEL), BF16)],
        compiler_params=_params("arbitrary"),
        name="post_mix",
    )(a, b, c, x, w_out, gate1, ln_g.reshape(1, D_MODEL), ln_b.reshape(1, D_MODEL),
      shift2, scale2, wr_t, br_col)


def _moe_kernel(h_ref, comb_ref, x_ref, gate_ref, g2_ref, b2_ref, w1_ref, w3_ref, w2_ref,
                o_ref, acc_ref, *, tm):
    e = pl.program_id(1)

    @pl.when(e == 0)
    def _():
        acc_ref[...] = jnp.zeros_like(acc_ref)

    h = h_ref[...]
    a = jnp.dot(h, w1_ref[0], preferred_element_type=F32)
    g = jnp.dot(h, w3_ref[0], preferred_element_type=F32)
    lane = lax.broadcasted_iota(jnp.int32, (tm, LANES), 1)
    ce = jnp.sum(jnp.where(lane == e, comb_ref[...], 0.0), axis=-1, keepdims=True)
    act = _silu(a) * g * ce
    acc_ref[...] += jnp.dot(act.astype(BF16), w2_ref[0], preferred_element_type=F32)

    @pl.when(e == N_EXPERTS - 1)
    def _():
        y = ALPHA * x_ref[...] + gate_ref[...] * acc_ref[...]
        o_ref[...] = _ln(y) * g2_ref[...] + b2_ref[...]


def _moe(h2, comb, x1, gate2, ln_g, ln_b, w1, w3, w2, tm):
    rows = x1.shape[0]
    row = lambda w: pl.BlockSpec((tm, w), lambda i, e: (i, 0))
    const = pl.BlockSpec((1, D_MODEL), lambda i, e: (0, 0))
    gate_spec = (pl.BlockSpec((1, D_MODEL), lambda i, e: (0, 0)) if gate2.shape[0] == 1
                 else row(D_MODEL))
    return pl.pallas_call(
        functools.partial(_moe_kernel, tm=tm),
        grid=(rows // tm, N_EXPERTS),
        in_specs=[row(D_MODEL), row(LANES), row(D_MODEL), gate_spec, const, const,
                  pl.BlockSpec((1, D_MODEL, D_EXPERT), lambda i, e: (e, 0, 0)),
                  pl.BlockSpec((1, D_MODEL, D_EXPERT), lambda i, e: (e, 0, 0)),
                  pl.BlockSpec((1, D_EXPERT, D_MODEL), lambda i, e: (e, 0, 0))],
        out_specs=row(D_MODEL),
        out_shape=jax.ShapeDtypeStruct((rows, D_MODEL), F32),
        scratch_shapes=[pltpu.VMEM((tm, D_MODEL), F32)],
        compiler_params=_params("parallel", "arbitrary"),
        name="experts",
    )(h2, comb, x1, gate2, ln_g.reshape(1, D_MODEL), ln_b.reshape(1, D_MODEL), w1, w3, w2)


def _group_plan(group, tm):
    rows = group.shape[0]
    n_tiles = rows // tm + N_GROUPS
    onehot = (group[:, None] == jnp.arange(N_GROUPS, dtype=jnp.int32)[None, :]).astype(jnp.int32)
    rank = jnp.cumsum(onehot, axis=0) - onehot
    tiles_per = (jnp.sum(onehot, axis=0) + tm - 1) // tm
    tile_start = jnp.cumsum(tiles_per) - tiles_per
    pos = jnp.sum(onehot * (tile_start[None, :] * tm + rank), axis=1).astype(jnp.int32)
    src = jnp.zeros((n_tiles * tm,), jnp.int32).at[pos].set(jnp.arange(rows, dtype=jnp.int32))
    tile = jnp.arange(n_tiles, dtype=jnp.int32)
    tile_group = jnp.clip(jnp.sum((tile[:, None] >= tile_start[None, :]).astype(jnp.int32), axis=1) - 1,
                          0, N_GROUPS - 1).astype(jnp.int32)
    n_used = jnp.sum(tiles_per).astype(jnp.int32).reshape(1)
    return pos, src, tile_group, n_used


def _start_rows(idx_ref, src_hbm, dst_ref, sem, n):
    def issue(r, carry):
        pltpu.make_async_copy(src_hbm.at[pl.ds(idx_ref[0, 0, r], 1)], dst_ref.at[pl.ds(r, 1)], sem).start()
        return carry

    lax.fori_loop(0, n, issue, 0, unroll=8)


def _wait_rows(src_hbm, dst_ref, sem, n):
    pltpu.make_async_copy(src_hbm.at[pl.ds(0, n)], dst_ref, sem).wait()


def _moe_group_kernel(tg_ref, nused_ref, src_ref, src_next_ref, hx_hbm, w1_ref, w3_ref, w2_ref, y_ref,
                      xbuf, sems, *, tm):
    i = pl.program_id(0)
    n_used = nused_ref[0]
    slot = i % 2

    @pl.when(jnp.logical_and(i == 0, n_used > 0))
    def _():
        _start_rows(src_ref, hx_hbm, xbuf.at[0], sems.at[0], tm)

    @pl.when(i + 1 < n_used)
    def _():
        _start_rows(src_next_ref, hx_hbm, xbuf.at[1 - slot], sems.at[1 - slot], tm)

    @pl.when(i < n_used)
    def _():
        _wait_rows(hx_hbm, xbuf.at[slot], sems.at[slot], tm)
        x = xbuf[slot]
        h = x[:, :D_MODEL].astype(BF16)
        comb = x[:, D_MODEL:]
        lane = lax.broadcasted_iota(jnp.int32, (tm, LANES), 1)
        first = tg_ref[i] * EXPERTS_PER_GROUP
        acc = jnp.zeros((tm, D_MODEL), F32)
        for e in range(EXPERTS_PER_GROUP):
            a = jnp.dot(h, w1_ref[0, e], preferred_element_type=F32)
            g = jnp.dot(h, w3_ref[0, e], preferred_element_type=F32)
            ce = jnp.sum(jnp.where(lane == first + e, comb, 0.0), axis=-1, keepdims=True)
            act = _silu(a) * g * ce
            acc = acc + jnp.dot(act.astype(BF16), w2_ref[0, e], preferred_element_type=F32)
        y_ref[...] = acc

    @pl.when(i >= n_used)
    def _():
        y_ref[...] = jnp.zeros_like(y_ref)


def _moe_grouped(hx, src, tile_group, n_used, w1, w3, w2, tm):
    n_tiles = tile_group.shape[0]
    wspec = lambda k, n: pl.BlockSpec((1, EXPERTS_PER_GROUP, k, n), lambda i, tg, nu: (tg[i], 0, 0, 0))
    grouped = lambda w: w.reshape(N_GROUPS, EXPERTS_PER_GROUP, *w.shape[1:])
    src3 = src.reshape(n_tiles, 1, tm)
    return pl.pallas_call(
        functools.partial(_moe_group_kernel, tm=tm),
        grid_spec=pltpu.PrefetchScalarGridSpec(
            num_scalar_prefetch=2,
            grid=(n_tiles,),
            in_specs=[pl.BlockSpec((1, 1, tm), lambda i, tg, nu: (i, 0, 0), memory_space=pltpu.SMEM),
                      pl.BlockSpec((1, 1, tm), lambda i, tg, nu: (jnp.minimum(i + 1, n_tiles - 1), 0, 0),
                                   memory_space=pltpu.SMEM),
                      pl.BlockSpec(memory_space=pl.ANY),
                      wspec(D_MODEL, D_EXPERT), wspec(D_MODEL, D_EXPERT), wspec(D_EXPERT, D_MODEL)],
            out_specs=pl.BlockSpec((tm, D_MODEL), lambda i, tg, nu: (i, 0)),
            scratch_shapes=[pltpu.VMEM((2, tm, D_MODEL + LANES), F32), pltpu.SemaphoreType.DMA((2,))]),
        out_shape=jax.ShapeDtypeStruct((n_tiles * tm, D_MODEL), F32),
        compiler_params=_params("arbitrary"),
        name="experts_grouped",
    )(tile_group, n_used, src3, src3, hx, grouped(w1), grouped(w3), grouped(w2))


def _combine_kernel(pos_ref, pos_next_ref, y_hbm, x_ref, gate_ref, g2_ref, b2_ref, o_ref, ybuf, sems, *, tm, n_tiles):
    i = pl.program_id(0)
    slot = i % 2

    @pl.when(i == 0)
    def _():
        _start_rows(pos_ref, y_hbm, ybuf.at[0], sems.at[0], tm)

    @pl.when(i + 1 < n_tiles)
    def _():
        _start_rows(pos_next_ref, y_hbm, ybuf.at[1 - slot], sems.at[1 - slot], tm)

    _wait_rows(y_hbm, ybuf.at[slot], sems.at[slot], tm)
    y = ALPHA * x_ref[...] + gate_ref[...] * ybuf[slot]
    o_ref[...] = _ln(y) * g2_ref[...] + b2_ref[...]


def _combine(y_sorted, pos, x1, gate2, ln_g, ln_b, tm):
    rows = x1.shape[0]
    n_tiles = rows // tm
    row = pl.BlockSpec((tm, D_MODEL), lambda i: (i, 0))
    const = pl.BlockSpec((1, D_MODEL), lambda i: (0, 0))
    pos3 = pos.reshape(n_tiles, 1, tm)
    return pl.pallas_call(
        functools.partial(_combine_kernel, tm=tm, n_tiles=n_tiles),
        grid=(n_tiles,),
        in_specs=[pl.BlockSpec((1, 1, tm), lambda i: (i, 0, 0), memory_space=pltpu.SMEM),
                  pl.BlockSpec((1, 1, tm), lambda i: (jnp.minimum(i + 1, n_tiles - 1), 0, 0), memory_space=pltpu.SMEM),
                  pl.BlockSpec(memory_space=pl.ANY),
                  row, _row_spec(gate2.shape[0], tm, D_MODEL, rows), const, const],
        out_specs=row,
        out_shape=jax.ShapeDtypeStruct((rows, D_MODEL), F32),
        scratch_shapes=[pltpu.VMEM((2, tm, D_MODEL), F32), pltpu.SemaphoreType.DMA((2,))],
        compiler_params=_params("arbitrary"),
        name="combine",
    )(pos3, pos3, y_sorted, x1, gate2, ln_g.reshape(1, D_MODEL), ln_b.reshape(1, D_MODEL))


def _trunk_layer(x, mods, bsz, seq, pos0, s0, k_past, v_past, p, tiles):
    shift1, scale1, gate1, shift2, scale2, gate2 = mods
    uv, ret, q_c, k_c, v_c, k_bf, v_bf = _inproj(x, shift1, scale1, p["w_in"], p["layer"], tiles["tm_in"])
    a_out, v_rows = _gmlp(uv, p["w_sp"], p["b_sp"], p["ln_v_g"], p["ln_v_b"], min(seq, GMLP_CHUNK))
    b_out, s_new = _retention(ret, s0, pos0, bsz, seq, p["gn_g"], p["gn_b"])
    tq, tk = tiles["sb"]
    if k_past is None:
        k_all, v_all, kv_len = k_bf, v_bf, seq
    else:
        past = k_past.shape[1]
        kv_len = -(-(past + seq) // tk) * tk
        pad = jnp.zeros((bsz, kv_len - past - seq, W_C), BF16)
        k_all = jnp.concatenate([k_past.astype(BF16), k_bf.reshape(bsz, seq, W_C), pad], axis=1).reshape(bsz * kv_len, W_C)
        v_all = jnp.concatenate([v_past.astype(BF16), v_bf.reshape(bsz, seq, W_C), pad], axis=1).reshape(bsz * kv_len, W_C)
    c_out = _stick_breaking(q_c, k_all, v_all, bsz, seq, kv_len, pos0, tq, tk)
    grouped = "tm_group" in tiles
    post = _post(a_out, b_out, c_out, x, p["w_out"], p["layer"], gate1, p["ln1_g"], p["ln1_b"],
                 shift2, scale2, p["wr"], p["br"], tiles["tm_post"], grouped)
    if grouped:
        x1, _, comb, hx = post
        tm = tiles["tm_group"]
        pos, src, tile_group, n_used = _group_plan(comb[:, N_EXPERTS].astype(jnp.int32), tm)
        y_sorted = _moe_grouped(hx, src, tile_group, n_used, p["w1"], p["w3"], p["w2"], tm)
        y = _combine(y_sorted, pos, x1, gate2, p["ln2_g"], p["ln2_b"], tm)
    else:
        x1, h2, comb = post
        y = _moe(h2, comb, x1, gate2, p["ln2_g"], p["ln2_b"], p["w1"], p["w3"], p["w2"], tiles["tm_moe"])
    return y, v_rows, s_new, k_c, v_c


def kernel(x_prompt, x_sample, cache_sb_k, cache_sb_v, state_ret, c_prompt, c_sample, w_ada, b_ada, w_in, w_out, ln_v_g, ln_v_b, w_spatial, b_spatial, gn_g, gn_b, ln1_g, ln1_b, ln2_g, ln2_b, w_router, b_router, w1, w3, w2):
    bp, tp, _ = x_prompt.shape
    bs, ts, _ = x_sample.shape
    past_len = cache_sb_k.shape[2]
    assert bp == 1

    n_c = bp + bs
    c_rows = -(-n_c // 8) * 8
    c_all = jnp.concatenate([c_prompt, c_sample, jnp.zeros((c_rows - n_c, D_MODEL), F32)], axis=0)
    mods = _adaln(c_all, w_ada, b_ada)

    wr_t = w_router.T
    br_col = b_router.reshape(N_EXPERTS, 1)

    tiles_p = dict(tm_in=256, sb=(256, 256), tm_post=512, tm_group=256)
    tiles_s = dict(tm_in=bs * ts, sb=(ts, 256), tm_post=bs * ts, tm_moe=bs * ts)

    y_p = x_prompt.reshape(bp * tp, D_MODEL)
    y_s = x_sample.reshape(bs * ts, D_MODEL)
    zero_state = jnp.zeros((bp, H_B, HEAD_DIM, HEAD_DIM), F32)
    outs = [[] for _ in range(7)]
    for l in range(DEPTH):
        p = dict(w_in=w_in, w_out=w_out,
                 w_sp=w_spatial[l], b_sp=b_spatial[l], ln_v_g=ln_v_g[l], ln_v_b=ln_v_b[l],
                 gn_g=gn_g[l], gn_b=gn_b[l], ln1_g=ln1_g[l], ln1_b=ln1_b[l],
                 ln2_g=ln2_g[l], ln2_b=ln2_b[l], wr=wr_t, br=br_col,
                 w1=w1[l].astype(BF16), w3=w3[l].astype(BF16), w2=w2[l].astype(BF16), layer=l)
        m = mods[l]
        mods_p = [m[0:1, i * D_MODEL:(i + 1) * D_MODEL] for i in range(6)]
        mods_s = [jnp.repeat(m[bp:bp + bs, i * D_MODEL:(i + 1) * D_MODEL], ts, axis=0) for i in range(6)]
        y_p, _, s_p, k_p, v_p = _trunk_layer(y_p, mods_p, bp, tp, 0, zero_state, None, None, p, tiles_p)
        y_s, g_s, s_s, k_s, v_s = _trunk_layer(
            y_s, mods_s, bs, ts, past_len, state_ret[l],
            cache_sb_k[l].reshape(bs, past_len, W_C), cache_sb_v[l].reshape(bs, past_len, W_C), p, tiles_s)
        outs[0].append(s_p)
        outs[1].append(k_p.reshape(bp, tp, H_C, HEAD_DIM))
        outs[2].append(v_p.reshape(bp, tp, H_C, HEAD_DIM))
        outs[3].append(s_s)
        outs[4].append(k_s.reshape(bs, ts, H_C, HEAD_DIM))
        outs[5].append(v_s.reshape(bs, ts, H_C, HEAD_DIM))
        outs[6].append(g_s.reshape(bs, ts, W_A))
    return (y_p.reshape(bp, tp, D_MODEL), y_s.reshape(bs, ts, D_MODEL)) + tuple(jnp.stack(o) for o in outs)
```

```python
import functools

import numpy as np
import jax
import jax.numpy as jnp
from jax import lax
from jax.experimental import pallas as pl
from jax.experimental.pallas import tpu as pltpu

F32 = jnp.float32
BF16 = jnp.bfloat16
HIGHEST = lax.Precision.HIGHEST

D_MODEL = 1024
DEPTH = 2
HEAD_DIM = 64
W_A = D_MODEL // 4
W_B = 3 * D_MODEL // 8
W_C = D_MODEL - W_A - W_B
H_A = W_A // HEAD_DIM
H_B = W_B // HEAD_DIM
H_C = W_C // HEAD_DIM
GMLP_CHUNK = 128
ROPE_BASE = 10000.0
N_EXPERTS = 16
N_GROUPS = 4
EXPERTS_PER_GROUP = N_EXPERTS // N_GROUPS
D_EXPERT = D_MODEL // 2
ALPHA = (2 * DEPTH) ** 0.25
LN_EPS = 1e-5
D_IN = 2 * W_A + 4 * W_B + 3 * W_C
LANES = 128
VMEM_LIMIT = 48 * 1024 * 1024
VMEM_LIMIT_GROUPED = 58 * 1024 * 1024

NT_DIMS = (((1,), (1,)), ((), ()))
TN_DIMS = (((0,), (0,)), ((), ()))
INV_LN2 = 1.4426950408889634
SB_QSCALE = HEAD_DIM ** -0.5 * INV_LN2
SB_DEAD = -152.0
GMLP_SUB = 4
RET_BLOCK = 256


def _ln(x):
    mu = jnp.mean(x, axis=-1, keepdims=True)
    xc = x - mu
    var = jnp.mean(xc * xc, axis=-1, keepdims=True)
    return xc * lax.rsqrt(var + LN_EPS)


def _silu(x):
    return x * jax.nn.sigmoid(x)


def _params(*sem, vmem=VMEM_LIMIT):
    return pltpu.CompilerParams(dimension_semantics=sem, vmem_limit_bytes=vmem)


def _row_spec(rows, tm, width, total_rows):
    if rows == 1:
        return pl.BlockSpec((1, width), lambda i: (0, 0))
    assert rows == total_rows
    return pl.BlockSpec((tm, width), lambda i: (i, 0))


def _adaln_kernel(c_ref, w_ref, b_ref, o_ref):
    sc = _silu(c_ref[...])
    o_ref[0] = jnp.dot(sc, w_ref[0], preferred_element_type=F32, precision=HIGHEST) + b_ref[0]


def _adaln(c_all, w_ada, b_ada):
    rows = c_all.shape[0]
    tn = 1536
    return pl.pallas_call(
        _adaln_kernel,
        grid=(DEPTH, 6 * D_MODEL // tn),
        in_specs=[pl.BlockSpec((rows, D_MODEL), lambda l, j: (0, 0)),
                  pl.BlockSpec((1, D_MODEL, tn), lambda l, j: (l, 0, j)),
                  pl.BlockSpec((1, 1, tn), lambda l, j: (l, 0, j))],
        out_specs=pl.BlockSpec((1, rows, tn), lambda l, j: (l, 0, j)),
        out_shape=jax.ShapeDtypeStruct((DEPTH, rows, 6 * D_MODEL), F32),
        compiler_params=_params("parallel", "parallel"),
        name="adaln",
    )(c_all, w_ada, b_ada.reshape(DEPTH, 1, 6 * D_MODEL))


def _inproj_kernel(x_ref, shift_ref, scale_ref, w_ref, uv_ref, ret_ref, q_ref, k_ref, v_ref, kb_ref, vb_ref, wb_ref):
    @pl.when(pl.program_id(0) == 0)
    def _():
        wb_ref[...] = w_ref[0].astype(BF16)

    h = _ln(x_ref[...]) * (1.0 + scale_ref[...]) + shift_ref[...]
    r = jnp.dot(h.astype(BF16), wb_ref[...], preferred_element_type=F32)
    c0 = 2 * W_A
    c1 = c0 + 4 * W_B
    uv_ref[...] = r[:, :c0]
    ret_ref[...] = r[:, c0:c1]
    q_ref[...] = r[:, c1:c1 + W_C]
    k = r[:, c1 + W_C:c1 + 2 * W_C]
    v = r[:, c1 + 2 * W_C:]
    k_ref[...] = k
    v_ref[...] = v
    kb_ref[...] = k.astype(BF16)
    vb_ref[...] = v.astype(BF16)


def _inproj(x, shift, scale, w_in, layer, tm):
    rows = x.shape[0]
    widths = (2 * W_A, 4 * W_B, W_C, W_C, W_C, W_C, W_C)
    dtypes = (F32,) * 5 + (BF16,) * 2
    return pl.pallas_call(
        _inproj_kernel,
        grid=(rows // tm,),
        in_specs=[pl.BlockSpec((tm, D_MODEL), lambda i: (i, 0)),
                  _row_spec(shift.shape[0], tm, D_MODEL, rows),
                  _row_spec(scale.shape[0], tm, D_MODEL, rows),
                  pl.BlockSpec((1, D_MODEL, D_IN), lambda i: (layer, 0, 0), pipeline_mode=pl.Buffered(1))],
        out_specs=[pl.BlockSpec((tm, w), lambda i: (i, 0)) for w in widths],
        out_shape=[jax.ShapeDtypeStruct((rows, w), dt) for w, dt in zip(widths, dtypes)],
        scratch_shapes=[pltpu.VMEM((D_MODEL, D_IN), BF16)],
        compiler_params=_params("arbitrary"),
        name="inproj",
    )(x, shift, scale, w_in)


def _gmlp_kernel(uv_ref, wsp_ref, bias_ref, g_ref, b_ref, a_ref, vn_ref, *, chunk, n_sub):
    uv = uv_ref[...]
    u = jax.nn.gelu(uv[:, :W_A])
    v = _ln(jax.nn.gelu(uv[:, W_A:])) * g_ref[...] + b_ref[...]
    vn_ref[...] = v
    row = lax.broadcasted_iota(jnp.int32, (chunk, chunk), 0)
    col = lax.broadcasted_iota(jnp.int32, (chunk, chunk), 1)
    lane_head = lax.broadcasted_iota(jnp.int32, (chunk, W_A), 1) // HEAD_DIM
    w = [jnp.where(col <= row, wsp_ref[h], 0.0).astype(BF16) for h in range(H_A)]
    for c in range(n_sub):
        rows = slice(c * chunk, (c + 1) * chunk)
        vc = v[rows]
        mixed = bias_ref[...]
        for h in range(H_A):
            vh = jnp.where(lane_head == h, vc, 0.0).astype(BF16)
            mixed = mixed + jnp.dot(w[h], vh, preferred_element_type=F32)
        a_ref[rows, :] = u[rows] * mixed


def _gmlp(uv, w_sp, b_sp, ln_g, ln_b, chunk):
    rows = uv.shape[0]
    n_sub = min(GMLP_SUB, rows // chunk)
    tm = n_sub * chunk
    wsp = w_sp[:, :chunk, :chunk]
    bias = jnp.repeat(b_sp[:, :chunk].T, HEAD_DIM, axis=1)
    return pl.pallas_call(
        functools.partial(_gmlp_kernel, chunk=chunk, n_sub=n_sub),
        grid=(rows // tm,),
        in_specs=[pl.BlockSpec((tm, 2 * W_A), lambda i: (i, 0)),
                  pl.BlockSpec((H_A, chunk, chunk), lambda i: (0, 0, 0)),
                  pl.BlockSpec((chunk, W_A), lambda i: (0, 0)),
                  pl.BlockSpec((1, W_A), lambda i: (0, 0)),
                  pl.BlockSpec((1, W_A), lambda i: (0, 0))],
        out_specs=[pl.BlockSpec((tm, W_A), lambda i: (i, 0)),
                   pl.BlockSpec((tm, W_A), lambda i: (i, 0))],
        out_shape=[jax.ShapeDtypeStruct((rows, W_A), F32),
                   jax.ShapeDtypeStruct((rows, W_A), F32)],
        compiler_params=_params("parallel"),
        name="gmlp",
    )(uv, wsp, bias, ln_g.reshape(1, W_A), ln_b.reshape(1, W_A))


def _rope(x, cos, sin):
    lane = lax.broadcasted_iota(jnp.int32, (x.shape[0], LANES), 1)
    first_half = (lane & (HEAD_DIM // 2)) == 0
    parts = []
    for c in range(x.shape[1] // LANES):
        xc = x[:, c * LANES:(c + 1) * LANES]
        rot = jnp.where(first_half,
                        pltpu.roll(xc, LANES - HEAD_DIM // 2, 1),
                        pltpu.roll(xc, HEAD_DIM // 2, 1))
        parts.append(xc * cos + rot * sin)
    return jnp.concatenate(parts, axis=1)


def _ret_kernel(r_ref, cos_ref, sin_ref, qdec_ref, kdec_ref, dec_ref, blk_ref, s0_ref,
                gng_ref, gnb_ref, o_ref, sout_ref, s_scr, o_scr, *, n_blocks):
    n = pl.program_id(1)

    @pl.when(n == 0)
    def _():
        s_scr[...] = s0_ref[0]

    r = r_ref[...]
    cos = cos_ref[...]
    sin = sin_ref[...]
    qr = _rope(r[:, :W_B], cos, sin)
    kr = _rope(r[:, W_B:2 * W_B], cos, sin) * (HEAD_DIM ** -0.5)
    vb = r[:, 2 * W_B:3 * W_B].astype(BF16)
    gate = r[:, 3 * W_B:]
    qb = qr.astype(BF16)
    kb = kr.astype(BF16)
    qdb = (qr * qdec_ref[...]).astype(BF16)
    kdb = (kr * kdec_ref[...]).astype(BF16)
    for h in range(H_B):
        sl = slice(h * HEAD_DIM, (h + 1) * HEAD_DIM)
        scores = lax.dot_general(qb[:, sl], kb[:, sl], NT_DIMS, preferred_element_type=F32) * dec_ref[h]
        s_h = s_scr[h]
        o_h = (jnp.dot(scores.astype(BF16), vb[:, sl], preferred_element_type=F32)
               + jnp.dot(qdb[:, sl], s_h.astype(BF16), preferred_element_type=F32))
        s_scr[h] = s_h * blk_ref[h] + lax.dot_general(kdb[:, sl], vb[:, sl], TN_DIMS,
                                                      preferred_element_type=F32)
        o_scr[:, sl] = _ln(o_h)
    o_ref[...] = (o_scr[...] * gng_ref[...] + gnb_ref[...]) * _silu(gate)

    @pl.when(n == n_blocks - 1)
    def _():
        sout_ref[0] = s_scr[...]


def _retention(ret, s0, pos0, bsz, seq, gn_g, gn_b):
    L = min(seq, RET_BLOCK)
    n_blocks = seq // L
    half = HEAD_DIM // 2
    inv = ROPE_BASE ** (-jnp.arange(half, dtype=F32) / half)
    ang = (pos0 + jnp.arange(seq)).astype(F32)[:, None] * inv[None, :]
    cos, sin = jnp.cos(ang), jnp.sin(ang)
    cos_t = jnp.tile(jnp.concatenate([cos, cos], axis=1), (1, LANES // HEAD_DIM))
    sin_t = jnp.tile(jnp.concatenate([-sin, sin], axis=1), (1, LANES // HEAD_DIM))
    log_g = jnp.log1p(-jnp.exp2(-5.0 - jnp.arange(H_B, dtype=F32)))
    idx = jnp.arange(L, dtype=F32)
    diff = idx[:, None] - idx[None, :]
    decay = jnp.where(diff >= 0, jnp.exp(diff[None] * log_g[:, None, None]), 0.0)
    q_decay = jnp.exp((idx[None, :] + 1.0) * log_g[:, None])
    k_decay = jnp.exp((L - 1.0 - idx[None, :]) * log_g[:, None])
    blk_decay = jnp.exp(L * log_g)
    qdec = jnp.repeat(q_decay.T, HEAD_DIM, axis=1)
    kdec = jnp.repeat(k_decay.T, HEAD_DIM, axis=1)
    blk = jnp.broadcast_to(blk_decay[:, None, None], (H_B, HEAD_DIM, HEAD_DIM))
    const2 = lambda b, n: (0, 0)
    const3 = lambda b, n: (0, 0, 0)
    return pl.pallas_call(
        functools.partial(_ret_kernel, n_blocks=n_blocks),
        grid=(bsz, n_blocks),
        in_specs=[pl.BlockSpec((L, 4 * W_B), lambda b, n: (b * n_blocks + n, 0)),
                  pl.BlockSpec((L, LANES), lambda b, n: (n, 0)),
                  pl.BlockSpec((L, LANES), lambda b, n: (n, 0)),
                  pl.BlockSpec((L, W_B), const2),
                  pl.BlockSpec((L, W_B), const2),
                  pl.BlockSpec((H_B, L, L), const3),
                  pl.BlockSpec((H_B, HEAD_DIM, HEAD_DIM), const3),
                  pl.BlockSpec((1, H_B, HEAD_DIM, HEAD_DIM), lambda b, n: (b, 0, 0, 0)),
                  pl.BlockSpec((1, W_B), const2),
                  pl.BlockSpec((1, W_B), const2)],
        out_specs=[pl.BlockSpec((L, W_B), lambda b, n: (b * n_blocks + n, 0)),
                   pl.BlockSpec((1, H_B, HEAD_DIM, HEAD_DIM), lambda b, n: (b, 0, 0, 0))],
        out_shape=[jax.ShapeDtypeStruct((bsz * seq, W_B), F32),
                   jax.ShapeDtypeStruct((bsz, H_B, HEAD_DIM, HEAD_DIM), F32)],
        scratch_shapes=[pltpu.VMEM((H_B, HEAD_DIM, HEAD_DIM), F32),
                        pltpu.VMEM((L, W_B), F32)],
        compiler_params=_params("parallel", "arbitrary"),
        name="retention",
    )(ret, cos_t, sin_t, qdec, kdec, decay, blk, s0, gn_g.reshape(1, W_B), gn_b.reshape(1, W_B))


def _sb_last_block(i, tq, tk, q_pos0):
    return (q_pos0 + (i + 1) * tq - 2) // tk


def _sb_block(qm_ref, k_ref, v_ref, u2_ref, acc_ref, carry_ref, causal):
    tk = k_ref.shape[0]
    half = lax.broadcasted_iota(jnp.int32, (tk, LANES), 1) // HEAD_DIM
    u2 = u2_ref[...]
    kp = [k_ref[:, p * LANES:(p + 1) * LANES] for p in range(H_C // 2)]

    def scores(h):
        z = lax.dot_general(qm_ref[h], kp[h // 2], NT_DIMS, preferred_element_type=F32)
        neg_abs = pltpu.bitcast(pltpu.bitcast(z, jnp.uint32) | jnp.uint32(0x80000000), F32)
        ls_pos = jnp.minimum(z, 0.0) - jnp.log(1.0 + jnp.exp2(neg_abs)) * INV_LN2
        log_stay = ls_pos - z
        if causal is not None:
            log_stay = jnp.where(causal, log_stay, 0.0)
        hi = log_stay.astype(BF16)
        lo = (log_stay - hi.astype(F32)).astype(BF16)
        return ls_pos, log_stay[:, :1], jnp.concatenate([hi, lo], axis=1)

    def cumsum(hi_lo):
        return jnp.dot(hi_lo, u2, preferred_element_type=F32)

    def weigh(h, ls_pos, first_col, excl):
        carry = carry_ref[h]
        att = jnp.exp2(ls_pos + excl + carry)
        if causal is not None:
            att = jnp.where(causal, att, 0.0)
        vp = v_ref[:, (h // 2) * LANES:(h // 2 + 1) * LANES]
        vp = jnp.where(half == h % 2, vp, jnp.zeros_like(vp))
        carry_ref[h] = carry + (excl[:, :1] + first_col)
        return jnp.dot(att.astype(BF16), vp, preferred_element_type=F32)

    stage_a, stage_b, outs = {}, {}, {}
    for step in range(H_C + 2):
        if step - 2 >= 0:
            h = step - 2
            outs[h] = weigh(h, stage_a[h][0], stage_a[h][1], stage_b.pop(h))
            del stage_a[h]
            if h % 2 == 1:
                p = h // 2
                acc_ref[:, p * LANES:(p + 1) * LANES] += outs.pop(h - 1) + outs.pop(h)
        if 0 <= step - 1 < H_C:
            stage_b[step - 1] = cumsum(stage_a[step - 1][2])
        if step < H_C:
            stage_a[step] = scores(step)


def _sb_kernel(q_ref, k_ref, v_ref, u2_ref, o_ref, qm_ref, acc_ref, carry_ref, *, tq, tk, q_pos0):
    i = pl.program_id(1)
    last = _sb_last_block(i, tq, tk, q_pos0)
    q0 = q_pos0 + i * tq

    acc_ref[...] = jnp.zeros_like(acc_ref)
    carry_ref[...] = jnp.zeros_like(carry_ref)
    half = lax.broadcasted_iota(jnp.int32, (tq, LANES), 1) // HEAD_DIM
    for h in range(H_C):
        qp = q_ref[:, (h // 2) * LANES:(h // 2 + 1) * LANES] * SB_QSCALE
        qm_ref[h] = jnp.where(half == h % 2, qp, 0.0).astype(BF16)

    def key_block(state):
        j, _ = state
        k0 = pl.multiple_of((last - j) * tk, tk)
        k_blk = k_ref.at[pl.ds(k0, tk), :]
        v_blk = v_ref.at[pl.ds(k0, tk), :]
        on_diagonal = k0 + tk > q0

        @pl.when(on_diagonal)
        def _():
            kpos = k0 + lax.broadcasted_iota(jnp.int32, (tq, tk), 1)
            qpos = q0 + lax.broadcasted_iota(jnp.int32, (tq, tk), 0)
            _sb_block(qm_ref, k_blk, v_blk, u2_ref, acc_ref, carry_ref, kpos < qpos)

        @pl.when(jnp.logical_not(on_diagonal))
        def _():
            _sb_block(qm_ref, k_blk, v_blk, u2_ref, acc_ref, carry_ref, None)

        dead = jnp.max(carry_ref[...]) < SB_DEAD
        return j + 1, dead.astype(jnp.int32)

    lax.while_loop(lambda state: jnp.logical_and(state[0] <= last, state[1] == 0),
                   key_block, (jnp.int32(0), jnp.int32(0)))
    o_ref[...] = acc_ref[...]


def _stick_breaking(q, k, v, bsz, seq, kv_len, q_pos0, tq, tk):
    nq = seq // tq
    assert (q_pos0 + seq - 2) // tk + 1 <= kv_len // tk
    tri = np.tril(np.ones((tk, tk), np.float32), -1)
    u2 = jnp.asarray(np.concatenate([tri, tri], axis=0), dtype=BF16)
    resident = dict(pipeline_mode=pl.Buffered(1))
    return pl.pallas_call(
        functools.partial(_sb_kernel, tq=tq, tk=tk, q_pos0=q_pos0),
        grid=(bsz, nq),
        in_specs=[pl.BlockSpec((tq, W_C), lambda b, i: (b * nq + i, 0)),
                  pl.BlockSpec((kv_len, W_C), lambda b, i: (b, 0), **resident),
                  pl.BlockSpec((kv_len, W_C), lambda b, i: (b, 0), **resident),
                  pl.BlockSpec((2 * tk, tk), lambda b, i: (0, 0), **resident)],
        out_specs=pl.BlockSpec((tq, W_C), lambda b, i: (b * nq + i, 0)),
        out_shape=jax.ShapeDtypeStruct((bsz * seq, W_C), F32),
        scratch_shapes=[pltpu.VMEM((H_C, tq, LANES), BF16),
                        pltpu.VMEM((tq, W_C), F32),
                        pltpu.VMEM((H_C, tq, 1), F32)],
        compiler_params=_params("parallel", "arbitrary"),
        name="stick_breaking",
    )(q, k, v, u2)


def _route(sel, s):
    g_scores = []
    for g in range(N_GROUPS):
        a, b, c, d = sel[EXPERTS_PER_GROUP * g:EXPERTS_PER_GROUP * (g + 1)]
        ab_hi, ab_lo = jnp.maximum(a, b), jnp.minimum(a, b)
        cd_hi, cd_lo = jnp.maximum(c, d), jnp.minimum(c, d)
        top1 = jnp.maximum(ab_hi, cd_hi)
        top2 = jnp.maximum(jnp.minimum(ab_hi, cd_hi), jnp.maximum(ab_lo, cd_lo))
        g_scores.append(top1 + top2)
    best = g_scores[0]
    gi = jnp.zeros(best.shape, jnp.int32)
    for g in range(1, N_GROUPS):
        upd = g_scores[g] > best
        gi = jnp.where(upd, g, gi)
        best = jnp.where(upd, g_scores[g], best)

    def pick_group(rows, l):
        out = rows[(N_GROUPS - 1) * EXPERTS_PER_GROUP + l]
        for g in range(N_GROUPS - 2, -1, -1):
            out = jnp.where(gi == g, rows[g * EXPERTS_PER_GROUP + l], out)
        return out

    ig = [pick_group(sel, l) for l in range(EXPERTS_PER_GROUP)]
    sg = [pick_group(s, l) for l in range(EXPERTS_PER_GROUP)]
    b1 = ig[0]
    i1 = jnp.zeros(best.shape, jnp.int32)
    for l in range(1, EXPERTS_PER_GROUP):
        upd = ig[l] > b1
        i1 = jnp.where(upd, l, i1)
        b1 = jnp.where(upd, ig[l], b1)
    b2 = jnp.full(best.shape, -jnp.inf, F32)
    i2 = jnp.zeros(best.shape, jnp.int32)
    for l in range(EXPERTS_PER_GROUP):
        upd = jnp.logical_and(i1 != l, ig[l] > b2)
        i2 = jnp.where(upd, l, i2)
        b2 = jnp.where(upd, ig[l], b2)

    def pick_local(idx):
        out = sg[EXPERTS_PER_GROUP - 1]
        for l in range(EXPERTS_PER_GROUP - 2, -1, -1):
            out = jnp.where(idx == l, sg[l], out)
        return out

    w1 = pick_local(i1)
    w2 = pick_local(i2)
    tot = w1 + w2
    return gi * EXPERTS_PER_GROUP + i1, gi * EXPERTS_PER_GROUP + i2, w1 / tot, w2 / tot


def _post_kernel(a_ref, b_ref, c_ref, x_ref, wo_ref, gate_ref, g1_ref, b1_ref, sh2_ref, sc2_ref,
                 wrt_ref, br_ref, x1_ref, *rest, tm, with_hx):
    if with_hx:
        h2_ref = None
        comb_ref, hx_ref, wob_ref = rest
    else:
        hx_ref = None
        h2_ref, comb_ref, wob_ref = rest
    @pl.when(pl.program_id(0) == 0)
    def _():
        wob_ref[...] = wo_ref[0].astype(BF16)

    proj = (jnp.dot(a_ref[...].astype(BF16), wob_ref[:W_A], preferred_element_type=F32)
            + jnp.dot(b_ref[...].astype(BF16), wob_ref[W_A:W_A + W_B], preferred_element_type=F32)
            + jnp.dot(c_ref[...].astype(BF16), wob_ref[W_A + W_B:], preferred_element_type=F32))
    x1 = _ln(ALPHA * x_ref[...] + gate_ref[...] * proj) * g1_ref[...] + b1_ref[...]
    x1_ref[...] = x1
    h2 = _ln(x1) * (1.0 + sc2_ref[...]) + sh2_ref[...]
    if h2_ref is not None:
        h2_ref[...] = h2.astype(BF16)
    logits_t = lax.dot_general(wrt_ref[...], h2, NT_DIMS, preferred_element_type=F32, precision=HIGHEST)
    s_t = jax.nn.sigmoid(logits_t)
    sel_t = s_t + br_ref[...]
    s = [s_t[e:e + 1, :] for e in range(N_EXPERTS)]
    sel = [sel_t[e:e + 1, :] for e in range(N_EXPERTS)]
    e1, e2, w1, w2 = _route(sel, s)
    expert = lax.broadcasted_iota(jnp.int32, (LANES, tm), 0)
    comb_t = jnp.where(expert == e1, w1, jnp.where(expert == e2, w2, 0.0))
    group = (e1 // EXPERTS_PER_GROUP).astype(F32)
    comb_t = jnp.where(expert == N_EXPERTS, group, comb_t)
    comb_ref[...] = comb_t.T
    if hx_ref is not None:
        hx_ref[:, :D_MODEL] = h2
        hx_ref[:, D_MODEL:] = comb_t.T


def _post(a, b, c, x, w_out, layer, gate1, ln_g, ln_b, shift2, scale2, wr_t, br_col, tm, with_hx):
    rows = x.shape[0]
    if with_hx:
        tail_spec = [pl.BlockSpec((tm, LANES), lambda i: (i, 0)), pl.BlockSpec((tm, D_MODEL + LANES), lambda i: (i, 0))]
        tail_shape = [jax.ShapeDtypeStruct((rows, LANES), F32), jax.ShapeDtypeStruct((rows, D_MODEL + LANES), F32)]
    else:
        tail_spec = [pl.BlockSpec((tm, D_MODEL), lambda i: (i, 0)), pl.BlockSpec((tm, LANES), lambda i: (i, 0))]
        tail_shape = [jax.ShapeDtypeStruct((rows, D_MODEL), BF16), jax.ShapeDtypeStruct((rows, LANES), F32)]
    row = lambda w: pl.BlockSpec((tm, w), lambda i: (i, 0))
    const = lambda r, w: pl.BlockSpec((r, w), lambda i: (0, 0))
    return pl.pallas_call(
        functools.partial(_post_kernel, tm=tm, with_hx=with_hx),
        grid=(rows // tm,),
        in_specs=[row(W_A), row(W_B), row(W_C), row(D_MODEL),
                  pl.BlockSpec((1, D_MODEL, D_MODEL), lambda i: (layer, 0, 0), pipeline_mode=pl.Buffered(1)),
                  _row_spec(gate1.shape[0], tm, D_MODEL, rows),
                  const(1, D_MODEL), const(1, D_MODEL),
                  _row_spec(shift2.shape[0], tm, D_MODEL, rows),
                  _row_spec(scale2.shape[0], tm, D_MODEL, rows),
                  const(N_EXPERTS, D_MODEL), const(N_EXPERTS, 1)],
        out_specs=[row(D_MODEL)] + tail_spec,
        out_shape=[jax.ShapeDtypeStruct((rows, D_MODEL), F32)] + tail_shape,
        scratch_shapes=[pltpu.VMEM((D_MODEL, D_MODEL), BF16)],
        compiler_params=_params("arbitrary"),
        name="post_mix",
    )(a, b, c, x, w_out, gate1, ln_g.reshape(1, D_MODEL), ln_b.reshape(1, D_MODEL),
      shift2, scale2, wr_t, br_col)


def _moe_kernel(h_ref, comb_ref, x_ref, gate_ref, g2_ref, b2_ref, w1_ref, w3_ref, w2_ref,
                o_ref, acc_ref, *, tm):
    e = pl.program_id(1)

    @pl.when(e == 0)
    def _():
        acc_ref[...] = jnp.zeros_like(acc_ref)

    h = h_ref[...]
    a = jnp.dot(h, w1_ref[0, 0].astype(BF16), preferred_element_type=F32)
    g = jnp.dot(h, w3_ref[0, 0].astype(BF16), preferred_element_type=F32)
    lane = lax.broadcasted_iota(jnp.int32, (tm, LANES), 1)
    ce = jnp.sum(jnp.where(lane == e, comb_ref[...], 0.0), axis=-1, keepdims=True)
    act = _silu(a) * g * ce
    acc_ref[...] += jnp.dot(act.astype(BF16), w2_ref[0, 0].astype(BF16), preferred_element_type=F32)

    @pl.when(e == N_EXPERTS - 1)
    def _():
        y = ALPHA * x_ref[...] + gate_ref[...] * acc_ref[...]
        o_ref[...] = _ln(y) * g2_ref[...] + b2_ref[...]


def _moe(h2, comb, x1, gate2, ln_g, ln_b, w1, w3, w2, layer, tm):
    rows = x1.shape[0]
    row = lambda w: pl.BlockSpec((tm, w), lambda i, e: (i, 0))
    const = pl.BlockSpec((1, D_MODEL), lambda i, e: (0, 0))
    gate_spec = (pl.BlockSpec((1, D_MODEL), lambda i, e: (0, 0)) if gate2.shape[0] == 1
                 else row(D_MODEL))
    return pl.pallas_call(
        functools.partial(_moe_kernel, tm=tm),
        grid=(rows // tm, N_EXPERTS),
        in_specs=[row(D_MODEL), row(LANES), row(D_MODEL), gate_spec, const, const,
                  pl.BlockSpec((1, 1, D_MODEL, D_EXPERT), lambda i, e: (layer, e, 0, 0)),
                  pl.BlockSpec((1, 1, D_MODEL, D_EXPERT), lambda i, e: (layer, e, 0, 0)),
                  pl.BlockSpec((1, 1, D_EXPERT, D_MODEL), lambda i, e: (layer, e, 0, 0))],
        out_specs=row(D_MODEL),
        out_shape=jax.ShapeDtypeStruct((rows, D_MODEL), F32),
        scratch_shapes=[pltpu.VMEM((tm, D_MODEL), F32)],
        compiler_params=_params("parallel", "arbitrary"),
        name="experts",
    )(h2, comb, x1, gate2, ln_g.reshape(1, D_MODEL), ln_b.reshape(1, D_MODEL), w1, w3, w2)


def _group_plan(group, tm):
    rows = group.shape[0]
    n_tiles = rows // tm + N_GROUPS
    onehot = (group[:, None] == jnp.arange(N_GROUPS, dtype=jnp.int32)[None, :]).astype(jnp.int32)
    rank = jnp.cumsum(onehot, axis=0) - onehot
    tiles_per = (jnp.sum(onehot, axis=0) + tm - 1) // tm
    tile_start = jnp.cumsum(tiles_per) - tiles_per
    pos = jnp.sum(onehot * (tile_start[None, :] * tm + rank), axis=1).astype(jnp.int32)
    src = jnp.zeros((n_tiles * tm,), jnp.int32).at[pos].set(jnp.arange(rows, dtype=jnp.int32))
    tile = jnp.arange(n_tiles, dtype=jnp.int32)
    tile_group = jnp.clip(jnp.sum((tile[:, None] >= tile_start[None, :]).astype(jnp.int32), axis=1) - 1,
                          0, N_GROUPS - 1).astype(jnp.int32)
    n_used = jnp.sum(tiles_per).astype(jnp.int32).reshape(1)
    return pos, src, tile_group, n_used


def _start_rows(idx_ref, src_hbm, dst_ref, sem, n):
    def issue(r, carry):
        pltpu.make_async_copy(src_hbm.at[pl.ds(idx_ref[0, 0, r], 1)], dst_ref.at[pl.ds(r, 1)], sem).start()
        return carry

    lax.fori_loop(0, n, issue, 0, unroll=8)


def _wait_rows(src_hbm, dst_ref, sem, n):
    pltpu.make_async_copy(src_hbm.at[pl.ds(0, n)], dst_ref, sem).wait()


def _cast_group_weights(w_ref, wb_ref):
    n_e, k, _ = wb_ref.shape
    per_expert = k // 256

    def body(t, carry):
        e = t // per_expert
        r = pl.multiple_of((t % per_expert) * 256, 256)
        wb_ref[e, pl.ds(r, 256), :] = w_ref[0, 0, e, pl.ds(r, 256), :].astype(BF16)
        return carry

    lax.fori_loop(0, n_e * per_expert, body, 0)


def _moe_group_kernel(tg_ref, nused_ref, src_ref, src_next_ref, hx_hbm, w1f_ref, w3f_ref, w2f_ref, y_ref,
                      xbuf, sems, w1_ref, w3_ref, w2_ref, *, tm):
    i = pl.program_id(0)
    n_used = nused_ref[0]
    slot = i % 2

    @pl.when(jnp.logical_or(i == 0, tg_ref[i] != tg_ref[jnp.maximum(i - 1, 0)]))
    def _():
        _cast_group_weights(w1f_ref, w1_ref)
        _cast_group_weights(w3f_ref, w3_ref)
        _cast_group_weights(w2f_ref, w2_ref)

    @pl.when(jnp.logical_and(i == 0, n_used > 0))
    def _():
        _start_rows(src_ref, hx_hbm, xbuf.at[0], sems.at[0], tm)

    @pl.when(i + 1 < n_used)
    def _():
        _start_rows(src_next_ref, hx_hbm, xbuf.at[1 - slot], sems.at[1 - slot], tm)

    @pl.when(i < n_used)
    def _():
        _wait_rows(hx_hbm, xbuf.at[slot], sems.at[slot], tm)
        x = xbuf[slot]
        h = x[:, :D_MODEL].astype(BF16)
        comb = x[:, D_MODEL:]
        lane = lax.broadcasted_iota(jnp.int32, (tm, LANES), 1)
        first = tg_ref[i] * EXPERTS_PER_GROUP
        acc = jnp.zeros((tm, D_MODEL), F32)
        for e in range(EXPERTS_PER_GROUP):
            a = jnp.dot(h, w1_ref[e], preferred_element_type=F32)
            g = jnp.dot(h, w3_ref[e], preferred_element_type=F32)
            ce = jnp.sum(jnp.where(lane == first + e, comb, 0.0), axis=-1, keepdims=True)
            act = _silu(a) * g * ce
            acc = acc + jnp.dot(act.astype(BF16), w2_ref[e], preferred_element_type=F32)
        y_ref[...] = acc

    @pl.when(i >= n_used)
    def _():
        y_ref[...] = jnp.zeros_like(y_ref)


def _moe_grouped(hx, src, tile_group, n_used, w1, w3, w2, layer, tm):
    n_tiles = tile_group.shape[0]
    wspec = lambda k, n: pl.BlockSpec((1, 1, EXPERTS_PER_GROUP, k, n), lambda i, tg, nu: (layer, tg[i], 0, 0, 0),
                                      pipeline_mode=pl.Buffered(1))
    grouped = lambda w: w.reshape(DEPTH, N_GROUPS, EXPERTS_PER_GROUP, *w.shape[2:])
    wscratch = lambda k, n: pltpu.VMEM((EXPERTS_PER_GROUP, k, n), BF16)
    src3 = src.reshape(n_tiles, 1, tm)
    return pl.pallas_call(
        functools.partial(_moe_group_kernel, tm=tm),
        grid_spec=pltpu.PrefetchScalarGridSpec(
            num_scalar_prefetch=2,
            grid=(n_tiles,),
            in_specs=[pl.BlockSpec((1, 1, tm), lambda i, tg, nu: (i, 0, 0), memory_space=pltpu.SMEM),
                      pl.BlockSpec((1, 1, tm), lambda i, tg, nu: (jnp.minimum(i + 1, n_tiles - 1), 0, 0),
                                   memory_space=pltpu.SMEM),
                      pl.BlockSpec(memory_space=pl.ANY),
                      wspec(D_MODEL, D_EXPERT), wspec(D_MODEL, D_EXPERT), wspec(D_EXPERT, D_MODEL)],
            out_specs=pl.BlockSpec((tm, D_MODEL), lambda i, tg, nu: (i, 0)),
            scratch_shapes=[pltpu.VMEM((2, tm, D_MODEL + LANES), F32), pltpu.SemaphoreType.DMA((2,)),
                            wscratch(D_MODEL, D_EXPERT), wscratch(D_MODEL, D_EXPERT), wscratch(D_EXPERT, D_MODEL)]),
        out_shape=jax.ShapeDtypeStruct((n_tiles * tm, D_MODEL), F32),
        compiler_params=_params("arbitrary", vmem=VMEM_LIMIT_GROUPED),
        name="experts_grouped",
    )(tile_group, n_used, src3, src3, hx, grouped(w1), grouped(w3), grouped(w2))


def _combine_kernel(pos_ref, pos_next_ref, y_hbm, x_ref, gate_ref, g2_ref, b2_ref, o_ref, ybuf, sems, *, tm, n_tiles):
    i = pl.program_id(0)
    slot = i % 2

    @pl.when(i == 0)
    def _():
        _start_rows(pos_ref, y_hbm, ybuf.at[0], sems.at[0], tm)

    @pl.when(i + 1 < n_tiles)
    def _():
        _start_rows(pos_next_ref, y_hbm, ybuf.at[1 - slot], sems.at[1 - slot], tm)

    _wait_rows(y_hbm, ybuf.at[slot], sems.at[slot], tm)
    y = ALPHA * x_ref[...] + gate_ref[...] * ybuf[slot]
    o_ref[...] = _ln(y) * g2_ref[...] + b2_ref[...]


def _combine(y_sorted, pos, x1, gate2, ln_g, ln_b, tm):
    rows = x1.shape[0]
    n_tiles = rows // tm
    row = pl.BlockSpec((tm, D_MODEL), lambda i: (i, 0))
    const = pl.BlockSpec((1, D_MODEL), lambda i: (0, 0))
    pos3 = pos.reshape(n_tiles, 1, tm)
    return pl.pallas_call(
        functools.partial(_combine_kernel, tm=tm, n_tiles=n_tiles),
        grid=(n_tiles,),
        in_specs=[pl.BlockSpec((1, 1, tm), lambda i: (i, 0, 0), memory_space=pltpu.SMEM),
                  pl.BlockSpec((1, 1, tm), lambda i: (jnp.minimum(i + 1, n_tiles - 1), 0, 0), memory_space=pltpu.SMEM),
                  pl.BlockSpec(memory_space=pl.ANY),
                  row, _row_spec(gate2.shape[0], tm, D_MODEL, rows), const, const],
        out_specs=row,
        out_shape=jax.ShapeDtypeStruct((rows, D_MODEL), F32),
        scratch_shapes=[pltpu.VMEM((2, tm, D_MODEL), F32), pltpu.SemaphoreType.DMA((2,))],
        compiler_params=_params("arbitrary"),
        name="combine",
    )(pos3, pos3, y_sorted, x1, gate2, ln_g.reshape(1, D_MODEL), ln_b.reshape(1, D_MODEL))


def _trunk_layer(x, mods, bsz, seq, pos0, s0, k_past, v_past, p, tiles):
    shift1, scale1, gate1, shift2, scale2, gate2 = mods
    uv, ret, q_c, k_c, v_c, k_bf, v_bf = _inproj(x, shift1, scale1, p["w_in"], p["layer"], tiles["tm_in"])
    a_out, v_rows = _gmlp(uv, p["w_sp"], p["b_sp"], p["ln_v_g"], p["ln_v_b"], min(seq, GMLP_CHUNK))
    b_out, s_new = _retention(ret, s0, pos0, bsz, seq, p["gn_g"], p["gn_b"])
    tq, tk = tiles["sb"]
    if k_past is None:
        k_all, v_all, kv_len = k_bf, v_bf, seq
    else:
        past = k_past.shape[1]
        kv_len = -(-(past + seq) // tk) * tk
        pad = jnp.zeros((bsz, kv_len - past - seq, W_C), BF16)
        k_all = jnp.concatenate([k_past.astype(BF16), k_bf.reshape(bsz, seq, W_C), pad], axis=1).reshape(bsz * kv_len, W_C)
        v_all = jnp.concatenate([v_past.astype(BF16), v_bf.reshape(bsz, seq, W_C), pad], axis=1).reshape(bsz * kv_len, W_C)
    c_out = _stick_breaking(q_c, k_all, v_all, bsz, seq, kv_len, pos0, tq, tk)
    grouped = "tm_group" in tiles
    post = _post(a_out, b_out, c_out, x, p["w_out"], p["layer"], gate1, p["ln1_g"], p["ln1_b"],
                 shift2, scale2, p["wr"], p["br"], tiles["tm_post"], grouped)
    if grouped:
        x1, comb, hx = post
        tm = tiles["tm_group"]
        pos, src, tile_group, n_used = _group_plan(comb[:, N_EXPERTS].astype(jnp.int32), tm)
        y_sorted = _moe_grouped(hx, src, tile_group, n_used, p["w1"], p["w3"], p["w2"], p["layer"], tm)
        y = _combine(y_sorted, pos, x1, gate2, p["ln2_g"], p["ln2_b"], tm)
    else:
        x1, h2, comb = post
        y = _moe(h2, comb, x1, gate2, p["ln2_g"], p["ln2_b"], p["w1"], p["w3"], p["w2"], p["layer"], tiles["tm_moe"])
    return y, v_rows, s_new, k_c, v_c


def kernel(x_prompt, x_sample, cache_sb_k, cache_sb_v, state_ret, c_prompt, c_sample, w_ada, b_ada, w_in, w_out, ln_v_g, ln_v_b, w_spatial, b_spatial, gn_g, gn_b, ln1_g, ln1_b, ln2_g, ln2_b, w_router, b_router, w1, w3, w2):
    bp, tp, _ = x_prompt.shape
    bs, ts, _ = x_sample.shape
    past_len = cache_sb_k.shape[2]
    assert bp == 1

    n_c = bp + bs
    c_rows = -(-n_c // 8) * 8
    c_all = jnp.concatenate([c_prompt, c_sample, jnp.zeros((c_rows - n_c, D_MODEL), F32)], axis=0)
    mods = _adaln(c_all, w_ada, b_ada)

    wr_t = w_router.T
    br_col = b_router.reshape(N_EXPERTS, 1)

    tiles_p = dict(tm_in=256, sb=(256, 256), tm_post=512, tm_group=256)
    tiles_s = dict(tm_in=bs * ts, sb=(ts, 256), tm_post=bs * ts, tm_moe=bs * ts)

    y_p = x_prompt.reshape(bp * tp, D_MODEL)
    y_s = x_sample.reshape(bs * ts, D_MODEL)
    zero_state = jnp.zeros((bp, H_B, HEAD_DIM, HEAD_DIM), F32)
    outs = [[] for _ in range(7)]
    for l in range(DEPTH):
        p = dict(w_in=w_in, w_out=w_out,
                 w_sp=w_spatial[l], b_sp=b_spatial[l], ln_v_g=ln_v_g[l], ln_v_b=ln_v_b[l],
                 gn_g=gn_g[l], gn_b=gn_b[l], ln1_g=ln1_g[l], ln1_b=ln1_b[l],
                 ln2_g=ln2_g[l], ln2_b=ln2_b[l], wr=wr_t, br=br_col,
                 w1=w1, w3=w3, w2=w2, layer=l)
        m = mods[l]
        mods_p = [m[0:1, i * D_MODEL:(i + 1) * D_MODEL] for i in range(6)]
        mods_s = [jnp.repeat(m[bp:bp + bs, i * D_MODEL:(i + 1) * D_MODEL], ts, axis=0) for i in range(6)]
        y_p, _, s_p, k_p, v_p = _trunk_layer(y_p, mods_p, bp, tp, 0, zero_state, None, None, p, tiles_p)
        y_s, g_s, s_s, k_s, v_s = _trunk_layer(
            y_s, mods_s, bs, ts, past_len, state_ret[l],
            cache_sb_k[l].reshape(bs, past_len, W_C), cache_sb_v[l].reshape(bs, past_len, W_C), p, tiles_s)
        outs[0].append(s_p)
        outs[1].append(k_p.reshape(bp, tp, H_C, HEAD_DIM))
        outs[2].append(v_p.reshape(bp, tp, H_C, HEAD_DIM))
        outs[3].append(s_s)
        outs[4].append(k_s.reshape(bs, ts, H_C, HEAD_DIM))
        outs[5].append(v_s.reshape(bs, ts, H_C, HEAD_DIM))
        outs[6].append(g_s.reshape(bs, ts, W_A))
    return (y_p.reshape(bp, tp, D_MODEL), y_s.reshape(bs, ts, D_MODEL)) + tuple(jnp.stack(o) for o in outs)
```

```python
import functools

import numpy as np
import jax
import jax.numpy as jnp
from jax import lax
from jax.experimental import pallas as pl
from jax.experimental.pallas import tpu as pltpu

F32 = jnp.float32
BF16 = jnp.bfloat16
HIGHEST = lax.Precision.HIGHEST

D_MODEL = 1024
DEPTH = 2
HEAD_DIM = 64
W_A = D_MODEL // 4
W_B = 3 * D_MODEL // 8
W_C = D_MODEL - W_A - W_B
H_A = W_A // HEAD_DIM
H_B = W_B // HEAD_DIM
H_C = W_C // HEAD_DIM
GMLP_CHUNK = 128
ROPE_BASE = 10000.0
N_EXPERTS = 16
N_GROUPS = 4
EXPERTS_PER_GROUP = N_EXPERTS // N_GROUPS
D_EXPERT = D_MODEL // 2
ALPHA = (2 * DEPTH) ** 0.25
LN_EPS = 1e-5
D_IN = 2 * W_A + 4 * W_B + 3 * W_C
LANES = 128
VMEM_LIMIT = 48 * 1024 * 1024

NT_DIMS = (((1,), (1,)), ((), ()))
TN_DIMS = (((0,), (0,)), ((), ()))
INV_LN2 = 1.4426950408889634
SB_QSCALE = HEAD_DIM ** -0.5 * INV_LN2
SB_DEAD = -152.0
GMLP_SUB = 4
RET_BLOCK = 256


def _ln(x):
    mu = jnp.mean(x, axis=-1, keepdims=True)
    xc = x - mu
    var = jnp.mean(xc * xc, axis=-1, keepdims=True)
    return xc * lax.rsqrt(var + LN_EPS)


def _silu(x):
    return x * jax.nn.sigmoid(x)


def _params(*sem):
    return pltpu.CompilerParams(dimension_semantics=sem, vmem_limit_bytes=VMEM_LIMIT)


def _row_spec(rows, tm, width, total_rows):
    if rows == 1:
        return pl.BlockSpec((1, width), lambda i: (0, 0))
    assert rows == total_rows
    return pl.BlockSpec((tm, width), lambda i: (i, 0))


def _adaln_kernel(c_ref, w_ref, b_ref, o_ref):
    sc = _silu(c_ref[...])
    o_ref[0] = jnp.dot(sc, w_ref[0], preferred_element_type=F32, precision=HIGHEST) + b_ref[0]


def _adaln(c_all, w_ada, b_ada):
    rows = c_all.shape[0]
    tn = 1536
    return pl.pallas_call(
        _adaln_kernel,
        grid=(DEPTH, 6 * D_MODEL // tn),
        in_specs=[pl.BlockSpec((rows, D_MODEL), lambda l, j: (0, 0)),
                  pl.BlockSpec((1, D_MODEL, tn), lambda l, j: (l, 0, j)),
                  pl.BlockSpec((1, 1, tn), lambda l, j: (l, 0, j))],
        out_specs=pl.BlockSpec((1, rows, tn), lambda l, j: (l, 0, j)),
        out_shape=jax.ShapeDtypeStruct((DEPTH, rows, 6 * D_MODEL), F32),
        compiler_params=_params("parallel", "parallel"),
        name="adaln",
    )(c_all, w_ada, b_ada.reshape(DEPTH, 1, 6 * D_MODEL))


def _inproj_kernel(x_ref, shift_ref, scale_ref, w_ref, uv_ref, ret_ref, q_ref, k_ref, v_ref, kb_ref, vb_ref, wb_ref):
    @pl.when(pl.program_id(0) == 0)
    def _():
        wb_ref[...] = w_ref[0].astype(BF16)

    h = _ln(x_ref[...]) * (1.0 + scale_ref[...]) + shift_ref[...]
    r = jnp.dot(h.astype(BF16), wb_ref[...], preferred_element_type=F32)
    c0 = 2 * W_A
    c1 = c0 + 4 * W_B
    uv_ref[...] = r[:, :c0]
    ret_ref[...] = r[:, c0:c1]
    q_ref[...] = r[:, c1:c1 + W_C]
    k = r[:, c1 + W_C:c1 + 2 * W_C]
    v = r[:, c1 + 2 * W_C:]
    k_ref[...] = k
    v_ref[...] = v
    kb_ref[...] = k.astype(BF16)
    vb_ref[...] = v.astype(BF16)


def _inproj(x, shift, scale, w_in, layer, tm):
    rows = x.shape[0]
    widths = (2 * W_A, 4 * W_B, W_C, W_C, W_C, W_C, W_C)
    dtypes = (F32,) * 5 + (BF16,) * 2
    return pl.pallas_call(
        _inproj_kernel,
        grid=(rows // tm,),
        in_specs=[pl.BlockSpec((tm, D_MODEL), lambda i: (i, 0)),
                  _row_spec(shift.shape[0], tm, D_MODEL, rows),
                  _row_spec(scale.shape[0], tm, D_MODEL, rows),
                  pl.BlockSpec((1, D_MODEL, D_IN), lambda i: (layer, 0, 0), pipeline_mode=pl.Buffered(1))],
        out_specs=[pl.BlockSpec((tm, w), lambda i: (i, 0)) for w in widths],
        out_shape=[jax.ShapeDtypeStruct((rows, w), dt) for w, dt in zip(widths, dtypes)],
        scratch_shapes=[pltpu.VMEM((D_MODEL, D_IN), BF16)],
        compiler_params=_params("arbitrary"),
        name="inproj",
    )(x, shift, scale, w_in)


def _gmlp_kernel(uv_ref, wsp_ref, bias_ref, g_ref, b_ref, a_ref, vn_ref, *, chunk, n_sub):
    uv = uv_ref[...]
    u = jax.nn.gelu(uv[:, :W_A])
    v = _ln(jax.nn.gelu(uv[:, W_A:])) * g_ref[...] + b_ref[...]
    vn_ref[...] = v
    row = lax.broadcasted_iota(jnp.int32, (chunk, chunk), 0)
    col = lax.broadcasted_iota(jnp.int32, (chunk, chunk), 1)
    lane_head = lax.broadcasted_iota(jnp.int32, (chunk, W_A), 1) // HEAD_DIM
    w = [jnp.where(col <= row, wsp_ref[h], 0.0).astype(BF16) for h in range(H_A)]
    for c in range(n_sub):
        rows = slice(c * chunk, (c + 1) * chunk)
        vc = v[rows]
        mixed = bias_ref[...]
        for h in range(H_A):
            vh = jnp.where(lane_head == h, vc, 0.0).astype(BF16)
            mixed = mixed + jnp.dot(w[h], vh, preferred_element_type=F32)
        a_ref[rows, :] = u[rows] * mixed


def _gmlp(uv, w_sp, b_sp, ln_g, ln_b, chunk):
    rows = uv.shape[0]
    n_sub = min(GMLP_SUB, rows // chunk)
    tm = n_sub * chunk
    wsp = w_sp[:, :chunk, :chunk]
    bias = jnp.repeat(b_sp[:, :chunk].T, HEAD_DIM, axis=1)
    return pl.pallas_call(
        functools.partial(_gmlp_kernel, chunk=chunk, n_sub=n_sub),
        grid=(rows // tm,),
        in_specs=[pl.BlockSpec((tm, 2 * W_A), lambda i: (i, 0)),
                  pl.BlockSpec((H_A, chunk, chunk), lambda i: (0, 0, 0)),
                  pl.BlockSpec((chunk, W_A), lambda i: (0, 0)),
                  pl.BlockSpec((1, W_A), lambda i: (0, 0)),
                  pl.BlockSpec((1, W_A), lambda i: (0, 0))],
        out_specs=[pl.BlockSpec((tm, W_A), lambda i: (i, 0)),
                   pl.BlockSpec((tm, W_A), lambda i: (i, 0))],
        out_shape=[jax.ShapeDtypeStruct((rows, W_A), F32),
                   jax.ShapeDtypeStruct((rows, W_A), F32)],
        compiler_params=_params("parallel"),
        name="gmlp",
    )(uv, wsp, bias, ln_g.reshape(1, W_A), ln_b.reshape(1, W_A))


def _rope(x, cos, sin):
    lane = lax.broadcasted_iota(jnp.int32, (x.shape[0], LANES), 1)
    first_half = (lane & (HEAD_DIM // 2)) == 0
    parts = []
    for c in range(x.shape[1] // LANES):
        xc = x[:, c * LANES:(c + 1) * LANES]
        rot = jnp.where(first_half,
                        pltpu.roll(xc, LANES - HEAD_DIM // 2, 1),
                        pltpu.roll(xc, HEAD_DIM // 2, 1))
        parts.append(xc * cos + rot * sin)
    return jnp.concatenate(parts, axis=1)


def _ret_kernel(r_ref, cos_ref, sin_ref, qdec_ref, kdec_ref, dec_ref, blk_ref, s0_ref,
                gng_ref, gnb_ref, o_ref, sout_ref, s_scr, o_scr, *, n_blocks):
    n = pl.program_id(1)

    @pl.when(n == 0)
    def _():
        s_scr[...] = s0_ref[0]

    r = r_ref[...]
    cos = cos_ref[...]
    sin = sin_ref[...]
    qr = _rope(r[:, :W_B], cos, sin)
    kr = _rope(r[:, W_B:2 * W_B], cos, sin) * (HEAD_DIM ** -0.5)
    vb = r[:, 2 * W_B:3 * W_B].astype(BF16)
    gate = r[:, 3 * W_B:]
    qb = qr.astype(BF16)
    kb = kr.astype(BF16)
    qdb = (qr * qdec_ref[...]).astype(BF16)
    kdb = (kr * kdec_ref[...]).astype(BF16)
    for h in range(H_B):
        sl = slice(h * HEAD_DIM, (h + 1) * HEAD_DIM)
        scores = lax.dot_general(qb[:, sl], kb[:, sl], NT_DIMS, preferred_element_type=F32) * dec_ref[h]
        s_h = s_scr[h]
        o_h = (jnp.dot(scores.astype(BF16), vb[:, sl], preferred_element_type=F32)
               + jnp.dot(qdb[:, sl], s_h.astype(BF16), preferred_element_type=F32))
        s_scr[h] = s_h * blk_ref[h] + lax.dot_general(kdb[:, sl], vb[:, sl], TN_DIMS,
                                                      preferred_element_type=F32)
        o_scr[:, sl] = _ln(o_h)
    o_ref[...] = (o_scr[...] * gng_ref[...] + gnb_ref[...]) * _silu(gate)

    @pl.when(n == n_blocks - 1)
    def _():
        sout_ref[0] = s_scr[...]


def _retention(ret, s0, pos0, bsz, seq, gn_g, gn_b):
    L = min(seq, RET_BLOCK)
    n_blocks = seq // L
    half = HEAD_DIM // 2
    inv = ROPE_BASE ** (-jnp.arange(half, dtype=F32) / half)
    ang = (pos0 + jnp.arange(seq)).astype(F32)[:, None] * inv[None, :]
    cos, sin = jnp.cos(ang), jnp.sin(ang)
    cos_t = jnp.tile(jnp.concatenate([cos, cos], axis=1), (1, LANES // HEAD_DIM))
    sin_t = jnp.tile(jnp.concatenate([-sin, sin], axis=1), (1, LANES // HEAD_DIM))
    log_g = jnp.log1p(-jnp.exp2(-5.0 - jnp.arange(H_B, dtype=F32)))
    idx = jnp.arange(L, dtype=F32)
    diff = idx[:, None] - idx[None, :]
    decay = jnp.where(diff >= 0, jnp.exp(diff[None] * log_g[:, None, None]), 0.0)
    q_decay = jnp.exp((idx[None, :] + 1.0) * log_g[:, None])
    k_decay = jnp.exp((L - 1.0 - idx[None, :]) * log_g[:, None])
    blk_decay = jnp.exp(L * log_g)
    qdec = jnp.repeat(q_decay.T, HEAD_DIM, axis=1)
    kdec = jnp.repeat(k_decay.T, HEAD_DIM, axis=1)
    blk = jnp.broadcast_to(blk_decay[:, None, None], (H_B, HEAD_DIM, HEAD_DIM))
    const2 = lambda b, n: (0, 0)
    const3 = lambda b, n: (0, 0, 0)
    return pl.pallas_call(
        functools.partial(_ret_kernel, n_blocks=n_blocks),
        grid=(bsz, n_blocks),
        in_specs=[pl.BlockSpec((L, 4 * W_B), lambda b, n: (b * n_blocks + n, 0)),
                  pl.BlockSpec((L, LANES), lambda b, n: (n, 0)),
                  pl.BlockSpec((L, LANES), lambda b, n: (n, 0)),
                  pl.BlockSpec((L, W_B), const2),
                  pl.BlockSpec((L, W_B), const2),
                  pl.BlockSpec((H_B, L, L), const3),
                  pl.BlockSpec((H_B, HEAD_DIM, HEAD_DIM), const3),
                  pl.BlockSpec((1, H_B, HEAD_DIM, HEAD_DIM), lambda b, n: (b, 0, 0, 0)),
                  pl.BlockSpec((1, W_B), const2),
                  pl.BlockSpec((1, W_B), const2)],
        out_specs=[pl.BlockSpec((L, W_B), lambda b, n: (b * n_blocks + n, 0)),
                   pl.BlockSpec((1, H_B, HEAD_DIM, HEAD_DIM), lambda b, n: (b, 0, 0, 0))],
        out_shape=[jax.ShapeDtypeStruct((bsz * seq, W_B), F32),
                   jax.ShapeDtypeStruct((bsz, H_B, HEAD_DIM, HEAD_DIM), F32)],
        scratch_shapes=[pltpu.VMEM((H_B, HEAD_DIM, HEAD_DIM), F32),
                        pltpu.VMEM((L, W_B), F32)],
        compiler_params=_params("parallel", "arbitrary"),
        name="retention",
    )(ret, cos_t, sin_t, qdec, kdec, decay, blk, s0, gn_g.reshape(1, W_B), gn_b.reshape(1, W_B))


def _sb_last_block(i, tq, tk, q_pos0):
    return (q_pos0 + (i + 1) * tq - 2) // tk


def _sb_block(qm_ref, k_ref, v_ref, u2_ref, acc_ref, carry_ref, causal):
    tk = k_ref.shape[0]
    half = lax.broadcasted_iota(jnp.int32, (tk, LANES), 1) // HEAD_DIM
    u2 = u2_ref[...]
    kp = [k_ref[:, p * LANES:(p + 1) * LANES] for p in range(H_C // 2)]

    def scores(h):
        z = lax.dot_general(qm_ref[h], kp[h // 2], NT_DIMS, preferred_element_type=F32)
        neg_abs = pltpu.bitcast(pltpu.bitcast(z, jnp.uint32) | jnp.uint32(0x80000000), F32)
        ls_pos = jnp.minimum(z, 0.0) - jnp.log(1.0 + jnp.exp2(neg_abs)) * INV_LN2
        log_stay = ls_pos - z
        if causal is not None:
            log_stay = jnp.where(causal, log_stay, 0.0)
        hi = log_stay.astype(BF16)
        lo = (log_stay - hi.astype(F32)).astype(BF16)
        return ls_pos, log_stay[:, :1], jnp.concatenate([hi, lo], axis=1)

    def cumsum(hi_lo):
        return jnp.dot(hi_lo, u2, preferred_element_type=F32)

    def weigh(h, ls_pos, first_col, excl):
        carry = carry_ref[h]
        att = jnp.exp2(ls_pos + excl + carry)
        if causal is not None:
            att = jnp.where(causal, att, 0.0)
        vp = v_ref[:, (h // 2) * LANES:(h // 2 + 1) * LANES]
        vp = jnp.where(half == h % 2, vp, jnp.zeros_like(vp))
        carry_ref[h] = carry + (excl[:, :1] + first_col)
        return jnp.dot(att.astype(BF16), vp, preferred_element_type=F32)

    stage_a, stage_b, outs = {}, {}, {}
    for step in range(H_C + 2):
        if step - 2 >= 0:
            h = step - 2
            outs[h] = weigh(h, stage_a[h][0], stage_a[h][1], stage_b.pop(h))
            del stage_a[h]
            if h % 2 == 1:
                p = h // 2
                acc_ref[:, p * LANES:(p + 1) * LANES] += outs.pop(h - 1) + outs.pop(h)
        if 0 <= step - 1 < H_C:
            stage_b[step - 1] = cumsum(stage_a[step - 1][2])
        if step < H_C:
            stage_a[step] = scores(step)


def _sb_kernel(*refs, tq, tk, q_pos0, past):
    if past:
        q_ref, kc_ref, vc_ref, kn_ref, vn_ref, u2_ref, o_ref, qm_ref, acc_ref, carry_ref, k_ref, v_ref = refs
        seq = kn_ref.shape[0]
        for cache, new, buf in ((kc_ref, kn_ref, k_ref), (vc_ref, vn_ref, v_ref)):
            buf[:past, :] = cache[...].astype(BF16)
            buf[past:past + seq, :] = new[...]
            buf[past + seq:, :] = jnp.zeros((buf.shape[0] - past - seq, W_C), BF16)
    else:
        q_ref, k_ref, v_ref, u2_ref, o_ref, qm_ref, acc_ref, carry_ref = refs
    i = pl.program_id(1)
    last = _sb_last_block(i, tq, tk, q_pos0)
    q0 = q_pos0 + i * tq

    acc_ref[...] = jnp.zeros_like(acc_ref)
    carry_ref[...] = jnp.zeros_like(carry_ref)
    half = lax.broadcasted_iota(jnp.int32, (tq, LANES), 1) // HEAD_DIM
    for h in range(H_C):
        qp = q_ref[:, (h // 2) * LANES:(h // 2 + 1) * LANES] * SB_QSCALE
        qm_ref[h] = jnp.where(half == h % 2, qp, 0.0).astype(BF16)

    def key_block(state):
        j, _ = state
        k0 = pl.multiple_of((last - j) * tk, tk)
        k_blk = k_ref.at[pl.ds(k0, tk), :]
        v_blk = v_ref.at[pl.ds(k0, tk), :]
        on_diagonal = k0 + tk > q0

        @pl.when(on_diagonal)
        def _():
            kpos = k0 + lax.broadcasted_iota(jnp.int32, (tq, tk), 1)
            qpos = q0 + lax.broadcasted_iota(jnp.int32, (tq, tk), 0)
            _sb_block(qm_ref, k_blk, v_blk, u2_ref, acc_ref, carry_ref, kpos < qpos)

        @pl.when(jnp.logical_not(on_diagonal))
        def _():
            _sb_block(qm_ref, k_blk, v_blk, u2_ref, acc_ref, carry_ref, None)

        dead = jnp.max(carry_ref[...]) < SB_DEAD
        return j + 1, dead.astype(jnp.int32)

    lax.while_loop(lambda state: jnp.logical_and(state[0] <= last, state[1] == 0),
                   key_block, (jnp.int32(0), jnp.int32(0)))
    o_ref[...] = acc_ref[...]


def _stick_breaking(q, k, v, bsz, seq, q_pos0, tq, tk, k_past=None, v_past=None):
    nq = seq // tq
    past = 0 if k_past is None else q_pos0
    kv_len = -(-(past + seq) // tk) * tk
    assert (past == 0 and q_pos0 == 0) or nq == 1
    tri = np.tril(np.ones((tk, tk), np.float32), -1)
    u2 = jnp.asarray(np.concatenate([tri, tri], axis=0), dtype=BF16)
    resident = dict(pipeline_mode=pl.Buffered(1))
    per_batch = lambda rows: pl.BlockSpec((rows, W_C), lambda b, i: (b, 0), **resident)
    scratch = [pltpu.VMEM((H_C, tq, LANES), BF16), pltpu.VMEM((tq, W_C), F32), pltpu.VMEM((H_C, tq, 1), F32)]
    if past:
        kv_specs = [per_batch(past), per_batch(past), per_batch(seq), per_batch(seq)]
        kv_args = (k_past, v_past, k, v)
        scratch += [pltpu.VMEM((kv_len, W_C), BF16), pltpu.VMEM((kv_len, W_C), BF16)]
    else:
        kv_specs = [per_batch(kv_len), per_batch(kv_len)]
        kv_args = (k, v)
    return pl.pallas_call(
        functools.partial(_sb_kernel, tq=tq, tk=tk, q_pos0=q_pos0, past=past),
        grid=(bsz, nq),
        in_specs=[pl.BlockSpec((tq, W_C), lambda b, i: (b * nq + i, 0))] + kv_specs
                 + [pl.BlockSpec((2 * tk, tk), lambda b, i: (0, 0), **resident)],
        out_specs=pl.BlockSpec((tq, W_C), lambda b, i: (b * nq + i, 0)),
        out_shape=jax.ShapeDtypeStruct((bsz * seq, W_C), F32),
        scratch_shapes=scratch,
        compiler_params=_params("parallel", "arbitrary"),
        name="stick_breaking",
    )(q, *kv_args, u2)


def _route(sel, s):
    g_scores = []
    for g in range(N_GROUPS):
        a, b, c, d = sel[EXPERTS_PER_GROUP * g:EXPERTS_PER_GROUP * (g + 1)]
        ab_hi, ab_lo = jnp.maximum(a, b), jnp.minimum(a, b)
        cd_hi, cd_lo = jnp.maximum(c, d), jnp.minimum(c, d)
        top1 = jnp.maximum(ab_hi, cd_hi)
        top2 = jnp.maximum(jnp.minimum(ab_hi, cd_hi), jnp.maximum(ab_lo, cd_lo))
        g_scores.append(top1 + top2)
    best = g_scores[0]
    gi = jnp.zeros(best.shape, jnp.int32)
    for g in range(1, N_GROUPS):
        upd = g_scores[g] > best
        gi = jnp.where(upd, g, gi)
        best = jnp.where(upd, g_scores[g], best)

    def pick_group(rows, l):
        out = rows[(N_GROUPS - 1) * EXPERTS_PER_GROUP + l]
        for g in range(N_GROUPS - 2, -1, -1):
            out = jnp.where(gi == g, rows[g * EXPERTS_PER_GROUP + l], out)
        return out

    ig = [pick_group(sel, l) for l in range(EXPERTS_PER_GROUP)]
    sg = [pick_group(s, l) for l in range(EXPERTS_PER_GROUP)]
    b1 = ig[0]
    i1 = jnp.zeros(best.shape, jnp.int32)
    for l in range(1, EXPERTS_PER_GROUP):
        upd = ig[l] > b1
        i1 = jnp.where(upd, l, i1)
        b1 = jnp.where(upd, ig[l], b1)
    b2 = jnp.full(best.shape, -jnp.inf, F32)
    i2 = jnp.zeros(best.shape, jnp.int32)
    for l in range(EXPERTS_PER_GROUP):
        upd = jnp.logical_and(i1 != l, ig[l] > b2)
        i2 = jnp.where(upd, l, i2)
        b2 = jnp.where(upd, ig[l], b2)

    def pick_local(idx):
        out = sg[EXPERTS_PER_GROUP - 1]
        for l in range(EXPERTS_PER_GROUP - 2, -1, -1):
            out = jnp.where(idx == l, sg[l], out)
        return out

    w1 = pick_local(i1)
    w2 = pick_local(i2)
    tot = w1 + w2
    return gi * EXPERTS_PER_GROUP + i1, gi * EXPERTS_PER_GROUP + i2, w1 / tot, w2 / tot


def _post_kernel(a_ref, b_ref, c_ref, x_ref, wo_ref, gate_ref, g1_ref, b1_ref, sh2_ref, sc2_ref,
                 wrt_ref, br_ref, x1_ref, h2_ref, comb_ref, *rest, tm, with_hx):
    hx_ref, wob_ref = rest if with_hx else (None,) + rest
    @pl.when(pl.program_id(0) == 0)
    def _():
        wob_ref[...] = wo_ref[0].astype(BF16)

    proj = (jnp.dot(a_ref[...].astype(BF16), wob_ref[:W_A], preferred_element_type=F32)
            + jnp.dot(b_ref[...].astype(BF16), wob_ref[W_A:W_A + W_B], preferred_element_type=F32)
            + jnp.dot(c_ref[...].astype(BF16), wob_ref[W_A + W_B:], preferred_element_type=F32))
    x1 = _ln(ALPHA * x_ref[...] + gate_ref[...] * proj) * g1_ref[...] + b1_ref[...]
    x1_ref[...] = x1
    h2 = _ln(x1) * (1.0 + sc2_ref[...]) + sh2_ref[...]
    h2_ref[...] = h2.astype(BF16)
    logits_t = lax.dot_general(wrt_ref[...], h2, NT_DIMS, preferred_element_type=F32, precision=HIGHEST)
    s_t = jax.nn.sigmoid(logits_t)
    sel_t = s_t + br_ref[...]
    s = [s_t[e:e + 1, :] for e in range(N_EXPERTS)]
    sel = [sel_t[e:e + 1, :] for e in range(N_EXPERTS)]
    e1, e2, w1, w2 = _route(sel, s)
    expert = lax.broadcasted_iota(jnp.int32, (LANES, tm), 0)
    comb_t = jnp.where(expert == e1, w1, jnp.where(expert == e2, w2, 0.0))
    group = (e1 // EXPERTS_PER_GROUP).astype(F32)
    comb_t = jnp.where(expert == N_EXPERTS, group, comb_t)
    comb_ref[...] = comb_t.T
    if hx_ref is not None:
        hx_ref[:, :D_MODEL] = h2
        hx_ref[:, D_MODEL:] = comb_t.T


def _post(a, b, c, x, w_out, layer, gate1, ln_g, ln_b, shift2, scale2, wr_t, br_col, tm, with_hx):
    rows = x.shape[0]
    extra_spec = [pl.BlockSpec((tm, D_MODEL + LANES), lambda i: (i, 0))] if with_hx else []
    extra_shape = [jax.ShapeDtypeStruct((rows, D_MODEL + LANES), F32)] if with_hx else []
    row = lambda w: pl.BlockSpec((tm, w), lambda i: (i, 0))
    const = lambda r, w: pl.BlockSpec((r, w), lambda i: (0, 0))
    return pl.pallas_call(
        functools.partial(_post_kernel, tm=tm, with_hx=with_hx),
        grid=(rows // tm,),
        in_specs=[row(W_A), row(W_B), row(W_C), row(D_MODEL),
                  pl.BlockSpec((1, D_MODEL, D_MODEL), lambda i: (layer, 0, 0), pipeline_mode=pl.Buffered(1)),
                  _row_spec(gate1.shape[0], tm, D_MODEL, rows),
                  const(1, D_MODEL), const(1, D_MODEL),
                  _row_spec(shift2.shape[0], tm, D_MODEL, rows),
                  _row_spec(scale2.shape[0], tm, D_MODEL, rows),
                  const(N_EXPERTS, D_MODEL), const(N_EXPERTS, 1)],
        out_specs=[row(D_MODEL), row(D_MODEL), row(LANES)] + extra_spec,
        out_shape=[jax.ShapeDtypeStruct((rows, D_MODEL), F32),
                   jax.ShapeDtypeStruct((rows, D_MODEL), BF16),
                   jax.ShapeDtypeStruct((rows, LANES), F32)] + extra_shape,
        scratch_shapes=[pltpu.VMEM((D_MODEL, D_MODEL), BF16)],
        compiler_params=_params("arbitrary"),
        name="post_mix",
    )(a, b, c, x, w_out, gate1, ln_g.reshape(1, D_MODEL), ln_b.reshape(1, D_MODEL),
      shift2, scale2, wr_t, br_col)


def _moe_kernel(h_ref, comb_ref, x_ref, gate_ref, g2_ref, b2_ref, w1_ref, w3_ref, w2_ref,
                o_ref, acc_ref, *, tm):
    e = pl.program_id(1)

    @pl.when(e == 0)
    def _():
        acc_ref[...] = jnp.zeros_like(acc_ref)

    h = h_ref[...]
    a = jnp.dot(h, w1_ref[0], preferred_element_type=F32)
    g = jnp.dot(h, w3_ref[0], preferred_element_type=F32)
    lane = lax.broadcasted_iota(jnp.int32, (tm, LANES), 1)
    ce = jnp.sum(jnp.where(lane == e, comb_ref[...], 0.0), axis=-1, keepdims=True)
    act = _silu(a) * g * ce
    acc_ref[...] += jnp.dot(act.astype(BF16), w2_ref[0], preferred_element_type=F32)

    @pl.when(e == N_EXPERTS - 1)
    def _():
        y = ALPHA * x_ref[...] + gate_ref[...] * acc_ref[...]
        o_ref[...] = _ln(y) * g2_ref[...] + b2_ref[...]


def _moe(h2, comb, x1, gate2, ln_g, ln_b, w1, w3, w2, tm):
    rows = x1.shape[0]
    row = lambda w: pl.BlockSpec((tm, w), lambda i, e: (i, 0))
    const = pl.BlockSpec((1, D_MODEL), lambda i, e: (0, 0))
    gate_spec = (pl.BlockSpec((1, D_MODEL), lambda i, e: (0, 0)) if gate2.shape[0] == 1
                 else row(D_MODEL))
    return pl.pallas_call(
        functools.partial(_moe_kernel, tm=tm),
        grid=(rows // tm, N_EXPERTS),
        in_specs=[row(D_MODEL), row(LANES), row(D_MODEL), gate_spec, const, const,
                  pl.BlockSpec((1, D_MODEL, D_EXPERT), lambda i, e: (e, 0, 0)),
                  pl.BlockSpec((1, D_MODEL, D_EXPERT), lambda i, e: (e, 0, 0)),
                  pl.BlockSpec((1, D_EXPERT, D_MODEL), lambda i, e: (e, 0, 0))],
        out_specs=row(D_MODEL),
        out_shape=jax.ShapeDtypeStruct((rows, D_MODEL), F32),
        scratch_shapes=[pltpu.VMEM((tm, D_MODEL), F32)],
        compiler_params=_params("parallel", "arbitrary"),
        name="experts",
    )(h2, comb, x1, gate2, ln_g.reshape(1, D_MODEL), ln_b.reshape(1, D_MODEL), w1, w3, w2)


def _group_plan(group, tm):
    rows = group.shape[0]
    n_tiles = rows // tm + N_GROUPS
    onehot = (group[:, None] == jnp.arange(N_GROUPS, dtype=jnp.int32)[None, :]).astype(jnp.int32)
    rank = jnp.cumsum(onehot, axis=0) - onehot
    tiles_per = (jnp.sum(onehot, axis=0) + tm - 1) // tm
    tile_start = jnp.cumsum(tiles_per) - tiles_per
    pos = jnp.sum(onehot * (tile_start[None, :] * tm + rank), axis=1).astype(jnp.int32)
    src = jnp.zeros((n_tiles * tm,), jnp.int32).at[pos].set(jnp.arange(rows, dtype=jnp.int32))
    tile = jnp.arange(n_tiles, dtype=jnp.int32)
    tile_group = jnp.clip(jnp.sum((tile[:, None] >= tile_start[None, :]).astype(jnp.int32), axis=1) - 1,
                          0, N_GROUPS - 1).astype(jnp.int32)
    n_used = jnp.sum(tiles_per).astype(jnp.int32).reshape(1)
    return pos, src, tile_group, n_used


def _start_rows(idx_ref, src_hbm, dst_ref, sem, n):
    def issue(r, carry):
        pltpu.make_async_copy(src_hbm.at[pl.ds(idx_ref[0, 0, r], 1)], dst_ref.at[pl.ds(r, 1)], sem).start()
        return carry

    lax.fori_loop(0, n, issue, 0, unroll=True)


def _wait_rows(src_hbm, dst_ref, sem, n):
    pltpu.make_async_copy(src_hbm.at[pl.ds(0, n)], dst_ref, sem).wait()


def _moe_group_kernel(tg_ref, nused_ref, src_ref, src_next_ref, hx_hbm, w1_ref, w3_ref, w2_ref, y_ref,
                      xbuf, sems, *, tm):
    i = pl.program_id(0)
    n_used = nused_ref[0]
    slot = i % 2

    @pl.when(jnp.logical_and(i == 0, n_used > 0))
    def _():
        _start_rows(src_ref, hx_hbm, xbuf.at[0], sems.at[0], tm)

    @pl.when(i + 1 < n_used)
    def _():
        _start_rows(src_next_ref, hx_hbm, xbuf.at[1 - slot], sems.at[1 - slot], tm)

    @pl.when(i < n_used)
    def _():
        _wait_rows(hx_hbm, xbuf.at[slot], sems.at[slot], tm)
        x = xbuf[slot]
        h = x[:, :D_MODEL].astype(BF16)
        comb = x[:, D_MODEL:]
        lane = lax.broadcasted_iota(jnp.int32, (tm, LANES), 1)
        first = tg_ref[i] * EXPERTS_PER_GROUP
        acc = jnp.zeros((tm, D_MODEL), F32)
        for e in range(EXPERTS_PER_GROUP):
            a = jnp.dot(h, w1_ref[0, e], preferred_element_type=F32)
            g = jnp.dot(h, w3_ref[0, e], preferred_element_type=F32)
            ce = jnp.sum(jnp.where(lane == first + e, comb, 0.0), axis=-1, keepdims=True)
            act = _silu(a) * g * ce
            acc = acc + jnp.dot(act.astype(BF16), w2_ref[0, e], preferred_element_type=F32)
        y_ref[...] = acc

    @pl.when(i >= n_used)
    def _():
        y_ref[...] = jnp.zeros_like(y_ref)


def _moe_grouped(hx, src, tile_group, n_used, w1, w3, w2, tm):
    n_tiles = tile_group.shape[0]
    wspec = lambda k, n: pl.BlockSpec((1, EXPERTS_PER_GROUP, k, n), lambda i, tg, nu: (tg[i], 0, 0, 0))
    grouped = lambda w: w.reshape(N_GROUPS, EXPERTS_PER_GROUP, *w.shape[1:])
    src3 = src.reshape(n_tiles, 1, tm)
    return pl.pallas_call(
        functools.partial(_moe_group_kernel, tm=tm),
        grid_spec=pltpu.PrefetchScalarGridSpec(
            num_scalar_prefetch=2,
            grid=(n_tiles,),
            in_specs=[pl.BlockSpec((1, 1, tm), lambda i, tg, nu: (i, 0, 0), memory_space=pltpu.SMEM),
                      pl.BlockSpec((1, 1, tm), lambda i, tg, nu: (jnp.minimum(i + 1, n_tiles - 1), 0, 0),
                                   memory_space=pltpu.SMEM),
                      pl.BlockSpec(memory_space=pl.ANY),
                      wspec(D_MODEL, D_EXPERT), wspec(D_MODEL, D_EXPERT), wspec(D_EXPERT, D_MODEL)],
            out_specs=pl.BlockSpec((tm, D_MODEL), lambda i, tg, nu: (i, 0)),
            scratch_shapes=[pltpu.VMEM((2, tm, D_MODEL + LANES), F32), pltpu.SemaphoreType.DMA((2,))]),
        out_shape=jax.ShapeDtypeStruct((n_tiles * tm, D_MODEL), F32),
        compiler_params=_params("arbitrary"),
        name="experts_grouped",
    )(tile_group, n_used, src3, src3, hx, grouped(w1), grouped(w3), grouped(w2))


def _combine_kernel(pos_ref, pos_next_ref, y_hbm, x_ref, gate_ref, g2_ref, b2_ref, o_ref, ybuf, sems, *, tm, n_tiles):
    i = pl.program_id(0)
    slot = i % 2

    @pl.when(i == 0)
    def _():
        _start_rows(pos_ref, y_hbm, ybuf.at[0], sems.at[0], tm)

    @pl.when(i + 1 < n_tiles)
    def _():
        _start_rows(pos_next_ref, y_hbm, ybuf.at[1 - slot], sems.at[1 - slot], tm)

    _wait_rows(y_hbm, ybuf.at[slot], sems.at[slot], tm)
    y = ALPHA * x_ref[...] + gate_ref[...] * ybuf[slot]
    o_ref[...] = _ln(y) * g2_ref[...] + b2_ref[...]


def _combine(y_sorted, pos, x1, gate2, ln_g, ln_b, tm):
    rows = x1.shape[0]
    n_tiles = rows // tm
    row = pl.BlockSpec((tm, D_MODEL), lambda i: (i, 0))
    const = pl.BlockSpec((1, D_MODEL), lambda i: (0, 0))
    pos3 = pos.reshape(n_tiles, 1, tm)
    return pl.pallas_call(
        functools.partial(_combine_kernel, tm=tm, n_tiles=n_tiles),
        grid=(n_tiles,),
        in_specs=[pl.BlockSpec((1, 1, tm), lambda i: (i, 0, 0), memory_space=pltpu.SMEM),
                  pl.BlockSpec((1, 1, tm), lambda i: (jnp.minimum(i + 1, n_tiles - 1), 0, 0), memory_space=pltpu.SMEM),
                  pl.BlockSpec(memory_space=pl.ANY),
                  row, _row_spec(gate2.shape[0], tm, D_MODEL, rows), const, const],
        out_specs=row,
        out_shape=jax.ShapeDtypeStruct((rows, D_MODEL), F32),
        scratch_shapes=[pltpu.VMEM((2, tm, D_MODEL), F32), pltpu.SemaphoreType.DMA((2,))],
        compiler_params=_params("arbitrary"),
        name="combine",
    )(pos3, pos3, y_sorted, x1, gate2, ln_g.reshape(1, D_MODEL), ln_b.reshape(1, D_MODEL))


def _trunk_layer(x, mods, bsz, seq, pos0, s0, k_past, v_past, p, tiles):
    shift1, scale1, gate1, shift2, scale2, gate2 = mods
    uv, ret, q_c, k_c, v_c, k_bf, v_bf = _inproj(x, shift1, scale1, p["w_in"], p["layer"], tiles["tm_in"])
    a_out, v_rows = _gmlp(uv, p["w_sp"], p["b_sp"], p["ln_v_g"], p["ln_v_b"], min(seq, GMLP_CHUNK))
    b_out, s_new = _retention(ret, s0, pos0, bsz, seq, p["gn_g"], p["gn_b"])
    tq, tk = tiles["sb"]
    c_out = _stick_breaking(q_c, k_bf, v_bf, bsz, seq, pos0, tq, tk, k_past, v_past)
    grouped = "tm_group" in tiles
    post = _post(a_out, b_out, c_out, x, p["w_out"], p["layer"], gate1, p["ln1_g"], p["ln1_b"],
                 shift2, scale2, p["wr"], p["br"], tiles["tm_post"], grouped)
    if grouped:
        x1, _, comb, hx = post
        tm = tiles["tm_group"]
        pos, src, tile_group, n_used = _group_plan(comb[:, N_EXPERTS].astype(jnp.int32), tm)
        y_sorted = _moe_grouped(hx, src, tile_group, n_used, p["w1"], p["w3"], p["w2"], tm)
        y = _combine(y_sorted, pos, x1, gate2, p["ln2_g"], p["ln2_b"], tm)
    else:
        x1, h2, comb = post
        y = _moe(h2, comb, x1, gate2, p["ln2_g"], p["ln2_b"], p["w1"], p["w3"], p["w2"], tiles["tm_moe"])
    return y, v_rows, s_new, k_c, v_c


def kernel(x_prompt, x_sample, cache_sb_k, cache_sb_v, state_ret, c_prompt, c_sample, w_ada, b_ada, w_in, w_out, ln_v_g, ln_v_b, w_spatial, b_spatial, gn_g, gn_b, ln1_g, ln1_b, ln2_g, ln2_b, w_router, b_router, w1, w3, w2):
    bp, tp, _ = x_prompt.shape
    bs, ts, _ = x_sample.shape
    past_len = cache_sb_k.shape[2]
    assert bp == 1

    n_c = bp + bs
    c_rows = -(-n_c // 8) * 8
    c_all = jnp.concatenate([c_prompt, c_sample, jnp.zeros((c_rows - n_c, D_MODEL), F32)], axis=0)
    mods = _adaln(c_all, w_ada, b_ada)

    wr_t = w_router.T
    br_col = b_router.reshape(N_EXPERTS, 1)

    tiles_p = dict(tm_in=256, sb=(256, 256), tm_post=512, tm_group=256)
    tiles_s = dict(tm_in=bs * ts, sb=(ts, 256), tm_post=bs * ts, tm_moe=bs * ts)

    y_p = x_prompt.reshape(bp * tp, D_MODEL)
    y_s = x_sample.reshape(bs * ts, D_MODEL)
    zero_state = jnp.zeros((bp, H_B, HEAD_DIM, HEAD_DIM), F32)
    outs = [[] for _ in range(7)]
    for l in range(DEPTH):
        p = dict(w_in=w_in, w_out=w_out,
                 w_sp=w_spatial[l], b_sp=b_spatial[l], ln_v_g=ln_v_g[l], ln_v_b=ln_v_b[l],
                 gn_g=gn_g[l], gn_b=gn_b[l], ln1_g=ln1_g[l], ln1_b=ln1_b[l],
                 ln2_g=ln2_g[l], ln2_b=ln2_b[l], wr=wr_t, br=br_col,
                 w1=w1[l].astype(BF16), w3=w3[l].astype(BF16), w2=w2[l].astype(BF16), layer=l)
        m = mods[l]
        mods_p = [m[0:1, i * D_MODEL:(i + 1) * D_MODEL] for i in range(6)]
        mods_s = [jnp.repeat(m[bp:bp + bs, i * D_MODEL:(i + 1) * D_MODEL], ts, axis=0) for i in range(6)]
        y_p, _, s_p, k_p, v_p = _trunk_layer(y_p, mods_p, bp, tp, 0, zero_state, None, None, p, tiles_p)
        y_s, g_s, s_s, k_s, v_s = _trunk_layer(
            y_s, mods_s, bs, ts, past_len, state_ret[l],
            cache_sb_k[l].reshape(bs * past_len, W_C), cache_sb_v[l].reshape(bs * past_len, W_C), p, tiles_s)
        outs[0].append(s_p)
        outs[1].append(k_p.reshape(bp, tp, H_C, HEAD_DIM))
        outs[2].append(v_p.reshape(bp, tp, H_C, HEAD_DIM))
        outs[3].append(s_s)
        outs[4].append(k_s.reshape(bs, ts, H_C, HEAD_DIM))
        outs[5].append(v_s.reshape(bs, ts, H_C, HEAD_DIM))
        outs[6].append(g_s.reshape(bs, ts, W_A))
    return (y_p.reshape(bp, tp, D_MODEL), y_s.reshape(bs, ts, D_MODEL)) + tuple(jnp.stack(o) for o in outs)
```

```python
import functools

import numpy as np
import jax
import jax.numpy as jnp
from jax import lax
from jax.experimental import pallas as pl
from jax.experimental.pallas import tpu as pltpu

F32 = jnp.float32
BF16 = jnp.bfloat16
HIGHEST = lax.Precision.HIGHEST

D_MODEL = 1024
DEPTH = 2
HEAD_DIM = 64
W_A = D_MODEL // 4
W_B = 3 * D_MODEL // 8
W_C = D_MODEL - W_A - W_B
H_A = W_A // HEAD_DIM
H_B = W_B // HEAD_DIM
H_C = W_C // HEAD_DIM
GMLP_CHUNK = 128
ROPE_BASE = 10000.0
N_EXPERTS = 16
N_GROUPS = 4
EXPERTS_PER_GROUP = N_EXPERTS // N_GROUPS
D_EXPERT = D_MODEL // 2
ALPHA = (2 * DEPTH) ** 0.25
LN_EPS = 1e-5
D_IN = 2 * W_A + 4 * W_B + 3 * W_C
LANES = 128
VMEM_LIMIT = 48 * 1024 * 1024

NT_DIMS = (((1,), (1,)), ((), ()))
TN_DIMS = (((0,), (0,)), ((), ()))
INV_LN2 = 1.4426950408889634
SB_QSCALE = HEAD_DIM ** -0.5 * INV_LN2
SB_DEAD = -152.0
GMLP_SUB = 4
RET_BLOCK = 256


def _ln(x):
    mu = jnp.mean(x, axis=-1, keepdims=True)
    xc = x - mu
    var = jnp.mean(xc * xc, axis=-1, keepdims=True)
    return xc * lax.rsqrt(var + LN_EPS)


def _silu(x):
    return x * jax.nn.sigmoid(x)


def _params(*sem):
    return pltpu.CompilerParams(dimension_semantics=sem, vmem_limit_bytes=VMEM_LIMIT)


def _row_spec(rows, tm, width, total_rows):
    if rows == 1:
        return pl.BlockSpec((1, width), lambda i: (0, 0))
    assert rows == total_rows
    return pl.BlockSpec((tm, width), lambda i: (i, 0))


def _adaln_kernel(c_ref, w_ref, b_ref, o_ref):
    sc = _silu(c_ref[...])
    o_ref[0] = jnp.dot(sc, w_ref[0], preferred_element_type=F32, precision=HIGHEST) + b_ref[0]


def _adaln(c_all, w_ada, b_ada):
    rows = c_all.shape[0]
    tn = 1536
    return pl.pallas_call(
        _adaln_kernel,
        grid=(DEPTH, 6 * D_MODEL // tn),
        in_specs=[pl.BlockSpec((rows, D_MODEL), lambda l, j: (0, 0)),
                  pl.BlockSpec((1, D_MODEL, tn), lambda l, j: (l, 0, j)),
                  pl.BlockSpec((1, 1, tn), lambda l, j: (l, 0, j))],
        out_specs=pl.BlockSpec((1, rows, tn), lambda l, j: (l, 0, j)),
        out_shape=jax.ShapeDtypeStruct((DEPTH, rows, 6 * D_MODEL), F32),
        compiler_params=_params("parallel", "parallel"),
        name="adaln",
    )(c_all, w_ada, b_ada.reshape(DEPTH, 1, 6 * D_MODEL))


def _inproj_kernel(x_ref, shift_ref, scale_ref, w_ref, uv_ref, ret_ref, q_ref, k_ref, v_ref, kb_ref, vb_ref, wb_ref):
    @pl.when(pl.program_id(0) == 0)
    def _():
        wb_ref[...] = w_ref[0].astype(BF16)

    h = _ln(x_ref[...]) * (1.0 + scale_ref[...]) + shift_ref[...]
    r = jnp.dot(h.astype(BF16), wb_ref[...], preferred_element_type=F32)
    c0 = 2 * W_A
    c1 = c0 + 4 * W_B
    uv_ref[...] = r[:, :c0]
    ret_ref[...] = r[:, c0:c1]
    q_ref[...] = r[:, c1:c1 + W_C]
    k = r[:, c1 + W_C:c1 + 2 * W_C]
    v = r[:, c1 + 2 * W_C:]
    k_ref[...] = k
    v_ref[...] = v
    kb_ref[...] = k.astype(BF16)
    vb_ref[...] = v.astype(BF16)


def _inproj(x, shift, scale, w_in, layer, tm):
    rows = x.shape[0]
    widths = (2 * W_A, 4 * W_B, W_C, W_C, W_C, W_C, W_C)
    dtypes = (F32,) * 5 + (BF16,) * 2
    return pl.pallas_call(
        _inproj_kernel,
        grid=(rows // tm,),
        in_specs=[pl.BlockSpec((tm, D_MODEL), lambda i: (i, 0)),
                  _row_spec(shift.shape[0], tm, D_MODEL, rows),
                  _row_spec(scale.shape[0], tm, D_MODEL, rows),
                  pl.BlockSpec((1, D_MODEL, D_IN), lambda i: (layer, 0, 0), pipeline_mode=pl.Buffered(1))],
        out_specs=[pl.BlockSpec((tm, w), lambda i: (i, 0)) for w in widths],
        out_shape=[jax.ShapeDtypeStruct((rows, w), dt) for w, dt in zip(widths, dtypes)],
        scratch_shapes=[pltpu.VMEM((D_MODEL, D_IN), BF16)],
        compiler_params=_params("arbitrary"),
        name="inproj",
    )(x, shift, scale, w_in)


def _gmlp_kernel(uv_ref, wsp_ref, bias_ref, g_ref, b_ref, a_ref, vn_ref, *, chunk, n_sub):
    uv = uv_ref[...]
    u = jax.nn.gelu(uv[:, :W_A])
    v = _ln(jax.nn.gelu(uv[:, W_A:])) * g_ref[...] + b_ref[...]
    vn_ref[...] = v
    row = lax.broadcasted_iota(jnp.int32, (chunk, chunk), 0)
    col = lax.broadcasted_iota(jnp.int32, (chunk, chunk), 1)
    lane_head = lax.broadcasted_iota(jnp.int32, (chunk, W_A), 1) // HEAD_DIM
    w = [jnp.where(col <= row, wsp_ref[h], 0.0).astype(BF16) for h in range(H_A)]
    for c in range(n_sub):
        rows = slice(c * chunk, (c + 1) * chunk)
        vc = v[rows]
        mixed = bias_ref[...]
        for h in range(H_A):
            vh = jnp.where(lane_head == h, vc, 0.0).astype(BF16)
            mixed = mixed + jnp.dot(w[h], vh, preferred_element_type=F32)
        a_ref[rows, :] = u[rows] * mixed


def _gmlp(uv, w_sp, b_sp, ln_g, ln_b, chunk):
    rows = uv.shape[0]
    n_sub = min(GMLP_SUB, rows // chunk)
    tm = n_sub * chunk
    wsp = w_sp[:, :chunk, :chunk]
    bias = jnp.repeat(b_sp[:, :chunk].T, HEAD_DIM, axis=1)
    return pl.pallas_call(
        functools.partial(_gmlp_kernel, chunk=chunk, n_sub=n_sub),
        grid=(rows // tm,),
        in_specs=[pl.BlockSpec((tm, 2 * W_A), lambda i: (i, 0)),
                  pl.BlockSpec((H_A, chunk, chunk), lambda i: (0, 0, 0)),
                  pl.BlockSpec((chunk, W_A), lambda i: (0, 0)),
                  pl.BlockSpec((1, W_A), lambda i: (0, 0)),
                  pl.BlockSpec((1, W_A), lambda i: (0, 0))],
        out_specs=[pl.BlockSpec((tm, W_A), lambda i: (i, 0)),
                   pl.BlockSpec((tm, W_A), lambda i: (i, 0))],
        out_shape=[jax.ShapeDtypeStruct((rows, W_A), F32),
                   jax.ShapeDtypeStruct((rows, W_A), F32)],
        compiler_params=_params("parallel"),
        name="gmlp",
    )(uv, wsp, bias, ln_g.reshape(1, W_A), ln_b.reshape(1, W_A))


def _rope(x, cos, sin):
    lane = lax.broadcasted_iota(jnp.int32, (x.shape[0], LANES), 1)
    first_half = (lane & (HEAD_DIM // 2)) == 0
    parts = []
    for c in range(x.shape[1] // LANES):
        xc = x[:, c * LANES:(c + 1) * LANES]
        rot = jnp.where(first_half,
                        pltpu.roll(xc, LANES - HEAD_DIM // 2, 1),
                        pltpu.roll(xc, HEAD_DIM // 2, 1))
        parts.append(xc * cos + rot * sin)
    return jnp.concatenate(parts, axis=1)


def _ret_kernel(r_ref, cos_ref, sin_ref, qdec_ref, kdec_ref, dec_ref, blk_ref, s0_ref,
                gng_ref, gnb_ref, o_ref, sout_ref, s_scr, o_scr, *, n_blocks):
    n = pl.program_id(1)

    @pl.when(n == 0)
    def _():
        s_scr[...] = s0_ref[0]

    r = r_ref[...]
    cos = cos_ref[...]
    sin = sin_ref[...]
    qr = _rope(r[:, :W_B], cos, sin)
    kr = _rope(r[:, W_B:2 * W_B], cos, sin) * (HEAD_DIM ** -0.5)
    vb = r[:, 2 * W_B:3 * W_B].astype(BF16)
    gate = r[:, 3 * W_B:]
    qb = qr.astype(BF16)
    kb = kr.astype(BF16)
    qdb = (qr * qdec_ref[...]).astype(BF16)
    kdb = (kr * kdec_ref[...]).astype(BF16)
    for h in range(H_B):
        sl = slice(h * HEAD_DIM, (h + 1) * HEAD_DIM)
        scores = lax.dot_general(qb[:, sl], kb[:, sl], NT_DIMS, preferred_element_type=F32) * dec_ref[h]
        s_h = s_scr[h]
        o_h = (jnp.dot(scores.astype(BF16), vb[:, sl], preferred_element_type=F32)
               + jnp.dot(qdb[:, sl], s_h.astype(BF16), preferred_element_type=F32))
        s_scr[h] = s_h * blk_ref[h] + lax.dot_general(kdb[:, sl], vb[:, sl], TN_DIMS,
                                                      preferred_element_type=F32)
        o_scr[:, sl] = _ln(o_h)
    o_ref[...] = (o_scr[...] * gng_ref[...] + gnb_ref[...]) * _silu(gate)

    @pl.when(n == n_blocks - 1)
    def _():
        sout_ref[0] = s_scr[...]


def _retention(ret, s0, pos0, bsz, seq, gn_g, gn_b):
    L = min(seq, RET_BLOCK)
    n_blocks = seq // L
    half = HEAD_DIM // 2
    inv = ROPE_BASE ** (-jnp.arange(half, dtype=F32) / half)
    ang = (pos0 + jnp.arange(seq)).astype(F32)[:, None] * inv[None, :]
    cos, sin = jnp.cos(ang), jnp.sin(ang)
    cos_t = jnp.tile(jnp.concatenate([cos, cos], axis=1), (1, LANES // HEAD_DIM))
    sin_t = jnp.tile(jnp.concatenate([-sin, sin], axis=1), (1, LANES // HEAD_DIM))
    log_g = jnp.log1p(-jnp.exp2(-5.0 - jnp.arange(H_B, dtype=F32)))
    idx = jnp.arange(L, dtype=F32)
    diff = idx[:, None] - idx[None, :]
    decay = jnp.where(diff >= 0, jnp.exp(diff[None] * log_g[:, None, None]), 0.0)
    q_decay = jnp.exp((idx[None, :] + 1.0) * log_g[:, None])
    k_decay = jnp.exp((L - 1.0 - idx[None, :]) * log_g[:, None])
    blk_decay = jnp.exp(L * log_g)
    qdec = jnp.repeat(q_decay.T, HEAD_DIM, axis=1)
    kdec = jnp.repeat(k_decay.T, HEAD_DIM, axis=1)
    blk = jnp.broadcast_to(blk_decay[:, None, None], (H_B, HEAD_DIM, HEAD_DIM))
    const2 = lambda b, n: (0, 0)
    const3 = lambda b, n: (0, 0, 0)
    return pl.pallas_call(
        functools.partial(_ret_kernel, n_blocks=n_blocks),
        grid=(bsz, n_blocks),
        in_specs=[pl.BlockSpec((L, 4 * W_B), lambda b, n: (b * n_blocks + n, 0)),
                  pl.BlockSpec((L, LANES), lambda b, n: (n, 0)),
                  pl.BlockSpec((L, LANES), lambda b, n: (n, 0)),
                  pl.BlockSpec((L, W_B), const2),
                  pl.BlockSpec((L, W_B), const2),
                  pl.BlockSpec((H_B, L, L), const3),
                  pl.BlockSpec((H_B, HEAD_DIM, HEAD_DIM), const3),
                  pl.BlockSpec((1, H_B, HEAD_DIM, HEAD_DIM), lambda b, n: (b, 0, 0, 0)),
                  pl.BlockSpec((1, W_B), const2),
                  pl.BlockSpec((1, W_B), const2)],
        out_specs=[pl.BlockSpec((L, W_B), lambda b, n: (b * n_blocks + n, 0)),
                   pl.BlockSpec((1, H_B, HEAD_DIM, HEAD_DIM), lambda b, n: (b, 0, 0, 0))],
        out_shape=[jax.ShapeDtypeStruct((bsz * seq, W_B), F32),
                   jax.ShapeDtypeStruct((bsz, H_B, HEAD_DIM, HEAD_DIM), F32)],
        scratch_shapes=[pltpu.VMEM((H_B, HEAD_DIM, HEAD_DIM), F32),
                        pltpu.VMEM((L, W_B), F32)],
        compiler_params=_params("parallel", "arbitrary"),
        name="retention",
    )(ret, cos_t, sin_t, qdec, kdec, decay, blk, s0, gn_g.reshape(1, W_B), gn_b.reshape(1, W_B))


def _sb_last_block(i, tq, tk, q_pos0):
    return (q_pos0 + (i + 1) * tq - 2) // tk


def _sb_block(qm_ref, k_ref, v_ref, u2_ref, acc_ref, carry_ref, causal):
    tk = k_ref.shape[0]
    half = lax.broadcasted_iota(jnp.int32, (tk, LANES), 1) // HEAD_DIM
    u2 = u2_ref[...]
    kp = [k_ref[:, p * LANES:(p + 1) * LANES] for p in range(H_C // 2)]

    def scores(h):
        z = lax.dot_general(qm_ref[h], kp[h // 2], NT_DIMS, preferred_element_type=F32)
        neg_abs = pltpu.bitcast(pltpu.bitcast(z, jnp.uint32) | jnp.uint32(0x80000000), F32)
        ls_pos = jnp.minimum(z, 0.0) - jnp.log(1.0 + jnp.exp2(neg_abs)) * INV_LN2
        log_stay = ls_pos - z
        if causal is not None:
            log_stay = jnp.where(causal, log_stay, 0.0)
        hi = log_stay.astype(BF16)
        lo = (log_stay - hi.astype(F32)).astype(BF16)
        return ls_pos, log_stay[:, :1], jnp.concatenate([hi, lo], axis=1)

    def cumsum(hi_lo):
        return jnp.dot(hi_lo, u2, preferred_element_type=F32)

    def weigh(h, ls_pos, first_col, excl):
        carry = carry_ref[h]
        att = jnp.exp2(ls_pos + excl + carry)
        if causal is not None:
            att = jnp.where(causal, att, 0.0)
        vp = v_ref[:, (h // 2) * LANES:(h // 2 + 1) * LANES]
        vp = jnp.where(half == h % 2, vp, jnp.zeros_like(vp))
        carry_ref[h] = carry + (excl[:, :1] + first_col)
        return jnp.dot(att.astype(BF16), vp, preferred_element_type=F32)

    stage_a, stage_b, outs = {}, {}, {}
    for step in range(H_C + 2):
        if step - 2 >= 0:
            h = step - 2
            outs[h] = weigh(h, stage_a[h][0], stage_a[h][1], stage_b.pop(h))
            del stage_a[h]
            if h % 2 == 1:
                p = h // 2
                acc_ref[:, p * LANES:(p + 1) * LANES] += outs.pop(h - 1) + outs.pop(h)
        if 0 <= step - 1 < H_C:
            stage_b[step - 1] = cumsum(stage_a[step - 1][2])
        if step < H_C:
            stage_a[step] = scores(step)


def _sb_kernel(q_ref, k_ref, v_ref, u2_ref, o_ref, qm_ref, acc_ref, carry_ref, *, tq, tk, q_pos0):
    i = pl.program_id(1)
    last = _sb_last_block(i, tq, tk, q_pos0)
    q0 = q_pos0 + i * tq

    acc_ref[...] = jnp.zeros_like(acc_ref)
    carry_ref[...] = jnp.zeros_like(carry_ref)
    half = lax.broadcasted_iota(jnp.int32, (tq, LANES), 1) // HEAD_DIM
    for h in range(H_C):
        qp = q_ref[:, (h // 2) * LANES:(h // 2 + 1) * LANES] * SB_QSCALE
        qm_ref[h] = jnp.where(half == h % 2, qp, 0.0).astype(BF16)

    def key_block(state):
        j, _ = state
        k0 = pl.multiple_of((last - j) * tk, tk)
        k_blk = k_ref.at[pl.ds(k0, tk), :]
        v_blk = v_ref.at[pl.ds(k0, tk), :]
        on_diagonal = k0 + tk > q0

        @pl.when(on_diagonal)
        def _():
            kpos = k0 + lax.broadcasted_iota(jnp.int32, (tq, tk), 1)
            qpos = q0 + lax.broadcasted_iota(jnp.int32, (tq, tk), 0)
            _sb_block(qm_ref, k_blk, v_blk, u2_ref, acc_ref, carry_ref, kpos < qpos)

        @pl.when(jnp.logical_not(on_diagonal))
        def _():
            _sb_block(qm_ref, k_blk, v_blk, u2_ref, acc_ref, carry_ref, None)

        dead = jnp.max(carry_ref[...]) < SB_DEAD
        return j + 1, dead.astype(jnp.int32)

    lax.while_loop(lambda state: jnp.logical_and(state[0] <= last, state[1] == 0),
                   key_block, (jnp.int32(0), jnp.int32(0)))
    o_ref[...] = acc_ref[...]


def _stick_breaking(q, k, v, bsz, seq, kv_len, q_pos0, tq, tk):
    nq = seq // tq
    assert (q_pos0 + seq - 2) // tk + 1 <= kv_len // tk
    tri = np.tril(np.ones((tk, tk), np.float32), -1)
    u2 = jnp.asarray(np.concatenate([tri, tri], axis=0), dtype=BF16)
    resident = dict(pipeline_mode=pl.Buffered(1))
    return pl.pallas_call(
        functools.partial(_sb_kernel, tq=tq, tk=tk, q_pos0=q_pos0),
        grid=(bsz, nq),
        in_specs=[pl.BlockSpec((tq, W_C), lambda b, i: (b * nq + i, 0)),
                  pl.BlockSpec((kv_len, W_C), lambda b, i: (b, 0), **resident),
                  pl.BlockSpec((kv_len, W_C), lambda b, i: (b, 0), **resident),
                  pl.BlockSpec((2 * tk, tk), lambda b, i: (0, 0), **resident)],
        out_specs=pl.BlockSpec((tq, W_C), lambda b, i: (b * nq + i, 0)),
        out_shape=jax.ShapeDtypeStruct((bsz * seq, W_C), F32),
        scratch_shapes=[pltpu.VMEM((H_C, tq, LANES), BF16),
                        pltpu.VMEM((tq, W_C), F32),
                        pltpu.VMEM((H_C, tq, 1), F32)],
        compiler_params=_params("parallel", "arbitrary"),
        name="stick_breaking",
    )(q, k, v, u2)


def _sb_scores(z, causal=None):
    neg_abs = pltpu.bitcast(pltpu.bitcast(z, jnp.uint32) | jnp.uint32(0x80000000), F32)
    ls_pos = jnp.minimum(z, 0.0) - jnp.log(1.0 + jnp.exp2(neg_abs)) * INV_LN2
    log_stay = ls_pos - z
    if causal is not None:
        log_stay = jnp.where(causal, log_stay, 0.0)
    hi = log_stay.astype(BF16)
    lo = (log_stay - hi.astype(F32)).astype(BF16)
    return ls_pos, log_stay, jnp.concatenate([hi, lo], axis=1)


def _sb_step_kernel(q_ref, kt_ref, vt_ref, kn_ref, vn_ref, u2_ref, un_ref, o_ref, acc_ref, carry_ref, *, tk):
    seq = q_ref.shape[0]
    past = kt_ref.shape[-1]
    qs = (q_ref[...] * SB_QSCALE).astype(BF16)
    row = lax.broadcasted_iota(jnp.int32, (seq, seq), 0)
    col = lax.broadcasted_iota(jnp.int32, (seq, seq), 1)
    causal = col < row
    un = un_ref[...]
    for h in range(H_C):
        sl = slice(h * HEAD_DIM, (h + 1) * HEAD_DIM)
        z = lax.dot_general(qs[:, sl], kn_ref[:, sl], NT_DIMS, preferred_element_type=F32)
        ls_pos, log_stay, hi_lo = _sb_scores(z, causal)
        excl = jnp.dot(hi_lo, un, preferred_element_type=F32)
        att = jnp.where(causal, jnp.exp2(ls_pos + excl), 0.0)
        acc_ref[h] = jnp.dot(att.astype(BF16), vn_ref[:, sl], preferred_element_type=F32)
        carry_ref[h] = excl[:, :1] + log_stay[:, :1]

    u2 = u2_ref[...]

    def key_block(state):
        j, _ = state
        k0 = pl.multiple_of(past - (j + 1) * tk, tk)
        for h in range(H_C):
            kt = kt_ref[0, 0, h, :, pl.ds(k0, tk)].astype(BF16)
            vt = vt_ref[0, 0, h, :, pl.ds(k0, tk)].astype(BF16)
            z = jnp.dot(qs[:, h * HEAD_DIM:(h + 1) * HEAD_DIM], kt, preferred_element_type=F32)
            ls_pos, log_stay, hi_lo = _sb_scores(z)
            excl = jnp.dot(hi_lo, u2, preferred_element_type=F32)
            carry = carry_ref[h]
            att = jnp.exp2(ls_pos + excl + carry)
            acc_ref[h] += lax.dot_general(att.astype(BF16), vt, NT_DIMS, preferred_element_type=F32)
            carry_ref[h] = carry + (excl[:, :1] + log_stay[:, :1])
        dead = jnp.max(carry_ref[...]) < SB_DEAD
        return j + 1, dead.astype(jnp.int32)

    lax.while_loop(lambda state: jnp.logical_and(state[0] < past // tk, state[1] == 0),
                   key_block, (jnp.int32(0), jnp.int32(0)))
    for h in range(H_C):
        o_ref[:, h * HEAD_DIM:(h + 1) * HEAD_DIM] = acc_ref[h]


def _stick_breaking_step(q, k_new, v_new, cache_kt, cache_vt, layer, bsz, seq, tk):
    past = cache_kt.shape[-1]
    assert past % tk == 0
    tri = lambda n: np.tril(np.ones((n, n), np.float32), -1)
    stacked = lambda n: jnp.asarray(np.concatenate([tri(n), tri(n)], axis=0), dtype=BF16)
    rows = pl.BlockSpec((seq, W_C), lambda b: (b, 0))
    cache = pl.BlockSpec((1, 1, H_C, HEAD_DIM, past), lambda b: (layer, b, 0, 0, 0))
    return pl.pallas_call(
        functools.partial(_sb_step_kernel, tk=tk),
        grid=(bsz,),
        in_specs=[rows, cache, cache, rows, rows,
                  pl.BlockSpec((2 * tk, tk), lambda b: (0, 0)), pl.BlockSpec((2 * seq, seq), lambda b: (0, 0))],
        out_specs=rows,
        out_shape=jax.ShapeDtypeStruct((bsz * seq, W_C), F32),
        scratch_shapes=[pltpu.VMEM((H_C, seq, HEAD_DIM), F32), pltpu.VMEM((H_C, seq, 1), F32)],
        compiler_params=_params("parallel"),
        name="stick_breaking_step",
    )(q, cache_kt, cache_vt, k_new, v_new, stacked(tk), stacked(seq))


def _route(sel, s):
    g_scores = []
    for g in range(N_GROUPS):
        a, b, c, d = sel[EXPERTS_PER_GROUP * g:EXPERTS_PER_GROUP * (g + 1)]
        ab_hi, ab_lo = jnp.maximum(a, b), jnp.minimum(a, b)
        cd_hi, cd_lo = jnp.maximum(c, d), jnp.minimum(c, d)
        top1 = jnp.maximum(ab_hi, cd_hi)
        top2 = jnp.maximum(jnp.minimum(ab_hi, cd_hi), jnp.maximum(ab_lo, cd_lo))
        g_scores.append(top1 + top2)
    best = g_scores[0]
    gi = jnp.zeros(best.shape, jnp.int32)
    for g in range(1, N_GROUPS):
        upd = g_scores[g] > best
        gi = jnp.where(upd, g, gi)
        best = jnp.where(upd, g_scores[g], best)

    def pick_group(rows, l):
        out = rows[(N_GROUPS - 1) * EXPERTS_PER_GROUP + l]
        for g in range(N_GROUPS - 2, -1, -1):
            out = jnp.where(gi == g, rows[g * EXPERTS_PER_GROUP + l], out)
        return out

    ig = [pick_group(sel, l) for l in range(EXPERTS_PER_GROUP)]
    sg = [pick_group(s, l) for l in range(EXPERTS_PER_GROUP)]
    b1 = ig[0]
    i1 = jnp.zeros(best.shape, jnp.int32)
    for l in range(1, EXPERTS_PER_GROUP):
        upd = ig[l] > b1
        i1 = jnp.where(upd, l, i1)
        b1 = jnp.where(upd, ig[l], b1)
    b2 = jnp.full(best.shape, -jnp.inf, F32)
    i2 = jnp.zeros(best.shape, jnp.int32)
    for l in range(EXPERTS_PER_GROUP):
        upd = jnp.logical_and(i1 != l, ig[l] > b2)
        i2 = jnp.where(upd, l, i2)
        b2 = jnp.where(upd, ig[l], b2)

    def pick_local(idx):
        out = sg[EXPERTS_PER_GROUP - 1]
        for l in range(EXPERTS_PER_GROUP - 2, -1, -1):
            out = jnp.where(idx == l, sg[l], out)
        return out

    w1 = pick_local(i1)
    w2 = pick_local(i2)
    tot = w1 + w2
    return gi * EXPERTS_PER_GROUP + i1, gi * EXPERTS_PER_GROUP + i2, w1 / tot, w2 / tot


def _post_kernel(a_ref, b_ref, c_ref, x_ref, wo_ref, gate_ref, g1_ref, b1_ref, sh2_ref, sc2_ref,
                 wrt_ref, br_ref, x1_ref, h2_ref, comb_ref, *rest, tm, with_hx):
    hx_ref, wob_ref = rest if with_hx else (None,) + rest
    @pl.when(pl.program_id(0) == 0)
    def _():
        wob_ref[...] = wo_ref[0].astype(BF16)

    proj = (jnp.dot(a_ref[...].astype(BF16), wob_ref[:W_A], preferred_element_type=F32)
            + jnp.dot(b_ref[...].astype(BF16), wob_ref[W_A:W_A + W_B], preferred_element_type=F32)
            + jnp.dot(c_ref[...].astype(BF16), wob_ref[W_A + W_B:], preferred_element_type=F32))
    x1 = _ln(ALPHA * x_ref[...] + gate_ref[...] * proj) * g1_ref[...] + b1_ref[...]
    x1_ref[...] = x1
    h2 = _ln(x1) * (1.0 + sc2_ref[...]) + sh2_ref[...]
    h2_ref[...] = h2.astype(BF16)
    logits_t = lax.dot_general(wrt_ref[...], h2, NT_DIMS, preferred_element_type=F32, precision=HIGHEST)
    s_t = jax.nn.sigmoid(logits_t)
    sel_t = s_t + br_ref[...]
    s = [s_t[e:e + 1, :] for e in range(N_EXPERTS)]
    sel = [sel_t[e:e + 1, :] for e in range(N_EXPERTS)]
    e1, e2, w1, w2 = _route(sel, s)
    expert = lax.broadcasted_iota(jnp.int32, (LANES, tm), 0)
    comb_t = jnp.where(expert == e1, w1, jnp.where(expert == e2, w2, 0.0))
    group = (e1 // EXPERTS_PER_GROUP).astype(F32)
    comb_t = jnp.where(expert == N_EXPERTS, group, comb_t)
    comb_ref[...] = comb_t.T
    if hx_ref is not None:
        hx_ref[:, :D_MODEL] = h2
        hx_ref[:, D_MODEL:] = comb_t.T


def _post(a, b, c, x, w_out, layer, gate1, ln_g, ln_b, shift2, scale2, wr_t, br_col, tm, with_hx):
    rows = x.shape[0]
    extra_spec = [pl.BlockSpec((tm, D_MODEL + LANES), lambda i: (i, 0))] if with_hx else []
    extra_shape = [jax.ShapeDtypeStruct((rows, D_MODEL + LANES), F32)] if with_hx else []
    row = lambda w: pl.BlockSpec((tm, w), lambda i: (i, 0))
    const = lambda r, w: pl.BlockSpec((r, w), lambda i: (0, 0))
    return pl.pallas_call(
        functools.partial(_post_kernel, tm=tm, with_hx=with_hx),
        grid=(rows // tm,),
        in_specs=[row(W_A), row(W_B), row(W_C), row(D_MODEL),
                  pl.BlockSpec((1, D_MODEL, D_MODEL), lambda i: (layer, 0, 0), pipeline_mode=pl.Buffered(1)),
                  _row_spec(gate1.shape[0], tm, D_MODEL, rows),
                  const(1, D_MODEL), const(1, D_MODEL),
                  _row_spec(shift2.shape[0], tm, D_MODEL, rows),
                  _row_spec(scale2.shape[0], tm, D_MODEL, rows),
                  const(N_EXPERTS, D_MODEL), const(N_EXPERTS, 1)],
        out_specs=[row(D_MODEL), row(D_MODEL), row(LANES)] + extra_spec,
        out_shape=[jax.ShapeDtypeStruct((rows, D_MODEL), F32),
                   jax.ShapeDtypeStruct((rows, D_MODEL), BF16),
                   jax.ShapeDtypeStruct((rows, LANES), F32)] + extra_shape,
        scratch_shapes=[pltpu.VMEM((D_MODEL, D_MODEL), BF16)],
        compiler_params=_params("arbitrary"),
        name="post_mix",
    )(a, b, c, x, w_out, gate1, ln_g.reshape(1, D_MODEL), ln_b.reshape(1, D_MODEL),
      shift2, scale2, wr_t, br_col)


def _moe_kernel(h_ref, comb_ref, x_ref, gate_ref, g2_ref, b2_ref, w1_ref, w3_ref, w2_ref,
                o_ref, acc_ref, *, tm):
    e = pl.program_id(1)

    @pl.when(e == 0)
    def _():
        acc_ref[...] = jnp.zeros_like(acc_ref)

    h = h_ref[...]
    a = jnp.dot(h, w1_ref[0], preferred_element_type=F32)
    g = jnp.dot(h, w3_ref[0], preferred_element_type=F32)
    lane = lax.broadcasted_iota(jnp.int32, (tm, LANES), 1)
    ce = jnp.sum(jnp.where(lane == e, comb_ref[...], 0.0), axis=-1, keepdims=True)
    act = _silu(a) * g * ce
    acc_ref[...] += jnp.dot(act.astype(BF16), w2_ref[0], preferred_element_type=F32)

    @pl.when(e == N_EXPERTS - 1)
    def _():
        y = ALPHA * x_ref[...] + gate_ref[...] * acc_ref[...]
        o_ref[...] = _ln(y) * g2_ref[...] + b2_ref[...]


def _moe(h2, comb, x1, gate2, ln_g, ln_b, w1, w3, w2, tm):
    rows = x1.shape[0]
    row = lambda w: pl.BlockSpec((tm, w), lambda i, e: (i, 0))
    const = pl.BlockSpec((1, D_MODEL), lambda i, e: (0, 0))
    gate_spec = (pl.BlockSpec((1, D_MODEL), lambda i, e: (0, 0)) if gate2.shape[0] == 1
                 else row(D_MODEL))
    return pl.pallas_call(
        functools.partial(_moe_kernel, tm=tm),
        grid=(rows // tm, N_EXPERTS),
        in_specs=[row(D_MODEL), row(LANES), row(D_MODEL), gate_spec, const, const,
                  pl.BlockSpec((1, D_MODEL, D_EXPERT), lambda i, e: (e, 0, 0)),
                  pl.BlockSpec((1, D_MODEL, D_EXPERT), lambda i, e: (e, 0, 0)),
                  pl.BlockSpec((1, D_EXPERT, D_MODEL), lambda i, e: (e, 0, 0))],
        out_specs=row(D_MODEL),
        out_shape=jax.ShapeDtypeStruct((rows, D_MODEL), F32),
        scratch_shapes=[pltpu.VMEM((tm, D_MODEL), F32)],
        compiler_params=_params("parallel", "arbitrary"),
        name="experts",
    )(h2, comb, x1, gate2, ln_g.reshape(1, D_MODEL), ln_b.reshape(1, D_MODEL), w1, w3, w2)


def _group_plan(group, tm):
    rows = group.shape[0]
    n_tiles = rows // tm + N_GROUPS
    onehot = (group[:, None] == jnp.arange(N_GROUPS, dtype=jnp.int32)[None, :]).astype(jnp.int32)
    rank = jnp.cumsum(onehot, axis=0) - onehot
    tiles_per = (jnp.sum(onehot, axis=0) + tm - 1) // tm
    tile_start = jnp.cumsum(tiles_per) - tiles_per
    pos = jnp.sum(onehot * (tile_start[None, :] * tm + rank), axis=1).astype(jnp.int32)
    src = jnp.zeros((n_tiles * tm,), jnp.int32).at[pos].set(jnp.arange(rows, dtype=jnp.int32))
    tile = jnp.arange(n_tiles, dtype=jnp.int32)
    tile_group = jnp.clip(jnp.sum((tile[:, None] >= tile_start[None, :]).astype(jnp.int32), axis=1) - 1,
                          0, N_GROUPS - 1).astype(jnp.int32)
    n_used = jnp.sum(tiles_per).astype(jnp.int32).reshape(1)
    return pos, src, tile_group, n_used


def _start_rows(idx_ref, src_hbm, dst_ref, sem, n):
    def issue(r, carry):
        pltpu.make_async_copy(src_hbm.at[pl.ds(idx_ref[0, 0, r], 1)], dst_ref.at[pl.ds(r, 1)], sem).start()
        return carry

    lax.fori_loop(0, n, issue, 0, unroll=True)


def _wait_rows(src_hbm, dst_ref, sem, n):
    pltpu.make_async_copy(src_hbm.at[pl.ds(0, n)], dst_ref, sem).wait()


def _moe_group_kernel(tg_ref, nused_ref, src_ref, src_next_ref, hx_hbm, w1_ref, w3_ref, w2_ref, y_ref,
                      xbuf, sems, *, tm):
    i = pl.program_id(0)
    n_used = nused_ref[0]
    slot = i % 2

    @pl.when(jnp.logical_and(i == 0, n_used > 0))
    def _():
        _start_rows(src_ref, hx_hbm, xbuf.at[0], sems.at[0], tm)

    @pl.when(i + 1 < n_used)
    def _():
        _start_rows(src_next_ref, hx_hbm, xbuf.at[1 - slot], sems.at[1 - slot], tm)

    @pl.when(i < n_used)
    def _():
        _wait_rows(hx_hbm, xbuf.at[slot], sems.at[slot], tm)
        x = xbuf[slot]
        h = x[:, :D_MODEL].astype(BF16)
        comb = x[:, D_MODEL:]
        lane = lax.broadcasted_iota(jnp.int32, (tm, LANES), 1)
        first = tg_ref[i] * EXPERTS_PER_GROUP
        acc = jnp.zeros((tm, D_MODEL), F32)
        for e in range(EXPERTS_PER_GROUP):
            a = jnp.dot(h, w1_ref[0, e], preferred_element_type=F32)
            g = jnp.dot(h, w3_ref[0, e], preferred_element_type=F32)
            ce = jnp.sum(jnp.where(lane == first + e, comb, 0.0), axis=-1, keepdims=True)
            act = _silu(a) * g * ce
            acc = acc + jnp.dot(act.astype(BF16), w2_ref[0, e], preferred_element_type=F32)
        y_ref[...] = acc

    @pl.when(i >= n_used)
    def _():
        y_ref[...] = jnp.zeros_like(y_ref)


def _moe_grouped(hx, src, tile_group, n_used, w1, w3, w2, tm):
    n_tiles = tile_group.shape[0]
    wspec = lambda k, n: pl.BlockSpec((1, EXPERTS_PER_GROUP, k, n), lambda i, tg, nu: (tg[i], 0, 0, 0))
    grouped = lambda w: w.reshape(N_GROUPS, EXPERTS_PER_GROUP, *w.shape[1:])
    src3 = src.reshape(n_tiles, 1, tm)
    return pl.pallas_call(
        functools.partial(_moe_group_kernel, tm=tm),
        grid_spec=pltpu.PrefetchScalarGridSpec(
            num_scalar_prefetch=2,
            grid=(n_tiles,),
            in_specs=[pl.BlockSpec((1, 1, tm), lambda i, tg, nu: (i, 0, 0), memory_space=pltpu.SMEM),
                      pl.BlockSpec((1, 1, tm), lambda i, tg, nu: (jnp.minimum(i + 1, n_tiles - 1), 0, 0),
                                   memory_space=pltpu.SMEM),
                      pl.BlockSpec(memory_space=pl.ANY),
                      wspec(D_MODEL, D_EXPERT), wspec(D_MODEL, D_EXPERT), wspec(D_EXPERT, D_MODEL)],
            out_specs=pl.BlockSpec((tm, D_MODEL), lambda i, tg, nu: (i, 0)),
            scratch_shapes=[pltpu.VMEM((2, tm, D_MODEL + LANES), F32), pltpu.SemaphoreType.DMA((2,))]),
        out_shape=jax.ShapeDtypeStruct((n_tiles * tm, D_MODEL), F32),
        compiler_params=_params("arbitrary"),
        name="experts_grouped",
    )(tile_group, n_used, src3, src3, hx, grouped(w1), grouped(w3), grouped(w2))


def _combine_kernel(pos_ref, pos_next_ref, y_hbm, x_ref, gate_ref, g2_ref, b2_ref, o_ref, ybuf, sems, *, tm, n_tiles):
    i = pl.program_id(0)
    slot = i % 2

    @pl.when(i == 0)
    def _():
        _start_rows(pos_ref, y_hbm, ybuf.at[0], sems.at[0], tm)

    @pl.when(i + 1 < n_tiles)
    def _():
        _start_rows(pos_next_ref, y_hbm, ybuf.at[1 - slot], sems.at[1 - slot], tm)

    _wait_rows(y_hbm, ybuf.at[slot], sems.at[slot], tm)
    y = ALPHA * x_ref[...] + gate_ref[...] * ybuf[slot]
    o_ref[...] = _ln(y) * g2_ref[...] + b2_ref[...]


def _combine(y_sorted, pos, x1, gate2, ln_g, ln_b, tm):
    rows = x1.shape[0]
    n_tiles = rows // tm
    row = pl.BlockSpec((tm, D_MODEL), lambda i: (i, 0))
    const = pl.BlockSpec((1, D_MODEL), lambda i: (0, 0))
    pos3 = pos.reshape(n_tiles, 1, tm)
    return pl.pallas_call(
        functools.partial(_combine_kernel, tm=tm, n_tiles=n_tiles),
        grid=(n_tiles,),
        in_specs=[pl.BlockSpec((1, 1, tm), lambda i: (i, 0, 0), memory_space=pltpu.SMEM),
                  pl.BlockSpec((1, 1, tm), lambda i: (jnp.minimum(i + 1, n_tiles - 1), 0, 0), memory_space=pltpu.SMEM),
                  pl.BlockSpec(memory_space=pl.ANY),
                  row, _row_spec(gate2.shape[0], tm, D_MODEL, rows), const, const],
        out_specs=row,
        out_shape=jax.ShapeDtypeStruct((rows, D_MODEL), F32),
        scratch_shapes=[pltpu.VMEM((2, tm, D_MODEL), F32), pltpu.SemaphoreType.DMA((2,))],
        compiler_params=_params("arbitrary"),
        name="combine",
    )(pos3, pos3, y_sorted, x1, gate2, ln_g.reshape(1, D_MODEL), ln_b.reshape(1, D_MODEL))


def _trunk_layer(x, mods, bsz, seq, pos0, s0, k_past, v_past, p, tiles):
    shift1, scale1, gate1, shift2, scale2, gate2 = mods
    uv, ret, q_c, k_c, v_c, k_bf, v_bf = _inproj(x, shift1, scale1, p["w_in"], p["layer"], tiles["tm_in"])
    a_out, v_rows = _gmlp(uv, p["w_sp"], p["b_sp"], p["ln_v_g"], p["ln_v_b"], min(seq, GMLP_CHUNK))
    b_out, s_new = _retention(ret, s0, pos0, bsz, seq, p["gn_g"], p["gn_b"])
    tq, tk = tiles["sb"]
    if k_past is None:
        c_out = _stick_breaking(q_c, k_bf, v_bf, bsz, seq, seq, pos0, tq, tk)
    else:
        c_out = _stick_breaking_step(q_c, k_bf, v_bf, k_past, v_past, p["layer"], bsz, seq, tk)
    grouped = "tm_group" in tiles
    post = _post(a_out, b_out, c_out, x, p["w_out"], p["layer"], gate1, p["ln1_g"], p["ln1_b"],
                 shift2, scale2, p["wr"], p["br"], tiles["tm_post"], grouped)
    if grouped:
        x1, _, comb, hx = post
        tm = tiles["tm_group"]
        pos, src, tile_group, n_used = _group_plan(comb[:, N_EXPERTS].astype(jnp.int32), tm)
        y_sorted = _moe_grouped(hx, src, tile_group, n_used, p["w1"], p["w3"], p["w2"], tm)
        y = _combine(y_sorted, pos, x1, gate2, p["ln2_g"], p["ln2_b"], tm)
    else:
        x1, h2, comb = post
        y = _moe(h2, comb, x1, gate2, p["ln2_g"], p["ln2_b"], p["w1"], p["w3"], p["w2"], tiles["tm_moe"])
    return y, v_rows, s_new, k_c, v_c


def kernel(x_prompt, x_sample, cache_sb_k, cache_sb_v, state_ret, c_prompt, c_sample, w_ada, b_ada, w_in, w_out, ln_v_g, ln_v_b, w_spatial, b_spatial, gn_g, gn_b, ln1_g, ln1_b, ln2_g, ln2_b, w_router, b_router, w1, w3, w2):
    bp, tp, _ = x_prompt.shape
    bs, ts, _ = x_sample.shape
    past_len = cache_sb_k.shape[2]
    assert bp == 1

    n_c = bp + bs
    c_rows = -(-n_c // 8) * 8
    c_all = jnp.concatenate([c_prompt, c_sample, jnp.zeros((c_rows - n_c, D_MODEL), F32)], axis=0)
    mods = _adaln(c_all, w_ada, b_ada)

    wr_t = w_router.T
    cache_kt = jnp.transpose(cache_sb_k, (0, 1, 3, 4, 2))
    cache_vt = jnp.transpose(cache_sb_v, (0, 1, 3, 4, 2))
    br_col = b_router.reshape(N_EXPERTS, 1)

    tiles_p = dict(tm_in=256, sb=(256, 256), tm_post=512, tm_group=256)
    tiles_s = dict(tm_in=bs * ts, sb=(ts, 256), tm_post=bs * ts, tm_moe=bs * ts)

    y_p = x_prompt.reshape(bp * tp, D_MODEL)
    y_s = x_sample.reshape(bs * ts, D_MODEL)
    zero_state = jnp.zeros((bp, H_B, HEAD_DIM, HEAD_DIM), F32)
    outs = [[] for _ in range(7)]
    for l in range(DEPTH):
        p = dict(w_in=w_in, w_out=w_out,
                 w_sp=w_spatial[l], b_sp=b_spatial[l], ln_v_g=ln_v_g[l], ln_v_b=ln_v_b[l],
                 gn_g=gn_g[l], gn_b=gn_b[l], ln1_g=ln1_g[l], ln1_b=ln1_b[l],
                 ln2_g=ln2_g[l], ln2_b=ln2_b[l], wr=wr_t, br=br_col,
                 w1=w1[l].astype(BF16), w3=w3[l].astype(BF16), w2=w2[l].astype(BF16), layer=l)
        m = mods[l]
        mods_p = [m[0:1, i * D_MODEL:(i + 1) * D_MODEL] for i in range(6)]
        mods_s = [jnp.repeat(m[bp:bp + bs, i * D_MODEL:(i + 1) * D_MODEL], ts, axis=0) for i in range(6)]
        y_p, _, s_p, k_p, v_p = _trunk_layer(y_p, mods_p, bp, tp, 0, zero_state, None, None, p, tiles_p)
        y_s, g_s, s_s, k_s, v_s = _trunk_layer(
            y_s, mods_s, bs, ts, past_len, state_ret[l],
            cache_kt, cache_vt, p, tiles_s)
        outs[0].append(s_p)
        outs[1].append(k_p.reshape(bp, tp, H_C, HEAD_DIM))
        outs[2].append(v_p.reshape(bp, tp, H_C, HEAD_DIM))
        outs[3].append(s_s)
        outs[4].append(k_s.reshape(bs, ts, H_C, HEAD_DIM))
        outs[5].append(v_s.reshape(bs, ts, H_C, HEAD_DIM))
        outs[6].append(g_s.reshape(bs, ts, W_A))
    return (y_p.reshape(bp, tp, D_MODEL), y_s.reshape(bs, ts, D_MODEL)) + tuple(jnp.stack(o) for o in outs)
```

```python
import functools

import numpy as np
import jax
import jax.numpy as jnp
from jax import lax
from jax.experimental import pallas as pl
from jax.experimental.pallas import tpu as pltpu

F32 = jnp.float32
BF16 = jnp.bfloat16
HIGHEST = lax.Precision.HIGHEST

D_MODEL = 1024
DEPTH = 2
HEAD_DIM = 64
W_A = D_MODEL // 4
W_B = 3 * D_MODEL // 8
W_C = D_MODEL - W_A - W_B
H_A = W_A // HEAD_DIM
H_B = W_B // HEAD_DIM
H_C = W_C // HEAD_DIM
GMLP_CHUNK = 128
ROPE_BASE = 10000.0
N_EXPERTS = 16
N_GROUPS = 4
EXPERTS_PER_GROUP = N_EXPERTS // N_GROUPS
D_EXPERT = D_MODEL // 2
ALPHA = (2 * DEPTH) ** 0.25
LN_EPS = 1e-5
D_IN = 2 * W_A + 4 * W_B + 3 * W_C
LANES = 128
VMEM_LIMIT = 48 * 1024 * 1024

NT_DIMS = (((1,), (1,)), ((), ()))
TN_DIMS = (((0,), (0,)), ((), ()))
INV_LN2 = 1.4426950408889634
SB_QSCALE = HEAD_DIM ** -0.5 * INV_LN2
SB_DEAD = -152.0
GMLP_SUB = 4
RET_BLOCK = 256


def _ln(x):
    mu = jnp.mean(x, axis=-1, keepdims=True)
    xc = x - mu
    var = jnp.mean(xc * xc, axis=-1, keepdims=True)
    return xc * lax.rsqrt(var + LN_EPS)


def _silu(x):
    return x * jax.nn.sigmoid(x)


def _params(*sem):
    return pltpu.CompilerParams(dimension_semantics=sem, vmem_limit_bytes=VMEM_LIMIT)


def _row_spec(rows, tm, width, total_rows):
    if rows == 1:
        return pl.BlockSpec((1, width), lambda i: (0, 0))
    assert rows == total_rows
    return pl.BlockSpec((tm, width), lambda i: (i, 0))


def _adaln_kernel(c_ref, w_ref, b_ref, o_ref):
    sc = _silu(c_ref[...])
    o_ref[0] = jnp.dot(sc, w_ref[0], preferred_element_type=F32, precision=HIGHEST) + b_ref[0]


def _adaln(c_all, w_ada, b_ada):
    rows = c_all.shape[0]
    tn = 1536
    return pl.pallas_call(
        _adaln_kernel,
        grid=(DEPTH, 6 * D_MODEL // tn),
        in_specs=[pl.BlockSpec((rows, D_MODEL), lambda l, j: (0, 0)),
                  pl.BlockSpec((1, D_MODEL, tn), lambda l, j: (l, 0, j)),
                  pl.BlockSpec((1, 1, tn), lambda l, j: (l, 0, j))],
        out_specs=pl.BlockSpec((1, rows, tn), lambda l, j: (l, 0, j)),
        out_shape=jax.ShapeDtypeStruct((DEPTH, rows, 6 * D_MODEL), F32),
        compiler_params=_params("parallel", "parallel"),
        name="adaln",
    )(c_all, w_ada, b_ada.reshape(DEPTH, 1, 6 * D_MODEL))


def _inproj_kernel(x_ref, shift_ref, scale_ref, w_ref, uv_ref, ret_ref, q_ref, k_ref, v_ref, kb_ref, vb_ref, wb_ref):
    @pl.when(pl.program_id(0) == 0)
    def _():
        wb_ref[...] = w_ref[0].astype(BF16)

    h = _ln(x_ref[...]) * (1.0 + scale_ref[...]) + shift_ref[...]
    r = jnp.dot(h.astype(BF16), wb_ref[...], preferred_element_type=F32)
    c0 = 2 * W_A
    c1 = c0 + 4 * W_B
    uv_ref[...] = r[:, :c0]
    ret_ref[...] = r[:, c0:c1]
    q_ref[...] = r[:, c1:c1 + W_C]
    k = r[:, c1 + W_C:c1 + 2 * W_C]
    v = r[:, c1 + 2 * W_C:]
    k_ref[...] = k
    v_ref[...] = v
    kb_ref[...] = k.astype(BF16)
    vb_ref[...] = v.astype(BF16)


def _inproj(x, shift, scale, w_in, layer, tm):
    rows = x.shape[0]
    widths = (2 * W_A, 4 * W_B, W_C, W_C, W_C, W_C, W_C)
    dtypes = (F32,) * 5 + (BF16,) * 2
    return pl.pallas_call(
        _inproj_kernel,
        grid=(rows // tm,),
        in_specs=[pl.BlockSpec((tm, D_MODEL), lambda i: (i, 0)),
                  _row_spec(shift.shape[0], tm, D_MODEL, rows),
                  _row_spec(scale.shape[0], tm, D_MODEL, rows),
                  pl.BlockSpec((1, D_MODEL, D_IN), lambda i: (layer, 0, 0), pipeline_mode=pl.Buffered(1))],
        out_specs=[pl.BlockSpec((tm, w), lambda i: (i, 0)) for w in widths],
        out_shape=[jax.ShapeDtypeStruct((rows, w), dt) for w, dt in zip(widths, dtypes)],
        scratch_shapes=[pltpu.VMEM((D_MODEL, D_IN), BF16)],
        compiler_params=_params("arbitrary"),
        name="inproj",
    )(x, shift, scale, w_in)


def _gmlp_kernel(uv_ref, wsp_ref, bias_ref, g_ref, b_ref, a_ref, vn_ref, *, chunk, n_sub):
    uv = uv_ref[...]
    u = jax.nn.gelu(uv[:, :W_A])
    v = _ln(jax.nn.gelu(uv[:, W_A:])) * g_ref[...] + b_ref[...]
    vn_ref[...] = v
    row = lax.broadcasted_iota(jnp.int32, (chunk, chunk), 0)
    col = lax.broadcasted_iota(jnp.int32, (chunk, chunk), 1)
    lane_head = lax.broadcasted_iota(jnp.int32, (chunk, W_A), 1) // HEAD_DIM
    w = [jnp.where(col <= row, wsp_ref[h], 0.0).astype(BF16) for h in range(H_A)]
    for c in range(n_sub):
        rows = slice(c * chunk, (c + 1) * chunk)
        vc = v[rows]
        mixed = bias_ref[...]
        for h in range(H_A):
            vh = jnp.where(lane_head == h, vc, 0.0).astype(BF16)
            mixed = mixed + jnp.dot(w[h], vh, preferred_element_type=F32)
        a_ref[rows, :] = u[rows] * mixed


def _gmlp(uv, w_sp, b_sp, ln_g, ln_b, chunk):
    rows = uv.shape[0]
    n_sub = min(GMLP_SUB, rows // chunk)
    tm = n_sub * chunk
    wsp = w_sp[:, :chunk, :chunk]
    bias = jnp.repeat(b_sp[:, :chunk].T, HEAD_DIM, axis=1)
    return pl.pallas_call(
        functools.partial(_gmlp_kernel, chunk=chunk, n_sub=n_sub),
        grid=(rows // tm,),
        in_specs=[pl.BlockSpec((tm, 2 * W_A), lambda i: (i, 0)),
                  pl.BlockSpec((H_A, chunk, chunk), lambda i: (0, 0, 0)),
                  pl.BlockSpec((chunk, W_A), lambda i: (0, 0)),
                  pl.BlockSpec((1, W_A), lambda i: (0, 0)),
                  pl.BlockSpec((1, W_A), lambda i: (0, 0))],
        out_specs=[pl.BlockSpec((tm, W_A), lambda i: (i, 0)),
                   pl.BlockSpec((tm, W_A), lambda i: (i, 0))],
        out_shape=[jax.ShapeDtypeStruct((rows, W_A), F32),
                   jax.ShapeDtypeStruct((rows, W_A), F32)],
        compiler_params=_params("parallel"),
        name="gmlp",
    )(uv, wsp, bias, ln_g.reshape(1, W_A), ln_b.reshape(1, W_A))


def _rope(x, cos, sin):
    lane = lax.broadcasted_iota(jnp.int32, (x.shape[0], LANES), 1)
    first_half = (lane & (HEAD_DIM // 2)) == 0
    parts = []
    for c in range(x.shape[1] // LANES):
        xc = x[:, c * LANES:(c + 1) * LANES]
        rot = jnp.where(first_half,
                        pltpu.roll(xc, LANES - HEAD_DIM // 2, 1),
                        pltpu.roll(xc, HEAD_DIM // 2, 1))
        parts.append(xc * cos + rot * sin)
    return jnp.concatenate(parts, axis=1)


def _ret_kernel(r_ref, cos_ref, sin_ref, qdec_ref, kdec_ref, dec_ref, blk_ref, s0_ref,
                gng_ref, gnb_ref, o_ref, sout_ref, s_scr, o_scr, *, n_blocks):
    n = pl.program_id(1)

    @pl.when(n == 0)
    def _():
        s_scr[...] = s0_ref[0]

    r = r_ref[...]
    cos = cos_ref[...]
    sin = sin_ref[...]
    qr = _rope(r[:, :W_B], cos, sin)
    kr = _rope(r[:, W_B:2 * W_B], cos, sin) * (HEAD_DIM ** -0.5)
    vb = r[:, 2 * W_B:3 * W_B].astype(BF16)
    gate = r[:, 3 * W_B:]
    qb = qr.astype(BF16)
    kb = kr.astype(BF16)
    qdb = (qr * qdec_ref[...]).astype(BF16)
    kdb = (kr * kdec_ref[...]).astype(BF16)
    for h in range(H_B):
        sl = slice(h * HEAD_DIM, (h + 1) * HEAD_DIM)
        scores = lax.dot_general(qb[:, sl], kb[:, sl], NT_DIMS, preferred_element_type=F32) * dec_ref[h]
        s_h = s_scr[h]
        o_h = (jnp.dot(scores.astype(BF16), vb[:, sl], preferred_element_type=F32)
               + jnp.dot(qdb[:, sl], s_h.astype(BF16), preferred_element_type=F32))
        s_scr[h] = s_h * blk_ref[h] + lax.dot_general(kdb[:, sl], vb[:, sl], TN_DIMS,
                                                      preferred_element_type=F32)
        o_scr[:, sl] = _ln(o_h)
    o_ref[...] = (o_scr[...] * gng_ref[...] + gnb_ref[...]) * _silu(gate)

    @pl.when(n == n_blocks - 1)
    def _():
        sout_ref[0] = s_scr[...]


def _retention(ret, s0, pos0, bsz, seq, gn_g, gn_b):
    L = min(seq, RET_BLOCK)
    n_blocks = seq // L
    half = HEAD_DIM // 2
    inv = ROPE_BASE ** (-jnp.arange(half, dtype=F32) / half)
    ang = (pos0 + jnp.arange(seq)).astype(F32)[:, None] * inv[None, :]
    cos, sin = jnp.cos(ang), jnp.sin(ang)
    cos_t = jnp.tile(jnp.concatenate([cos, cos], axis=1), (1, LANES // HEAD_DIM))
    sin_t = jnp.tile(jnp.concatenate([-sin, sin], axis=1), (1, LANES // HEAD_DIM))
    log_g = jnp.log1p(-jnp.exp2(-5.0 - jnp.arange(H_B, dtype=F32)))
    idx = jnp.arange(L, dtype=F32)
    diff = idx[:, None] - idx[None, :]
    decay = jnp.where(diff >= 0, jnp.exp(diff[None] * log_g[:, None, None]), 0.0)
    q_decay = jnp.exp((idx[None, :] + 1.0) * log_g[:, None])
    k_decay = jnp.exp((L - 1.0 - idx[None, :]) * log_g[:, None])
    blk_decay = jnp.exp(L * log_g)
    qdec = jnp.repeat(q_decay.T, HEAD_DIM, axis=1)
    kdec = jnp.repeat(k_decay.T, HEAD_DIM, axis=1)
    blk = jnp.broadcast_to(blk_decay[:, None, None], (H_B, HEAD_DIM, HEAD_DIM))
    const2 = lambda b, n: (0, 0)
    const3 = lambda b, n: (0, 0, 0)
    return pl.pallas_call(
        functools.partial(_ret_kernel, n_blocks=n_blocks),
        grid=(bsz, n_blocks),
        in_specs=[pl.BlockSpec((L, 4 * W_B), lambda b, n: (b * n_blocks + n, 0)),
                  pl.BlockSpec((L, LANES), lambda b, n: (n, 0)),
                  pl.BlockSpec((L, LANES), lambda b, n: (n, 0)),
                  pl.BlockSpec((L, W_B), const2),
                  pl.BlockSpec((L, W_B), const2),
                  pl.BlockSpec((H_B, L, L), const3),
                  pl.BlockSpec((H_B, HEAD_DIM, HEAD_DIM), const3),
                  pl.BlockSpec((1, H_B, HEAD_DIM, HEAD_DIM), lambda b, n: (b, 0, 0, 0)),
                  pl.BlockSpec((1, W_B), const2),
                  pl.BlockSpec((1, W_B), const2)],
        out_specs=[pl.BlockSpec((L, W_B), lambda b, n: (b * n_blocks + n, 0)),
                   pl.BlockSpec((1, H_B, HEAD_DIM, HEAD_DIM), lambda b, n: (b, 0, 0, 0))],
        out_shape=[jax.ShapeDtypeStruct((bsz * seq, W_B), F32),
                   jax.ShapeDtypeStruct((bsz, H_B, HEAD_DIM, HEAD_DIM), F32)],
        scratch_shapes=[pltpu.VMEM((H_B, HEAD_DIM, HEAD_DIM), F32),
                        pltpu.VMEM((L, W_B), F32)],
        compiler_params=_params("parallel", "arbitrary"),
        name="retention",
    )(ret, cos_t, sin_t, qdec, kdec, decay, blk, s0, gn_g.reshape(1, W_B), gn_b.reshape(1, W_B))


def _sb_block(qm_ref, k_ref, v_ref, u2_ref, acc_ref, carry_ref, causal):
    tk = k_ref.shape[0]
    half = lax.broadcasted_iota(jnp.int32, (tk, LANES), 1) // HEAD_DIM
    u2 = u2_ref[...]
    kp = [k_ref[:, p * LANES:(p + 1) * LANES] for p in range(H_C // 2)]

    def scores(h):
        z = lax.dot_general(qm_ref[h], kp[h // 2], NT_DIMS, preferred_element_type=F32)
        neg_abs = pltpu.bitcast(pltpu.bitcast(z, jnp.uint32) | jnp.uint32(0x80000000), F32)
        ls_pos = jnp.minimum(z, 0.0) - jnp.log(1.0 + jnp.exp2(neg_abs)) * INV_LN2
        log_stay = ls_pos - z
        if causal is not None:
            log_stay = jnp.where(causal, log_stay, 0.0)
        hi = log_stay.astype(BF16)
        lo = (log_stay - hi.astype(F32)).astype(BF16)
        return ls_pos, log_stay[:, :1], jnp.concatenate([hi, lo], axis=1)

    def cumsum(hi_lo):
        return jnp.dot(hi_lo, u2, preferred_element_type=F32)

    def weigh(h, ls_pos, first_col, excl):
        carry = carry_ref[h]
        att = jnp.exp2(ls_pos + excl + carry)
        if causal is not None:
            att = jnp.where(causal, att, 0.0)
        vp = v_ref[:, (h // 2) * LANES:(h // 2 + 1) * LANES]
        vp = jnp.where(half == h % 2, vp, jnp.zeros_like(vp))
        carry_ref[h] = carry + (excl[:, :1] + first_col)
        return jnp.dot(att.astype(BF16), vp, preferred_element_type=F32)

    stage_a, stage_b, outs = {}, {}, {}
    for step in range(H_C + 2):
        if step - 2 >= 0:
            h = step - 2
            outs[h] = weigh(h, stage_a[h][0], stage_a[h][1], stage_b.pop(h))
            del stage_a[h]
            if h % 2 == 1:
                p = h // 2
                acc_ref[:, p * LANES:(p + 1) * LANES] += outs.pop(h - 1) + outs.pop(h)
        if 0 <= step - 1 < H_C:
            stage_b[step - 1] = cumsum(stage_a[step - 1][2])
        if step < H_C:
            stage_a[step] = scores(step)


def _sb_kernel(q_ref, k_ref, v_ref, u2_ref, o_ref, qm_ref, acc_ref, carry_ref, *, t):
    i = pl.program_id(1)
    acc_ref[...] = jnp.zeros_like(acc_ref)
    carry_ref[...] = jnp.zeros_like(carry_ref)
    half = lax.broadcasted_iota(jnp.int32, (t, LANES), 1) // HEAD_DIM
    for h in range(H_C):
        qp = q_ref[:, (h // 2) * LANES:(h // 2 + 1) * LANES] * SB_QSCALE
        qm_ref[h] = jnp.where(half == h % 2, qp, 0.0).astype(BF16)

    def block(j, causal):
        k0 = pl.multiple_of((i - j) * t, t)
        _sb_block(qm_ref, k_ref.at[pl.ds(k0, t), :], v_ref.at[pl.ds(k0, t), :], u2_ref, acc_ref, carry_ref, causal)

    block(0, lax.broadcasted_iota(jnp.int32, (t, t), 1) < lax.broadcasted_iota(jnp.int32, (t, t), 0))

    def key_block(state):
        j, _ = state
        block(j, None)
        dead = jnp.max(carry_ref[...]) < SB_DEAD
        return j + 1, dead.astype(jnp.int32)

    lax.while_loop(lambda state: jnp.logical_and(state[0] <= i, state[1] == 0),
                   key_block, (jnp.int32(1), jnp.int32(0)))
    o_ref[...] = acc_ref[...]


def _stick_breaking(q, k, v, bsz, seq, t):
    nq = seq // t
    tri = np.tril(np.ones((t, t), np.float32), -1)
    u2 = jnp.asarray(np.concatenate([tri, tri], axis=0), dtype=BF16)
    resident = dict(pipeline_mode=pl.Buffered(1))
    return pl.pallas_call(
        functools.partial(_sb_kernel, t=t),
        grid=(bsz, nq),
        in_specs=[pl.BlockSpec((t, W_C), lambda b, i: (b * nq + i, 0)),
                  pl.BlockSpec((seq, W_C), lambda b, i: (b, 0), **resident),
                  pl.BlockSpec((seq, W_C), lambda b, i: (b, 0), **resident),
                  pl.BlockSpec((2 * t, t), lambda b, i: (0, 0), **resident)],
        out_specs=pl.BlockSpec((t, W_C), lambda b, i: (b * nq + i, 0)),
        out_shape=jax.ShapeDtypeStruct((bsz * seq, W_C), F32),
        scratch_shapes=[pltpu.VMEM((H_C, t, LANES), BF16),
                        pltpu.VMEM((t, W_C), F32),
                        pltpu.VMEM((H_C, t, 1), F32)],
        compiler_params=_params("parallel", "arbitrary"),
        name="stick_breaking",
    )(q, k, v, u2)


def _sb_scores(z, causal=None):
    neg_abs = pltpu.bitcast(pltpu.bitcast(z, jnp.uint32) | jnp.uint32(0x80000000), F32)
    ls_pos = jnp.minimum(z, 0.0) - jnp.log(1.0 + jnp.exp2(neg_abs)) * INV_LN2
    log_stay = ls_pos - z
    if causal is not None:
        log_stay = jnp.where(causal, log_stay, 0.0)
    hi = log_stay.astype(BF16)
    lo = (log_stay - hi.astype(F32)).astype(BF16)
    return ls_pos, log_stay, jnp.concatenate([hi, lo], axis=1)


def _sb_step_kernel(q_ref, kt_ref, vt_ref, kn_ref, vn_ref, u2_ref, un_ref, o_ref, acc_ref, carry_ref, *, tk):
    seq = q_ref.shape[0]
    past = kt_ref.shape[-1]
    qs = (q_ref[...] * SB_QSCALE).astype(BF16)
    row = lax.broadcasted_iota(jnp.int32, (seq, seq), 0)
    col = lax.broadcasted_iota(jnp.int32, (seq, seq), 1)
    causal = col < row
    un = un_ref[...]
    for h in range(H_C):
        sl = slice(h * HEAD_DIM, (h + 1) * HEAD_DIM)
        z = lax.dot_general(qs[:, sl], kn_ref[:, sl], NT_DIMS, preferred_element_type=F32)
        ls_pos, log_stay, hi_lo = _sb_scores(z, causal)
        excl = jnp.dot(hi_lo, un, preferred_element_type=F32)
        att = jnp.where(causal, jnp.exp2(ls_pos + excl), 0.0)
        acc_ref[h] = jnp.dot(att.astype(BF16), vn_ref[:, sl], preferred_element_type=F32)
        carry_ref[h] = excl[:, :1] + log_stay[:, :1]

    u2 = u2_ref[...]

    def key_block(state):
        j, _ = state
        k0 = pl.multiple_of(past - (j + 1) * tk, tk)
        for h in range(H_C):
            kt = kt_ref[0, 0, h, :, pl.ds(k0, tk)].astype(BF16)
            vt = vt_ref[0, 0, h, :, pl.ds(k0, tk)].astype(BF16)
            z = jnp.dot(qs[:, h * HEAD_DIM:(h + 1) * HEAD_DIM], kt, preferred_element_type=F32)
            ls_pos, log_stay, hi_lo = _sb_scores(z)
            excl = jnp.dot(hi_lo, u2, preferred_element_type=F32)
            carry = carry_ref[h]
            att = jnp.exp2(ls_pos + excl + carry)
            acc_ref[h] += lax.dot_general(att.astype(BF16), vt, NT_DIMS, preferred_element_type=F32)
            carry_ref[h] = carry + (excl[:, :1] + log_stay[:, :1])
        dead = jnp.max(carry_ref[...]) < SB_DEAD
        return j + 1, dead.astype(jnp.int32)

    lax.while_loop(lambda state: jnp.logical_and(state[0] < past // tk, state[1] == 0),
                   key_block, (jnp.int32(0), jnp.int32(0)))
    for h in range(H_C):
        o_ref[:, h * HEAD_DIM:(h + 1) * HEAD_DIM] = acc_ref[h]


def _stick_breaking_step(q, k_new, v_new, cache_kt, cache_vt, layer, bsz, seq, tk):
    past = cache_kt.shape[-1]
    assert past % tk == 0
    tri = lambda n: np.tril(np.ones((n, n), np.float32), -1)
    stacked = lambda n: jnp.asarray(np.concatenate([tri(n), tri(n)], axis=0), dtype=BF16)
    rows = pl.BlockSpec((seq, W_C), lambda b: (b, 0))
    cache = pl.BlockSpec((1, 1, H_C, HEAD_DIM, past), lambda b: (layer, b, 0, 0, 0))
    return pl.pallas_call(
        functools.partial(_sb_step_kernel, tk=tk),
        grid=(bsz,),
        in_specs=[rows, cache, cache, rows, rows,
                  pl.BlockSpec((2 * tk, tk), lambda b: (0, 0)), pl.BlockSpec((2 * seq, seq), lambda b: (0, 0))],
        out_specs=rows,
        out_shape=jax.ShapeDtypeStruct((bsz * seq, W_C), F32),
        scratch_shapes=[pltpu.VMEM((H_C, seq, HEAD_DIM), F32), pltpu.VMEM((H_C, seq, 1), F32)],
        compiler_params=_params("parallel"),
        name="stick_breaking_step",
    )(q, cache_kt, cache_vt, k_new, v_new, stacked(tk), stacked(seq))


def _route(sel, s):
    g_scores = []
    for g in range(N_GROUPS):
        a, b, c, d = sel[EXPERTS_PER_GROUP * g:EXPERTS_PER_GROUP * (g + 1)]
        ab_hi, ab_lo = jnp.maximum(a, b), jnp.minimum(a, b)
        cd_hi, cd_lo = jnp.maximum(c, d), jnp.minimum(c, d)
        top1 = jnp.maximum(ab_hi, cd_hi)
        top2 = jnp.maximum(jnp.minimum(ab_hi, cd_hi), jnp.maximum(ab_lo, cd_lo))
        g_scores.append(top1 + top2)
    best = g_scores[0]
    gi = jnp.zeros(best.shape, jnp.int32)
    for g in range(1, N_GROUPS):
        upd = g_scores[g] > best
        gi = jnp.where(upd, g, gi)
        best = jnp.where(upd, g_scores[g], best)

    def pick_group(rows, l):
        out = rows[(N_GROUPS - 1) * EXPERTS_PER_GROUP + l]
        for g in range(N_GROUPS - 2, -1, -1):
            out = jnp.where(gi == g, rows[g * EXPERTS_PER_GROUP + l], out)
        return out

    ig = [pick_group(sel, l) for l in range(EXPERTS_PER_GROUP)]
    sg = [pick_group(s, l) for l in range(EXPERTS_PER_GROUP)]
    b1 = ig[0]
    i1 = jnp.zeros(best.shape, jnp.int32)
    for l in range(1, EXPERTS_PER_GROUP):
        upd = ig[l] > b1
        i1 = jnp.where(upd, l, i1)
        b1 = jnp.where(upd, ig[l], b1)
    b2 = jnp.full(best.shape, -jnp.inf, F32)
    i2 = jnp.zeros(best.shape, jnp.int32)
    for l in range(EXPERTS_PER_GROUP):
        upd = jnp.logical_and(i1 != l, ig[l] > b2)
        i2 = jnp.where(upd, l, i2)
        b2 = jnp.where(upd, ig[l], b2)

    def pick_local(idx):
        out = sg[EXPERTS_PER_GROUP - 1]
        for l in range(EXPERTS_PER_GROUP - 2, -1, -1):
            out = jnp.where(idx == l, sg[l], out)
        return out

    w1 = pick_local(i1)
    w2 = pick_local(i2)
    tot = w1 + w2
    return gi * EXPERTS_PER_GROUP + i1, gi * EXPERTS_PER_GROUP + i2, w1 / tot, w2 / tot


def _post_kernel(a_ref, b_ref, c_ref, x_ref, wo_ref, gate_ref, g1_ref, b1_ref, sh2_ref, sc2_ref,
                 wrt_ref, br_ref, x1_ref, *rest, tm, with_hx):
    if with_hx:
        h2_ref = None
        comb_ref, hx_ref, wob_ref = rest
    else:
        hx_ref = None
        h2_ref, comb_ref, wob_ref = rest
    @pl.when(pl.program_id(0) == 0)
    def _():
        wob_ref[...] = wo_ref[0].astype(BF16)

    proj = (jnp.dot(a_ref[...].astype(BF16), wob_ref[:W_A], preferred_element_type=F32)
            + jnp.dot(b_ref[...].astype(BF16), wob_ref[W_A:W_A + W_B], preferred_element_type=F32)
            + jnp.dot(c_ref[...].astype(BF16), wob_ref[W_A + W_B:], preferred_element_type=F32))
    x1 = _ln(ALPHA * x_ref[...] + gate_ref[...] * proj) * g1_ref[...] + b1_ref[...]
    x1_ref[...] = x1
    h2 = _ln(x1) * (1.0 + sc2_ref[...]) + sh2_ref[...]
    if h2_ref is not None:
        h2_ref[...] = h2.astype(BF16)
    logits_t = lax.dot_general(wrt_ref[...], h2, NT_DIMS, preferred_element_type=F32, precision=HIGHEST)
    s_t = jax.nn.sigmoid(logits_t)
    sel_t = s_t + br_ref[...]
    s = [s_t[e:e + 1, :] for e in range(N_EXPERTS)]
    sel = [sel_t[e:e + 1, :] for e in range(N_EXPERTS)]
    e1, e2, w1, w2 = _route(sel, s)
    expert = lax.broadcasted_iota(jnp.int32, (LANES, tm), 0)
    comb_t = jnp.where(expert == e1, w1, jnp.where(expert == e2, w2, 0.0))
    group = (e1 // EXPERTS_PER_GROUP).astype(F32)
    comb_t = jnp.where(expert == N_EXPERTS, group, comb_t)
    comb_ref[...] = comb_t.T
    if hx_ref is not None:
        hx_ref[:, :D_MODEL] = h2
        hx_ref[:, D_MODEL:] = comb_t.T


def _post(a, b, c, x, w_out, layer, gate1, ln_g, ln_b, shift2, scale2, wr_t, br_col, tm, with_hx):
    rows = x.shape[0]
    if with_hx:
        tail_spec = [pl.BlockSpec((tm, LANES), lambda i: (i, 0)), pl.BlockSpec((tm, D_MODEL + LANES), lambda i: (i, 0))]
        tail_shape = [jax.ShapeDtypeStruct((rows, LANES), F32), jax.ShapeDtypeStruct((rows, D_MODEL + LANES), F32)]
    else:
        tail_spec = [pl.BlockSpec((tm, D_MODEL), lambda i: (i, 0)), pl.BlockSpec((tm, LANES), lambda i: (i, 0))]
        tail_shape = [jax.ShapeDtypeStruct((rows, D_MODEL), BF16), jax.ShapeDtypeStruct((rows, LANES), F32)]
    row = lambda w: pl.BlockSpec((tm, w), lambda i: (i, 0))
    const = lambda r, w: pl.BlockSpec((r, w), lambda i: (0, 0))
    return pl.pallas_call(
        functools.partial(_post_kernel, tm=tm, with_hx=with_hx),
        grid=(rows // tm,),
        in_specs=[row(W_A), row(W_B), row(W_C), row(D_MODEL),
                  pl.BlockSpec((1, D_MODEL, D_MODEL), lambda i: (layer, 0, 0), pipeline_mode=pl.Buffered(1)),
                  _row_spec(gate1.shape[0], tm, D_MODEL, rows),
                  const(1, D_MODEL), const(1, D_MODEL),
                  _row_spec(shift2.shape[0], tm, D_MODEL, rows),
                  _row_spec(scale2.shape[0], tm, D_MODEL, rows),
                  const(N_EXPERTS, D_MODEL), const(N_EXPERTS, 1)],
        out_specs=[row(D_MODEL)] + tail_spec,
        out_shape=[jax.ShapeDtypeStruct((rows, D_MODEL), F32)] + tail_shape,
        scratch_shapes=[pltpu.VMEM((D_MODEL, D_MODEL), BF16)],
        compiler_params=_params("arbitrary"),
        name="post_mix",
    )(a, b, c, x, w_out, gate1, ln_g.reshape(1, D_MODEL), ln_b.reshape(1, D_MODEL),
      shift2, scale2, wr_t, br_col)


def _moe_kernel(h_ref, comb_ref, x_ref, gate_ref, g2_ref, b2_ref, w1_ref, w3_ref, w2_ref,
                o_ref, acc_ref, *, tm):
    e = pl.program_id(1)

    @pl.when(e == 0)
    def _():
        acc_ref[...] = jnp.zeros_like(acc_ref)

    h = h_ref[...]
    a = jnp.dot(h, w1_ref[0], preferred_element_type=F32)
    g = jnp.dot(h, w3_ref[0], preferred_element_type=F32)
    lane = lax.broadcasted_iota(jnp.int32, (tm, LANES), 1)
    ce = jnp.sum(jnp.where(lane == e, comb_ref[...], 0.0), axis=-1, keepdims=True)
    act = _silu(a) * g * ce
    acc_ref[...] += jnp.dot(act.astype(BF16), w2_ref[0], preferred_element_type=F32)

    @pl.when(e == N_EXPERTS - 1)
    def _():
        y = ALPHA * x_ref[...] + gate_ref[...] * acc_ref[...]
        o_ref[...] = _ln(y) * g2_ref[...] + b2_ref[...]


def _moe(h2, comb, x1, gate2, ln_g, ln_b, w1, w3, w2, tm):
    rows = x1.shape[0]
    row = lambda w: pl.BlockSpec((tm, w), lambda i, e: (i, 0))
    const = pl.BlockSpec((1, D_MODEL), lambda i, e: (0, 0))
    gate_spec = (pl.BlockSpec((1, D_MODEL), lambda i, e: (0, 0)) if gate2.shape[0] == 1
                 else row(D_MODEL))
    return pl.pallas_call(
        functools.partial(_moe_kernel, tm=tm),
        grid=(rows // tm, N_EXPERTS),
        in_specs=[row(D_MODEL), row(LANES), row(D_MODEL), gate_spec, const, const,
                  pl.BlockSpec((1, D_MODEL, D_EXPERT), lambda i, e: (e, 0, 0)),
                  pl.BlockSpec((1, D_MODEL, D_EXPERT), lambda i, e: (e, 0, 0)),
                  pl.BlockSpec((1, D_EXPERT, D_MODEL), lambda i, e: (e, 0, 0))],
        out_specs=row(D_MODEL),
        out_shape=jax.ShapeDtypeStruct((rows, D_MODEL), F32),
        scratch_shapes=[pltpu.VMEM((tm, D_MODEL), F32)],
        compiler_params=_params("parallel", "arbitrary"),
        name="experts",
    )(h2, comb, x1, gate2, ln_g.reshape(1, D_MODEL), ln_b.reshape(1, D_MODEL), w1, w3, w2)


def _group_plan(group, tm):
    rows = group.shape[0]
    n_tiles = rows // tm + N_GROUPS
    onehot = (group[:, None] == jnp.arange(N_GROUPS, dtype=jnp.int32)[None, :]).astype(jnp.int32)
    rank = jnp.cumsum(onehot, axis=0) - onehot
    tiles_per = (jnp.sum(onehot, axis=0) + tm - 1) // tm
    tile_start = jnp.cumsum(tiles_per) - tiles_per
    pos = jnp.sum(onehot * (tile_start[None, :] * tm + rank), axis=1).astype(jnp.int32)
    src = jnp.zeros((n_tiles * tm,), jnp.int32).at[pos].set(jnp.arange(rows, dtype=jnp.int32))
    tile = jnp.arange(n_tiles, dtype=jnp.int32)
    tile_group = jnp.clip(jnp.sum((tile[:, None] >= tile_start[None, :]).astype(jnp.int32), axis=1) - 1,
                          0, N_GROUPS - 1).astype(jnp.int32)
    n_used = jnp.sum(tiles_per).astype(jnp.int32).reshape(1)
    return pos, src, tile_group, n_used


def _start_rows(idx_ref, src_hbm, dst_ref, sem, n):
    def issue(r, carry):
        pltpu.make_async_copy(src_hbm.at[pl.ds(idx_ref[0, 0, r], 1)], dst_ref.at[pl.ds(r, 1)], sem).start()
        return carry

    lax.fori_loop(0, n, issue, 0, unroll=True)


def _wait_rows(src_hbm, dst_ref, sem, n):
    pltpu.make_async_copy(src_hbm.at[pl.ds(0, n)], dst_ref, sem).wait()


def _moe_group_kernel(tg_ref, nused_ref, src_ref, src_next_ref, hx_hbm, w1_ref, w3_ref, w2_ref, y_ref,
                      xbuf, sems, *, tm):
    i = pl.program_id(0)
    n_used = nused_ref[0]
    slot = i % 2

    @pl.when(jnp.logical_and(i == 0, n_used > 0))
    def _():
        _start_rows(src_ref, hx_hbm, xbuf.at[0], sems.at[0], tm)

    @pl.when(i + 1 < n_used)
    def _():
        _start_rows(src_next_ref, hx_hbm, xbuf.at[1 - slot], sems.at[1 - slot], tm)

    @pl.when(i < n_used)
    def _():
        _wait_rows(hx_hbm, xbuf.at[slot], sems.at[slot], tm)
        x = xbuf[slot]
        h = x[:, :D_MODEL].astype(BF16)
        comb = x[:, D_MODEL:]
        lane = lax.broadcasted_iota(jnp.int32, (tm, LANES), 1)
        first = tg_ref[i] * EXPERTS_PER_GROUP
        acc = jnp.zeros((tm, D_MODEL), F32)
        for e in range(EXPERTS_PER_GROUP):
            a = jnp.dot(h, w1_ref[0, e], preferred_element_type=F32)
            g = jnp.dot(h, w3_ref[0, e], preferred_element_type=F32)
            ce = jnp.sum(jnp.where(lane == first + e, comb, 0.0), axis=-1, keepdims=True)
            act = _silu(a) * g * ce
            acc = acc + jnp.dot(act.astype(BF16), w2_ref[0, e], preferred_element_type=F32)
        y_ref[...] = acc

    @pl.when(i >= n_used)
    def _():
        y_ref[...] = jnp.zeros_like(y_ref)


def _moe_grouped(hx, src, tile_group, n_used, w1, w3, w2, tm):
    n_tiles = tile_group.shape[0]
    wspec = lambda k, n: pl.BlockSpec((1, EXPERTS_PER_GROUP, k, n), lambda i, tg, nu: (tg[i], 0, 0, 0))
    grouped = lambda w: w.reshape(N_GROUPS, EXPERTS_PER_GROUP, *w.shape[1:])
    src3 = src.reshape(n_tiles, 1, tm)
    return pl.pallas_call(
        functools.partial(_moe_group_kernel, tm=tm),
        grid_spec=pltpu.PrefetchScalarGridSpec(
            num_scalar_prefetch=2,
            grid=(n_tiles,),
            in_specs=[pl.BlockSpec((1, 1, tm), lambda i, tg, nu: (i, 0, 0), memory_space=pltpu.SMEM),
                      pl.BlockSpec((1, 1, tm), lambda i, tg, nu: (jnp.minimum(i + 1, n_tiles - 1), 0, 0),
                                   memory_space=pltpu.SMEM),
                      pl.BlockSpec(memory_space=pl.ANY),
                      wspec(D_MODEL, D_EXPERT), wspec(D_MODEL, D_EXPERT), wspec(D_EXPERT, D_MODEL)],
            out_specs=pl.BlockSpec((tm, D_MODEL), lambda i, tg, nu: (i, 0)),
            scratch_shapes=[pltpu.VMEM((2, tm, D_MODEL + LANES), F32), pltpu.SemaphoreType.DMA((2,))]),
        out_shape=jax.ShapeDtypeStruct((n_tiles * tm, D_MODEL), F32),
        compiler_params=_params("arbitrary"),
        name="experts_grouped",
    )(tile_group, n_used, src3, src3, hx, grouped(w1), grouped(w3), grouped(w2))


def _combine_kernel(pos_ref, pos_next_ref, y_hbm, x_ref, gate_ref, g2_ref, b2_ref, o_ref, ybuf, sems, *, tm, n_tiles):
    i = pl.program_id(0)
    slot = i % 2

    @pl.when(i == 0)
    def _():
        _start_rows(pos_ref, y_hbm, ybuf.at[0], sems.at[0], tm)

    @pl.when(i + 1 < n_tiles)
    def _():
        _start_rows(pos_next_ref, y_hbm, ybuf.at[1 - slot], sems.at[1 - slot], tm)

    _wait_rows(y_hbm, ybuf.at[slot], sems.at[slot], tm)
    y = ALPHA * x_ref[...] + gate_ref[...] * ybuf[slot]
    o_ref[...] = _ln(y) * g2_ref[...] + b2_ref[...]


def _combine(y_sorted, pos, x1, gate2, ln_g, ln_b, tm):
    rows = x1.shape[0]
    n_tiles = rows // tm
    row = pl.BlockSpec((tm, D_MODEL), lambda i: (i, 0))
    const = pl.BlockSpec((1, D_MODEL), lambda i: (0, 0))
    pos3 = pos.reshape(n_tiles, 1, tm)
    return pl.pallas_call(
        functools.partial(_combine_kernel, tm=tm, n_tiles=n_tiles),
        grid=(n_tiles,),
        in_specs=[pl.BlockSpec((1, 1, tm), lambda i: (i, 0, 0), memory_space=pltpu.SMEM),
                  pl.BlockSpec((1, 1, tm), lambda i: (jnp.minimum(i + 1, n_tiles - 1), 0, 0), memory_space=pltpu.SMEM),
                  pl.BlockSpec(memory_space=pl.ANY),
                  row, _row_spec(gate2.shape[0], tm, D_MODEL, rows), const, const],
        out_specs=row,
        out_shape=jax.ShapeDtypeStruct((rows, D_MODEL), F32),
        scratch_shapes=[pltpu.VMEM((2, tm, D_MODEL), F32), pltpu.SemaphoreType.DMA((2,))],
        compiler_params=_params("arbitrary"),
        name="combine",
    )(pos3, pos3, y_sorted, x1, gate2, ln_g.reshape(1, D_MODEL), ln_b.reshape(1, D_MODEL))


def _trunk_layer(x, mods, bsz, seq, pos0, s0, k_past, v_past, p, tiles):
    shift1, scale1, gate1, shift2, scale2, gate2 = mods
    uv, ret, q_c, k_c, v_c, k_bf, v_bf = _inproj(x, shift1, scale1, p["w_in"], p["layer"], tiles["tm_in"])
    a_out, v_rows = _gmlp(uv, p["w_sp"], p["b_sp"], p["ln_v_g"], p["ln_v_b"], min(seq, GMLP_CHUNK))
    b_out, s_new = _retention(ret, s0, pos0, bsz, seq, p["gn_g"], p["gn_b"])
    if k_past is None:
        c_out = _stick_breaking(q_c, k_bf, v_bf, bsz, seq, tiles["sb"])
    else:
        c_out = _stick_breaking_step(q_c, k_bf, v_bf, k_past, v_past, p["layer"], bsz, seq, tiles["sb"])
    grouped = "tm_group" in tiles
    post = _post(a_out, b_out, c_out, x, p["w_out"], p["layer"], gate1, p["ln1_g"], p["ln1_b"],
                 shift2, scale2, p["wr"], p["br"], tiles["tm_post"], grouped)
    if grouped:
        x1, comb, hx = post
        tm = tiles["tm_group"]
        pos, src, tile_group, n_used = _group_plan(comb[:, N_EXPERTS].astype(jnp.int32), tm)
        y_sorted = _moe_grouped(hx, src, tile_group, n_used, p["w1"], p["w3"], p["w2"], tm)
        y = _combine(y_sorted, pos, x1, gate2, p["ln2_g"], p["ln2_b"], tm)
    else:
        x1, h2, comb = post
        y = _moe(h2, comb, x1, gate2, p["ln2_g"], p["ln2_b"], p["w1"], p["w3"], p["w2"], tiles["tm_moe"])
    return y, v_rows, s_new, k_c, v_c


def kernel(x_prompt, x_sample, cache_sb_k, cache_sb_v, state_ret, c_prompt, c_sample, w_ada, b_ada, w_in, w_out, ln_v_g, ln_v_b, w_spatial, b_spatial, gn_g, gn_b, ln1_g, ln1_b, ln2_g, ln2_b, w_router, b_router, w1, w3, w2):
    bp, tp, _ = x_prompt.shape
    bs, ts, _ = x_sample.shape
    past_len = cache_sb_k.shape[2]
    assert bp == 1

    n_c = bp + bs
    c_rows = -(-n_c // 8) * 8
    c_all = jnp.concatenate([c_prompt, c_sample, jnp.zeros((c_rows - n_c, D_MODEL), F32)], axis=0)
    mods = _adaln(c_all, w_ada, b_ada)

    wr_t = w_router.T
    cache_kt = jnp.transpose(cache_sb_k, (0, 1, 3, 4, 2))
    cache_vt = jnp.transpose(cache_sb_v, (0, 1, 3, 4, 2))
    br_col = b_router.reshape(N_EXPERTS, 1)

    tiles_p = dict(tm_in=256, sb=256, tm_post=512, tm_group=256)
    tiles_s = dict(tm_in=bs * ts, sb=256, tm_post=bs * ts, tm_moe=bs * ts)

    y_p = x_prompt.reshape(bp * tp, D_MODEL)
    y_s = x_sample.reshape(bs * ts, D_MODEL)
    zero_state = jnp.zeros((bp, H_B, HEAD_DIM, HEAD_DIM), F32)
    outs = [[] for _ in range(7)]
    for l in range(DEPTH):
        p = dict(w_in=w_in, w_out=w_out,
                 w_sp=w_spatial[l], b_sp=b_spatial[l], ln_v_g=ln_v_g[l], ln_v_b=ln_v_b[l],
                 gn_g=gn_g[l], gn_b=gn_b[l], ln1_g=ln1_g[l], ln1_b=ln1_b[l],
                 ln2_g=ln2_g[l], ln2_b=ln2_b[l], wr=wr_t, br=br_col,
                 w1=w1[l].astype(BF16), w3=w3[l].astype(BF16), w2=w2[l].astype(BF16), layer=l)
        m = mods[l]
        mods_p = [m[0:1, i * D_MODEL:(i + 1) * D_MODEL] for i in range(6)]
        mods_s = [jnp.repeat(m[bp:bp + bs, i * D_MODEL:(i + 1) * D_MODEL], ts, axis=0) for i in range(6)]
        y_p, _, s_p, k_p, v_p = _trunk_layer(y_p, mods_p, bp, tp, 0, zero_state, None, None, p, tiles_p)
        y_s, g_s, s_s, k_s, v_s = _trunk_layer(
            y_s, mods_s, bs, ts, past_len, state_ret[l],
            cache_kt, cache_vt, p, tiles_s)
        outs[0].append(s_p)
        outs[1].append(k_p.reshape(bp, tp, H_C, HEAD_DIM))
        outs[2].append(v_p.reshape(bp, tp, H_C, HEAD_DIM))
        outs[3].append(s_s)
        outs[4].append(k_s.reshape(bs, ts, H_C, HEAD_DIM))
        outs[5].append(v_s.reshape(bs, ts, H_C, HEAD_DIM))
        outs[6].append(g_s.reshape(bs, ts, W_A))
    return (y_p.reshape(bp, tp, D_MODEL), y_s.reshape(bs, ts, D_MODEL)) + tuple(jnp.stack(o) for o in outs)
```

```python
import functools

import numpy as np
import jax
import jax.numpy as jnp
from jax import lax
from jax.experimental import pallas as pl
from jax.experimental.pallas import tpu as pltpu

F32 = jnp.float32
BF16 = jnp.bfloat16
HIGHEST = lax.Precision.HIGHEST

D_MODEL = 1024
DEPTH = 2
HEAD_DIM = 64
W_A = D_MODEL // 4
W_B = 3 * D_MODEL // 8
W_C = D_MODEL - W_A - W_B
H_A = W_A // HEAD_DIM
H_B = W_B // HEAD_DIM
H_C = W_C // HEAD_DIM
GMLP_CHUNK = 128
ROPE_BASE = 10000.0
N_EXPERTS = 16
N_GROUPS = 4
EXPERTS_PER_GROUP = N_EXPERTS // N_GROUPS
D_EXPERT = D_MODEL // 2
ALPHA = (2 * DEPTH) ** 0.25
LN_EPS = 1e-5
D_IN = 2 * W_A + 4 * W_B + 3 * W_C
LANES = 128
VMEM_LIMIT = 48 * 1024 * 1024

NT_DIMS = (((1,), (1,)), ((), ()))
TN_DIMS = (((0,), (0,)), ((), ()))
INV_LN2 = 1.4426950408889634
SB_QSCALE = HEAD_DIM ** -0.5 * INV_LN2
SB_DEAD = -152.0
GMLP_SUB = 8
RET_BLOCK = 512


def _ln(x):
    mu = jnp.mean(x, axis=-1, keepdims=True)
    xc = x - mu
    var = jnp.mean(xc * xc, axis=-1, keepdims=True)
    return xc * lax.rsqrt(var + LN_EPS)


def _silu(x):
    return x * jax.nn.sigmoid(x)


def _params(*sem):
    return pltpu.CompilerParams(dimension_semantics=sem, vmem_limit_bytes=VMEM_LIMIT)


def _row_spec(rows, tm, width, total_rows):
    if rows == 1:
        return pl.BlockSpec((1, width), lambda i: (0, 0))
    assert rows == total_rows
    return pl.BlockSpec((tm, width), lambda i: (i, 0))


def _adaln_kernel(c_ref, w_ref, b_ref, o_ref):
    sc = _silu(c_ref[...])
    o_ref[0] = jnp.dot(sc, w_ref[0], preferred_element_type=F32, precision=HIGHEST) + b_ref[0]


def _adaln(c_all, w_ada, b_ada):
    rows = c_all.shape[0]
    tn = 1536
    return pl.pallas_call(
        _adaln_kernel,
        grid=(DEPTH, 6 * D_MODEL // tn),
        in_specs=[pl.BlockSpec((rows, D_MODEL), lambda l, j: (0, 0)),
                  pl.BlockSpec((1, D_MODEL, tn), lambda l, j: (l, 0, j)),
                  pl.BlockSpec((1, 1, tn), lambda l, j: (l, 0, j))],
        out_specs=pl.BlockSpec((1, rows, tn), lambda l, j: (l, 0, j)),
        out_shape=jax.ShapeDtypeStruct((DEPTH, rows, 6 * D_MODEL), F32),
        compiler_params=_params("parallel", "parallel"),
        name="adaln",
    )(c_all, w_ada, b_ada.reshape(DEPTH, 1, 6 * D_MODEL))


def _inproj_kernel(x_ref, shift_ref, scale_ref, w_ref, uv_ref, ret_ref, q_ref, k_ref, v_ref, kb_ref, vb_ref, wb_ref):
    @pl.when(pl.program_id(0) == 0)
    def _():
        wb_ref[...] = w_ref[0].astype(BF16)

    h = _ln(x_ref[...]) * (1.0 + scale_ref[...]) + shift_ref[...]
    r = jnp.dot(h.astype(BF16), wb_ref[...], preferred_element_type=F32)
    c0 = 2 * W_A
    c1 = c0 + 4 * W_B
    uv_ref[...] = r[:, :c0]
    ret_ref[...] = r[:, c0:c1]
    q_ref[...] = r[:, c1:c1 + W_C]
    k = r[:, c1 + W_C:c1 + 2 * W_C]
    v = r[:, c1 + 2 * W_C:]
    k_ref[...] = k
    v_ref[...] = v
    kb_ref[...] = k.astype(BF16)
    vb_ref[...] = v.astype(BF16)


def _inproj(x, shift, scale, w_in, layer, tm):
    rows = x.shape[0]
    widths = (2 * W_A, 4 * W_B, W_C, W_C, W_C, W_C, W_C)
    dtypes = (F32,) * 5 + (BF16,) * 2
    return pl.pallas_call(
        _inproj_kernel,
        grid=(rows // tm,),
        in_specs=[pl.BlockSpec((tm, D_MODEL), lambda i: (i, 0)),
                  _row_spec(shift.shape[0], tm, D_MODEL, rows),
                  _row_spec(scale.shape[0], tm, D_MODEL, rows),
                  pl.BlockSpec((1, D_MODEL, D_IN), lambda i: (layer, 0, 0), pipeline_mode=pl.Buffered(1))],
        out_specs=[pl.BlockSpec((tm, w), lambda i: (i, 0)) for w in widths],
        out_shape=[jax.ShapeDtypeStruct((rows, w), dt) for w, dt in zip(widths, dtypes)],
        scratch_shapes=[pltpu.VMEM((D_MODEL, D_IN), BF16)],
        compiler_params=_params("arbitrary"),
        name="inproj",
    )(x, shift, scale, w_in)


def _gmlp_kernel(uv_ref, wsp_ref, bias_ref, g_ref, b_ref, a_ref, vn_ref, *, chunk, n_sub):
    uv = uv_ref[...]
    u = jax.nn.gelu(uv[:, :W_A])
    v = _ln(jax.nn.gelu(uv[:, W_A:])) * g_ref[...] + b_ref[...]
    vn_ref[...] = v
    row = lax.broadcasted_iota(jnp.int32, (chunk, chunk), 0)
    col = lax.broadcasted_iota(jnp.int32, (chunk, chunk), 1)
    lane_head = lax.broadcasted_iota(jnp.int32, (chunk, W_A), 1) // HEAD_DIM
    w = [jnp.where(col <= row, wsp_ref[h], 0.0).astype(BF16) for h in range(H_A)]
    for c in range(n_sub):
        rows = slice(c * chunk, (c + 1) * chunk)
        vc = v[rows]
        mixed = bias_ref[...]
        for h in range(H_A):
            vh = jnp.where(lane_head == h, vc, 0.0).astype(BF16)
            mixed = mixed + jnp.dot(w[h], vh, preferred_element_type=F32)
        a_ref[rows, :] = u[rows] * mixed


def _gmlp(uv, w_sp, b_sp, ln_g, ln_b, chunk):
    rows = uv.shape[0]
    n_sub = min(GMLP_SUB, rows // chunk)
    tm = n_sub * chunk
    wsp = w_sp[:, :chunk, :chunk]
    bias = jnp.repeat(b_sp[:, :chunk].T, HEAD_DIM, axis=1)
    return pl.pallas_call(
        functools.partial(_gmlp_kernel, chunk=chunk, n_sub=n_sub),
        grid=(rows // tm,),
        in_specs=[pl.BlockSpec((tm, 2 * W_A), lambda i: (i, 0)),
                  pl.BlockSpec((H_A, chunk, chunk), lambda i: (0, 0, 0)),
                  pl.BlockSpec((chunk, W_A), lambda i: (0, 0)),
                  pl.BlockSpec((1, W_A), lambda i: (0, 0)),
                  pl.BlockSpec((1, W_A), lambda i: (0, 0))],
        out_specs=[pl.BlockSpec((tm, W_A), lambda i: (i, 0)),
                   pl.BlockSpec((tm, W_A), lambda i: (i, 0))],
        out_shape=[jax.ShapeDtypeStruct((rows, W_A), F32),
                   jax.ShapeDtypeStruct((rows, W_A), F32)],
        compiler_params=_params("parallel"),
        name="gmlp",
    )(uv, wsp, bias, ln_g.reshape(1, W_A), ln_b.reshape(1, W_A))


def _rope(x, cos, sin):
    lane = lax.broadcasted_iota(jnp.int32, (x.shape[0], LANES), 1)
    first_half = (lane & (HEAD_DIM // 2)) == 0
    parts = []
    for c in range(x.shape[1] // LANES):
        xc = x[:, c * LANES:(c + 1) * LANES]
        rot = jnp.where(first_half,
                        pltpu.roll(xc, LANES - HEAD_DIM // 2, 1),
                        pltpu.roll(xc, HEAD_DIM // 2, 1))
        parts.append(xc * cos + rot * sin)
    return jnp.concatenate(parts, axis=1)


def _ret_kernel(r_ref, cos_ref, sin_ref, qdec_ref, kdec_ref, dec_ref, blk_ref, s0_ref,
                gng_ref, gnb_ref, o_ref, sout_ref, s_scr, o_scr, *, n_blocks):
    n = pl.program_id(1)

    @pl.when(n == 0)
    def _():
        s_scr[...] = s0_ref[0]

    r = r_ref[...]
    cos = cos_ref[...]
    sin = sin_ref[...]
    qr = _rope(r[:, :W_B], cos, sin)
    kr = _rope(r[:, W_B:2 * W_B], cos, sin) * (HEAD_DIM ** -0.5)
    vb = r[:, 2 * W_B:3 * W_B].astype(BF16)
    gate = r[:, 3 * W_B:]
    qb = qr.astype(BF16)
    kb = kr.astype(BF16)
    qdb = (qr * qdec_ref[...]).astype(BF16)
    kdb = (kr * kdec_ref[...]).astype(BF16)
    for h in range(H_B):
        sl = slice(h * HEAD_DIM, (h + 1) * HEAD_DIM)
        scores = lax.dot_general(qb[:, sl], kb[:, sl], NT_DIMS, preferred_element_type=F32) * dec_ref[h]
        s_h = s_scr[h]
        o_h = (jnp.dot(scores.astype(BF16), vb[:, sl], preferred_element_type=F32)
               + jnp.dot(qdb[:, sl], s_h.astype(BF16), preferred_element_type=F32))
        s_scr[h] = s_h * blk_ref[h] + lax.dot_general(kdb[:, sl], vb[:, sl], TN_DIMS,
                                                      preferred_element_type=F32)
        o_scr[:, sl] = _ln(o_h)
    o_ref[...] = (o_scr[...] * gng_ref[...] + gnb_ref[...]) * _silu(gate)

    @pl.when(n == n_blocks - 1)
    def _():
        sout_ref[0] = s_scr[...]


def _retention(ret, s0, pos0, bsz, seq, gn_g, gn_b):
    L = min(seq, RET_BLOCK)
    n_blocks = seq // L
    half = HEAD_DIM // 2
    inv = ROPE_BASE ** (-jnp.arange(half, dtype=F32) / half)
    ang = (pos0 + jnp.arange(seq)).astype(F32)[:, None] * inv[None, :]
    cos, sin = jnp.cos(ang), jnp.sin(ang)
    cos_t = jnp.tile(jnp.concatenate([cos, cos], axis=1), (1, LANES // HEAD_DIM))
    sin_t = jnp.tile(jnp.concatenate([-sin, sin], axis=1), (1, LANES // HEAD_DIM))
    log_g = jnp.log1p(-jnp.exp2(-5.0 - jnp.arange(H_B, dtype=F32)))
    idx = jnp.arange(L, dtype=F32)
    diff = idx[:, None] - idx[None, :]
    decay = jnp.where(diff >= 0, jnp.exp(diff[None] * log_g[:, None, None]), 0.0)
    q_decay = jnp.exp((idx[None, :] + 1.0) * log_g[:, None])
    k_decay = jnp.exp((L - 1.0 - idx[None, :]) * log_g[:, None])
    blk_decay = jnp.exp(L * log_g)
    qdec = jnp.repeat(q_decay.T, HEAD_DIM, axis=1)
    kdec = jnp.repeat(k_decay.T, HEAD_DIM, axis=1)
    blk = jnp.broadcast_to(blk_decay[:, None, None], (H_B, HEAD_DIM, HEAD_DIM))
    const2 = lambda b, n: (0, 0)
    const3 = lambda b, n: (0, 0, 0)
    return pl.pallas_call(
        functools.partial(_ret_kernel, n_blocks=n_blocks),
        grid=(bsz, n_blocks),
        in_specs=[pl.BlockSpec((L, 4 * W_B), lambda b, n: (b * n_blocks + n, 0)),
                  pl.BlockSpec((L, LANES), lambda b, n: (n, 0)),
                  pl.BlockSpec((L, LANES), lambda b, n: (n, 0)),
                  pl.BlockSpec((L, W_B), const2),
                  pl.BlockSpec((L, W_B), const2),
                  pl.BlockSpec((H_B, L, L), const3),
                  pl.BlockSpec((H_B, HEAD_DIM, HEAD_DIM), const3),
                  pl.BlockSpec((1, H_B, HEAD_DIM, HEAD_DIM), lambda b, n: (b, 0, 0, 0)),
                  pl.BlockSpec((1, W_B), const2),
                  pl.BlockSpec((1, W_B), const2)],
        out_specs=[pl.BlockSpec((L, W_B), lambda b, n: (b * n_blocks + n, 0)),
                   pl.BlockSpec((1, H_B, HEAD_DIM, HEAD_DIM), lambda b, n: (b, 0, 0, 0))],
        out_shape=[jax.ShapeDtypeStruct((bsz * seq, W_B), F32),
                   jax.ShapeDtypeStruct((bsz, H_B, HEAD_DIM, HEAD_DIM), F32)],
        scratch_shapes=[pltpu.VMEM((H_B, HEAD_DIM, HEAD_DIM), F32),
                        pltpu.VMEM((L, W_B), F32)],
        compiler_params=_params("parallel", "arbitrary"),
        name="retention",
    )(ret, cos_t, sin_t, qdec, kdec, decay, blk, s0, gn_g.reshape(1, W_B), gn_b.reshape(1, W_B))


def _sb_block(qm_ref, k_ref, v_ref, u2_ref, acc_ref, carry_ref, causal):
    tk = k_ref.shape[0]
    half = lax.broadcasted_iota(jnp.int32, (tk, LANES), 1) // HEAD_DIM
    u2 = u2_ref[...]
    kp = [k_ref[:, p * LANES:(p + 1) * LANES] for p in range(H_C // 2)]

    def scores(h):
        z = lax.dot_general(qm_ref[h], kp[h // 2], NT_DIMS, preferred_element_type=F32)
        neg_abs = pltpu.bitcast(pltpu.bitcast(z, jnp.uint32) | jnp.uint32(0x80000000), F32)
        ls_pos = jnp.minimum(z, 0.0) - jnp.log(1.0 + jnp.exp2(neg_abs)) * INV_LN2
        log_stay = ls_pos - z
        if causal is not None:
            log_stay = jnp.where(causal, log_stay, 0.0)
        hi = log_stay.astype(BF16)
        lo = (log_stay - hi.astype(F32)).astype(BF16)
        return ls_pos, log_stay[:, :1], jnp.concatenate([hi, lo], axis=1)

    def cumsum(hi_lo):
        return jnp.dot(hi_lo, u2, preferred_element_type=F32)

    def weigh(h, ls_pos, first_col, excl):
        carry = carry_ref[h]
        att = jnp.exp2(ls_pos + excl + carry)
        if causal is not None:
            att = jnp.where(causal, att, 0.0)
        vp = v_ref[:, (h // 2) * LANES:(h // 2 + 1) * LANES]
        vp = jnp.where(half == h % 2, vp, jnp.zeros_like(vp))
        carry_ref[h] = carry + (excl[:, :1] + first_col)
        return jnp.dot(att.astype(BF16), vp, preferred_element_type=F32)

    stage_a, stage_b, outs = {}, {}, {}
    for step in range(H_C + 2):
        if step - 2 >= 0:
            h = step - 2
            outs[h] = weigh(h, stage_a[h][0], stage_a[h][1], stage_b.pop(h))
            del stage_a[h]
            if h % 2 == 1:
                p = h // 2
                acc_ref[:, p * LANES:(p + 1) * LANES] += outs.pop(h - 1) + outs.pop(h)
        if 0 <= step - 1 < H_C:
            stage_b[step - 1] = cumsum(stage_a[step - 1][2])
        if step < H_C:
            stage_a[step] = scores(step)


def _sb_kernel(q_ref, k_ref, v_ref, u2_ref, o_ref, qm_ref, acc_ref, carry_ref, *, t):
    i = pl.program_id(1)
    acc_ref[...] = jnp.zeros_like(acc_ref)
    carry_ref[...] = jnp.zeros_like(carry_ref)
    half = lax.broadcasted_iota(jnp.int32, (t, LANES), 1) // HEAD_DIM
    for h in range(H_C):
        qp = q_ref[:, (h // 2) * LANES:(h // 2 + 1) * LANES] * SB_QSCALE
        qm_ref[h] = jnp.where(half == h % 2, qp, 0.0).astype(BF16)

    def block(j, causal):
        k0 = pl.multiple_of((i - j) * t, t)
        _sb_block(qm_ref, k_ref.at[pl.ds(k0, t), :], v_ref.at[pl.ds(k0, t), :], u2_ref, acc_ref, carry_ref, causal)

    block(0, lax.broadcasted_iota(jnp.int32, (t, t), 1) < lax.broadcasted_iota(jnp.int32, (t, t), 0))

    def key_block(state):
        j, _ = state
        block(j, None)
        dead = jnp.max(carry_ref[...]) < SB_DEAD
        return j + 1, dead.astype(jnp.int32)

    lax.while_loop(lambda state: jnp.logical_and(state[0] <= i, state[1] == 0),
                   key_block, (jnp.int32(1), jnp.int32(0)))
    o_ref[...] = acc_ref[...]


def _stick_breaking(q, k, v, bsz, seq, t):
    nq = seq // t
    tri = np.tril(np.ones((t, t), np.float32), -1)
    u2 = jnp.asarray(np.concatenate([tri, tri], axis=0), dtype=BF16)
    resident = dict(pipeline_mode=pl.Buffered(1))
    return pl.pallas_call(
        functools.partial(_sb_kernel, t=t),
        grid=(bsz, nq),
        in_specs=[pl.BlockSpec((t, W_C), lambda b, i: (b * nq + i, 0)),
                  pl.BlockSpec((seq, W_C), lambda b, i: (b, 0), **resident),
                  pl.BlockSpec((seq, W_C), lambda b, i: (b, 0), **resident),
                  pl.BlockSpec((2 * t, t), lambda b, i: (0, 0), **resident)],
        out_specs=pl.BlockSpec((t, W_C), lambda b, i: (b * nq + i, 0)),
        out_shape=jax.ShapeDtypeStruct((bsz * seq, W_C), F32),
        scratch_shapes=[pltpu.VMEM((H_C, t, LANES), BF16),
                        pltpu.VMEM((t, W_C), F32),
                        pltpu.VMEM((H_C, t, 1), F32)],
        compiler_params=_params("parallel", "arbitrary"),
        name="stick_breaking",
    )(q, k, v, u2)


def _sb_scores(z, causal=None):
    neg_abs = pltpu.bitcast(pltpu.bitcast(z, jnp.uint32) | jnp.uint32(0x80000000), F32)
    ls_pos = jnp.minimum(z, 0.0) - jnp.log(1.0 + jnp.exp2(neg_abs)) * INV_LN2
    log_stay = ls_pos - z
    if causal is not None:
        log_stay = jnp.where(causal, log_stay, 0.0)
    hi = log_stay.astype(BF16)
    lo = (log_stay - hi.astype(F32)).astype(BF16)
    return ls_pos, log_stay, jnp.concatenate([hi, lo], axis=1)


def _sb_step_kernel(q_ref, kt_ref, vt_ref, kn_ref, vn_ref, u2_ref, un_ref, o_ref, acc_ref, carry_ref, *, tk):
    seq = q_ref.shape[0]
    past = kt_ref.shape[-1]
    qs = (q_ref[...] * SB_QSCALE).astype(BF16)
    row = lax.broadcasted_iota(jnp.int32, (seq, seq), 0)
    col = lax.broadcasted_iota(jnp.int32, (seq, seq), 1)
    causal = col < row
    un = un_ref[...]
    for h in range(H_C):
        sl = slice(h * HEAD_DIM, (h + 1) * HEAD_DIM)
        z = lax.dot_general(qs[:, sl], kn_ref[:, sl], NT_DIMS, preferred_element_type=F32)
        ls_pos, log_stay, hi_lo = _sb_scores(z, causal)
        excl = jnp.dot(hi_lo, un, preferred_element_type=F32)
        att = jnp.where(causal, jnp.exp2(ls_pos + excl), 0.0)
        acc_ref[h] = jnp.dot(att.astype(BF16), vn_ref[:, sl], preferred_element_type=F32)
        carry_ref[h] = excl[:, :1] + log_stay[:, :1]

    u2 = u2_ref[...]

    def key_block(state):
        j, _ = state
        k0 = pl.multiple_of(past - (j + 1) * tk, tk)
        for h in range(H_C):
            kt = kt_ref[0, 0, h, :, pl.ds(k0, tk)].astype(BF16)
            vt = vt_ref[0, 0, h, :, pl.ds(k0, tk)].astype(BF16)
            z = jnp.dot(qs[:, h * HEAD_DIM:(h + 1) * HEAD_DIM], kt, preferred_element_type=F32)
            ls_pos, log_stay, hi_lo = _sb_scores(z)
            excl = jnp.dot(hi_lo, u2, preferred_element_type=F32)
            carry = carry_ref[h]
            att = jnp.exp2(ls_pos + excl + carry)
            acc_ref[h] += lax.dot_general(att.astype(BF16), vt, NT_DIMS, preferred_element_type=F32)
            carry_ref[h] = carry + (excl[:, :1] + log_stay[:, :1])
        dead = jnp.max(carry_ref[...]) < SB_DEAD
        return j + 1, dead.astype(jnp.int32)

    lax.while_loop(lambda state: jnp.logical_and(state[0] < past // tk, state[1] == 0),
                   key_block, (jnp.int32(0), jnp.int32(0)))
    for h in range(H_C):
        o_ref[:, h * HEAD_DIM:(h + 1) * HEAD_DIM] = acc_ref[h]


def _stick_breaking_step(q, k_new, v_new, cache_kt, cache_vt, layer, bsz, seq, tk):
    past = cache_kt.shape[-1]
    assert past % tk == 0
    tri = lambda n: np.tril(np.ones((n, n), np.float32), -1)
    stacked = lambda n: jnp.asarray(np.concatenate([tri(n), tri(n)], axis=0), dtype=BF16)
    rows = pl.BlockSpec((seq, W_C), lambda b: (b, 0))
    cache = pl.BlockSpec((1, 1, H_C, HEAD_DIM, past), lambda b: (layer, b, 0, 0, 0))
    return pl.pallas_call(
        functools.partial(_sb_step_kernel, tk=tk),
        grid=(bsz,),
        in_specs=[rows, cache, cache, rows, rows,
                  pl.BlockSpec((2 * tk, tk), lambda b: (0, 0)), pl.BlockSpec((2 * seq, seq), lambda b: (0, 0))],
        out_specs=rows,
        out_shape=jax.ShapeDtypeStruct((bsz * seq, W_C), F32),
        scratch_shapes=[pltpu.VMEM((H_C, seq, HEAD_DIM), F32), pltpu.VMEM((H_C, seq, 1), F32)],
        compiler_params=_params("parallel"),
        name="stick_breaking_step",
    )(q, cache_kt, cache_vt, k_new, v_new, stacked(tk), stacked(seq))


def _route(sel, s):
    g_scores = []
    for g in range(N_GROUPS):
        a, b, c, d = sel[EXPERTS_PER_GROUP * g:EXPERTS_PER_GROUP * (g + 1)]
        ab_hi, ab_lo = jnp.maximum(a, b), jnp.minimum(a, b)
        cd_hi, cd_lo = jnp.maximum(c, d), jnp.minimum(c, d)
        top1 = jnp.maximum(ab_hi, cd_hi)
        top2 = jnp.maximum(jnp.minimum(ab_hi, cd_hi), jnp.maximum(ab_lo, cd_lo))
        g_scores.append(top1 + top2)
    best = g_scores[0]
    gi = jnp.zeros(best.shape, jnp.int32)
    for g in range(1, N_GROUPS):
        upd = g_scores[g] > best
        gi = jnp.where(upd, g, gi)
        best = jnp.where(upd, g_scores[g], best)

    def pick_group(rows, l):
        out = rows[(N_GROUPS - 1) * EXPERTS_PER_GROUP + l]
        for g in range(N_GROUPS - 2, -1, -1):
            out = jnp.where(gi == g, rows[g * EXPERTS_PER_GROUP + l], out)
        return out

    ig = [pick_group(sel, l) for l in range(EXPERTS_PER_GROUP)]
    sg = [pick_group(s, l) for l in range(EXPERTS_PER_GROUP)]
    b1 = ig[0]
    i1 = jnp.zeros(best.shape, jnp.int32)
    for l in range(1, EXPERTS_PER_GROUP):
        upd = ig[l] > b1
        i1 = jnp.where(upd, l, i1)
        b1 = jnp.where(upd, ig[l], b1)
    b2 = jnp.full(best.shape, -jnp.inf, F32)
    i2 = jnp.zeros(best.shape, jnp.int32)
    for l in range(EXPERTS_PER_GROUP):
        upd = jnp.logical_and(i1 != l, ig[l] > b2)
        i2 = jnp.where(upd, l, i2)
        b2 = jnp.where(upd, ig[l], b2)

    def pick_local(idx):
        out = sg[EXPERTS_PER_GROUP - 1]
        for l in range(EXPERTS_PER_GROUP - 2, -1, -1):
            out = jnp.where(idx == l, sg[l], out)
        return out

    w1 = pick_local(i1)
    w2 = pick_local(i2)
    tot = w1 + w2
    return gi * EXPERTS_PER_GROUP + i1, gi * EXPERTS_PER_GROUP + i2, w1 / tot, w2 / tot


def _post_kernel(a_ref, b_ref, c_ref, x_ref, wo_ref, gate_ref, g1_ref, b1_ref, sh2_ref, sc2_ref,
                 wrt_ref, br_ref, x1_ref, *rest, tm, with_hx):
    if with_hx:
        h2_ref = None
        comb_ref, hx_ref, wob_ref = rest
    else:
        hx_ref = None
        h2_ref, comb_ref, wob_ref = rest
    @pl.when(pl.program_id(0) == 0)
    def _():
        wob_ref[...] = wo_ref[0].astype(BF16)

    proj = (jnp.dot(a_ref[...].astype(BF16), wob_ref[:W_A], preferred_element_type=F32)
            + jnp.dot(b_ref[...].astype(BF16), wob_ref[W_A:W_A + W_B], preferred_element_type=F32)
            + jnp.dot(c_ref[...].astype(BF16), wob_ref[W_A + W_B:], preferred_element_type=F32))
    x1 = _ln(ALPHA * x_ref[...] + gate_ref[...] * proj) * g1_ref[...] + b1_ref[...]
    x1_ref[...] = x1
    h2 = _ln(x1) * (1.0 + sc2_ref[...]) + sh2_ref[...]
    if h2_ref is not None:
        h2_ref[...] = h2.astype(BF16)
    logits_t = lax.dot_general(wrt_ref[...], h2, NT_DIMS, preferred_element_type=F32, precision=HIGHEST)
    s_t = jax.nn.sigmoid(logits_t)
    sel_t = s_t + br_ref[...]
    s = [s_t[e:e + 1, :] for e in range(N_EXPERTS)]
    sel = [sel_t[e:e + 1, :] for e in range(N_EXPERTS)]
    e1, e2, w1, w2 = _route(sel, s)
    expert = lax.broadcasted_iota(jnp.int32, (LANES, tm), 0)
    comb_t = jnp.where(expert == e1, w1, jnp.where(expert == e2, w2, 0.0))
    group = (e1 // EXPERTS_PER_GROUP).astype(F32)
    comb_t = jnp.where(expert == N_EXPERTS, group, comb_t)
    comb_ref[...] = comb_t.T
    if hx_ref is not None:
        hx_ref[:, :D_MODEL] = h2
        hx_ref[:, D_MODEL:] = comb_t.T


def _post(a, b, c, x, w_out, layer, gate1, ln_g, ln_b, shift2, scale2, wr_t, br_col, tm, with_hx):
    rows = x.shape[0]
    if with_hx:
        tail_spec = [pl.BlockSpec((tm, LANES), lambda i: (i, 0)), pl.BlockSpec((tm, D_MODEL + LANES), lambda i: (i, 0))]
        tail_shape = [jax.ShapeDtypeStruct((rows, LANES), F32), jax.ShapeDtypeStruct((rows, D_MODEL + LANES), F32)]
    else:
        tail_spec = [pl.BlockSpec((tm, D_MODEL), lambda i: (i, 0)), pl.BlockSpec((tm, LANES), lambda i: (i, 0))]
        tail_shape = [jax.ShapeDtypeStruct((rows, D_MODEL), BF16), jax.ShapeDtypeStruct((rows, LANES), F32)]
    row = lambda w: pl.BlockSpec((tm, w), lambda i: (i, 0))
    const = lambda r, w: pl.BlockSpec((r, w), lambda i: (0, 0))
    return pl.pallas_call(
        functools.partial(_post_kernel, tm=tm, with_hx=with_hx),
        grid=(rows // tm,),
        in_specs=[row(W_A), row(W_B), row(W_C), row(D_MODEL),
                  pl.BlockSpec((1, D_MODEL, D_MODEL), lambda i: (layer, 0, 0), pipeline_mode=pl.Buffered(1)),
                  _row_spec(gate1.shape[0], tm, D_MODEL, rows),
                  const(1, D_MODEL), const(1, D_MODEL),
                  _row_spec(shift2.shape[0], tm, D_MODEL, rows),
                  _row_spec(scale2.shape[0], tm, D_MODEL, rows),
                  const(N_EXPERTS, D_MODEL), const(N_EXPERTS, 1)],
        out_specs=[row(D_MODEL)] + tail_spec,
        out_shape=[jax.ShapeDtypeStruct((rows, D_MODEL), F32)] + tail_shape,
        scratch_shapes=[pltpu.VMEM((D_MODEL, D_MODEL), BF16)],
        compiler_params=_params("arbitrary"),
        name="post_mix",
    )(a, b, c, x, w_out, gate1, ln_g.reshape(1, D_MODEL), ln_b.reshape(1, D_MODEL),
      shift2, scale2, wr_t, br_col)


def _moe_kernel(h_ref, comb_ref, x_ref, gate_ref, g2_ref, b2_ref, w1_ref, w3_ref, w2_ref,
                o_ref, acc_ref, *, tm):
    e = pl.program_id(1)

    @pl.when(e == 0)
    def _():
        acc_ref[...] = jnp.zeros_like(acc_ref)

    h = h_ref[...]
    a = jnp.dot(h, w1_ref[0], preferred_element_type=F32)
    g = jnp.dot(h, w3_ref[0], preferred_element_type=F32)
    lane = lax.broadcasted_iota(jnp.int32, (tm, LANES), 1)
    ce = jnp.sum(jnp.where(lane == e, comb_ref[...], 0.0), axis=-1, keepdims=True)
    act = _silu(a) * g * ce
    acc_ref[...] += jnp.dot(act.astype(BF16), w2_ref[0], preferred_element_type=F32)

    @pl.when(e == N_EXPERTS - 1)
    def _():
        y = ALPHA * x_ref[...] + gate_ref[...] * acc_ref[...]
        o_ref[...] = _ln(y) * g2_ref[...] + b2_ref[...]


def _moe(h2, comb, x1, gate2, ln_g, ln_b, w1, w3, w2, tm):
    rows = x1.shape[0]
    row = lambda w: pl.BlockSpec((tm, w), lambda i, e: (i, 0))
    const = pl.BlockSpec((1, D_MODEL), lambda i, e: (0, 0))
    gate_spec = (pl.BlockSpec((1, D_MODEL), lambda i, e: (0, 0)) if gate2.shape[0] == 1
                 else row(D_MODEL))
    return pl.pallas_call(
        functools.partial(_moe_kernel, tm=tm),
        grid=(rows // tm, N_EXPERTS),
        in_specs=[row(D_MODEL), row(LANES), row(D_MODEL), gate_spec, const, const,
                  pl.BlockSpec((1, D_MODEL, D_EXPERT), lambda i, e: (e, 0, 0)),
                  pl.BlockSpec((1, D_MODEL, D_EXPERT), lambda i, e: (e, 0, 0)),
                  pl.BlockSpec((1, D_EXPERT, D_MODEL), lambda i, e: (e, 0, 0))],
        out_specs=row(D_MODEL),
        out_shape=jax.ShapeDtypeStruct((rows, D_MODEL), F32),
        scratch_shapes=[pltpu.VMEM((tm, D_MODEL), F32)],
        compiler_params=_params("parallel", "arbitrary"),
        name="experts",
    )(h2, comb, x1, gate2, ln_g.reshape(1, D_MODEL), ln_b.reshape(1, D_MODEL), w1, w3, w2)


def _group_plan(group, tm):
    rows = group.shape[0]
    n_tiles = rows // tm + N_GROUPS
    onehot = (group[:, None] == jnp.arange(N_GROUPS, dtype=jnp.int32)[None, :]).astype(jnp.int32)
    rank = jnp.cumsum(onehot, axis=0) - onehot
    tiles_per = (jnp.sum(onehot, axis=0) + tm - 1) // tm
    tile_start = jnp.cumsum(tiles_per) - tiles_per
    pos = jnp.sum(onehot * (tile_start[None, :] * tm + rank), axis=1).astype(jnp.int32)
    src = jnp.zeros((n_tiles * tm,), jnp.int32).at[pos].set(jnp.arange(rows, dtype=jnp.int32))
    tile = jnp.arange(n_tiles, dtype=jnp.int32)
    tile_group = jnp.clip(jnp.sum((tile[:, None] >= tile_start[None, :]).astype(jnp.int32), axis=1) - 1,
                          0, N_GROUPS - 1).astype(jnp.int32)
    n_used = jnp.sum(tiles_per).astype(jnp.int32).reshape(1)
    return pos, src, tile_group, n_used


def _start_rows(idx_ref, src_hbm, dst_ref, sem, n):
    def issue(r, carry):
        pltpu.make_async_copy(src_hbm.at[pl.ds(idx_ref[0, 0, r], 1)], dst_ref.at[pl.ds(r, 1)], sem).start()
        return carry

    lax.fori_loop(0, n, issue, 0, unroll=True)


def _wait_rows(src_hbm, dst_ref, sem, n):
    pltpu.make_async_copy(src_hbm.at[pl.ds(0, n)], dst_ref, sem).wait()


def _moe_group_kernel(tg_ref, nused_ref, src_ref, src_next_ref, hx_hbm, w1_ref, w3_ref, w2_ref, y_ref,
                      xbuf, sems, *, tm):
    i = pl.program_id(0)
    n_used = nused_ref[0]
    slot = i % 2

    @pl.when(jnp.logical_and(i == 0, n_used > 0))
    def _():
        _start_rows(src_ref, hx_hbm, xbuf.at[0], sems.at[0], tm)

    @pl.when(i + 1 < n_used)
    def _():
        _start_rows(src_next_ref, hx_hbm, xbuf.at[1 - slot], sems.at[1 - slot], tm)

    @pl.when(i < n_used)
    def _():
        _wait_rows(hx_hbm, xbuf.at[slot], sems.at[slot], tm)
        x = xbuf[slot]
        h = x[:, :D_MODEL].astype(BF16)
        comb = x[:, D_MODEL:]
        lane = lax.broadcasted_iota(jnp.int32, (tm, LANES), 1)
        first = tg_ref[i] * EXPERTS_PER_GROUP
        acc = jnp.zeros((tm, D_MODEL), F32)
        for e in range(EXPERTS_PER_GROUP):
            a = jnp.dot(h, w1_ref[0, e], preferred_element_type=F32)
            g = jnp.dot(h, w3_ref[0, e], preferred_element_type=F32)
            ce = jnp.sum(jnp.where(lane == first + e, comb, 0.0), axis=-1, keepdims=True)
            act = _silu(a) * g * ce
            acc = acc + jnp.dot(act.astype(BF16), w2_ref[0, e], preferred_element_type=F32)
        y_ref[...] = acc

    @pl.when(i >= n_used)
    def _():
        y_ref[...] = jnp.zeros_like(y_ref)


def _moe_grouped(hx, src, tile_group, n_used, w1, w3, w2, tm):
    n_tiles = tile_group.shape[0]
    wspec = lambda k, n: pl.BlockSpec((1, EXPERTS_PER_GROUP, k, n), lambda i, tg, nu: (tg[i], 0, 0, 0))
    grouped = lambda w: w.reshape(N_GROUPS, EXPERTS_PER_GROUP, *w.shape[1:])
    src3 = src.reshape(n_tiles, 1, tm)
    return pl.pallas_call(
        functools.partial(_moe_group_kernel, tm=tm),
        grid_spec=pltpu.PrefetchScalarGridSpec(
            num_scalar_prefetch=2,
            grid=(n_tiles,),
            in_specs=[pl.BlockSpec((1, 1, tm), lambda i, tg, nu: (i, 0, 0), memory_space=pltpu.SMEM),
                      pl.BlockSpec((1, 1, tm), lambda i, tg, nu: (jnp.minimum(i + 1, n_tiles - 1), 0, 0),
                                   memory_space=pltpu.SMEM),
                      pl.BlockSpec(memory_space=pl.ANY),
                      wspec(D_MODEL, D_EXPERT), wspec(D_MODEL, D_EXPERT), wspec(D_EXPERT, D_MODEL)],
            out_specs=pl.BlockSpec((tm, D_MODEL), lambda i, tg, nu: (i, 0)),
            scratch_shapes=[pltpu.VMEM((2, tm, D_MODEL + LANES), F32), pltpu.SemaphoreType.DMA((2,))]),
        out_shape=jax.ShapeDtypeStruct((n_tiles * tm, D_MODEL), F32),
        compiler_params=_params("arbitrary"),
        name="experts_grouped",
    )(tile_group, n_used, src3, src3, hx, grouped(w1), grouped(w3), grouped(w2))


def _combine_kernel(pos_ref, pos_next_ref, y_hbm, x_ref, gate_ref, g2_ref, b2_ref, o_ref, ybuf, sems, *, tm, n_tiles):
    i = pl.program_id(0)
    slot = i % 2

    @pl.when(i == 0)
    def _():
        _start_rows(pos_ref, y_hbm, ybuf.at[0], sems.at[0], tm)

    @pl.when(i + 1 < n_tiles)
    def _():
        _start_rows(pos_next_ref, y_hbm, ybuf.at[1 - slot], sems.at[1 - slot], tm)

    _wait_rows(y_hbm, ybuf.at[slot], sems.at[slot], tm)
    y = ALPHA * x_ref[...] + gate_ref[...] * ybuf[slot]
    o_ref[...] = _ln(y) * g2_ref[...] + b2_ref[...]


def _combine(y_sorted, pos, x1, gate2, ln_g, ln_b, tm):
    rows = x1.shape[0]
    n_tiles = rows // tm
    row = pl.BlockSpec((tm, D_MODEL), lambda i: (i, 0))
    const = pl.BlockSpec((1, D_MODEL), lambda i: (0, 0))
    pos3 = pos.reshape(n_tiles, 1, tm)
    return pl.pallas_call(
        functools.partial(_combine_kernel, tm=tm, n_tiles=n_tiles),
        grid=(n_tiles,),
        in_specs=[pl.BlockSpec((1, 1, tm), lambda i: (i, 0, 0), memory_space=pltpu.SMEM),
                  pl.BlockSpec((1, 1, tm), lambda i: (jnp.minimum(i + 1, n_tiles - 1), 0, 0), memory_space=pltpu.SMEM),
                  pl.BlockSpec(memory_space=pl.ANY),
                  row, _row_spec(gate2.shape[0], tm, D_MODEL, rows), const, const],
        out_specs=row,
        out_shape=jax.ShapeDtypeStruct((rows, D_MODEL), F32),
        scratch_shapes=[pltpu.VMEM((2, tm, D_MODEL), F32), pltpu.SemaphoreType.DMA((2,))],
        compiler_params=_params("arbitrary"),
        name="combine",
    )(pos3, pos3, y_sorted, x1, gate2, ln_g.reshape(1, D_MODEL), ln_b.reshape(1, D_MODEL))


def _trunk_layer(x, mods, bsz, seq, pos0, s0, k_past, v_past, p, tiles):
    shift1, scale1, gate1, shift2, scale2, gate2 = mods
    uv, ret, q_c, k_c, v_c, k_bf, v_bf = _inproj(x, shift1, scale1, p["w_in"], p["layer"], tiles["tm_in"])
    a_out, v_rows = _gmlp(uv, p["w_sp"], p["b_sp"], p["ln_v_g"], p["ln_v_b"], min(seq, GMLP_CHUNK))
    b_out, s_new = _retention(ret, s0, pos0, bsz, seq, p["gn_g"], p["gn_b"])
    if k_past is None:
        c_out = _stick_breaking(q_c, k_bf, v_bf, bsz, seq, tiles["sb"])
    else:
        c_out = _stick_breaking_step(q_c, k_bf, v_bf, k_past, v_past, p["layer"], bsz, seq, tiles["sb"])
    grouped = "tm_group" in tiles
    post = _post(a_out, b_out, c_out, x, p["w_out"], p["layer"], gate1, p["ln1_g"], p["ln1_b"],
                 shift2, scale2, p["wr"], p["br"], tiles["tm_post"], grouped)
    if grouped:
        x1, comb, hx = post
        tm = tiles["tm_group"]
        pos, src, tile_group, n_used = _group_plan(comb[:, N_EXPERTS].astype(jnp.int32), tm)
        y_sorted = _moe_grouped(hx, src, tile_group, n_used, p["w1"], p["w3"], p["w2"], tm)
        y = _combine(y_sorted, pos, x1, gate2, p["ln2_g"], p["ln2_b"], tm)
    else:
        x1, h2, comb = post
        y = _moe(h2, comb, x1, gate2, p["ln2_g"], p["ln2_b"], p["w1"], p["w3"], p["w2"], tiles["tm_moe"])
    return y, v_rows, s_new, k_c, v_c


def kernel(x_prompt, x_sample, cache_sb_k, cache_sb_v, state_ret, c_prompt, c_sample, w_ada, b_ada, w_in, w_out, ln_v_g, ln_v_b, w_spatial, b_spatial, gn_g, gn_b, ln1_g, ln1_b, ln2_g, ln2_b, w_router, b_router, w1, w3, w2):
    bp, tp, _ = x_prompt.shape
    bs, ts, _ = x_sample.shape
    past_len = cache_sb_k.shape[2]
    assert bp == 1

    n_c = bp + bs
    c_rows = -(-n_c // 8) * 8
    c_all = jnp.concatenate([c_prompt, c_sample, jnp.zeros((c_rows - n_c, D_MODEL), F32)], axis=0)
    mods = _adaln(c_all, w_ada, b_ada)

    wr_t = w_router.T
    cache_kt = jnp.transpose(cache_sb_k, (0, 1, 3, 4, 2))
    cache_vt = jnp.transpose(cache_sb_v, (0, 1, 3, 4, 2))
    br_col = b_router.reshape(N_EXPERTS, 1)

    tiles_p = dict(tm_in=256, sb=256, tm_post=512, tm_group=256)
    tiles_s = dict(tm_in=bs * ts, sb=256, tm_post=bs * ts, tm_moe=bs * ts)

    y_p = x_prompt.reshape(bp * tp, D_MODEL)
    y_s = x_sample.reshape(bs * ts, D_MODEL)
    zero_state = jnp.zeros((bp, H_B, HEAD_DIM, HEAD_DIM), F32)
    outs = [[] for _ in range(7)]
    for l in range(DEPTH):
        p = dict(w_in=w_in, w_out=w_out,
                 w_sp=w_spatial[l], b_sp=b_spatial[l], ln_v_g=ln_v_g[l], ln_v_b=ln_v_b[l],
                 gn_g=gn_g[l], gn_b=gn_b[l], ln1_g=ln1_g[l], ln1_b=ln1_b[l],
                 ln2_g=ln2_g[l], ln2_b=ln2_b[l], wr=wr_t, br=br_col,
                 w1=w1[l].astype(BF16), w3=w3[l].astype(BF16), w2=w2[l].astype(BF16), layer=l)
        m = mods[l]
        mods_p = [m[0:1, i * D_MODEL:(i + 1) * D_MODEL] for i in range(6)]
        mods_s = [jnp.repeat(m[bp:bp + bs, i * D_MODEL:(i + 1) * D_MODEL], ts, axis=0) for i in range(6)]
        y_p, _, s_p, k_p, v_p = _trunk_layer(y_p, mods_p, bp, tp, 0, zero_state, None, None, p, tiles_p)
        y_s, g_s, s_s, k_s, v_s = _trunk_layer(
            y_s, mods_s, bs, ts, past_len, state_ret[l],
            cache_kt, cache_vt, p, tiles_s)
        outs[0].append(s_p)
        outs[1].append(k_p.reshape(bp, tp, H_C, HEAD_DIM))
        outs[2].append(v_p.reshape(bp, tp, H_C, HEAD_DIM))
        outs[3].append(s_s)
        outs[4].append(k_s.reshape(bs, ts, H_C, HEAD_DIM))
        outs[5].append(v_s.reshape(bs, ts, H_C, HEAD_DIM))
        outs[6].append(g_s.reshape(bs, ts, W_A))
    return (y_p.reshape(bp, tp, D_MODEL), y_s.reshape(bs, ts, D_MODEL)) + tuple(jnp.stack(o) for o in outs)
```

```python
import functools

import numpy as np
import jax
import jax.numpy as jnp
from jax import lax
from jax.experimental import pallas as pl
from jax.experimental.pallas import tpu as pltpu

F32 = jnp.float32
BF16 = jnp.bfloat16
HIGHEST = lax.Precision.HIGHEST

D_MODEL = 1024
DEPTH = 2
HEAD_DIM = 64
W_A = D_MODEL // 4
W_B = 3 * D_MODEL // 8
W_C = D_MODEL - W_A - W_B
H_A = W_A // HEAD_DIM
H_B = W_B // HEAD_DIM
H_C = W_C // HEAD_DIM
GMLP_CHUNK = 128
ROPE_BASE = 10000.0
N_EXPERTS = 16
N_GROUPS = 4
EXPERTS_PER_GROUP = N_EXPERTS // N_GROUPS
D_EXPERT = D_MODEL // 2
ALPHA = (2 * DEPTH) ** 0.25
LN_EPS = 1e-5
D_IN = 2 * W_A + 4 * W_B + 3 * W_C
LANES = 128
VMEM_LIMIT = 48 * 1024 * 1024

NT_DIMS = (((1,), (1,)), ((), ()))
TN_DIMS = (((0,), (0,)), ((), ()))
INV_LN2 = 1.4426950408889634
SB_QSCALE = HEAD_DIM ** -0.5 * INV_LN2
SB_DEAD = -152.0
GMLP_SUB = 8
POST_SUB = 4
RET_BLOCK = 512


def _ln(x):
    mu = jnp.mean(x, axis=-1, keepdims=True)
    xc = x - mu
    var = jnp.mean(xc * xc, axis=-1, keepdims=True)
    return xc * lax.rsqrt(var + LN_EPS)


def _silu(x):
    return x * jax.nn.sigmoid(x)


def _params(*sem):
    return pltpu.CompilerParams(dimension_semantics=sem, vmem_limit_bytes=VMEM_LIMIT)


def _row_spec(rows, tm, width, total_rows):
    if rows == 1:
        return pl.BlockSpec((1, width), lambda i: (0, 0))
    assert rows == total_rows
    return pl.BlockSpec((tm, width), lambda i: (i, 0))


def _adaln_kernel(c_ref, w_ref, b_ref, o_ref):
    sc = _silu(c_ref[...])
    o_ref[0] = jnp.dot(sc, w_ref[0], preferred_element_type=F32, precision=HIGHEST) + b_ref[0]


def _adaln(c_all, w_ada, b_ada):
    rows = c_all.shape[0]
    tn = 1536
    return pl.pallas_call(
        _adaln_kernel,
        grid=(DEPTH, 6 * D_MODEL // tn),
        in_specs=[pl.BlockSpec((rows, D_MODEL), lambda l, j: (0, 0)),
                  pl.BlockSpec((1, D_MODEL, tn), lambda l, j: (l, 0, j)),
                  pl.BlockSpec((1, 1, tn), lambda l, j: (l, 0, j))],
        out_specs=pl.BlockSpec((1, rows, tn), lambda l, j: (l, 0, j)),
        out_shape=jax.ShapeDtypeStruct((DEPTH, rows, 6 * D_MODEL), F32),
        compiler_params=_params("parallel", "parallel"),
        name="adaln",
    )(c_all, w_ada, b_ada.reshape(DEPTH, 1, 6 * D_MODEL))


def _inproj_kernel(x_ref, shift_ref, scale_ref, w_ref, uv_ref, ret_ref, q_ref, k_ref, v_ref, kb_ref, vb_ref, wb_ref):
    @pl.when(pl.program_id(0) == 0)
    def _():
        wb_ref[...] = w_ref[0].astype(BF16)

    h = _ln(x_ref[...]) * (1.0 + scale_ref[...]) + shift_ref[...]
    r = jnp.dot(h.astype(BF16), wb_ref[...], preferred_element_type=F32)
    c0 = 2 * W_A
    c1 = c0 + 4 * W_B
    uv_ref[...] = r[:, :c0]
    ret_ref[...] = r[:, c0:c1]
    q_ref[...] = r[:, c1:c1 + W_C]
    k = r[:, c1 + W_C:c1 + 2 * W_C]
    v = r[:, c1 + 2 * W_C:]
    k_ref[...] = k
    v_ref[...] = v
    kb_ref[...] = k.astype(BF16)
    vb_ref[...] = v.astype(BF16)


def _inproj(x, shift, scale, w_in, layer, tm):
    rows = x.shape[0]
    widths = (2 * W_A, 4 * W_B, W_C, W_C, W_C, W_C, W_C)
    dtypes = (F32,) * 5 + (BF16,) * 2
    return pl.pallas_call(
        _inproj_kernel,
        grid=(rows // tm,),
        in_specs=[pl.BlockSpec((tm, D_MODEL), lambda i: (i, 0)),
                  _row_spec(shift.shape[0], tm, D_MODEL, rows),
                  _row_spec(scale.shape[0], tm, D_MODEL, rows),
                  pl.BlockSpec((1, D_MODEL, D_IN), lambda i: (layer, 0, 0), pipeline_mode=pl.Buffered(1))],
        out_specs=[pl.BlockSpec((tm, w), lambda i: (i, 0)) for w in widths],
        out_shape=[jax.ShapeDtypeStruct((rows, w), dt) for w, dt in zip(widths, dtypes)],
        scratch_shapes=[pltpu.VMEM((D_MODEL, D_IN), BF16)],
        compiler_params=_params("arbitrary"),
        name="inproj",
    )(x, shift, scale, w_in)


def _gmlp_kernel(uv_ref, wsp_ref, bias_ref, g_ref, b_ref, a_ref, vn_ref, *, chunk, n_sub):
    uv = uv_ref[...]
    u = jax.nn.gelu(uv[:, :W_A])
    v = _ln(jax.nn.gelu(uv[:, W_A:])) * g_ref[...] + b_ref[...]
    vn_ref[...] = v
    row = lax.broadcasted_iota(jnp.int32, (chunk, chunk), 0)
    col = lax.broadcasted_iota(jnp.int32, (chunk, chunk), 1)
    lane_head = lax.broadcasted_iota(jnp.int32, (chunk, W_A), 1) // HEAD_DIM
    w = [jnp.where(col <= row, wsp_ref[h], 0.0).astype(BF16) for h in range(H_A)]
    for c in range(n_sub):
        rows = slice(c * chunk, (c + 1) * chunk)
        vc = v[rows]
        mixed = bias_ref[...]
        for h in range(H_A):
            vh = jnp.where(lane_head == h, vc, 0.0).astype(BF16)
            mixed = mixed + jnp.dot(w[h], vh, preferred_element_type=F32)
        a_ref[rows, :] = u[rows] * mixed


def _gmlp(uv, w_sp, b_sp, ln_g, ln_b, chunk):
    rows = uv.shape[0]
    n_sub = min(GMLP_SUB, rows // chunk)
    tm = n_sub * chunk
    wsp = w_sp[:, :chunk, :chunk]
    bias = jnp.repeat(b_sp[:, :chunk].T, HEAD_DIM, axis=1)
    return pl.pallas_call(
        functools.partial(_gmlp_kernel, chunk=chunk, n_sub=n_sub),
        grid=(rows // tm,),
        in_specs=[pl.BlockSpec((tm, 2 * W_A), lambda i: (i, 0)),
                  pl.BlockSpec((H_A, chunk, chunk), lambda i: (0, 0, 0)),
                  pl.BlockSpec((chunk, W_A), lambda i: (0, 0)),
                  pl.BlockSpec((1, W_A), lambda i: (0, 0)),
                  pl.BlockSpec((1, W_A), lambda i: (0, 0))],
        out_specs=[pl.BlockSpec((tm, W_A), lambda i: (i, 0)),
                   pl.BlockSpec((tm, W_A), lambda i: (i, 0))],
        out_shape=[jax.ShapeDtypeStruct((rows, W_A), F32),
                   jax.ShapeDtypeStruct((rows, W_A), F32)],
        compiler_params=_params("parallel"),
        name="gmlp",
    )(uv, wsp, bias, ln_g.reshape(1, W_A), ln_b.reshape(1, W_A))


def _rope(x, cos, sin):
    lane = lax.broadcasted_iota(jnp.int32, (x.shape[0], LANES), 1)
    first_half = (lane & (HEAD_DIM // 2)) == 0
    parts = []
    for c in range(x.shape[1] // LANES):
        xc = x[:, c * LANES:(c + 1) * LANES]
        rot = jnp.where(first_half,
                        pltpu.roll(xc, LANES - HEAD_DIM // 2, 1),
                        pltpu.roll(xc, HEAD_DIM // 2, 1))
        parts.append(xc * cos + rot * sin)
    return jnp.concatenate(parts, axis=1)


def _ret_kernel(r_ref, cos_ref, sin_ref, qdec_ref, kdec_ref, dec_ref, blk_ref, s0_ref,
                gng_ref, gnb_ref, o_ref, sout_ref, s_scr, o_scr, *, n_blocks):
    n = pl.program_id(1)

    @pl.when(n == 0)
    def _():
        s_scr[...] = s0_ref[0]

    r = r_ref[...]
    cos = cos_ref[...]
    sin = sin_ref[...]
    qr = _rope(r[:, :W_B], cos, sin)
    kr = _rope(r[:, W_B:2 * W_B], cos, sin) * (HEAD_DIM ** -0.5)
    vb = r[:, 2 * W_B:3 * W_B].astype(BF16)
    gate = r[:, 3 * W_B:]
    qb = qr.astype(BF16)
    kb = kr.astype(BF16)
    qdb = (qr * qdec_ref[...]).astype(BF16)
    kdb = (kr * kdec_ref[...]).astype(BF16)
    for h in range(H_B):
        sl = slice(h * HEAD_DIM, (h + 1) * HEAD_DIM)
        scores = lax.dot_general(qb[:, sl], kb[:, sl], NT_DIMS, preferred_element_type=F32) * dec_ref[h]
        s_h = s_scr[h]
        o_h = (jnp.dot(scores.astype(BF16), vb[:, sl], preferred_element_type=F32)
               + jnp.dot(qdb[:, sl], s_h.astype(BF16), preferred_element_type=F32))
        s_scr[h] = s_h * blk_ref[h] + lax.dot_general(kdb[:, sl], vb[:, sl], TN_DIMS,
                                                      preferred_element_type=F32)
        o_scr[:, sl] = _ln(o_h)
    o_ref[...] = (o_scr[...] * gng_ref[...] + gnb_ref[...]) * _silu(gate)

    @pl.when(n == n_blocks - 1)
    def _():
        sout_ref[0] = s_scr[...]


def _retention(ret, s0, pos0, bsz, seq, gn_g, gn_b):
    L = min(seq, RET_BLOCK)
    n_blocks = seq // L
    half = HEAD_DIM // 2
    inv = ROPE_BASE ** (-jnp.arange(half, dtype=F32) / half)
    ang = (pos0 + jnp.arange(seq)).astype(F32)[:, None] * inv[None, :]
    cos, sin = jnp.cos(ang), jnp.sin(ang)
    cos_t = jnp.tile(jnp.concatenate([cos, cos], axis=1), (1, LANES // HEAD_DIM))
    sin_t = jnp.tile(jnp.concatenate([-sin, sin], axis=1), (1, LANES // HEAD_DIM))
    log_g = jnp.log1p(-jnp.exp2(-5.0 - jnp.arange(H_B, dtype=F32)))
    idx = jnp.arange(L, dtype=F32)
    diff = idx[:, None] - idx[None, :]
    decay = jnp.where(diff >= 0, jnp.exp(diff[None] * log_g[:, None, None]), 0.0)
    q_decay = jnp.exp((idx[None, :] + 1.0) * log_g[:, None])
    k_decay = jnp.exp((L - 1.0 - idx[None, :]) * log_g[:, None])
    blk_decay = jnp.exp(L * log_g)
    qdec = jnp.repeat(q_decay.T, HEAD_DIM, axis=1)
    kdec = jnp.repeat(k_decay.T, HEAD_DIM, axis=1)
    blk = jnp.broadcast_to(blk_decay[:, None, None], (H_B, HEAD_DIM, HEAD_DIM))
    const2 = lambda b, n: (0, 0)
    const3 = lambda b, n: (0, 0, 0)
    return pl.pallas_call(
        functools.partial(_ret_kernel, n_blocks=n_blocks),
        grid=(bsz, n_blocks),
        in_specs=[pl.BlockSpec((L, 4 * W_B), lambda b, n: (b * n_blocks + n, 0)),
                  pl.BlockSpec((L, LANES), lambda b, n: (n, 0)),
                  pl.BlockSpec((L, LANES), lambda b, n: (n, 0)),
                  pl.BlockSpec((L, W_B), const2),
                  pl.BlockSpec((L, W_B), const2),
                  pl.BlockSpec((H_B, L, L), const3),
                  pl.BlockSpec((H_B, HEAD_DIM, HEAD_DIM), const3),
                  pl.BlockSpec((1, H_B, HEAD_DIM, HEAD_DIM), lambda b, n: (b, 0, 0, 0)),
                  pl.BlockSpec((1, W_B), const2),
                  pl.BlockSpec((1, W_B), const2)],
        out_specs=[pl.BlockSpec((L, W_B), lambda b, n: (b * n_blocks + n, 0)),
                   pl.BlockSpec((1, H_B, HEAD_DIM, HEAD_DIM), lambda b, n: (b, 0, 0, 0))],
        out_shape=[jax.ShapeDtypeStruct((bsz * seq, W_B), F32),
                   jax.ShapeDtypeStruct((bsz, H_B, HEAD_DIM, HEAD_DIM), F32)],
        scratch_shapes=[pltpu.VMEM((H_B, HEAD_DIM, HEAD_DIM), F32),
                        pltpu.VMEM((L, W_B), F32)],
        compiler_params=_params("parallel", "arbitrary"),
        name="retention",
    )(ret, cos_t, sin_t, qdec, kdec, decay, blk, s0, gn_g.reshape(1, W_B), gn_b.reshape(1, W_B))


def _sb_block(qm_ref, k_ref, v_ref, u2_ref, acc_ref, carry_ref, causal):
    tk = k_ref.shape[0]
    half = lax.broadcasted_iota(jnp.int32, (tk, LANES), 1) // HEAD_DIM
    u2 = u2_ref[...]
    kp = [k_ref[:, p * LANES:(p + 1) * LANES] for p in range(H_C // 2)]

    def scores(h):
        z = lax.dot_general(qm_ref[h], kp[h // 2], NT_DIMS, preferred_element_type=F32)
        neg_abs = pltpu.bitcast(pltpu.bitcast(z, jnp.uint32) | jnp.uint32(0x80000000), F32)
        ls_pos = jnp.minimum(z, 0.0) - jnp.log(1.0 + jnp.exp2(neg_abs)) * INV_LN2
        log_stay = ls_pos - z
        if causal is not None:
            log_stay = jnp.where(causal, log_stay, 0.0)
        hi = log_stay.astype(BF16)
        lo = (log_stay - hi.astype(F32)).astype(BF16)
        return ls_pos, log_stay[:, :1], jnp.concatenate([hi, lo], axis=1)

    def cumsum(hi_lo):
        return jnp.dot(hi_lo, u2, preferred_element_type=F32)

    def weigh(h, ls_pos, first_col, excl):
        carry = carry_ref[h]
        att = jnp.exp2(ls_pos + excl + carry)
        if causal is not None:
            att = jnp.where(causal, att, 0.0)
        vp = v_ref[:, (h // 2) * LANES:(h // 2 + 1) * LANES]
        vp = jnp.where(half == h % 2, vp, jnp.zeros_like(vp))
        carry_ref[h] = carry + (excl[:, :1] + first_col)
        return jnp.dot(att.astype(BF16), vp, preferred_element_type=F32)

    stage_a, stage_b, outs = {}, {}, {}
    for step in range(H_C + 2):
        if step - 2 >= 0:
            h = step - 2
            outs[h] = weigh(h, stage_a[h][0], stage_a[h][1], stage_b.pop(h))
            del stage_a[h]
            if h % 2 == 1:
                p = h // 2
                acc_ref[:, p * LANES:(p + 1) * LANES] += outs.pop(h - 1) + outs.pop(h)
        if 0 <= step - 1 < H_C:
            stage_b[step - 1] = cumsum(stage_a[step - 1][2])
        if step < H_C:
            stage_a[step] = scores(step)


def _sb_kernel(q_ref, k_ref, v_ref, u2_ref, o_ref, qm_ref, acc_ref, carry_ref, *, t):
    i = pl.program_id(1)
    acc_ref[...] = jnp.zeros_like(acc_ref)
    carry_ref[...] = jnp.zeros_like(carry_ref)
    half = lax.broadcasted_iota(jnp.int32, (t, LANES), 1) // HEAD_DIM
    for h in range(H_C):
        qp = q_ref[:, (h // 2) * LANES:(h // 2 + 1) * LANES] * SB_QSCALE
        qm_ref[h] = jnp.where(half == h % 2, qp, 0.0).astype(BF16)

    def block(j, causal):
        k0 = pl.multiple_of((i - j) * t, t)
        _sb_block(qm_ref, k_ref.at[pl.ds(k0, t), :], v_ref.at[pl.ds(k0, t), :], u2_ref, acc_ref, carry_ref, causal)

    block(0, lax.broadcasted_iota(jnp.int32, (t, t), 1) < lax.broadcasted_iota(jnp.int32, (t, t), 0))

    def key_block(state):
        j, _ = state
        block(j, None)
        dead = jnp.max(carry_ref[...]) < SB_DEAD
        return j + 1, dead.astype(jnp.int32)

    lax.while_loop(lambda state: jnp.logical_and(state[0] <= i, state[1] == 0),
                   key_block, (jnp.int32(1), jnp.int32(0)))
    o_ref[...] = acc_ref[...]


def _stick_breaking(q, k, v, bsz, seq, t):
    nq = seq // t
    tri = np.tril(np.ones((t, t), np.float32), -1)
    u2 = jnp.asarray(np.concatenate([tri, tri], axis=0), dtype=BF16)
    resident = dict(pipeline_mode=pl.Buffered(1))
    return pl.pallas_call(
        functools.partial(_sb_kernel, t=t),
        grid=(bsz, nq),
        in_specs=[pl.BlockSpec((t, W_C), lambda b, i: (b * nq + i, 0)),
                  pl.BlockSpec((seq, W_C), lambda b, i: (b, 0), **resident),
                  pl.BlockSpec((seq, W_C), lambda b, i: (b, 0), **resident),
                  pl.BlockSpec((2 * t, t), lambda b, i: (0, 0), **resident)],
        out_specs=pl.BlockSpec((t, W_C), lambda b, i: (b * nq + i, 0)),
        out_shape=jax.ShapeDtypeStruct((bsz * seq, W_C), F32),
        scratch_shapes=[pltpu.VMEM((H_C, t, LANES), BF16),
                        pltpu.VMEM((t, W_C), F32),
                        pltpu.VMEM((H_C, t, 1), F32)],
        compiler_params=_params("parallel", "arbitrary"),
        name="stick_breaking",
    )(q, k, v, u2)


def _sb_scores(z, causal=None):
    neg_abs = pltpu.bitcast(pltpu.bitcast(z, jnp.uint32) | jnp.uint32(0x80000000), F32)
    ls_pos = jnp.minimum(z, 0.0) - jnp.log(1.0 + jnp.exp2(neg_abs)) * INV_LN2
    log_stay = ls_pos - z
    if causal is not None:
        log_stay = jnp.where(causal, log_stay, 0.0)
    hi = log_stay.astype(BF16)
    lo = (log_stay - hi.astype(F32)).astype(BF16)
    return ls_pos, log_stay, jnp.concatenate([hi, lo], axis=1)


def _sb_step_kernel(q_ref, kt_ref, vt_ref, kn_ref, vn_ref, u2_ref, un_ref, o_ref, acc_ref, carry_ref, *, tk):
    seq = q_ref.shape[0]
    past = kt_ref.shape[-1]
    qs = (q_ref[...] * SB_QSCALE).astype(BF16)
    row = lax.broadcasted_iota(jnp.int32, (seq, seq), 0)
    col = lax.broadcasted_iota(jnp.int32, (seq, seq), 1)
    causal = col < row
    un = un_ref[...]
    for h in range(H_C):
        sl = slice(h * HEAD_DIM, (h + 1) * HEAD_DIM)
        z = lax.dot_general(qs[:, sl], kn_ref[:, sl], NT_DIMS, preferred_element_type=F32)
        ls_pos, log_stay, hi_lo = _sb_scores(z, causal)
        excl = jnp.dot(hi_lo, un, preferred_element_type=F32)
        att = jnp.where(causal, jnp.exp2(ls_pos + excl), 0.0)
        acc_ref[h] = jnp.dot(att.astype(BF16), vn_ref[:, sl], preferred_element_type=F32)
        carry_ref[h] = excl[:, :1] + log_stay[:, :1]

    u2 = u2_ref[...]

    def key_block(state):
        j, _ = state
        k0 = pl.multiple_of(past - (j + 1) * tk, tk)
        for h in range(H_C):
            kt = kt_ref[0, 0, h, :, pl.ds(k0, tk)].astype(BF16)
            vt = vt_ref[0, 0, h, :, pl.ds(k0, tk)].astype(BF16)
            z = jnp.dot(qs[:, h * HEAD_DIM:(h + 1) * HEAD_DIM], kt, preferred_element_type=F32)
            ls_pos, log_stay, hi_lo = _sb_scores(z)
            excl = jnp.dot(hi_lo, u2, preferred_element_type=F32)
            carry = carry_ref[h]
            att = jnp.exp2(ls_pos + excl + carry)
            acc_ref[h] += lax.dot_general(att.astype(BF16), vt, NT_DIMS, preferred_element_type=F32)
            carry_ref[h] = carry + (excl[:, :1] + log_stay[:, :1])
        dead = jnp.max(carry_ref[...]) < SB_DEAD
        return j + 1, dead.astype(jnp.int32)

    lax.while_loop(lambda state: jnp.logical_and(state[0] < past // tk, state[1] == 0),
                   key_block, (jnp.int32(0), jnp.int32(0)))
    for h in range(H_C):
        o_ref[:, h * HEAD_DIM:(h + 1) * HEAD_DIM] = acc_ref[h]


def _stick_breaking_step(q, k_new, v_new, cache_kt, cache_vt, layer, bsz, seq, tk):
    past = cache_kt.shape[-1]
    assert past % tk == 0
    tri = lambda n: np.tril(np.ones((n, n), np.float32), -1)
    stacked = lambda n: jnp.asarray(np.concatenate([tri(n), tri(n)], axis=0), dtype=BF16)
    rows = pl.BlockSpec((seq, W_C), lambda b: (b, 0))
    cache = pl.BlockSpec((1, 1, H_C, HEAD_DIM, past), lambda b: (layer, b, 0, 0, 0))
    return pl.pallas_call(
        functools.partial(_sb_step_kernel, tk=tk),
        grid=(bsz,),
        in_specs=[rows, cache, cache, rows, rows,
                  pl.BlockSpec((2 * tk, tk), lambda b: (0, 0)), pl.BlockSpec((2 * seq, seq), lambda b: (0, 0))],
        out_specs=rows,
        out_shape=jax.ShapeDtypeStruct((bsz * seq, W_C), F32),
        scratch_shapes=[pltpu.VMEM((H_C, seq, HEAD_DIM), F32), pltpu.VMEM((H_C, seq, 1), F32)],
        compiler_params=_params("parallel"),
        name="stick_breaking_step",
    )(q, cache_kt, cache_vt, k_new, v_new, stacked(tk), stacked(seq))


def _route(sel, s):
    g_scores = []
    for g in range(N_GROUPS):
        a, b, c, d = sel[EXPERTS_PER_GROUP * g:EXPERTS_PER_GROUP * (g + 1)]
        ab_hi, ab_lo = jnp.maximum(a, b), jnp.minimum(a, b)
        cd_hi, cd_lo = jnp.maximum(c, d), jnp.minimum(c, d)
        top1 = jnp.maximum(ab_hi, cd_hi)
        top2 = jnp.maximum(jnp.minimum(ab_hi, cd_hi), jnp.maximum(ab_lo, cd_lo))
        g_scores.append(top1 + top2)
    best = g_scores[0]
    gi = jnp.zeros(best.shape, jnp.int32)
    for g in range(1, N_GROUPS):
        upd = g_scores[g] > best
        gi = jnp.where(upd, g, gi)
        best = jnp.where(upd, g_scores[g], best)

    def pick_group(rows, l):
        out = rows[(N_GROUPS - 1) * EXPERTS_PER_GROUP + l]
        for g in range(N_GROUPS - 2, -1, -1):
            out = jnp.where(gi == g, rows[g * EXPERTS_PER_GROUP + l], out)
        return out

    ig = [pick_group(sel, l) for l in range(EXPERTS_PER_GROUP)]
    sg = [pick_group(s, l) for l in range(EXPERTS_PER_GROUP)]
    b1 = ig[0]
    i1 = jnp.zeros(best.shape, jnp.int32)
    for l in range(1, EXPERTS_PER_GROUP):
        upd = ig[l] > b1
        i1 = jnp.where(upd, l, i1)
        b1 = jnp.where(upd, ig[l], b1)
    b2 = jnp.full(best.shape, -jnp.inf, F32)
    i2 = jnp.zeros(best.shape, jnp.int32)
    for l in range(EXPERTS_PER_GROUP):
        upd = jnp.logical_and(i1 != l, ig[l] > b2)
        i2 = jnp.where(upd, l, i2)
        b2 = jnp.where(upd, ig[l], b2)

    def pick_local(idx):
        out = sg[EXPERTS_PER_GROUP - 1]
        for l in range(EXPERTS_PER_GROUP - 2, -1, -1):
            out = jnp.where(idx == l, sg[l], out)
        return out

    w1 = pick_local(i1)
    w2 = pick_local(i2)
    tot = w1 + w2
    return gi * EXPERTS_PER_GROUP + i1, gi * EXPERTS_PER_GROUP + i2, w1 / tot, w2 / tot


def _post_kernel(a_ref, b_ref, c_ref, x_ref, wo_ref, gate_ref, g1_ref, b1_ref, sh2_ref, sc2_ref,
                 wrt_ref, br_ref, x1_ref, *rest, tm, with_hx):
    if with_hx:
        h2_ref = None
        comb_ref, hx_ref, wob_ref = rest
    else:
        hx_ref = None
        h2_ref, comb_ref, wob_ref = rest
    @pl.when(pl.program_id(0) == 0)
    def _():
        wob_ref[...] = wo_ref[0].astype(BF16)

    sub = tm // POST_SUB if tm % (POST_SUB * LANES) == 0 else tm
    tiles = [slice(r0, r0 + sub) for r0 in range(0, tm, sub)]
    per_row = lambda ref, rows: ref[rows, :] if ref.shape[0] == tm else ref[...]
    proj = [jnp.dot(a_ref[rows, :].astype(BF16), wob_ref[:W_A], preferred_element_type=F32)
            + jnp.dot(b_ref[rows, :].astype(BF16), wob_ref[W_A:W_A + W_B], preferred_element_type=F32)
            + jnp.dot(c_ref[rows, :].astype(BF16), wob_ref[W_A + W_B:], preferred_element_type=F32)
            for rows in tiles]
    x1 = [_ln(ALPHA * x_ref[rows, :] + per_row(gate_ref, rows) * pr) * g1_ref[...] + b1_ref[...]
          for rows, pr in zip(tiles, proj)]
    for rows, v in zip(tiles, x1):
        x1_ref[rows, :] = v
    h2 = [_ln(v) * (1.0 + per_row(sc2_ref, rows)) + per_row(sh2_ref, rows) for rows, v in zip(tiles, x1)]
    logits = [lax.dot_general(wrt_ref[...], v, NT_DIMS, preferred_element_type=F32, precision=HIGHEST) for v in h2]
    for rows, v, logits_t in zip(tiles, h2, logits):
        if h2_ref is not None:
            h2_ref[rows, :] = v.astype(BF16)
        s_t = jax.nn.sigmoid(logits_t)
        sel_t = s_t + br_ref[...]
        s = [s_t[e:e + 1, :] for e in range(N_EXPERTS)]
        sel = [sel_t[e:e + 1, :] for e in range(N_EXPERTS)]
        e1, e2, w1, w2 = _route(sel, s)
        expert = lax.broadcasted_iota(jnp.int32, (LANES, sub), 0)
        comb_t = jnp.where(expert == e1, w1, jnp.where(expert == e2, w2, 0.0))
        group = (e1 // EXPERTS_PER_GROUP).astype(F32)
        comb_t = jnp.where(expert == N_EXPERTS, group, comb_t)
        comb_ref[rows, :] = comb_t.T
        if hx_ref is not None:
            hx_ref[rows, :D_MODEL] = v
            hx_ref[rows, D_MODEL:] = comb_t.T


def _post(a, b, c, x, w_out, layer, gate1, ln_g, ln_b, shift2, scale2, wr_t, br_col, tm, with_hx):
    rows = x.shape[0]
    if with_hx:
        tail_spec = [pl.BlockSpec((tm, LANES), lambda i: (i, 0)), pl.BlockSpec((tm, D_MODEL + LANES), lambda i: (i, 0))]
        tail_shape = [jax.ShapeDtypeStruct((rows, LANES), F32), jax.ShapeDtypeStruct((rows, D_MODEL + LANES), F32)]
    else:
        tail_spec = [pl.BlockSpec((tm, D_MODEL), lambda i: (i, 0)), pl.BlockSpec((tm, LANES), lambda i: (i, 0))]
        tail_shape = [jax.ShapeDtypeStruct((rows, D_MODEL), BF16), jax.ShapeDtypeStruct((rows, LANES), F32)]
    row = lambda w: pl.BlockSpec((tm, w), lambda i: (i, 0))
    const = lambda r, w: pl.BlockSpec((r, w), lambda i: (0, 0))
    return pl.pallas_call(
        functools.partial(_post_kernel, tm=tm, with_hx=with_hx),
        grid=(rows // tm,),
        in_specs=[row(W_A), row(W_B), row(W_C), row(D_MODEL),
                  pl.BlockSpec((1, D_MODEL, D_MODEL), lambda i: (layer, 0, 0), pipeline_mode=pl.Buffered(1)),
                  _row_spec(gate1.shape[0], tm, D_MODEL, rows),
                  const(1, D_MODEL), const(1, D_MODEL),
                  _row_spec(shift2.shape[0], tm, D_MODEL, rows),
                  _row_spec(scale2.shape[0], tm, D_MODEL, rows),
                  const(N_EXPERTS, D_MODEL), const(N_EXPERTS, 1)],
        out_specs=[row(D_MODEL)] + tail_spec,
        out_shape=[jax.ShapeDtypeStruct((rows, D_MODEL), F32)] + tail_shape,
        scratch_shapes=[pltpu.VMEM((D_MODEL, D_MODEL), BF16)],
        compiler_params=_params("arbitrary"),
        name="post_mix",
    )(a, b, c, x, w_out, gate1, ln_g.reshape(1, D_MODEL), ln_b.reshape(1, D_MODEL),
      shift2, scale2, wr_t, br_col)


def _moe_kernel(h_ref, comb_ref, x_ref, gate_ref, g2_ref, b2_ref, w1_ref, w3_ref, w2_ref,
                o_ref, acc_ref, *, tm):
    e = pl.program_id(1)

    @pl.when(e == 0)
    def _():
        acc_ref[...] = jnp.zeros_like(acc_ref)

    h = h_ref[...]
    a = jnp.dot(h, w1_ref[0], preferred_element_type=F32)
    g = jnp.dot(h, w3_ref[0], preferred_element_type=F32)
    lane = lax.broadcasted_iota(jnp.int32, (tm, LANES), 1)
    ce = jnp.sum(jnp.where(lane == e, comb_ref[...], 0.0), axis=-1, keepdims=True)
    act = _silu(a) * g * ce
    acc_ref[...] += jnp.dot(act.astype(BF16), w2_ref[0], preferred_element_type=F32)

    @pl.when(e == N_EXPERTS - 1)
    def _():
        y = ALPHA * x_ref[...] + gate_ref[...] * acc_ref[...]
        o_ref[...] = _ln(y) * g2_ref[...] + b2_ref[...]


def _moe(h2, comb, x1, gate2, ln_g, ln_b, w1, w3, w2, tm):
    rows = x1.shape[0]
    row = lambda w: pl.BlockSpec((tm, w), lambda i, e: (i, 0))
    const = pl.BlockSpec((1, D_MODEL), lambda i, e: (0, 0))
    gate_spec = (pl.BlockSpec((1, D_MODEL), lambda i, e: (0, 0)) if gate2.shape[0] == 1
                 else row(D_MODEL))
    return pl.pallas_call(
        functools.partial(_moe_kernel, tm=tm),
        grid=(rows // tm, N_EXPERTS),
        in_specs=[row(D_MODEL), row(LANES), row(D_MODEL), gate_spec, const, const,
                  pl.BlockSpec((1, D_MODEL, D_EXPERT), lambda i, e: (e, 0, 0)),
                  pl.BlockSpec((1, D_MODEL, D_EXPERT), lambda i, e: (e, 0, 0)),
                  pl.BlockSpec((1, D_EXPERT, D_MODEL), lambda i, e: (e, 0, 0))],
        out_specs=row(D_MODEL),
        out_shape=jax.ShapeDtypeStruct((rows, D_MODEL), F32),
        scratch_shapes=[pltpu.VMEM((tm, D_MODEL), F32)],
        compiler_params=_params("parallel", "arbitrary"),
        name="experts",
    )(h2, comb, x1, gate2, ln_g.reshape(1, D_MODEL), ln_b.reshape(1, D_MODEL), w1, w3, w2)


def _group_plan(group, tm):
    rows = group.shape[0]
    n_tiles = rows // tm + N_GROUPS
    onehot = (group[:, None] == jnp.arange(N_GROUPS, dtype=jnp.int32)[None, :]).astype(jnp.int32)
    rank = jnp.cumsum(onehot, axis=0) - onehot
    tiles_per = (jnp.sum(onehot, axis=0) + tm - 1) // tm
    tile_start = jnp.cumsum(tiles_per) - tiles_per
    pos = jnp.sum(onehot * (tile_start[None, :] * tm + rank), axis=1).astype(jnp.int32)
    src = jnp.zeros((n_tiles * tm,), jnp.int32).at[pos].set(jnp.arange(rows, dtype=jnp.int32))
    tile = jnp.arange(n_tiles, dtype=jnp.int32)
    tile_group = jnp.clip(jnp.sum((tile[:, None] >= tile_start[None, :]).astype(jnp.int32), axis=1) - 1,
                          0, N_GROUPS - 1).astype(jnp.int32)
    n_used = jnp.sum(tiles_per).astype(jnp.int32).reshape(1)
    return pos, src, tile_group, n_used


def _start_rows(idx_ref, src_hbm, dst_ref, sem, n):
    def issue(r, carry):
        pltpu.make_async_copy(src_hbm.at[pl.ds(idx_ref[0, 0, r], 1)], dst_ref.at[pl.ds(r, 1)], sem).start()
        return carry

    lax.fori_loop(0, n, issue, 0, unroll=True)


def _wait_rows(src_hbm, dst_ref, sem, n):
    pltpu.make_async_copy(src_hbm.at[pl.ds(0, n)], dst_ref, sem).wait()


def _moe_group_kernel(tg_ref, nused_ref, src_ref, src_next_ref, hx_hbm, w1_ref, w3_ref, w2_ref, y_ref,
                      xbuf, sems, *, tm):
    i = pl.program_id(0)
    n_used = nused_ref[0]
    slot = i % 2

    @pl.when(jnp.logical_and(i == 0, n_used > 0))
    def _():
        _start_rows(src_ref, hx_hbm, xbuf.at[0], sems.at[0], tm)

    @pl.when(i + 1 < n_used)
    def _():
        _start_rows(src_next_ref, hx_hbm, xbuf.at[1 - slot], sems.at[1 - slot], tm)

    @pl.when(i < n_used)
    def _():
        _wait_rows(hx_hbm, xbuf.at[slot], sems.at[slot], tm)
        x = xbuf[slot]
        h = x[:, :D_MODEL].astype(BF16)
        comb = x[:, D_MODEL:]
        lane = lax.broadcasted_iota(jnp.int32, (tm, LANES), 1)
        first = tg_ref[i] * EXPERTS_PER_GROUP
        acc = jnp.zeros((tm, D_MODEL), F32)
        for e in range(EXPERTS_PER_GROUP):
            a = jnp.dot(h, w1_ref[0, e], preferred_element_type=F32)
            g = jnp.dot(h, w3_ref[0, e], preferred_element_type=F32)
            ce = jnp.sum(jnp.where(lane == first + e, comb, 0.0), axis=-1, keepdims=True)
            act = _silu(a) * g * ce
            acc = acc + jnp.dot(act.astype(BF16), w2_ref[0, e], preferred_element_type=F32)
        y_ref[...] = acc

    @pl.when(i >= n_used)
    def _():
        y_ref[...] = jnp.zeros_like(y_ref)


def _moe_grouped(hx, src, tile_group, n_used, w1, w3, w2, tm):
    n_tiles = tile_group.shape[0]
    wspec = lambda k, n: pl.BlockSpec((1, EXPERTS_PER_GROUP, k, n), lambda i, tg, nu: (tg[i], 0, 0, 0))
    grouped = lambda w: w.reshape(N_GROUPS, EXPERTS_PER_GROUP, *w.shape[1:])
    src3 = src.reshape(n_tiles, 1, tm)
    return pl.pallas_call(
        functools.partial(_moe_group_kernel, tm=tm),
        grid_spec=pltpu.PrefetchScalarGridSpec(
            num_scalar_prefetch=2,
            grid=(n_tiles,),
            in_specs=[pl.BlockSpec((1, 1, tm), lambda i, tg, nu: (i, 0, 0), memory_space=pltpu.SMEM),
                      pl.BlockSpec((1, 1, tm), lambda i, tg, nu: (jnp.minimum(i + 1, n_tiles - 1), 0, 0),
                                   memory_space=pltpu.SMEM),
                      pl.BlockSpec(memory_space=pl.ANY),
                      wspec(D_MODEL, D_EXPERT), wspec(D_MODEL, D_EXPERT), wspec(D_EXPERT, D_MODEL)],
            out_specs=pl.BlockSpec((tm, D_MODEL), lambda i, tg, nu: (i, 0)),
            scratch_shapes=[pltpu.VMEM((2, tm, D_MODEL + LANES), F32), pltpu.SemaphoreType.DMA((2,))]),
        out_shape=jax.ShapeDtypeStruct((n_tiles * tm, D_MODEL), F32),
        compiler_params=_params("arbitrary"),
        name="experts_grouped",
    )(tile_group, n_used, src3, src3, hx, grouped(w1), grouped(w3), grouped(w2))


def _combine_kernel(pos_ref, pos_next_ref, y_hbm, x_ref, gate_ref, g2_ref, b2_ref, o_ref, ybuf, sems, *, tm, n_tiles):
    i = pl.program_id(0)
    slot = i % 2

    @pl.when(i == 0)
    def _():
        _start_rows(pos_ref, y_hbm, ybuf.at[0], sems.at[0], tm)

    @pl.when(i + 1 < n_tiles)
    def _():
        _start_rows(pos_next_ref, y_hbm, ybuf.at[1 - slot], sems.at[1 - slot], tm)

    _wait_rows(y_hbm, ybuf.at[slot], sems.at[slot], tm)
    y = ALPHA * x_ref[...] + gate_ref[...] * ybuf[slot]
    o_ref[...] = _ln(y) * g2_ref[...] + b2_ref[...]


def _combine(y_sorted, pos, x1, gate2, ln_g, ln_b, tm):
    rows = x1.shape[0]
    n_tiles = rows // tm
    row = pl.BlockSpec((tm, D_MODEL), lambda i: (i, 0))
    const = pl.BlockSpec((1, D_MODEL), lambda i: (0, 0))
    pos3 = pos.reshape(n_tiles, 1, tm)
    return pl.pallas_call(
        functools.partial(_combine_kernel, tm=tm, n_tiles=n_tiles),
        grid=(n_tiles,),
        in_specs=[pl.BlockSpec((1, 1, tm), lambda i: (i, 0, 0), memory_space=pltpu.SMEM),
                  pl.BlockSpec((1, 1, tm), lambda i: (jnp.minimum(i + 1, n_tiles - 1), 0, 0), memory_space=pltpu.SMEM),
                  pl.BlockSpec(memory_space=pl.ANY),
                  row, _row_spec(gate2.shape[0], tm, D_MODEL, rows), const, const],
        out_specs=row,
        out_shape=jax.ShapeDtypeStruct((rows, D_MODEL), F32),
        scratch_shapes=[pltpu.VMEM((2, tm, D_MODEL), F32), pltpu.SemaphoreType.DMA((2,))],
        compiler_params=_params("arbitrary"),
        name="combine",
    )(pos3, pos3, y_sorted, x1, gate2, ln_g.reshape(1, D_MODEL), ln_b.reshape(1, D_MODEL))


def _trunk_layer(x, mods, bsz, seq, pos0, s0, k_past, v_past, p, tiles):
    shift1, scale1, gate1, shift2, scale2, gate2 = mods
    uv, ret, q_c, k_c, v_c, k_bf, v_bf = _inproj(x, shift1, scale1, p["w_in"], p["layer"], tiles["tm_in"])
    a_out, v_rows = _gmlp(uv, p["w_sp"], p["b_sp"], p["ln_v_g"], p["ln_v_b"], min(seq, GMLP_CHUNK))
    b_out, s_new = _retention(ret, s0, pos0, bsz, seq, p["gn_g"], p["gn_b"])
    if k_past is None:
        c_out = _stick_breaking(q_c, k_bf, v_bf, bsz, seq, tiles["sb"])
    else:
        c_out = _stick_breaking_step(q_c, k_bf, v_bf, k_past, v_past, p["layer"], bsz, seq, tiles["sb"])
    grouped = "tm_group" in tiles
    post = _post(a_out, b_out, c_out, x, p["w_out"], p["layer"], gate1, p["ln1_g"], p["ln1_b"],
                 shift2, scale2, p["wr"], p["br"], tiles["tm_post"], grouped)
    if grouped:
        x1, comb, hx = post
        tm = tiles["tm_group"]
        pos, src, tile_group, n_used = _group_plan(comb[:, N_EXPERTS].astype(jnp.int32), tm)
        y_sorted = _moe_grouped(hx, src, tile_group, n_used, p["w1"], p["w3"], p["w2"], tm)
        y = _combine(y_sorted, pos, x1, gate2, p["ln2_g"], p["ln2_b"], tm)
    else:
        x1, h2, comb = post
        y = _moe(h2, comb, x1, gate2, p["ln2_g"], p["ln2_b"], p["w1"], p["w3"], p["w2"], tiles["tm_moe"])
    return y, v_rows, s_new, k_c, v_c


def kernel(x_prompt, x_sample, cache_sb_k, cache_sb_v, state_ret, c_prompt, c_sample, w_ada, b_ada, w_in, w_out, ln_v_g, ln_v_b, w_spatial, b_spatial, gn_g, gn_b, ln1_g, ln1_b, ln2_g, ln2_b, w_router, b_router, w1, w3, w2):
    bp, tp, _ = x_prompt.shape
    bs, ts, _ = x_sample.shape
    past_len = cache_sb_k.shape[2]
    assert bp == 1

    n_c = bp + bs
    c_rows = -(-n_c // 8) * 8
    c_all = jnp.concatenate([c_prompt, c_sample, jnp.zeros((c_rows - n_c, D_MODEL), F32)], axis=0)
    mods = _adaln(c_all, w_ada, b_ada)

    wr_t = w_router.T
    cache_kt = jnp.transpose(cache_sb_k, (0, 1, 3, 4, 2))
    cache_vt = jnp.transpose(cache_sb_v, (0, 1, 3, 4, 2))
    br_col = b_router.reshape(N_EXPERTS, 1)

    tiles_p = dict(tm_in=256, sb=256, tm_post=512, tm_group=256)
    tiles_s = dict(tm_in=bs * ts, sb=256, tm_post=bs * ts, tm_moe=bs * ts)

    y_p = x_prompt.reshape(bp * tp, D_MODEL)
    y_s = x_sample.reshape(bs * ts, D_MODEL)
    zero_state = jnp.zeros((bp, H_B, HEAD_DIM, HEAD_DIM), F32)
    outs = [[] for _ in range(7)]
    for l in range(DEPTH):
        p = dict(w_in=w_in, w_out=w_out,
                 w_sp=w_spatial[l], b_sp=b_spatial[l], ln_v_g=ln_v_g[l], ln_v_b=ln_v_b[l],
                 gn_g=gn_g[l], gn_b=gn_b[l], ln1_g=ln1_g[l], ln1_b=ln1_b[l],
                 ln2_g=ln2_g[l], ln2_b=ln2_b[l], wr=wr_t, br=br_col,
                 w1=w1[l].astype(BF16), w3=w3[l].astype(BF16), w2=w2[l].astype(BF16), layer=l)
        m = mods[l]
        mods_p = [m[0:1, i * D_MODEL:(i + 1) * D_MODEL] for i in range(6)]
        mods_s = [jnp.repeat(m[bp:bp + bs, i * D_MODEL:(i + 1) * D_MODEL], ts, axis=0) for i in range(6)]
        y_p, _, s_p, k_p, v_p = _trunk_layer(y_p, mods_p, bp, tp, 0, zero_state, None, None, p, tiles_p)
        y_s, g_s, s_s, k_s, v_s = _trunk_layer(
            y_s, mods_s, bs, ts, past_len, state_ret[l],
            cache_kt, cache_vt, p, tiles_s)
        outs[0].append(s_p)
        outs[1].append(k_p.reshape(bp, tp, H_C, HEAD_DIM))
        outs[2].append(v_p.reshape(bp, tp, H_C, HEAD_DIM))
        outs[3].append(s_s)
        outs[4].append(k_s.reshape(bs, ts, H_C, HEAD_DIM))
        outs[5].append(v_s.reshape(bs, ts, H_C, HEAD_DIM))
        outs[6].append(g_s.reshape(bs, ts, W_A))
    return (y_p.reshape(bp, tp, D_MODEL), y_s.reshape(bs, ts, D_MODEL)) + tuple(jnp.stack(o) for o in outs)
```

```python
import functools

import numpy as np
import jax
import jax.numpy as jnp
from jax import lax
from jax.experimental import pallas as pl
from jax.experimental.pallas import tpu as pltpu

F32 = jnp.float32
BF16 = jnp.bfloat16
HIGHEST = lax.Precision.HIGHEST

D_MODEL = 1024
DEPTH = 2
HEAD_DIM = 64
W_A = D_MODEL // 4
W_B = 3 * D_MODEL // 8
W_C = D_MODEL - W_A - W_B
H_A = W_A // HEAD_DIM
H_B = W_B // HEAD_DIM
H_C = W_C // HEAD_DIM
GMLP_CHUNK = 128
ROPE_BASE = 10000.0
N_EXPERTS = 16
N_GROUPS = 4
EXPERTS_PER_GROUP = N_EXPERTS // N_GROUPS
D_EXPERT = D_MODEL // 2
PAIRS_PER_GROUP = EXPERTS_PER_GROUP * (EXPERTS_PER_GROUP - 1) // 2
N_PAIRS = N_GROUPS * PAIRS_PER_GROUP
ALPHA = (2 * DEPTH) ** 0.25
LN_EPS = 1e-5
D_IN = 2 * W_A + 4 * W_B + 3 * W_C
LANES = 128
VMEM_LIMIT = 48 * 1024 * 1024

NT_DIMS = (((1,), (1,)), ((), ()))
TN_DIMS = (((0,), (0,)), ((), ()))
INV_LN2 = 1.4426950408889634
SB_QSCALE = HEAD_DIM ** -0.5 * INV_LN2
SB_DEAD = -152.0
GMLP_SUB = 8
POST_SUB = 4
RET_BLOCK = 512


def _ln(x):
    mu = jnp.mean(x, axis=-1, keepdims=True)
    xc = x - mu
    var = jnp.mean(xc * xc, axis=-1, keepdims=True)
    return xc * lax.rsqrt(var + LN_EPS)


def _silu(x):
    return x * jax.nn.sigmoid(x)


def _params(*sem):
    return pltpu.CompilerParams(dimension_semantics=sem, vmem_limit_bytes=VMEM_LIMIT)


def _row_spec(rows, tm, width, total_rows):
    if rows == 1:
        return pl.BlockSpec((1, width), lambda i: (0, 0))
    assert rows == total_rows
    return pl.BlockSpec((tm, width), lambda i: (i, 0))


def _adaln_kernel(c_ref, w_ref, b_ref, o_ref):
    sc = _silu(c_ref[...])
    o_ref[0] = jnp.dot(sc, w_ref[0], preferred_element_type=F32, precision=HIGHEST) + b_ref[0]


def _adaln(c_all, w_ada, b_ada):
    rows = c_all.shape[0]
    tn = 1536
    return pl.pallas_call(
        _adaln_kernel,
        grid=(DEPTH, 6 * D_MODEL // tn),
        in_specs=[pl.BlockSpec((rows, D_MODEL), lambda l, j: (0, 0)),
                  pl.BlockSpec((1, D_MODEL, tn), lambda l, j: (l, 0, j)),
                  pl.BlockSpec((1, 1, tn), lambda l, j: (l, 0, j))],
        out_specs=pl.BlockSpec((1, rows, tn), lambda l, j: (l, 0, j)),
        out_shape=jax.ShapeDtypeStruct((DEPTH, rows, 6 * D_MODEL), F32),
        compiler_params=_params("parallel", "parallel"),
        name="adaln",
    )(c_all, w_ada, b_ada.reshape(DEPTH, 1, 6 * D_MODEL))


def _inproj_kernel(x_ref, shift_ref, scale_ref, w_ref, uv_ref, ret_ref, q_ref, k_ref, v_ref, kb_ref, vb_ref, wb_ref):
    @pl.when(pl.program_id(0) == 0)
    def _():
        wb_ref[...] = w_ref[0].astype(BF16)

    h = _ln(x_ref[...]) * (1.0 + scale_ref[...]) + shift_ref[...]
    r = jnp.dot(h.astype(BF16), wb_ref[...], preferred_element_type=F32)
    c0 = 2 * W_A
    c1 = c0 + 4 * W_B
    uv_ref[...] = r[:, :c0]
    ret_ref[...] = r[:, c0:c1]
    q_ref[...] = r[:, c1:c1 + W_C]
    k = r[:, c1 + W_C:c1 + 2 * W_C]
    v = r[:, c1 + 2 * W_C:]
    k_ref[...] = k
    v_ref[...] = v
    kb_ref[...] = k.astype(BF16)
    vb_ref[...] = v.astype(BF16)


def _inproj(x, shift, scale, w_in, layer, tm):
    rows = x.shape[0]
    widths = (2 * W_A, 4 * W_B, W_C, W_C, W_C, W_C, W_C)
    dtypes = (F32,) * 5 + (BF16,) * 2
    return pl.pallas_call(
        _inproj_kernel,
        grid=(rows // tm,),
        in_specs=[pl.BlockSpec((tm, D_MODEL), lambda i: (i, 0)),
                  _row_spec(shift.shape[0], tm, D_MODEL, rows),
                  _row_spec(scale.shape[0], tm, D_MODEL, rows),
                  pl.BlockSpec((1, D_MODEL, D_IN), lambda i: (layer, 0, 0), pipeline_mode=pl.Buffered(1))],
        out_specs=[pl.BlockSpec((tm, w), lambda i: (i, 0)) for w in widths],
        out_shape=[jax.ShapeDtypeStruct((rows, w), dt) for w, dt in zip(widths, dtypes)],
        scratch_shapes=[pltpu.VMEM((D_MODEL, D_IN), BF16)],
        compiler_params=_params("arbitrary"),
        name="inproj",
    )(x, shift, scale, w_in)


def _gmlp_kernel(uv_ref, wsp_ref, bias_ref, g_ref, b_ref, a_ref, vn_ref, *, chunk, n_sub):
    uv = uv_ref[...]
    u = jax.nn.gelu(uv[:, :W_A])
    v = _ln(jax.nn.gelu(uv[:, W_A:])) * g_ref[...] + b_ref[...]
    vn_ref[...] = v
    row = lax.broadcasted_iota(jnp.int32, (chunk, chunk), 0)
    col = lax.broadcasted_iota(jnp.int32, (chunk, chunk), 1)
    lane_head = lax.broadcasted_iota(jnp.int32, (chunk, W_A), 1) // HEAD_DIM
    w = [jnp.where(col <= row, wsp_ref[h], 0.0).astype(BF16) for h in range(H_A)]
    for c in range(n_sub):
        rows = slice(c * chunk, (c + 1) * chunk)
        vc = v[rows]
        mixed = bias_ref[...]
        for h in range(H_A):
            vh = jnp.where(lane_head == h, vc, 0.0).astype(BF16)
            mixed = mixed + jnp.dot(w[h], vh, preferred_element_type=F32)
        a_ref[rows, :] = u[rows] * mixed


def _gmlp(uv, w_sp, b_sp, ln_g, ln_b, chunk):
    rows = uv.shape[0]
    n_sub = min(GMLP_SUB, rows // chunk)
    tm = n_sub * chunk
    wsp = w_sp[:, :chunk, :chunk]
    bias = jnp.repeat(b_sp[:, :chunk].T, HEAD_DIM, axis=1)
    return pl.pallas_call(
        functools.partial(_gmlp_kernel, chunk=chunk, n_sub=n_sub),
        grid=(rows // tm,),
        in_specs=[pl.BlockSpec((tm, 2 * W_A), lambda i: (i, 0)),
                  pl.BlockSpec((H_A, chunk, chunk), lambda i: (0, 0, 0)),
                  pl.BlockSpec((chunk, W_A), lambda i: (0, 0)),
                  pl.BlockSpec((1, W_A), lambda i: (0, 0)),
                  pl.BlockSpec((1, W_A), lambda i: (0, 0))],
        out_specs=[pl.BlockSpec((tm, W_A), lambda i: (i, 0)),
                   pl.BlockSpec((tm, W_A), lambda i: (i, 0))],
        out_shape=[jax.ShapeDtypeStruct((rows, W_A), F32),
                   jax.ShapeDtypeStruct((rows, W_A), F32)],
        compiler_params=_params("parallel"),
        name="gmlp",
    )(uv, wsp, bias, ln_g.reshape(1, W_A), ln_b.reshape(1, W_A))


def _rope(x, cos, sin):
    lane = lax.broadcasted_iota(jnp.int32, (x.shape[0], LANES), 1)
    first_half = (lane & (HEAD_DIM // 2)) == 0
    parts = []
    for c in range(x.shape[1] // LANES):
        xc = x[:, c * LANES:(c + 1) * LANES]
        rot = jnp.where(first_half,
                        pltpu.roll(xc, LANES - HEAD_DIM // 2, 1),
                        pltpu.roll(xc, HEAD_DIM // 2, 1))
        parts.append(xc * cos + rot * sin)
    return jnp.concatenate(parts, axis=1)


def _ret_kernel(r_ref, cos_ref, sin_ref, qdec_ref, kdec_ref, dec_ref, blk_ref, s0_ref,
                gng_ref, gnb_ref, o_ref, sout_ref, s_scr, o_scr, *, n_blocks):
    n = pl.program_id(1)

    @pl.when(n == 0)
    def _():
        s_scr[...] = s0_ref[0]

    r = r_ref[...]
    cos = cos_ref[...]
    sin = sin_ref[...]
    qr = _rope(r[:, :W_B], cos, sin)
    kr = _rope(r[:, W_B:2 * W_B], cos, sin) * (HEAD_DIM ** -0.5)
    vb = r[:, 2 * W_B:3 * W_B].astype(BF16)
    gate = r[:, 3 * W_B:]
    qb = qr.astype(BF16)
    kb = kr.astype(BF16)
    qdb = (qr * qdec_ref[...]).astype(BF16)
    kdb = (kr * kdec_ref[...]).astype(BF16)
    for h in range(H_B):
        sl = slice(h * HEAD_DIM, (h + 1) * HEAD_DIM)
        scores = lax.dot_general(qb[:, sl], kb[:, sl], NT_DIMS, preferred_element_type=F32) * dec_ref[h]
        s_h = s_scr[h]
        o_h = (jnp.dot(scores.astype(BF16), vb[:, sl], preferred_element_type=F32)
               + jnp.dot(qdb[:, sl], s_h.astype(BF16), preferred_element_type=F32))
        s_scr[h] = s_h * blk_ref[h] + lax.dot_general(kdb[:, sl], vb[:, sl], TN_DIMS,
                                                      preferred_element_type=F32)
        o_scr[:, sl] = _ln(o_h)
    o_ref[...] = (o_scr[...] * gng_ref[...] + gnb_ref[...]) * _silu(gate)

    @pl.when(n == n_blocks - 1)
    def _():
        sout_ref[0] = s_scr[...]


def _retention(ret, s0, pos0, bsz, seq, gn_g, gn_b):
    L = min(seq, RET_BLOCK)
    n_blocks = seq // L
    half = HEAD_DIM // 2
    inv = ROPE_BASE ** (-jnp.arange(half, dtype=F32) / half)
    ang = (pos0 + jnp.arange(seq)).astype(F32)[:, None] * inv[None, :]
    cos, sin = jnp.cos(ang), jnp.sin(ang)
    cos_t = jnp.tile(jnp.concatenate([cos, cos], axis=1), (1, LANES // HEAD_DIM))
    sin_t = jnp.tile(jnp.concatenate([-sin, sin], axis=1), (1, LANES // HEAD_DIM))
    log_g = jnp.log1p(-jnp.exp2(-5.0 - jnp.arange(H_B, dtype=F32)))
    idx = jnp.arange(L, dtype=F32)
    diff = idx[:, None] - idx[None, :]
    decay = jnp.where(diff >= 0, jnp.exp(diff[None] * log_g[:, None, None]), 0.0)
    q_decay = jnp.exp((idx[None, :] + 1.0) * log_g[:, None])
    k_decay = jnp.exp((L - 1.0 - idx[None, :]) * log_g[:, None])
    blk_decay = jnp.exp(L * log_g)
    qdec = jnp.repeat(q_decay.T, HEAD_DIM, axis=1)
    kdec = jnp.repeat(k_decay.T, HEAD_DIM, axis=1)
    blk = jnp.broadcast_to(blk_decay[:, None, None], (H_B, HEAD_DIM, HEAD_DIM))
    const2 = lambda b, n: (0, 0)
    const3 = lambda b, n: (0, 0, 0)
    return pl.pallas_call(
        functools.partial(_ret_kernel, n_blocks=n_blocks),
        grid=(bsz, n_blocks),
        in_specs=[pl.BlockSpec((L, 4 * W_B), lambda b, n: (b * n_blocks + n, 0)),
                  pl.BlockSpec((L, LANES), lambda b, n: (n, 0)),
                  pl.BlockSpec((L, LANES), lambda b, n: (n, 0)),
                  pl.BlockSpec((L, W_B), const2),
                  pl.BlockSpec((L, W_B), const2),
                  pl.BlockSpec((H_B, L, L), const3),
                  pl.BlockSpec((H_B, HEAD_DIM, HEAD_DIM), const3),
                  pl.BlockSpec((1, H_B, HEAD_DIM, HEAD_DIM), lambda b, n: (b, 0, 0, 0)),
                  pl.BlockSpec((1, W_B), const2),
                  pl.BlockSpec((1, W_B), const2)],
        out_specs=[pl.BlockSpec((L, W_B), lambda b, n: (b * n_blocks + n, 0)),
                   pl.BlockSpec((1, H_B, HEAD_DIM, HEAD_DIM), lambda b, n: (b, 0, 0, 0))],
        out_shape=[jax.ShapeDtypeStruct((bsz * seq, W_B), F32),
                   jax.ShapeDtypeStruct((bsz, H_B, HEAD_DIM, HEAD_DIM), F32)],
        scratch_shapes=[pltpu.VMEM((H_B, HEAD_DIM, HEAD_DIM), F32),
                        pltpu.VMEM((L, W_B), F32)],
        compiler_params=_params("parallel", "arbitrary"),
        name="retention",
    )(ret, cos_t, sin_t, qdec, kdec, decay, blk, s0, gn_g.reshape(1, W_B), gn_b.reshape(1, W_B))


def _sb_block(qm_ref, k_ref, v_ref, u2_ref, acc_ref, carry_ref, causal):
    tk = k_ref.shape[0]
    half = lax.broadcasted_iota(jnp.int32, (tk, LANES), 1) // HEAD_DIM
    u2 = u2_ref[...]
    kp = [k_ref[:, p * LANES:(p + 1) * LANES] for p in range(H_C // 2)]

    def scores(h):
        z = lax.dot_general(qm_ref[h], kp[h // 2], NT_DIMS, preferred_element_type=F32)
        neg_abs = pltpu.bitcast(pltpu.bitcast(z, jnp.uint32) | jnp.uint32(0x80000000), F32)
        ls_pos = jnp.minimum(z, 0.0) - jnp.log(1.0 + jnp.exp2(neg_abs)) * INV_LN2
        log_stay = ls_pos - z
        if causal is not None:
            log_stay = jnp.where(causal, log_stay, 0.0)
        hi = log_stay.astype(BF16)
        lo = (log_stay - hi.astype(F32)).astype(BF16)
        return ls_pos, log_stay[:, :1], jnp.concatenate([hi, lo], axis=1)

    def cumsum(hi_lo):
        return jnp.dot(hi_lo, u2, preferred_element_type=F32)

    def weigh(h, ls_pos, first_col, excl):
        carry = carry_ref[h]
        att = jnp.exp2(ls_pos + excl + carry)
        if causal is not None:
            att = jnp.where(causal, att, 0.0)
        vp = v_ref[:, (h // 2) * LANES:(h // 2 + 1) * LANES]
        vp = jnp.where(half == h % 2, vp, jnp.zeros_like(vp))
        carry_ref[h] = carry + (excl[:, :1] + first_col)
        return jnp.dot(att.astype(BF16), vp, preferred_element_type=F32)

    stage_a, stage_b, outs = {}, {}, {}
    for step in range(H_C + 2):
        if step - 2 >= 0:
            h = step - 2
            outs[h] = weigh(h, stage_a[h][0], stage_a[h][1], stage_b.pop(h))
            del stage_a[h]
            if h % 2 == 1:
                p = h // 2
                acc_ref[:, p * LANES:(p + 1) * LANES] += outs.pop(h - 1) + outs.pop(h)
        if 0 <= step - 1 < H_C:
            stage_b[step - 1] = cumsum(stage_a[step - 1][2])
        if step < H_C:
            stage_a[step] = scores(step)


def _sb_kernel(q_ref, k_ref, v_ref, u2_ref, o_ref, qm_ref, acc_ref, carry_ref, *, t):
    i = pl.program_id(1)
    acc_ref[...] = jnp.zeros_like(acc_ref)
    carry_ref[...] = jnp.zeros_like(carry_ref)
    half = lax.broadcasted_iota(jnp.int32, (t, LANES), 1) // HEAD_DIM
    for h in range(H_C):
        qp = q_ref[:, (h // 2) * LANES:(h // 2 + 1) * LANES] * SB_QSCALE
        qm_ref[h] = jnp.where(half == h % 2, qp, 0.0).astype(BF16)

    def block(j, causal):
        k0 = pl.multiple_of((i - j) * t, t)
        _sb_block(qm_ref, k_ref.at[pl.ds(k0, t), :], v_ref.at[pl.ds(k0, t), :], u2_ref, acc_ref, carry_ref, causal)

    block(0, lax.broadcasted_iota(jnp.int32, (t, t), 1) < lax.broadcasted_iota(jnp.int32, (t, t), 0))

    def key_block(state):
        j, _ = state
        block(j, None)
        dead = jnp.max(carry_ref[...]) < SB_DEAD
        return j + 1, dead.astype(jnp.int32)

    lax.while_loop(lambda state: jnp.logical_and(state[0] <= i, state[1] == 0),
                   key_block, (jnp.int32(1), jnp.int32(0)))
    o_ref[...] = acc_ref[...]


def _stick_breaking(q, k, v, bsz, seq, t):
    nq = seq // t
    tri = np.tril(np.ones((t, t), np.float32), -1)
    u2 = jnp.asarray(np.concatenate([tri, tri], axis=0), dtype=BF16)
    resident = dict(pipeline_mode=pl.Buffered(1))
    return pl.pallas_call(
        functools.partial(_sb_kernel, t=t),
        grid=(bsz, nq),
        in_specs=[pl.BlockSpec((t, W_C), lambda b, i: (b * nq + i, 0)),
                  pl.BlockSpec((seq, W_C), lambda b, i: (b, 0), **resident),
                  pl.BlockSpec((seq, W_C), lambda b, i: (b, 0), **resident),
                  pl.BlockSpec((2 * t, t), lambda b, i: (0, 0), **resident)],
        out_specs=pl.BlockSpec((t, W_C), lambda b, i: (b * nq + i, 0)),
        out_shape=jax.ShapeDtypeStruct((bsz * seq, W_C), F32),
        scratch_shapes=[pltpu.VMEM((H_C, t, LANES), BF16),
                        pltpu.VMEM((t, W_C), F32),
                        pltpu.VMEM((H_C, t, 1), F32)],
        compiler_params=_params("parallel", "arbitrary"),
        name="stick_breaking",
    )(q, k, v, u2)


def _sb_scores(z, causal=None):
    neg_abs = pltpu.bitcast(pltpu.bitcast(z, jnp.uint32) | jnp.uint32(0x80000000), F32)
    ls_pos = jnp.minimum(z, 0.0) - jnp.log(1.0 + jnp.exp2(neg_abs)) * INV_LN2
    log_stay = ls_pos - z
    if causal is not None:
        log_stay = jnp.where(causal, log_stay, 0.0)
    hi = log_stay.astype(BF16)
    lo = (log_stay - hi.astype(F32)).astype(BF16)
    return ls_pos, log_stay, jnp.concatenate([hi, lo], axis=1)


def _sb_step_kernel(q_ref, kt_ref, vt_ref, kn_ref, vn_ref, u2_ref, un_ref, o_ref, acc_ref, carry_ref, *, tk):
    seq = q_ref.shape[0]
    past = kt_ref.shape[-1]
    qs = (q_ref[...] * SB_QSCALE).astype(BF16)
    row = lax.broadcasted_iota(jnp.int32, (seq, seq), 0)
    col = lax.broadcasted_iota(jnp.int32, (seq, seq), 1)
    causal = col < row
    un = un_ref[...]
    for h in range(H_C):
        sl = slice(h * HEAD_DIM, (h + 1) * HEAD_DIM)
        z = lax.dot_general(qs[:, sl], kn_ref[:, sl], NT_DIMS, preferred_element_type=F32)
        ls_pos, log_stay, hi_lo = _sb_scores(z, causal)
        excl = jnp.dot(hi_lo, un, preferred_element_type=F32)
        att = jnp.where(causal, jnp.exp2(ls_pos + excl), 0.0)
        acc_ref[h] = jnp.dot(att.astype(BF16), vn_ref[:, sl], preferred_element_type=F32)
        carry_ref[h] = excl[:, :1] + log_stay[:, :1]

    u2 = u2_ref[...]

    def key_block(state):
        j, _ = state
        k0 = pl.multiple_of(past - (j + 1) * tk, tk)
        for h in range(H_C):
            kt = kt_ref[0, 0, h, :, pl.ds(k0, tk)].astype(BF16)
            vt = vt_ref[0, 0, h, :, pl.ds(k0, tk)].astype(BF16)
            z = jnp.dot(qs[:, h * HEAD_DIM:(h + 1) * HEAD_DIM], kt, preferred_element_type=F32)
            ls_pos, log_stay, hi_lo = _sb_scores(z)
            excl = jnp.dot(hi_lo, u2, preferred_element_type=F32)
            carry = carry_ref[h]
            att = jnp.exp2(ls_pos + excl + carry)
            acc_ref[h] += lax.dot_general(att.astype(BF16), vt, NT_DIMS, preferred_element_type=F32)
            carry_ref[h] = carry + (excl[:, :1] + log_stay[:, :1])
        dead = jnp.max(carry_ref[...]) < SB_DEAD
        return j + 1, dead.astype(jnp.int32)

    lax.while_loop(lambda state: jnp.logical_and(state[0] < past // tk, state[1] == 0),
                   key_block, (jnp.int32(0), jnp.int32(0)))
    for h in range(H_C):
        o_ref[:, h * HEAD_DIM:(h + 1) * HEAD_DIM] = acc_ref[h]


def _stick_breaking_step(q, k_new, v_new, cache_kt, cache_vt, layer, bsz, seq, tk):
    past = cache_kt.shape[-1]
    assert past % tk == 0
    tri = lambda n: np.tril(np.ones((n, n), np.float32), -1)
    stacked = lambda n: jnp.asarray(np.concatenate([tri(n), tri(n)], axis=0), dtype=BF16)
    rows = pl.BlockSpec((seq, W_C), lambda b: (b, 0))
    cache = pl.BlockSpec((1, 1, H_C, HEAD_DIM, past), lambda b: (layer, b, 0, 0, 0))
    return pl.pallas_call(
        functools.partial(_sb_step_kernel, tk=tk),
        grid=(bsz,),
        in_specs=[rows, cache, cache, rows, rows,
                  pl.BlockSpec((2 * tk, tk), lambda b: (0, 0)), pl.BlockSpec((2 * seq, seq), lambda b: (0, 0))],
        out_specs=rows,
        out_shape=jax.ShapeDtypeStruct((bsz * seq, W_C), F32),
        scratch_shapes=[pltpu.VMEM((H_C, seq, HEAD_DIM), F32), pltpu.VMEM((H_C, seq, 1), F32)],
        compiler_params=_params("parallel"),
        name="stick_breaking_step",
    )(q, cache_kt, cache_vt, k_new, v_new, stacked(tk), stacked(seq))


def _route(sel, s):
    g_scores = []
    for g in range(N_GROUPS):
        a, b, c, d = sel[EXPERTS_PER_GROUP * g:EXPERTS_PER_GROUP * (g + 1)]
        ab_hi, ab_lo = jnp.maximum(a, b), jnp.minimum(a, b)
        cd_hi, cd_lo = jnp.maximum(c, d), jnp.minimum(c, d)
        top1 = jnp.maximum(ab_hi, cd_hi)
        top2 = jnp.maximum(jnp.minimum(ab_hi, cd_hi), jnp.maximum(ab_lo, cd_lo))
        g_scores.append(top1 + top2)
    best = g_scores[0]
    gi = jnp.zeros(best.shape, jnp.int32)
    for g in range(1, N_GROUPS):
        upd = g_scores[g] > best
        gi = jnp.where(upd, g, gi)
        best = jnp.where(upd, g_scores[g], best)

    def pick_group(rows, l):
        out = rows[(N_GROUPS - 1) * EXPERTS_PER_GROUP + l]
        for g in range(N_GROUPS - 2, -1, -1):
            out = jnp.where(gi == g, rows[g * EXPERTS_PER_GROUP + l], out)
        return out

    ig = [pick_group(sel, l) for l in range(EXPERTS_PER_GROUP)]
    sg = [pick_group(s, l) for l in range(EXPERTS_PER_GROUP)]
    b1 = ig[0]
    i1 = jnp.zeros(best.shape, jnp.int32)
    for l in range(1, EXPERTS_PER_GROUP):
        upd = ig[l] > b1
        i1 = jnp.where(upd, l, i1)
        b1 = jnp.where(upd, ig[l], b1)
    b2 = jnp.full(best.shape, -jnp.inf, F32)
    i2 = jnp.zeros(best.shape, jnp.int32)
    for l in range(EXPERTS_PER_GROUP):
        upd = jnp.logical_and(i1 != l, ig[l] > b2)
        i2 = jnp.where(upd, l, i2)
        b2 = jnp.where(upd, ig[l], b2)

    def pick_local(idx):
        out = sg[EXPERTS_PER_GROUP - 1]
        for l in range(EXPERTS_PER_GROUP - 2, -1, -1):
            out = jnp.where(idx == l, sg[l], out)
        return out

    w1 = pick_local(i1)
    w2 = pick_local(i2)
    tot = w1 + w2
    return gi * EXPERTS_PER_GROUP + i1, gi * EXPERTS_PER_GROUP + i2, w1 / tot, w2 / tot


def _pair_class(e1, e2):
    lo = jnp.minimum(e1, e2)
    hi = jnp.maximum(e1, e2)
    g = lo // EXPERTS_PER_GROUP
    llo = lo - g * EXPERTS_PER_GROUP
    lhi = hi - g * EXPERTS_PER_GROUP
    return g * PAIRS_PER_GROUP + ((llo * (2 * EXPERTS_PER_GROUP - 1 - llo)) >> 1) + (lhi - llo - 1)


def _post_kernel(a_ref, b_ref, c_ref, x_ref, wo_ref, gate_ref, g1_ref, b1_ref, sh2_ref, sc2_ref,
                 wrt_ref, br_ref, x1_ref, *rest, tm, with_hx):
    if with_hx:
        h2_ref = None
        comb_ref, hx_ref, wob_ref = rest
    else:
        hx_ref = None
        h2_ref, comb_ref, wob_ref = rest
    @pl.when(pl.program_id(0) == 0)
    def _():
        wob_ref[...] = wo_ref[0].astype(BF16)

    sub = tm // POST_SUB if tm % (POST_SUB * LANES) == 0 else tm
    tiles = [slice(r0, r0 + sub) for r0 in range(0, tm, sub)]
    per_row = lambda ref, rows: ref[rows, :] if ref.shape[0] == tm else ref[...]
    proj = [jnp.dot(a_ref[rows, :].astype(BF16), wob_ref[:W_A], preferred_element_type=F32)
            + jnp.dot(b_ref[rows, :].astype(BF16), wob_ref[W_A:W_A + W_B], preferred_element_type=F32)
            + jnp.dot(c_ref[rows, :].astype(BF16), wob_ref[W_A + W_B:], preferred_element_type=F32)
            for rows in tiles]
    x1 = [_ln(ALPHA * x_ref[rows, :] + per_row(gate_ref, rows) * pr) * g1_ref[...] + b1_ref[...]
          for rows, pr in zip(tiles, proj)]
    for rows, v in zip(tiles, x1):
        x1_ref[rows, :] = v
    h2 = [_ln(v) * (1.0 + per_row(sc2_ref, rows)) + per_row(sh2_ref, rows) for rows, v in zip(tiles, x1)]
    logits = [lax.dot_general(wrt_ref[...], v, NT_DIMS, preferred_element_type=F32, precision=HIGHEST) for v in h2]
    for rows, v, logits_t in zip(tiles, h2, logits):
        if h2_ref is not None:
            h2_ref[rows, :] = v.astype(BF16)
        s_t = jax.nn.sigmoid(logits_t)
        sel_t = s_t + br_ref[...]
        s = [s_t[e:e + 1, :] for e in range(N_EXPERTS)]
        sel = [sel_t[e:e + 1, :] for e in range(N_EXPERTS)]
        e1, e2, w1, w2 = _route(sel, s)
        expert = lax.broadcasted_iota(jnp.int32, (LANES, sub), 0)
        comb_t = jnp.where(expert == e1, w1, jnp.where(expert == e2, w2, 0.0))
        comb_t = jnp.where(expert == N_EXPERTS, _pair_class(e1, e2).astype(F32), comb_t)
        comb_ref[rows, :] = comb_t.T
        if hx_ref is not None:
            hx_ref[rows, :D_MODEL] = v
            hx_ref[rows, D_MODEL:] = comb_t.T


def _post(a, b, c, x, w_out, layer, gate1, ln_g, ln_b, shift2, scale2, wr_t, br_col, tm, with_hx):
    rows = x.shape[0]
    if with_hx:
        tail_spec = [pl.BlockSpec((tm, LANES), lambda i: (i, 0)), pl.BlockSpec((tm, D_MODEL + LANES), lambda i: (i, 0))]
        tail_shape = [jax.ShapeDtypeStruct((rows, LANES), F32), jax.ShapeDtypeStruct((rows, D_MODEL + LANES), F32)]
    else:
        tail_spec = [pl.BlockSpec((tm, D_MODEL), lambda i: (i, 0)), pl.BlockSpec((tm, LANES), lambda i: (i, 0))]
        tail_shape = [jax.ShapeDtypeStruct((rows, D_MODEL), BF16), jax.ShapeDtypeStruct((rows, LANES), F32)]
    row = lambda w: pl.BlockSpec((tm, w), lambda i: (i, 0))
    const = lambda r, w: pl.BlockSpec((r, w), lambda i: (0, 0))
    return pl.pallas_call(
        functools.partial(_post_kernel, tm=tm, with_hx=with_hx),
        grid=(rows // tm,),
        in_specs=[row(W_A), row(W_B), row(W_C), row(D_MODEL),
                  pl.BlockSpec((1, D_MODEL, D_MODEL), lambda i: (layer, 0, 0), pipeline_mode=pl.Buffered(1)),
                  _row_spec(gate1.shape[0], tm, D_MODEL, rows),
                  const(1, D_MODEL), const(1, D_MODEL),
                  _row_spec(shift2.shape[0], tm, D_MODEL, rows),
                  _row_spec(scale2.shape[0], tm, D_MODEL, rows),
                  const(N_EXPERTS, D_MODEL), const(N_EXPERTS, 1)],
        out_specs=[row(D_MODEL)] + tail_spec,
        out_shape=[jax.ShapeDtypeStruct((rows, D_MODEL), F32)] + tail_shape,
        scratch_shapes=[pltpu.VMEM((D_MODEL, D_MODEL), BF16)],
        compiler_params=_params("arbitrary"),
        name="post_mix",
    )(a, b, c, x, w_out, gate1, ln_g.reshape(1, D_MODEL), ln_b.reshape(1, D_MODEL),
      shift2, scale2, wr_t, br_col)


def _moe_kernel(h_ref, comb_ref, x_ref, gate_ref, g2_ref, b2_ref, w1_ref, w3_ref, w2_ref,
                o_ref, acc_ref, *, tm):
    e = pl.program_id(1)

    @pl.when(e == 0)
    def _():
        acc_ref[...] = jnp.zeros_like(acc_ref)

    h = h_ref[...]
    a = jnp.dot(h, w1_ref[0], preferred_element_type=F32)
    g = jnp.dot(h, w3_ref[0], preferred_element_type=F32)
    lane = lax.broadcasted_iota(jnp.int32, (tm, LANES), 1)
    ce = jnp.sum(jnp.where(lane == e, comb_ref[...], 0.0), axis=-1, keepdims=True)
    act = _silu(a) * g * ce
    acc_ref[...] += jnp.dot(act.astype(BF16), w2_ref[0], preferred_element_type=F32)

    @pl.when(e == N_EXPERTS - 1)
    def _():
        y = ALPHA * x_ref[...] + gate_ref[...] * acc_ref[...]
        o_ref[...] = _ln(y) * g2_ref[...] + b2_ref[...]


def _moe(h2, comb, x1, gate2, ln_g, ln_b, w1, w3, w2, tm):
    rows = x1.shape[0]
    row = lambda w: pl.BlockSpec((tm, w), lambda i, e: (i, 0))
    const = pl.BlockSpec((1, D_MODEL), lambda i, e: (0, 0))
    gate_spec = (pl.BlockSpec((1, D_MODEL), lambda i, e: (0, 0)) if gate2.shape[0] == 1
                 else row(D_MODEL))
    return pl.pallas_call(
        functools.partial(_moe_kernel, tm=tm),
        grid=(rows // tm, N_EXPERTS),
        in_specs=[row(D_MODEL), row(LANES), row(D_MODEL), gate_spec, const, const,
                  pl.BlockSpec((1, D_MODEL, D_EXPERT), lambda i, e: (e, 0, 0)),
                  pl.BlockSpec((1, D_MODEL, D_EXPERT), lambda i, e: (e, 0, 0)),
                  pl.BlockSpec((1, D_EXPERT, D_MODEL), lambda i, e: (e, 0, 0))],
        out_specs=row(D_MODEL),
        out_shape=jax.ShapeDtypeStruct((rows, D_MODEL), F32),
        scratch_shapes=[pltpu.VMEM((tm, D_MODEL), F32)],
        compiler_params=_params("parallel", "arbitrary"),
        name="experts",
    )(h2, comb, x1, gate2, ln_g.reshape(1, D_MODEL), ln_b.reshape(1, D_MODEL), w1, w3, w2)


def _pair_experts():
    lo, hi = [], []
    for g in range(N_GROUPS):
        for a in range(EXPERTS_PER_GROUP):
            for b in range(a + 1, EXPERTS_PER_GROUP):
                lo.append(g * EXPERTS_PER_GROUP + a)
                hi.append(g * EXPERTS_PER_GROUP + b)
    return np.asarray(lo, np.int32), np.asarray(hi, np.int32)


def _slot_plan(cls, tm):
    rows = cls.shape[0]
    n_tiles = rows // tm + N_PAIRS
    onehot = (cls[:, None] == jnp.arange(N_PAIRS, dtype=jnp.int32)[None, :]).astype(jnp.int32)
    rank = jnp.cumsum(onehot, axis=0) - onehot
    tiles_per = (jnp.sum(onehot, axis=0) + tm - 1) // tm
    tile_start = jnp.cumsum(tiles_per) - tiles_per
    pos = jnp.sum(onehot * (tile_start[None, :] * tm + rank), axis=1).astype(jnp.int32)
    src = jnp.zeros((n_tiles * tm,), jnp.int32).at[pos].set(jnp.arange(rows, dtype=jnp.int32))
    tile = jnp.arange(n_tiles, dtype=jnp.int32)
    tile_class = jnp.clip(jnp.sum((tile[:, None] >= tile_start[None, :]).astype(jnp.int32), axis=1) - 1,
                          0, N_PAIRS - 1)
    pair_lo, pair_hi = _pair_experts()
    n_used = jnp.sum(tiles_per).astype(jnp.int32).reshape(1)
    return pos, src, jnp.asarray(pair_lo)[tile_class], jnp.asarray(pair_hi)[tile_class], n_used


def _start_rows(idx_ref, src_hbm, dst_ref, sem, n):
    def issue(r, carry):
        pltpu.make_async_copy(src_hbm.at[pl.ds(idx_ref[0, 0, r], 1)], dst_ref.at[pl.ds(r, 1)], sem).start()
        return carry

    lax.fori_loop(0, n, issue, 0, unroll=True)


def _wait_rows(src_hbm, dst_ref, sem, n):
    pltpu.make_async_copy(src_hbm.at[pl.ds(0, n)], dst_ref, sem).wait()


def _moe_pair_kernel(lo_ref, hi_ref, nused_ref, src_ref, src_next_ref, hx_hbm,
                     w1a_ref, w3a_ref, w2a_ref, w1b_ref, w3b_ref, w2b_ref, y_ref, xbuf, sems, *, tm):
    i = pl.program_id(0)
    n_used = nused_ref[0]
    slot = i % 2

    @pl.when(jnp.logical_and(i == 0, n_used > 0))
    def _():
        _start_rows(src_ref, hx_hbm, xbuf.at[0], sems.at[0], tm)

    @pl.when(i + 1 < n_used)
    def _():
        _start_rows(src_next_ref, hx_hbm, xbuf.at[1 - slot], sems.at[1 - slot], tm)

    @pl.when(i < n_used)
    def _():
        _wait_rows(hx_hbm, xbuf.at[slot], sems.at[slot], tm)
        x = xbuf[slot]
        h = x[:, :D_MODEL].astype(BF16)
        comb = x[:, D_MODEL:]
        lane = lax.broadcasted_iota(jnp.int32, (tm, LANES), 1)
        acc = jnp.zeros((tm, D_MODEL), F32)
        for e, w1_ref, w3_ref, w2_ref in ((lo_ref[i], w1a_ref, w3a_ref, w2a_ref), (hi_ref[i], w1b_ref, w3b_ref, w2b_ref)):
            a = jnp.dot(h, w1_ref[0], preferred_element_type=F32)
            g = jnp.dot(h, w3_ref[0], preferred_element_type=F32)
            ce = jnp.sum(jnp.where(lane == e, comb, 0.0), axis=-1, keepdims=True)
            act = _silu(a) * g * ce
            acc = acc + jnp.dot(act.astype(BF16), w2_ref[0], preferred_element_type=F32)
        y_ref[...] = acc

    @pl.when(i >= n_used)
    def _():
        y_ref[...] = jnp.zeros_like(y_ref)


def _moe_pairs(hx, src, tile_lo, tile_hi, n_used, w1, w3, w2, tm):
    n_tiles = tile_lo.shape[0]
    lo_spec = lambda k, n: pl.BlockSpec((1, k, n), lambda i, lo, hi, nu: (lo[i], 0, 0))
    hi_spec = lambda k, n: pl.BlockSpec((1, k, n), lambda i, lo, hi, nu: (hi[i], 0, 0))
    src3 = src.reshape(n_tiles, 1, tm)
    return pl.pallas_call(
        functools.partial(_moe_pair_kernel, tm=tm),
        grid_spec=pltpu.PrefetchScalarGridSpec(
            num_scalar_prefetch=3,
            grid=(n_tiles,),
            in_specs=[pl.BlockSpec((1, 1, tm), lambda i, lo, hi, nu: (i, 0, 0), memory_space=pltpu.SMEM),
                      pl.BlockSpec((1, 1, tm), lambda i, lo, hi, nu: (jnp.minimum(i + 1, n_tiles - 1), 0, 0),
                                   memory_space=pltpu.SMEM),
                      pl.BlockSpec(memory_space=pl.ANY),
                      lo_spec(D_MODEL, D_EXPERT), lo_spec(D_MODEL, D_EXPERT), lo_spec(D_EXPERT, D_MODEL),
                      hi_spec(D_MODEL, D_EXPERT), hi_spec(D_MODEL, D_EXPERT), hi_spec(D_EXPERT, D_MODEL)],
            out_specs=pl.BlockSpec((tm, D_MODEL), lambda i, lo, hi, nu: (i, 0)),
            scratch_shapes=[pltpu.VMEM((2, tm, D_MODEL + LANES), F32), pltpu.SemaphoreType.DMA((2,))]),
        out_shape=jax.ShapeDtypeStruct((n_tiles * tm, D_MODEL), F32),
        compiler_params=_params("arbitrary"),
        name="experts_paired",
    )(tile_lo, tile_hi, n_used, src3, src3, hx, w1, w3, w2, w1, w3, w2)


def _combine_kernel(pos_ref, pos_next_ref, y_hbm, x_ref, gate_ref, g2_ref, b2_ref, o_ref, ybuf, sems, *, tm, n_tiles):
    i = pl.program_id(0)
    slot = i % 2

    @pl.when(i == 0)
    def _():
        _start_rows(pos_ref, y_hbm, ybuf.at[0], sems.at[0], tm)

    @pl.when(i + 1 < n_tiles)
    def _():
        _start_rows(pos_next_ref, y_hbm, ybuf.at[1 - slot], sems.at[1 - slot], tm)

    _wait_rows(y_hbm, ybuf.at[slot], sems.at[slot], tm)
    y = ALPHA * x_ref[...] + gate_ref[...] * ybuf[slot]
    o_ref[...] = _ln(y) * g2_ref[...] + b2_ref[...]


def _combine(y_sorted, pos, x1, gate2, ln_g, ln_b, tm):
    rows = x1.shape[0]
    n_tiles = rows // tm
    row = pl.BlockSpec((tm, D_MODEL), lambda i: (i, 0))
    const = pl.BlockSpec((1, D_MODEL), lambda i: (0, 0))
    pos3 = pos.reshape(n_tiles, 1, tm)
    return pl.pallas_call(
        functools.partial(_combine_kernel, tm=tm, n_tiles=n_tiles),
        grid=(n_tiles,),
        in_specs=[pl.BlockSpec((1, 1, tm), lambda i: (i, 0, 0), memory_space=pltpu.SMEM),
                  pl.BlockSpec((1, 1, tm), lambda i: (jnp.minimum(i + 1, n_tiles - 1), 0, 0), memory_space=pltpu.SMEM),
                  pl.BlockSpec(memory_space=pl.ANY),
                  row, _row_spec(gate2.shape[0], tm, D_MODEL, rows), const, const],
        out_specs=row,
        out_shape=jax.ShapeDtypeStruct((rows, D_MODEL), F32),
        scratch_shapes=[pltpu.VMEM((2, tm, D_MODEL), F32), pltpu.SemaphoreType.DMA((2,))],
        compiler_params=_params("arbitrary"),
        name="combine",
    )(pos3, pos3, y_sorted, x1, gate2, ln_g.reshape(1, D_MODEL), ln_b.reshape(1, D_MODEL))


def _trunk_layer(x, mods, bsz, seq, pos0, s0, k_past, v_past, p, tiles):
    shift1, scale1, gate1, shift2, scale2, gate2 = mods
    uv, ret, q_c, k_c, v_c, k_bf, v_bf = _inproj(x, shift1, scale1, p["w_in"], p["layer"], tiles["tm_in"])
    a_out, v_rows = _gmlp(uv, p["w_sp"], p["b_sp"], p["ln_v_g"], p["ln_v_b"], min(seq, GMLP_CHUNK))
    b_out, s_new = _retention(ret, s0, pos0, bsz, seq, p["gn_g"], p["gn_b"])
    if k_past is None:
        c_out = _stick_breaking(q_c, k_bf, v_bf, bsz, seq, tiles["sb"])
    else:
        c_out = _stick_breaking_step(q_c, k_bf, v_bf, k_past, v_past, p["layer"], bsz, seq, tiles["sb"])
    grouped = "tm_group" in tiles
    post = _post(a_out, b_out, c_out, x, p["w_out"], p["layer"], gate1, p["ln1_g"], p["ln1_b"],
                 shift2, scale2, p["wr"], p["br"], tiles["tm_post"], grouped)
    if grouped:
        x1, comb, hx = post
        tm = tiles["tm_group"]
        pos, src, tile_lo, tile_hi, n_used = _slot_plan(comb[:, N_EXPERTS].astype(jnp.int32), tm)
        y_sorted = _moe_pairs(hx, src, tile_lo, tile_hi, n_used, p["w1"], p["w3"], p["w2"], tm)
        y = _combine(y_sorted, pos, x1, gate2, p["ln2_g"], p["ln2_b"], tm)
    else:
        x1, h2, comb = post
        y = _moe(h2, comb, x1, gate2, p["ln2_g"], p["ln2_b"], p["w1"], p["w3"], p["w2"], tiles["tm_moe"])
    return y, v_rows, s_new, k_c, v_c


def kernel(x_prompt, x_sample, cache_sb_k, cache_sb_v, state_ret, c_prompt, c_sample, w_ada, b_ada, w_in, w_out, ln_v_g, ln_v_b, w_spatial, b_spatial, gn_g, gn_b, ln1_g, ln1_b, ln2_g, ln2_b, w_router, b_router, w1, w3, w2):
    bp, tp, _ = x_prompt.shape
    bs, ts, _ = x_sample.shape
    past_len = cache_sb_k.shape[2]
    assert bp == 1

    n_c = bp + bs
    c_rows = -(-n_c // 8) * 8
    c_all = jnp.concatenate([c_prompt, c_sample, jnp.zeros((c_rows - n_c, D_MODEL), F32)], axis=0)
    mods = _adaln(c_all, w_ada, b_ada)

    wr_t = w_router.T
    cache_kt = jnp.transpose(cache_sb_k, (0, 1, 3, 4, 2))
    cache_vt = jnp.transpose(cache_sb_v, (0, 1, 3, 4, 2))
    br_col = b_router.reshape(N_EXPERTS, 1)

    tiles_p = dict(tm_in=256, sb=256, tm_post=512, tm_group=256)
    tiles_s = dict(tm_in=bs * ts, sb=256, tm_post=bs * ts, tm_moe=bs * ts)

    y_p = x_prompt.reshape(bp * tp, D_MODEL)
    y_s = x_sample.reshape(bs * ts, D_MODEL)
    zero_state = jnp.zeros((bp, H_B, HEAD_DIM, HEAD_DIM), F32)
    outs = [[] for _ in range(7)]
    for l in range(DEPTH):
        p = dict(w_in=w_in, w_out=w_out,
                 w_sp=w_spatial[l], b_sp=b_spatial[l], ln_v_g=ln_v_g[l], ln_v_b=ln_v_b[l],
                 gn_g=gn_g[l], gn_b=gn_b[l], ln1_g=ln1_g[l], ln1_b=ln1_b[l],
                 ln2_g=ln2_g[l], ln2_b=ln2_b[l], wr=wr_t, br=br_col,
                 w1=w1[l].astype(BF16), w3=w3[l].astype(BF16), w2=w2[l].astype(BF16), layer=l)
        m = mods[l]
        mods_p = [m[0:1, i * D_MODEL:(i + 1) * D_MODEL] for i in range(6)]
        mods_s = [jnp.repeat(m[bp:bp + bs, i * D_MODEL:(i + 1) * D_MODEL], ts, axis=0) for i in range(6)]
        y_p, _, s_p, k_p, v_p = _trunk_layer(y_p, mods_p, bp, tp, 0, zero_state, None, None, p, tiles_p)
        y_s, g_s, s_s, k_s, v_s = _trunk_layer(
            y_s, mods_s, bs, ts, past_len, state_ret[l],
            cache_kt, cache_vt, p, tiles_s)
        outs[0].append(s_p)
        outs[1].append(k_p.reshape(bp, tp, H_C, HEAD_DIM))
        outs[2].append(v_p.reshape(bp, tp, H_C, HEAD_DIM))
        outs[3].append(s_s)
        outs[4].append(k_s.reshape(bs, ts, H_C, HEAD_DIM))
        outs[5].append(v_s.reshape(bs, ts, H_C, HEAD_DIM))
        outs[6].append(g_s.reshape(bs, ts, W_A))
    return (y_p.reshape(bp, tp, D_MODEL), y_s.reshape(bs, ts, D_MODEL)) + tuple(jnp.stack(o) for o in outs)
```

```python
import functools

import numpy as np
import jax
import jax.numpy as jnp
from jax import lax
from jax.experimental import pallas as pl
from jax.experimental.pallas import tpu as pltpu

F32 = jnp.float32
BF16 = jnp.bfloat16
HIGHEST = lax.Precision.HIGHEST

D_MODEL = 1024
DEPTH = 2
HEAD_DIM = 64
W_A = D_MODEL // 4
W_B = 3 * D_MODEL // 8
W_C = D_MODEL - W_A - W_B
H_A = W_A // HEAD_DIM
H_B = W_B // HEAD_DIM
H_C = W_C // HEAD_DIM
GMLP_CHUNK = 128
ROPE_BASE = 10000.0
N_EXPERTS = 16
N_GROUPS = 4
EXPERTS_PER_GROUP = N_EXPERTS // N_GROUPS
D_EXPERT = D_MODEL // 2
PAIRS_PER_GROUP = EXPERTS_PER_GROUP * (EXPERTS_PER_GROUP - 1) // 2
N_PAIRS = N_GROUPS * PAIRS_PER_GROUP
ALPHA = (2 * DEPTH) ** 0.25
LN_EPS = 1e-5
D_IN = 2 * W_A + 4 * W_B + 3 * W_C
LANES = 128
VMEM_LIMIT = 48 * 1024 * 1024

NT_DIMS = (((1,), (1,)), ((), ()))
TN_DIMS = (((0,), (0,)), ((), ()))
INV_LN2 = 1.4426950408889634
SB_QSCALE = HEAD_DIM ** -0.5 * INV_LN2
SB_DEAD = -152.0
GMLP_SUB = 8
POST_SUB = 4
RET_BLOCK = 512


def _ln(x):
    mu = jnp.mean(x, axis=-1, keepdims=True)
    xc = x - mu
    var = jnp.mean(xc * xc, axis=-1, keepdims=True)
    return xc * lax.rsqrt(var + LN_EPS)


def _silu(x):
    return x * jax.nn.sigmoid(x)


def _params(*sem):
    return pltpu.CompilerParams(dimension_semantics=sem, vmem_limit_bytes=VMEM_LIMIT)


def _row_spec(rows, tm, width, total_rows):
    if rows == 1:
        return pl.BlockSpec((1, width), lambda i: (0, 0))
    assert rows == total_rows
    return pl.BlockSpec((tm, width), lambda i: (i, 0))


def _adaln_kernel(c_ref, w_ref, b_ref, o_ref):
    sc = _silu(c_ref[...])
    o_ref[0] = jnp.dot(sc, w_ref[0], preferred_element_type=F32, precision=HIGHEST) + b_ref[0]


def _adaln(c_all, w_ada, b_ada):
    rows = c_all.shape[0]
    tn = 1536
    return pl.pallas_call(
        _adaln_kernel,
        grid=(DEPTH, 6 * D_MODEL // tn),
        in_specs=[pl.BlockSpec((rows, D_MODEL), lambda l, j: (0, 0)),
                  pl.BlockSpec((1, D_MODEL, tn), lambda l, j: (l, 0, j)),
                  pl.BlockSpec((1, 1, tn), lambda l, j: (l, 0, j))],
        out_specs=pl.BlockSpec((1, rows, tn), lambda l, j: (l, 0, j)),
        out_shape=jax.ShapeDtypeStruct((DEPTH, rows, 6 * D_MODEL), F32),
        compiler_params=_params("parallel", "parallel"),
        name="adaln",
    )(c_all, w_ada, b_ada.reshape(DEPTH, 1, 6 * D_MODEL))


def _inproj_kernel(x_ref, shift_ref, scale_ref, w_ref, uv_ref, ret_ref, q_ref, k_ref, v_ref, kb_ref, vb_ref, wb_ref):
    @pl.when(pl.program_id(0) == 0)
    def _():
        wb_ref[...] = w_ref[0].astype(BF16)

    h = _ln(x_ref[...]) * (1.0 + scale_ref[...]) + shift_ref[...]
    r = jnp.dot(h.astype(BF16), wb_ref[...], preferred_element_type=F32)
    c0 = 2 * W_A
    c1 = c0 + 4 * W_B
    uv_ref[...] = r[:, :c0]
    ret_ref[...] = r[:, c0:c1]
    q_ref[...] = r[:, c1:c1 + W_C]
    k = r[:, c1 + W_C:c1 + 2 * W_C]
    v = r[:, c1 + 2 * W_C:]
    k_ref[...] = k
    v_ref[...] = v
    kb_ref[...] = k.astype(BF16)
    vb_ref[...] = v.astype(BF16)


def _inproj(x, shift, scale, w_in, layer, tm):
    rows = x.shape[0]
    widths = (2 * W_A, 4 * W_B, W_C, W_C, W_C, W_C, W_C)
    dtypes = (F32,) * 5 + (BF16,) * 2
    return pl.pallas_call(
        _inproj_kernel,
        grid=(rows // tm,),
        in_specs=[pl.BlockSpec((tm, D_MODEL), lambda i: (i, 0)),
                  _row_spec(shift.shape[0], tm, D_MODEL, rows),
                  _row_spec(scale.shape[0], tm, D_MODEL, rows),
                  pl.BlockSpec((1, D_MODEL, D_IN), lambda i: (layer, 0, 0), pipeline_mode=pl.Buffered(1))],
        out_specs=[pl.BlockSpec((tm, w), lambda i: (i, 0)) for w in widths],
        out_shape=[jax.ShapeDtypeStruct((rows, w), dt) for w, dt in zip(widths, dtypes)],
        scratch_shapes=[pltpu.VMEM((D_MODEL, D_IN), BF16)],
        compiler_params=_params("arbitrary"),
        name="inproj",
    )(x, shift, scale, w_in)


def _gmlp_kernel(uv_ref, wsp_ref, bias_ref, g_ref, b_ref, a_ref, vn_ref, *, chunk, n_sub):
    uv = uv_ref[...]
    u = jax.nn.gelu(uv[:, :W_A])
    v = _ln(jax.nn.gelu(uv[:, W_A:])) * g_ref[...] + b_ref[...]
    vn_ref[...] = v
    row = lax.broadcasted_iota(jnp.int32, (chunk, chunk), 0)
    col = lax.broadcasted_iota(jnp.int32, (chunk, chunk), 1)
    lane_head = lax.broadcasted_iota(jnp.int32, (chunk, W_A), 1) // HEAD_DIM
    w = [jnp.where(col <= row, wsp_ref[h], 0.0).astype(BF16) for h in range(H_A)]
    for c in range(n_sub):
        rows = slice(c * chunk, (c + 1) * chunk)
        vc = v[rows]
        mixed = bias_ref[...]
        for h in range(H_A):
            vh = jnp.where(lane_head == h, vc, 0.0).astype(BF16)
            mixed = mixed + jnp.dot(w[h], vh, preferred_element_type=F32)
        a_ref[rows, :] = u[rows] * mixed


def _gmlp(uv, w_sp, b_sp, ln_g, ln_b, chunk):
    rows = uv.shape[0]
    n_sub = min(GMLP_SUB, rows // chunk)
    tm = n_sub * chunk
    wsp = w_sp[:, :chunk, :chunk]
    bias = jnp.repeat(b_sp[:, :chunk].T, HEAD_DIM, axis=1)
    return pl.pallas_call(
        functools.partial(_gmlp_kernel, chunk=chunk, n_sub=n_sub),
        grid=(rows // tm,),
        in_specs=[pl.BlockSpec((tm, 2 * W_A), lambda i: (i, 0)),
                  pl.BlockSpec((H_A, chunk, chunk), lambda i: (0, 0, 0)),
                  pl.BlockSpec((chunk, W_A), lambda i: (0, 0)),
                  pl.BlockSpec((1, W_A), lambda i: (0, 0)),
                  pl.BlockSpec((1, W_A), lambda i: (0, 0))],
        out_specs=[pl.BlockSpec((tm, W_A), lambda i: (i, 0)),
                   pl.BlockSpec((tm, W_A), lambda i: (i, 0))],
        out_shape=[jax.ShapeDtypeStruct((rows, W_A), F32),
                   jax.ShapeDtypeStruct((rows, W_A), F32)],
        compiler_params=_params("parallel"),
        name="gmlp",
    )(uv, wsp, bias, ln_g.reshape(1, W_A), ln_b.reshape(1, W_A))


def _rope(x, cos, sin):
    lane = lax.broadcasted_iota(jnp.int32, (x.shape[0], LANES), 1)
    first_half = (lane & (HEAD_DIM // 2)) == 0
    parts = []
    for c in range(x.shape[1] // LANES):
        xc = x[:, c * LANES:(c + 1) * LANES]
        rot = jnp.where(first_half,
                        pltpu.roll(xc, LANES - HEAD_DIM // 2, 1),
                        pltpu.roll(xc, HEAD_DIM // 2, 1))
        parts.append(xc * cos + rot * sin)
    return jnp.concatenate(parts, axis=1)


def _ret_kernel(r_ref, cos_ref, sin_ref, qdec_ref, kdec_ref, dec_ref, blk_ref, s0_ref,
                gng_ref, gnb_ref, o_ref, sout_ref, s_scr, o_scr, *, n_blocks):
    n = pl.program_id(1)

    @pl.when(n == 0)
    def _():
        s_scr[...] = s0_ref[0]

    r = r_ref[...]
    cos = cos_ref[...]
    sin = sin_ref[...]
    qr = _rope(r[:, :W_B], cos, sin)
    kr = _rope(r[:, W_B:2 * W_B], cos, sin) * (HEAD_DIM ** -0.5)
    vb = r[:, 2 * W_B:3 * W_B].astype(BF16)
    gate = r[:, 3 * W_B:]
    qb = qr.astype(BF16)
    kb = kr.astype(BF16)
    qdb = (qr * qdec_ref[...]).astype(BF16)
    kdb = (kr * kdec_ref[...]).astype(BF16)
    for h in range(H_B):
        sl = slice(h * HEAD_DIM, (h + 1) * HEAD_DIM)
        scores = lax.dot_general(qb[:, sl], kb[:, sl], NT_DIMS, preferred_element_type=F32) * dec_ref[h]
        s_h = s_scr[h]
        o_h = (jnp.dot(scores.astype(BF16), vb[:, sl], preferred_element_type=F32)
               + jnp.dot(qdb[:, sl], s_h.astype(BF16), preferred_element_type=F32))
        s_scr[h] = s_h * blk_ref[h] + lax.dot_general(kdb[:, sl], vb[:, sl], TN_DIMS,
                                                      preferred_element_type=F32)
        o_scr[:, sl] = _ln(o_h)
    o_ref[...] = (o_scr[...] * gng_ref[...] + gnb_ref[...]) * _silu(gate)

    @pl.when(n == n_blocks - 1)
    def _():
        sout_ref[0] = s_scr[...]


def _retention(ret, s0, pos0, bsz, seq, gn_g, gn_b):
    L = min(seq, RET_BLOCK)
    n_blocks = seq // L
    half = HEAD_DIM // 2
    inv = ROPE_BASE ** (-jnp.arange(half, dtype=F32) / half)
    ang = (pos0 + jnp.arange(seq)).astype(F32)[:, None] * inv[None, :]
    cos, sin = jnp.cos(ang), jnp.sin(ang)
    cos_t = jnp.tile(jnp.concatenate([cos, cos], axis=1), (1, LANES // HEAD_DIM))
    sin_t = jnp.tile(jnp.concatenate([-sin, sin], axis=1), (1, LANES // HEAD_DIM))
    log_g = jnp.log1p(-jnp.exp2(-5.0 - jnp.arange(H_B, dtype=F32)))
    idx = jnp.arange(L, dtype=F32)
    diff = idx[:, None] - idx[None, :]
    decay = jnp.where(diff >= 0, jnp.exp(diff[None] * log_g[:, None, None]), 0.0)
    q_decay = jnp.exp((idx[None, :] + 1.0) * log_g[:, None])
    k_decay = jnp.exp((L - 1.0 - idx[None, :]) * log_g[:, None])
    blk_decay = jnp.exp(L * log_g)
    qdec = jnp.repeat(q_decay.T, HEAD_DIM, axis=1)
    kdec = jnp.repeat(k_decay.T, HEAD_DIM, axis=1)
    blk = jnp.broadcast_to(blk_decay[:, None, None], (H_B, HEAD_DIM, HEAD_DIM))
    const2 = lambda b, n: (0, 0)
    const3 = lambda b, n: (0, 0, 0)
    return pl.pallas_call(
        functools.partial(_ret_kernel, n_blocks=n_blocks),
        grid=(bsz, n_blocks),
        in_specs=[pl.BlockSpec((L, 4 * W_B), lambda b, n: (b * n_blocks + n, 0)),
                  pl.BlockSpec((L, LANES), lambda b, n: (n, 0)),
                  pl.BlockSpec((L, LANES), lambda b, n: (n, 0)),
                  pl.BlockSpec((L, W_B), const2),
                  pl.BlockSpec((L, W_B), const2),
                  pl.BlockSpec((H_B, L, L), const3),
                  pl.BlockSpec((H_B, HEAD_DIM, HEAD_DIM), const3),
                  pl.BlockSpec((1, H_B, HEAD_DIM, HEAD_DIM), lambda b, n: (b, 0, 0, 0)),
                  pl.BlockSpec((1, W_B), const2),
                  pl.BlockSpec((1, W_B), const2)],
        out_specs=[pl.BlockSpec((L, W_B), lambda b, n: (b * n_blocks + n, 0)),
                   pl.BlockSpec((1, H_B, HEAD_DIM, HEAD_DIM), lambda b, n: (b, 0, 0, 0))],
        out_shape=[jax.ShapeDtypeStruct((bsz * seq, W_B), F32),
                   jax.ShapeDtypeStruct((bsz, H_B, HEAD_DIM, HEAD_DIM), F32)],
        scratch_shapes=[pltpu.VMEM((H_B, HEAD_DIM, HEAD_DIM), F32),
                        pltpu.VMEM((L, W_B), F32)],
        compiler_params=_params("parallel", "arbitrary"),
        name="retention",
    )(ret, cos_t, sin_t, qdec, kdec, decay, blk, s0, gn_g.reshape(1, W_B), gn_b.reshape(1, W_B))


def _sb_block(qm_ref, k_ref, v_ref, u2_ref, acc_ref, carry_ref, causal):
    tk = k_ref.shape[0]
    half = lax.broadcasted_iota(jnp.int32, (tk, LANES), 1) // HEAD_DIM
    u2 = u2_ref[...]
    kp = [k_ref[:, p * LANES:(p + 1) * LANES] for p in range(H_C // 2)]

    def scores(h):
        z = lax.dot_general(qm_ref[h], kp[h // 2], NT_DIMS, preferred_element_type=F32)
        neg_abs = pltpu.bitcast(pltpu.bitcast(z, jnp.uint32) | jnp.uint32(0x80000000), F32)
        ls_pos = jnp.minimum(z, 0.0) - jnp.log(1.0 + jnp.exp2(neg_abs)) * INV_LN2
        log_stay = ls_pos - z
        if causal is not None:
            log_stay = jnp.where(causal, log_stay, 0.0)
        hi = log_stay.astype(BF16)
        lo = (log_stay - hi.astype(F32)).astype(BF16)
        return ls_pos, log_stay[:, :1], jnp.concatenate([hi, lo], axis=1)

    def cumsum(hi_lo):
        return jnp.dot(hi_lo, u2, preferred_element_type=F32)

    def weigh(h, ls_pos, first_col, excl):
        carry = carry_ref[h]
        att = jnp.exp2(ls_pos + excl + carry)
        if causal is not None:
            att = jnp.where(causal, att, 0.0)
        vp = v_ref[:, (h // 2) * LANES:(h // 2 + 1) * LANES]
        vp = jnp.where(half == h % 2, vp, jnp.zeros_like(vp))
        carry_ref[h] = carry + (excl[:, :1] + first_col)
        return jnp.dot(att.astype(BF16), vp, preferred_element_type=F32)

    stage_a, stage_b, outs = {}, {}, {}
    for step in range(H_C + 2):
        if step - 2 >= 0:
            h = step - 2
            outs[h] = weigh(h, stage_a[h][0], stage_a[h][1], stage_b.pop(h))
            del stage_a[h]
            if h % 2 == 1:
                p = h // 2
                acc_ref[:, p * LANES:(p + 1) * LANES] += outs.pop(h - 1) + outs.pop(h)
        if 0 <= step - 1 < H_C:
            stage_b[step - 1] = cumsum(stage_a[step - 1][2])
        if step < H_C:
            stage_a[step] = scores(step)


def _sb_kernel(q_ref, k_ref, v_ref, u2_ref, o_ref, qm_ref, acc_ref, carry_ref, *, t):
    i = pl.program_id(1)
    acc_ref[...] = jnp.zeros_like(acc_ref)
    carry_ref[...] = jnp.zeros_like(carry_ref)
    half = lax.broadcasted_iota(jnp.int32, (t, LANES), 1) // HEAD_DIM
    for h in range(H_C):
        qp = q_ref[:, (h // 2) * LANES:(h // 2 + 1) * LANES] * SB_QSCALE
        qm_ref[h] = jnp.where(half == h % 2, qp, 0.0).astype(BF16)

    def block(j, causal):
        k0 = pl.multiple_of((i - j) * t, t)
        _sb_block(qm_ref, k_ref.at[pl.ds(k0, t), :], v_ref.at[pl.ds(k0, t), :], u2_ref, acc_ref, carry_ref, causal)

    block(0, lax.broadcasted_iota(jnp.int32, (t, t), 1) < lax.broadcasted_iota(jnp.int32, (t, t), 0))

    def key_block(state):
        j, _ = state
        block(j, None)
        dead = jnp.max(carry_ref[...]) < SB_DEAD
        return j + 1, dead.astype(jnp.int32)

    lax.while_loop(lambda state: jnp.logical_and(state[0] <= i, state[1] == 0),
                   key_block, (jnp.int32(1), jnp.int32(0)))
    o_ref[...] = acc_ref[...]


def _stick_breaking(q, k, v, bsz, seq, t):
    nq = seq // t
    tri = np.tril(np.ones((t, t), np.float32), -1)
    u2 = jnp.asarray(np.concatenate([tri, tri], axis=0), dtype=BF16)
    resident = dict(pipeline_mode=pl.Buffered(1))
    return pl.pallas_call(
        functools.partial(_sb_kernel, t=t),
        grid=(bsz, nq),
        in_specs=[pl.BlockSpec((t, W_C), lambda b, i: (b * nq + i, 0)),
                  pl.BlockSpec((seq, W_C), lambda b, i: (b, 0), **resident),
                  pl.BlockSpec((seq, W_C), lambda b, i: (b, 0), **resident),
                  pl.BlockSpec((2 * t, t), lambda b, i: (0, 0), **resident)],
        out_specs=pl.BlockSpec((t, W_C), lambda b, i: (b * nq + i, 0)),
        out_shape=jax.ShapeDtypeStruct((bsz * seq, W_C), F32),
        scratch_shapes=[pltpu.VMEM((H_C, t, LANES), BF16),
                        pltpu.VMEM((t, W_C), F32),
                        pltpu.VMEM((H_C, t, 1), F32)],
        compiler_params=_params("parallel", "arbitrary"),
        name="stick_breaking",
    )(q, k, v, u2)


def _sb_scores(z, causal=None):
    neg_abs = pltpu.bitcast(pltpu.bitcast(z, jnp.uint32) | jnp.uint32(0x80000000), F32)
    ls_pos = jnp.minimum(z, 0.0) - jnp.log(1.0 + jnp.exp2(neg_abs)) * INV_LN2
    log_stay = ls_pos - z
    if causal is not None:
        log_stay = jnp.where(causal, log_stay, 0.0)
    hi = log_stay.astype(BF16)
    lo = (log_stay - hi.astype(F32)).astype(BF16)
    return ls_pos, log_stay, jnp.concatenate([hi, lo], axis=1)


def _sb_step_kernel(q_ref, kt_ref, vt_ref, kn_ref, vn_ref, u2_ref, un_ref, o_ref, acc_ref, carry_ref, *, tk):
    seq = q_ref.shape[0]
    past = kt_ref.shape[-1]
    qs = (q_ref[...] * SB_QSCALE).astype(BF16)
    row = lax.broadcasted_iota(jnp.int32, (seq, seq), 0)
    col = lax.broadcasted_iota(jnp.int32, (seq, seq), 1)
    causal = col < row
    un = un_ref[...]
    for h in range(H_C):
        sl = slice(h * HEAD_DIM, (h + 1) * HEAD_DIM)
        z = lax.dot_general(qs[:, sl], kn_ref[:, sl], NT_DIMS, preferred_element_type=F32)
        ls_pos, log_stay, hi_lo = _sb_scores(z, causal)
        excl = jnp.dot(hi_lo, un, preferred_element_type=F32)
        att = jnp.where(causal, jnp.exp2(ls_pos + excl), 0.0)
        acc_ref[h] = jnp.dot(att.astype(BF16), vn_ref[:, sl], preferred_element_type=F32)
        carry_ref[h] = excl[:, :1] + log_stay[:, :1]

    u2 = u2_ref[...]

    def key_block(state):
        j, _ = state
        k0 = pl.multiple_of(past - (j + 1) * tk, tk)
        for h in range(H_C):
            kt = kt_ref[0, 0, h, :, pl.ds(k0, tk)].astype(BF16)
            vt = vt_ref[0, 0, h, :, pl.ds(k0, tk)].astype(BF16)
            z = jnp.dot(qs[:, h * HEAD_DIM:(h + 1) * HEAD_DIM], kt, preferred_element_type=F32)
            ls_pos, log_stay, hi_lo = _sb_scores(z)
            excl = jnp.dot(hi_lo, u2, preferred_element_type=F32)
            carry = carry_ref[h]
            att = jnp.exp2(ls_pos + excl + carry)
            acc_ref[h] += lax.dot_general(att.astype(BF16), vt, NT_DIMS, preferred_element_type=F32)
            carry_ref[h] = carry + (excl[:, :1] + log_stay[:, :1])
        dead = jnp.max(carry_ref[...]) < SB_DEAD
        return j + 1, dead.astype(jnp.int32)

    lax.while_loop(lambda state: jnp.logical_and(state[0] < past // tk, state[1] == 0),
                   key_block, (jnp.int32(0), jnp.int32(0)))
    for h in range(H_C):
        o_ref[:, h * HEAD_DIM:(h + 1) * HEAD_DIM] = acc_ref[h]


def _stick_breaking_step(q, k_new, v_new, cache_kt, cache_vt, layer, bsz, seq, tk):
    past = cache_kt.shape[-1]
    assert past % tk == 0
    tri = lambda n: np.tril(np.ones((n, n), np.float32), -1)
    stacked = lambda n: jnp.asarray(np.concatenate([tri(n), tri(n)], axis=0), dtype=BF16)
    rows = pl.BlockSpec((seq, W_C), lambda b: (b, 0))
    cache = pl.BlockSpec((1, 1, H_C, HEAD_DIM, past), lambda b: (layer, b, 0, 0, 0))
    return pl.pallas_call(
        functools.partial(_sb_step_kernel, tk=tk),
        grid=(bsz,),
        in_specs=[rows, cache, cache, rows, rows,
                  pl.BlockSpec((2 * tk, tk), lambda b: (0, 0)), pl.BlockSpec((2 * seq, seq), lambda b: (0, 0))],
        out_specs=rows,
        out_shape=jax.ShapeDtypeStruct((bsz * seq, W_C), F32),
        scratch_shapes=[pltpu.VMEM((H_C, seq, HEAD_DIM), F32), pltpu.VMEM((H_C, seq, 1), F32)],
        compiler_params=_params("parallel"),
        name="stick_breaking_step",
    )(q, cache_kt, cache_vt, k_new, v_new, stacked(tk), stacked(seq))


def _route(sel, s):
    g_scores = []
    for g in range(N_GROUPS):
        a, b, c, d = sel[EXPERTS_PER_GROUP * g:EXPERTS_PER_GROUP * (g + 1)]
        ab_hi, ab_lo = jnp.maximum(a, b), jnp.minimum(a, b)
        cd_hi, cd_lo = jnp.maximum(c, d), jnp.minimum(c, d)
        top1 = jnp.maximum(ab_hi, cd_hi)
        top2 = jnp.maximum(jnp.minimum(ab_hi, cd_hi), jnp.maximum(ab_lo, cd_lo))
        g_scores.append(top1 + top2)
    best = g_scores[0]
    gi = jnp.zeros(best.shape, jnp.int32)
    for g in range(1, N_GROUPS):
        upd = g_scores[g] > best
        gi = jnp.where(upd, g, gi)
        best = jnp.where(upd, g_scores[g], best)

    def pick_group(rows, l):
        out = rows[(N_GROUPS - 1) * EXPERTS_PER_GROUP + l]
        for g in range(N_GROUPS - 2, -1, -1):
            out = jnp.where(gi == g, rows[g * EXPERTS_PER_GROUP + l], out)
        return out

    ig = [pick_group(sel, l) for l in range(EXPERTS_PER_GROUP)]
    sg = [pick_group(s, l) for l in range(EXPERTS_PER_GROUP)]
    b1 = ig[0]
    i1 = jnp.zeros(best.shape, jnp.int32)
    for l in range(1, EXPERTS_PER_GROUP):
        upd = ig[l] > b1
        i1 = jnp.where(upd, l, i1)
        b1 = jnp.where(upd, ig[l], b1)
    b2 = jnp.full(best.shape, -jnp.inf, F32)
    i2 = jnp.zeros(best.shape, jnp.int32)
    for l in range(EXPERTS_PER_GROUP):
        upd = jnp.logical_and(i1 != l, ig[l] > b2)
        i2 = jnp.where(upd, l, i2)
        b2 = jnp.where(upd, ig[l], b2)

    def pick_local(idx):
        out = sg[EXPERTS_PER_GROUP - 1]
        for l in range(EXPERTS_PER_GROUP - 2, -1, -1):
            out = jnp.where(idx == l, sg[l], out)
        return out

    w1 = pick_local(i1)
    w2 = pick_local(i2)
    tot = w1 + w2
    return gi * EXPERTS_PER_GROUP + i1, gi * EXPERTS_PER_GROUP + i2, w1 / tot, w2 / tot


def _pair_class(e1, e2):
    lo = jnp.minimum(e1, e2)
    hi = jnp.maximum(e1, e2)
    g = lo // EXPERTS_PER_GROUP
    llo = lo - g * EXPERTS_PER_GROUP
    lhi = hi - g * EXPERTS_PER_GROUP
    return g * PAIRS_PER_GROUP + ((llo * (2 * EXPERTS_PER_GROUP - 1 - llo)) >> 1) + (lhi - llo - 1)


def _post_kernel(a_ref, b_ref, c_ref, x_ref, wo_ref, gate_ref, g1_ref, b1_ref, sh2_ref, sc2_ref,
                 wrt_ref, br_ref, x1_ref, *rest, tm, with_hx):
    if with_hx:
        h2_ref = None
        comb_ref, hx_ref, wob_ref = rest
    else:
        hx_ref = None
        h2_ref, comb_ref, wob_ref = rest
    @pl.when(pl.program_id(0) == 0)
    def _():
        wob_ref[...] = wo_ref[0].astype(BF16)

    sub = tm // POST_SUB if tm % (POST_SUB * LANES) == 0 else tm
    tiles = [slice(r0, r0 + sub) for r0 in range(0, tm, sub)]
    per_row = lambda ref, rows: ref[rows, :] if ref.shape[0] == tm else ref[...]
    proj = [jnp.dot(a_ref[rows, :].astype(BF16), wob_ref[:W_A], preferred_element_type=F32)
            + jnp.dot(b_ref[rows, :].astype(BF16), wob_ref[W_A:W_A + W_B], preferred_element_type=F32)
            + jnp.dot(c_ref[rows, :].astype(BF16), wob_ref[W_A + W_B:], preferred_element_type=F32)
            for rows in tiles]
    x1 = [_ln(ALPHA * x_ref[rows, :] + per_row(gate_ref, rows) * pr) * g1_ref[...] + b1_ref[...]
          for rows, pr in zip(tiles, proj)]
    for rows, v in zip(tiles, x1):
        x1_ref[rows, :] = v
    h2 = [_ln(v) * (1.0 + per_row(sc2_ref, rows)) + per_row(sh2_ref, rows) for rows, v in zip(tiles, x1)]
    logits = [lax.dot_general(wrt_ref[...], v, NT_DIMS, preferred_element_type=F32, precision=HIGHEST) for v in h2]
    for rows, v, logits_t in zip(tiles, h2, logits):
        if h2_ref is not None:
            h2_ref[rows, :] = v.astype(BF16)
        s_t = jax.nn.sigmoid(logits_t)
        sel_t = s_t + br_ref[...]
        s = [s_t[e:e + 1, :] for e in range(N_EXPERTS)]
        sel = [sel_t[e:e + 1, :] for e in range(N_EXPERTS)]
        e1, e2, w1, w2 = _route(sel, s)
        expert = lax.broadcasted_iota(jnp.int32, (LANES, sub), 0)
        comb_t = jnp.where(expert == e1, w1, jnp.where(expert == e2, w2, 0.0))
        comb_t = jnp.where(expert == N_EXPERTS, _pair_class(e1, e2).astype(F32), comb_t)
        comb_ref[rows, :] = comb_t.T
        if hx_ref is not None:
            hx_ref[rows, :D_MODEL] = v
            hx_ref[rows, D_MODEL:] = comb_t.T


def _post(a, b, c, x, w_out, layer, gate1, ln_g, ln_b, shift2, scale2, wr_t, br_col, tm, with_hx):
    rows = x.shape[0]
    if with_hx:
        tail_spec = [pl.BlockSpec((tm, LANES), lambda i: (i, 0)), pl.BlockSpec((tm, D_MODEL + LANES), lambda i: (i, 0))]
        tail_shape = [jax.ShapeDtypeStruct((rows, LANES), F32), jax.ShapeDtypeStruct((rows, D_MODEL + LANES), F32)]
    else:
        tail_spec = [pl.BlockSpec((tm, D_MODEL), lambda i: (i, 0)), pl.BlockSpec((tm, LANES), lambda i: (i, 0))]
        tail_shape = [jax.ShapeDtypeStruct((rows, D_MODEL), BF16), jax.ShapeDtypeStruct((rows, LANES), F32)]
    row = lambda w: pl.BlockSpec((tm, w), lambda i: (i, 0))
    const = lambda r, w: pl.BlockSpec((r, w), lambda i: (0, 0))
    return pl.pallas_call(
        functools.partial(_post_kernel, tm=tm, with_hx=with_hx),
        grid=(rows // tm,),
        in_specs=[row(W_A), row(W_B), row(W_C), row(D_MODEL),
                  pl.BlockSpec((1, D_MODEL, D_MODEL), lambda i: (layer, 0, 0), pipeline_mode=pl.Buffered(1)),
                  _row_spec(gate1.shape[0], tm, D_MODEL, rows),
                  const(1, D_MODEL), const(1, D_MODEL),
                  _row_spec(shift2.shape[0], tm, D_MODEL, rows),
                  _row_spec(scale2.shape[0], tm, D_MODEL, rows),
                  const(N_EXPERTS, D_MODEL), const(N_EXPERTS, 1)],
        out_specs=[row(D_MODEL)] + tail_spec,
        out_shape=[jax.ShapeDtypeStruct((rows, D_MODEL), F32)] + tail_shape,
        scratch_shapes=[pltpu.VMEM((D_MODEL, D_MODEL), BF16)],
        compiler_params=_params("arbitrary"),
        name="post_mix",
    )(a, b, c, x, w_out, gate1, ln_g.reshape(1, D_MODEL), ln_b.reshape(1, D_MODEL),
      shift2, scale2, wr_t, br_col)


def _moe_kernel(h_ref, comb_ref, x_ref, gate_ref, g2_ref, b2_ref, w1_ref, w3_ref, w2_ref,
                o_ref, acc_ref, *, tm):
    e = pl.program_id(1)

    @pl.when(e == 0)
    def _():
        acc_ref[...] = jnp.zeros_like(acc_ref)

    h = h_ref[...]
    a = jnp.dot(h, w1_ref[0], preferred_element_type=F32)
    g = jnp.dot(h, w3_ref[0], preferred_element_type=F32)
    lane = lax.broadcasted_iota(jnp.int32, (tm, LANES), 1)
    ce = jnp.sum(jnp.where(lane == e, comb_ref[...], 0.0), axis=-1, keepdims=True)
    act = _silu(a) * g * ce
    acc_ref[...] += jnp.dot(act.astype(BF16), w2_ref[0], preferred_element_type=F32)

    @pl.when(e == N_EXPERTS - 1)
    def _():
        y = ALPHA * x_ref[...] + gate_ref[...] * acc_ref[...]
        o_ref[...] = _ln(y) * g2_ref[...] + b2_ref[...]


def _moe(h2, comb, x1, gate2, ln_g, ln_b, w1, w3, w2, tm):
    rows = x1.shape[0]
    row = lambda w: pl.BlockSpec((tm, w), lambda i, e: (i, 0))
    const = pl.BlockSpec((1, D_MODEL), lambda i, e: (0, 0))
    gate_spec = (pl.BlockSpec((1, D_MODEL), lambda i, e: (0, 0)) if gate2.shape[0] == 1
                 else row(D_MODEL))
    return pl.pallas_call(
        functools.partial(_moe_kernel, tm=tm),
        grid=(rows // tm, N_EXPERTS),
        in_specs=[row(D_MODEL), row(LANES), row(D_MODEL), gate_spec, const, const,
                  pl.BlockSpec((1, D_MODEL, D_EXPERT), lambda i, e: (e, 0, 0)),
                  pl.BlockSpec((1, D_MODEL, D_EXPERT), lambda i, e: (e, 0, 0)),
                  pl.BlockSpec((1, D_EXPERT, D_MODEL), lambda i, e: (e, 0, 0))],
        out_specs=row(D_MODEL),
        out_shape=jax.ShapeDtypeStruct((rows, D_MODEL), F32),
        scratch_shapes=[pltpu.VMEM((tm, D_MODEL), F32)],
        compiler_params=_params("parallel", "arbitrary"),
        name="experts",
    )(h2, comb, x1, gate2, ln_g.reshape(1, D_MODEL), ln_b.reshape(1, D_MODEL), w1, w3, w2)


def _pair_experts():
    lo, hi = [], []
    for g in range(N_GROUPS):
        for a in range(EXPERTS_PER_GROUP):
            for b in range(a + 1, EXPERTS_PER_GROUP):
                lo.append(g * EXPERTS_PER_GROUP + a)
                hi.append(g * EXPERTS_PER_GROUP + b)
    return np.asarray(lo, np.int32), np.asarray(hi, np.int32)


def _slot_plan(cls, tm):
    rows = cls.shape[0]
    n_tiles = rows // tm + N_PAIRS
    onehot = (cls[:, None] == jnp.arange(N_PAIRS, dtype=jnp.int32)[None, :]).astype(jnp.int32)
    rank = jnp.cumsum(onehot, axis=0) - onehot
    tiles_per = (jnp.sum(onehot, axis=0) + tm - 1) // tm
    tile_start = jnp.cumsum(tiles_per) - tiles_per
    pos = jnp.sum(onehot * (tile_start[None, :] * tm + rank), axis=1).astype(jnp.int32)
    src = jnp.zeros((n_tiles * tm,), jnp.int32).at[pos].set(jnp.arange(rows, dtype=jnp.int32))
    tile = jnp.arange(n_tiles, dtype=jnp.int32)
    tile_class = jnp.clip(jnp.sum((tile[:, None] >= tile_start[None, :]).astype(jnp.int32), axis=1) - 1,
                          0, N_PAIRS - 1)
    pair_lo, pair_hi = _pair_experts()
    n_used = jnp.sum(tiles_per).astype(jnp.int32).reshape(1)
    return pos, src, jnp.asarray(pair_lo)[tile_class], jnp.asarray(pair_hi)[tile_class], n_used


def _start_rows(idx_ref, src_hbm, dst_ref, sem, n):
    for r in range(n):
        pltpu.make_async_copy(src_hbm.at[pl.ds(idx_ref[0, 0, r], 1)], dst_ref.at[pl.ds(r, 1)], sem).start(priority=r % 2)


def _wait_rows(src_hbm, dst_ref, sem, n):
    pltpu.make_async_copy(src_hbm.at[pl.ds(0, n)], dst_ref, sem).wait()


def _moe_pair_kernel(lo_ref, hi_ref, nused_ref, src_ref, src_next_ref, hx_hbm,
                     w1a_ref, w3a_ref, w2a_ref, w1b_ref, w3b_ref, w2b_ref, y_ref, xbuf, sems, *, tm):
    i = pl.program_id(0)
    n_used = nused_ref[0]
    slot = i % 2

    @pl.when(jnp.logical_and(i == 0, n_used > 0))
    def _():
        _start_rows(src_ref, hx_hbm, xbuf.at[0], sems.at[0], tm)

    @pl.when(i + 1 < n_used)
    def _():
        _start_rows(src_next_ref, hx_hbm, xbuf.at[1 - slot], sems.at[1 - slot], tm)

    @pl.when(i < n_used)
    def _():
        _wait_rows(hx_hbm, xbuf.at[slot], sems.at[slot], tm)
        x = xbuf[slot]
        h = x[:, :D_MODEL].astype(BF16)
        comb = x[:, D_MODEL:]
        lane = lax.broadcasted_iota(jnp.int32, (tm, LANES), 1)
        acc = jnp.zeros((tm, D_MODEL), F32)
        for e, w1_ref, w3_ref, w2_ref in ((lo_ref[i], w1a_ref, w3a_ref, w2a_ref), (hi_ref[i], w1b_ref, w3b_ref, w2b_ref)):
            a = jnp.dot(h, w1_ref[0], preferred_element_type=F32)
            g = jnp.dot(h, w3_ref[0], preferred_element_type=F32)
            ce = jnp.sum(jnp.where(lane == e, comb, 0.0), axis=-1, keepdims=True)
            act = _silu(a) * g * ce
            acc = acc + jnp.dot(act.astype(BF16), w2_ref[0], preferred_element_type=F32)
        y_ref[...] = acc

    @pl.when(i >= n_used)
    def _():
        y_ref[...] = jnp.zeros_like(y_ref)


def _moe_pairs(hx, src, tile_lo, tile_hi, n_used, w1, w3, w2, tm):
    n_tiles = tile_lo.shape[0]
    lo_spec = lambda k, n: pl.BlockSpec((1, k, n), lambda i, lo, hi, nu: (lo[i], 0, 0))
    hi_spec = lambda k, n: pl.BlockSpec((1, k, n), lambda i, lo, hi, nu: (hi[i], 0, 0))
    src3 = src.reshape(n_tiles, 1, tm)
    return pl.pallas_call(
        functools.partial(_moe_pair_kernel, tm=tm),
        grid_spec=pltpu.PrefetchScalarGridSpec(
            num_scalar_prefetch=3,
            grid=(n_tiles,),
            in_specs=[pl.BlockSpec((1, 1, tm), lambda i, lo, hi, nu: (i, 0, 0), memory_space=pltpu.SMEM),
                      pl.BlockSpec((1, 1, tm), lambda i, lo, hi, nu: (jnp.minimum(i + 1, n_tiles - 1), 0, 0),
                                   memory_space=pltpu.SMEM),
                      pl.BlockSpec(memory_space=pl.ANY),
                      lo_spec(D_MODEL, D_EXPERT), lo_spec(D_MODEL, D_EXPERT), lo_spec(D_EXPERT, D_MODEL),
                      hi_spec(D_MODEL, D_EXPERT), hi_spec(D_MODEL, D_EXPERT), hi_spec(D_EXPERT, D_MODEL)],
            out_specs=pl.BlockSpec((tm, D_MODEL), lambda i, lo, hi, nu: (i, 0)),
            scratch_shapes=[pltpu.VMEM((2, tm, D_MODEL + LANES), F32), pltpu.SemaphoreType.DMA((2,))]),
        out_shape=jax.ShapeDtypeStruct((n_tiles * tm, D_MODEL), F32),
        compiler_params=_params("arbitrary"),
        name="experts_paired",
    )(tile_lo, tile_hi, n_used, src3, src3, hx, w1, w3, w2, w1, w3, w2)


def _combine_kernel(pos_ref, pos_next_ref, y_hbm, x_ref, gate_ref, g2_ref, b2_ref, o_ref, ybuf, sems, *, tm, n_tiles):
    i = pl.program_id(0)
    slot = i % 2

    @pl.when(i == 0)
    def _():
        _start_rows(pos_ref, y_hbm, ybuf.at[0], sems.at[0], tm)

    @pl.when(i + 1 < n_tiles)
    def _():
        _start_rows(pos_next_ref, y_hbm, ybuf.at[1 - slot], sems.at[1 - slot], tm)

    _wait_rows(y_hbm, ybuf.at[slot], sems.at[slot], tm)
    y = ALPHA * x_ref[...] + gate_ref[...] * ybuf[slot]
    o_ref[...] = _ln(y) * g2_ref[...] + b2_ref[...]


def _combine(y_sorted, pos, x1, gate2, ln_g, ln_b, tm):
    rows = x1.shape[0]
    n_tiles = rows // tm
    row = pl.BlockSpec((tm, D_MODEL), lambda i: (i, 0))
    const = pl.BlockSpec((1, D_MODEL), lambda i: (0, 0))
    pos3 = pos.reshape(n_tiles, 1, tm)
    return pl.pallas_call(
        functools.partial(_combine_kernel, tm=tm, n_tiles=n_tiles),
        grid=(n_tiles,),
        in_specs=[pl.BlockSpec((1, 1, tm), lambda i: (i, 0, 0), memory_space=pltpu.SMEM),
                  pl.BlockSpec((1, 1, tm), lambda i: (jnp.minimum(i + 1, n_tiles - 1), 0, 0), memory_space=pltpu.SMEM),
                  pl.BlockSpec(memory_space=pl.ANY),
                  row, _row_spec(gate2.shape[0], tm, D_MODEL, rows), const, const],
        out_specs=row,
        out_shape=jax.ShapeDtypeStruct((rows, D_MODEL), F32),
        scratch_shapes=[pltpu.VMEM((2, tm, D_MODEL), F32), pltpu.SemaphoreType.DMA((2,))],
        compiler_params=_params("arbitrary"),
        name="combine",
    )(pos3, pos3, y_sorted, x1, gate2, ln_g.reshape(1, D_MODEL), ln_b.reshape(1, D_MODEL))


def _trunk_layer(x, mods, bsz, seq, pos0, s0, k_past, v_past, p, tiles):
    shift1, scale1, gate1, shift2, scale2, gate2 = mods
    uv, ret, q_c, k_c, v_c, k_bf, v_bf = _inproj(x, shift1, scale1, p["w_in"], p["layer"], tiles["tm_in"])
    a_out, v_rows = _gmlp(uv, p["w_sp"], p["b_sp"], p["ln_v_g"], p["ln_v_b"], min(seq, GMLP_CHUNK))
    b_out, s_new = _retention(ret, s0, pos0, bsz, seq, p["gn_g"], p["gn_b"])
    if k_past is None:
        c_out = _stick_breaking(q_c, k_bf, v_bf, bsz, seq, tiles["sb"])
    else:
        c_out = _stick_breaking_step(q_c, k_bf, v_bf, k_past, v_past, p["layer"], bsz, seq, tiles["sb"])
    grouped = "tm_group" in tiles
    post = _post(a_out, b_out, c_out, x, p["w_out"], p["layer"], gate1, p["ln1_g"], p["ln1_b"],
                 shift2, scale2, p["wr"], p["br"], tiles["tm_post"], grouped)
    if grouped:
        x1, comb, hx = post
        tm = tiles["tm_group"]
        pos, src, tile_lo, tile_hi, n_used = _slot_plan(comb[:, N_EXPERTS].astype(jnp.int32), tm)
        y_sorted = _moe_pairs(hx, src, tile_lo, tile_hi, n_used, p["w1"], p["w3"], p["w2"], tm)
        y = _combine(y_sorted, pos, x1, gate2, p["ln2_g"], p["ln2_b"], tm)
    else:
        x1, h2, comb = post
        y = _moe(h2, comb, x1, gate2, p["ln2_g"], p["ln2_b"], p["w1"], p["w3"], p["w2"], tiles["tm_moe"])
    return y, v_rows, s_new, k_c, v_c


def kernel(x_prompt, x_sample, cache_sb_k, cache_sb_v, state_ret, c_prompt, c_sample, w_ada, b_ada, w_in, w_out, ln_v_g, ln_v_b, w_spatial, b_spatial, gn_g, gn_b, ln1_g, ln1_b, ln2_g, ln2_b, w_router, b_router, w1, w3, w2):
    bp, tp, _ = x_prompt.shape
    bs, ts, _ = x_sample.shape
    past_len = cache_sb_k.shape[2]
    assert bp == 1

    n_c = bp + bs
    c_rows = -(-n_c // 8) * 8
    c_all = jnp.concatenate([c_prompt, c_sample, jnp.zeros((c_rows - n_c, D_MODEL), F32)], axis=0)
    mods = _adaln(c_all, w_ada, b_ada)

    wr_t = w_router.T
    cache_kt = jnp.transpose(cache_sb_k, (0, 1, 3, 4, 2))
    cache_vt = jnp.transpose(cache_sb_v, (0, 1, 3, 4, 2))
    br_col = b_router.reshape(N_EXPERTS, 1)

    tiles_p = dict(tm_in=256, sb=256, tm_post=512, tm_group=256)
    tiles_s = dict(tm_in=bs * ts, sb=256, tm_post=bs * ts, tm_moe=bs * ts)

    y_p = x_prompt.reshape(bp * tp, D_MODEL)
    y_s = x_sample.reshape(bs * ts, D_MODEL)
    zero_state = jnp.zeros((bp, H_B, HEAD_DIM, HEAD_DIM), F32)
    outs = [[] for _ in range(7)]
    for l in range(DEPTH):
        p = dict(w_in=w_in, w_out=w_out,
                 w_sp=w_spatial[l], b_sp=b_spatial[l], ln_v_g=ln_v_g[l], ln_v_b=ln_v_b[l],
                 gn_g=gn_g[l], gn_b=gn_b[l], ln1_g=ln1_g[l], ln1_b=ln1_b[l],
                 ln2_g=ln2_g[l], ln2_b=ln2_b[l], wr=wr_t, br=br_col,
                 w1=w1[l].astype(BF16), w3=w3[l].astype(BF16), w2=w2[l].astype(BF16), layer=l)
        m = mods[l]
        mods_p = [m[0:1, i * D_MODEL:(i + 1) * D_MODEL] for i in range(6)]
        mods_s = [jnp.repeat(m[bp:bp + bs, i * D_MODEL:(i + 1) * D_MODEL], ts, axis=0) for i in range(6)]
        y_p, _, s_p, k_p, v_p = _trunk_layer(y_p, mods_p, bp, tp, 0, zero_state, None, None, p, tiles_p)
        y_s, g_s, s_s, k_s, v_s = _trunk_layer(
            y_s, mods_s, bs, ts, past_len, state_ret[l],
            cache_kt, cache_vt, p, tiles_s)
        outs[0].append(s_p)
        outs[1].append(k_p.reshape(bp, tp, H_C, HEAD_DIM))
        outs[2].append(v_p.reshape(bp, tp, H_C, HEAD_DIM))
        outs[3].append(s_s)
        outs[4].append(k_s.reshape(bs, ts, H_C, HEAD_DIM))
        outs[5].append(v_s.reshape(bs, ts, H_C, HEAD_DIM))
        outs[6].append(g_s.reshape(bs, ts, W_A))
    return (y_p.reshape(bp, tp, D_MODEL), y_s.reshape(bs, ts, D_MODEL)) + tuple(jnp.stack(o) for o in outs)
```

```python
import functools

import numpy as np
import jax
import jax.numpy as jnp
from jax import lax
from jax.experimental import pallas as pl
from jax.experimental.pallas import tpu as pltpu

F32 = jnp.float32
BF16 = jnp.bfloat16
HIGHEST = lax.Precision.HIGHEST

D_MODEL = 1024
DEPTH = 2
HEAD_DIM = 64
W_A = D_MODEL // 4
W_B = 3 * D_MODEL // 8
W_C = D_MODEL - W_A - W_B
H_A = W_A // HEAD_DIM
H_B = W_B // HEAD_DIM
H_C = W_C // HEAD_DIM
GMLP_CHUNK = 128
ROPE_BASE = 10000.0
N_EXPERTS = 16
N_GROUPS = 4
EXPERTS_PER_GROUP = N_EXPERTS // N_GROUPS
D_EXPERT = D_MODEL // 2
PAIRS_PER_GROUP = EXPERTS_PER_GROUP * (EXPERTS_PER_GROUP - 1) // 2
N_PAIRS = N_GROUPS * PAIRS_PER_GROUP
ALPHA = (2 * DEPTH) ** 0.25
LN_EPS = 1e-5
D_IN = 2 * W_A + 4 * W_B + 3 * W_C
LANES = 128
VMEM_LIMIT = 48 * 1024 * 1024

NT_DIMS = (((1,), (1,)), ((), ()))
TN_DIMS = (((0,), (0,)), ((), ()))
INV_LN2 = 1.4426950408889634
SB_QSCALE = HEAD_DIM ** -0.5 * INV_LN2
SB_DEAD = -152.0
GMLP_SUB = 8
POST_SUB = 4
RET_BLOCK = 512


def _ln(x):
    mu = jnp.mean(x, axis=-1, keepdims=True)
    xc = x - mu
    var = jnp.mean(xc * xc, axis=-1, keepdims=True)
    return xc * lax.rsqrt(var + LN_EPS)


def _silu(x):
    return x * jax.nn.sigmoid(x)


def _params(*sem):
    return pltpu.CompilerParams(dimension_semantics=sem, vmem_limit_bytes=VMEM_LIMIT)


def _row_spec(rows, tm, width, total_rows):
    if rows == 1:
        return pl.BlockSpec((1, width), lambda i: (0, 0))
    assert rows == total_rows
    return pl.BlockSpec((tm, width), lambda i: (i, 0))


def _adaln_kernel(c_ref, w_ref, b_ref, o_ref):
    sc = _silu(c_ref[...])
    o_ref[0] = jnp.dot(sc, w_ref[0], preferred_element_type=F32, precision=HIGHEST) + b_ref[0]


def _adaln(c_all, w_ada, b_ada):
    rows = c_all.shape[0]
    tn = 1536
    return pl.pallas_call(
        _adaln_kernel,
        grid=(DEPTH, 6 * D_MODEL // tn),
        in_specs=[pl.BlockSpec((rows, D_MODEL), lambda l, j: (0, 0)),
                  pl.BlockSpec((1, D_MODEL, tn), lambda l, j: (l, 0, j)),
                  pl.BlockSpec((1, 1, tn), lambda l, j: (l, 0, j))],
        out_specs=pl.BlockSpec((1, rows, tn), lambda l, j: (l, 0, j)),
        out_shape=jax.ShapeDtypeStruct((DEPTH, rows, 6 * D_MODEL), F32),
        compiler_params=_params("parallel", "parallel"),
        name="adaln",
    )(c_all, w_ada, b_ada.reshape(DEPTH, 1, 6 * D_MODEL))


def _inproj_kernel(x_ref, shift_ref, scale_ref, w_ref, uv_ref, ret_ref, q_ref, k_ref, v_ref, kb_ref, vb_ref, wb_ref):
    @pl.when(pl.program_id(0) == 0)
    def _():
        wb_ref[...] = w_ref[0].astype(BF16)

    h = _ln(x_ref[...]) * (1.0 + scale_ref[...]) + shift_ref[...]
    r = jnp.dot(h.astype(BF16), wb_ref[...], preferred_element_type=F32)
    c0 = 2 * W_A
    c1 = c0 + 4 * W_B
    uv_ref[...] = r[:, :c0]
    ret_ref[...] = r[:, c0:c1]
    q_ref[...] = r[:, c1:c1 + W_C]
    k = r[:, c1 + W_C:c1 + 2 * W_C]
    v = r[:, c1 + 2 * W_C:]
    k_ref[...] = k
    v_ref[...] = v
    kb_ref[...] = k.astype(BF16)
    vb_ref[...] = v.astype(BF16)


def _inproj(x, shift, scale, w_in, layer, tm):
    rows = x.shape[0]
    widths = (2 * W_A, 4 * W_B, W_C, W_C, W_C, W_C, W_C)
    dtypes = (F32,) * 5 + (BF16,) * 2
    return pl.pallas_call(
        _inproj_kernel,
        grid=(rows // tm,),
        in_specs=[pl.BlockSpec((tm, D_MODEL), lambda i: (i, 0)),
                  _row_spec(shift.shape[0], tm, D_MODEL, rows),
                  _row_spec(scale.shape[0], tm, D_MODEL, rows),
                  pl.BlockSpec((1, D_MODEL, D_IN), lambda i: (layer, 0, 0), pipeline_mode=pl.Buffered(1))],
        out_specs=[pl.BlockSpec((tm, w), lambda i: (i, 0)) for w in widths],
        out_shape=[jax.ShapeDtypeStruct((rows, w), dt) for w, dt in zip(widths, dtypes)],
        scratch_shapes=[pltpu.VMEM((D_MODEL, D_IN), BF16)],
        compiler_params=_params("arbitrary"),
        name="inproj",
    )(x, shift, scale, w_in)


def _gmlp_kernel(uv_ref, wsp_ref, bias_ref, g_ref, b_ref, a_ref, vn_ref, *, chunk, n_sub):
    uv = uv_ref[...]
    u = jax.nn.gelu(uv[:, :W_A])
    v = _ln(jax.nn.gelu(uv[:, W_A:])) * g_ref[...] + b_ref[...]
    vn_ref[...] = v
    row = lax.broadcasted_iota(jnp.int32, (chunk, chunk), 0)
    col = lax.broadcasted_iota(jnp.int32, (chunk, chunk), 1)
    lane_head = lax.broadcasted_iota(jnp.int32, (chunk, W_A), 1) // HEAD_DIM
    w = [jnp.where(col <= row, wsp_ref[h], 0.0).astype(BF16) for h in range(H_A)]
    for c in range(n_sub):
        rows = slice(c * chunk, (c + 1) * chunk)
        vc = v[rows]
        mixed = bias_ref[...]
        for h in range(H_A):
            vh = jnp.where(lane_head == h, vc, 0.0).astype(BF16)
            mixed = mixed + jnp.dot(w[h], vh, preferred_element_type=F32)
        a_ref[rows, :] = u[rows] * mixed


def _gmlp(uv, w_sp, b_sp, ln_g, ln_b, chunk):
    rows = uv.shape[0]
    n_sub = min(GMLP_SUB, rows // chunk)
    tm = n_sub * chunk
    wsp = w_sp[:, :chunk, :chunk]
    bias = jnp.repeat(b_sp[:, :chunk].T, HEAD_DIM, axis=1)
    return pl.pallas_call(
        functools.partial(_gmlp_kernel, chunk=chunk, n_sub=n_sub),
        grid=(rows // tm,),
        in_specs=[pl.BlockSpec((tm, 2 * W_A), lambda i: (i, 0)),
                  pl.BlockSpec((H_A, chunk, chunk), lambda i: (0, 0, 0)),
                  pl.BlockSpec((chunk, W_A), lambda i: (0, 0)),
                  pl.BlockSpec((1, W_A), lambda i: (0, 0)),
                  pl.BlockSpec((1, W_A), lambda i: (0, 0))],
        out_specs=[pl.BlockSpec((tm, W_A), lambda i: (i, 0)),
                   pl.BlockSpec((tm, W_A), lambda i: (i, 0))],
        out_shape=[jax.ShapeDtypeStruct((rows, W_A), F32),
                   jax.ShapeDtypeStruct((rows, W_A), F32)],
        compiler_params=_params("parallel"),
        name="gmlp",
    )(uv, wsp, bias, ln_g.reshape(1, W_A), ln_b.reshape(1, W_A))


def _rope(x, cos, sin):
    lane = lax.broadcasted_iota(jnp.int32, (x.shape[0], LANES), 1)
    first_half = (lane & (HEAD_DIM // 2)) == 0
    parts = []
    for c in range(x.shape[1] // LANES):
        xc = x[:, c * LANES:(c + 1) * LANES]
        rot = jnp.where(first_half,
                        pltpu.roll(xc, LANES - HEAD_DIM // 2, 1),
                        pltpu.roll(xc, HEAD_DIM // 2, 1))
        parts.append(xc * cos + rot * sin)
    return jnp.concatenate(parts, axis=1)


def _ret_kernel(r_ref, cos_ref, sin_ref, qdec_ref, kdec_ref, dec_ref, blk_ref, s0_ref,
                gng_ref, gnb_ref, o_ref, sout_ref, s_scr, o_scr, *, n_blocks):
    n = pl.program_id(1)

    @pl.when(n == 0)
    def _():
        s_scr[...] = s0_ref[0]

    r = r_ref[...]
    cos = cos_ref[...]
    sin = sin_ref[...]
    qr = _rope(r[:, :W_B], cos, sin)
    kr = _rope(r[:, W_B:2 * W_B], cos, sin) * (HEAD_DIM ** -0.5)
    vb = r[:, 2 * W_B:3 * W_B].astype(BF16)
    gate = r[:, 3 * W_B:]
    qb = qr.astype(BF16)
    kb = kr.astype(BF16)
    qdb = (qr * qdec_ref[...]).astype(BF16)
    kdb = (kr * kdec_ref[...]).astype(BF16)
    for h in range(H_B):
        sl = slice(h * HEAD_DIM, (h + 1) * HEAD_DIM)
        scores = lax.dot_general(qb[:, sl], kb[:, sl], NT_DIMS, preferred_element_type=F32) * dec_ref[h]
        s_h = s_scr[h]
        o_h = (jnp.dot(scores.astype(BF16), vb[:, sl], preferred_element_type=F32)
               + jnp.dot(qdb[:, sl], s_h.astype(BF16), preferred_element_type=F32))
        s_scr[h] = s_h * blk_ref[h] + lax.dot_general(kdb[:, sl], vb[:, sl], TN_DIMS,
                                                      preferred_element_type=F32)
        o_scr[:, sl] = _ln(o_h)
    o_ref[...] = (o_scr[...] * gng_ref[...] + gnb_ref[...]) * _silu(gate)

    @pl.when(n == n_blocks - 1)
    def _():
        sout_ref[0] = s_scr[...]


def _retention(ret, s0, pos0, bsz, seq, gn_g, gn_b):
    L = min(seq, RET_BLOCK)
    n_blocks = seq // L
    half = HEAD_DIM // 2
    inv = ROPE_BASE ** (-jnp.arange(half, dtype=F32) / half)
    ang = (pos0 + jnp.arange(seq)).astype(F32)[:, None] * inv[None, :]
    cos, sin = jnp.cos(ang), jnp.sin(ang)
    cos_t = jnp.tile(jnp.concatenate([cos, cos], axis=1), (1, LANES // HEAD_DIM))
    sin_t = jnp.tile(jnp.concatenate([-sin, sin], axis=1), (1, LANES // HEAD_DIM))
    log_g = jnp.log1p(-jnp.exp2(-5.0 - jnp.arange(H_B, dtype=F32)))
    idx = jnp.arange(L, dtype=F32)
    diff = idx[:, None] - idx[None, :]
    decay = jnp.where(diff >= 0, jnp.exp(diff[None] * log_g[:, None, None]), 0.0)
    q_decay = jnp.exp((idx[None, :] + 1.0) * log_g[:, None])
    k_decay = jnp.exp((L - 1.0 - idx[None, :]) * log_g[:, None])
    blk_decay = jnp.exp(L * log_g)
    qdec = jnp.repeat(q_decay.T, HEAD_DIM, axis=1)
    kdec = jnp.repeat(k_decay.T, HEAD_DIM, axis=1)
    blk = jnp.broadcast_to(blk_decay[:, None, None], (H_B, HEAD_DIM, HEAD_DIM))
    const2 = lambda b, n: (0, 0)
    const3 = lambda b, n: (0, 0, 0)
    return pl.pallas_call(
        functools.partial(_ret_kernel, n_blocks=n_blocks),
        grid=(bsz, n_blocks),
        in_specs=[pl.BlockSpec((L, 4 * W_B), lambda b, n: (b * n_blocks + n, 0)),
                  pl.BlockSpec((L, LANES), lambda b, n: (n, 0)),
                  pl.BlockSpec((L, LANES), lambda b, n: (n, 0)),
                  pl.BlockSpec((L, W_B), const2),
                  pl.BlockSpec((L, W_B), const2),
                  pl.BlockSpec((H_B, L, L), const3),
                  pl.BlockSpec((H_B, HEAD_DIM, HEAD_DIM), const3),
                  pl.BlockSpec((1, H_B, HEAD_DIM, HEAD_DIM), lambda b, n: (b, 0, 0, 0)),
                  pl.BlockSpec((1, W_B), const2),
                  pl.BlockSpec((1, W_B), const2)],
        out_specs=[pl.BlockSpec((L, W_B), lambda b, n: (b * n_blocks + n, 0)),
                   pl.BlockSpec((1, H_B, HEAD_DIM, HEAD_DIM), lambda b, n: (b, 0, 0, 0))],
        out_shape=[jax.ShapeDtypeStruct((bsz * seq, W_B), F32),
                   jax.ShapeDtypeStruct((bsz, H_B, HEAD_DIM, HEAD_DIM), F32)],
        scratch_shapes=[pltpu.VMEM((H_B, HEAD_DIM, HEAD_DIM), F32),
                        pltpu.VMEM((L, W_B), F32)],
        compiler_params=_params("parallel", "arbitrary"),
        name="retention",
    )(ret, cos_t, sin_t, qdec, kdec, decay, blk, s0, gn_g.reshape(1, W_B), gn_b.reshape(1, W_B))


def _sb_block(qm_ref, k_ref, v_ref, u2_ref, acc_ref, carry_ref, causal):
    tk = k_ref.shape[0]
    half = lax.broadcasted_iota(jnp.int32, (tk, LANES), 1) // HEAD_DIM
    u2 = u2_ref[...]
    kp = [k_ref[:, p * LANES:(p + 1) * LANES] for p in range(H_C // 2)]

    def scores(h):
        z = lax.dot_general(qm_ref[h], kp[h // 2], NT_DIMS, preferred_element_type=F32)
        neg_abs = pltpu.bitcast(pltpu.bitcast(z, jnp.uint32) | jnp.uint32(0x80000000), F32)
        ls_pos = jnp.minimum(z, 0.0) - jnp.log(1.0 + jnp.exp2(neg_abs)) * INV_LN2
        log_stay = ls_pos - z
        if causal is not None:
            log_stay = jnp.where(causal, log_stay, 0.0)
        hi = log_stay.astype(BF16)
        lo = (log_stay - hi.astype(F32)).astype(BF16)
        return ls_pos, log_stay[:, :1], jnp.concatenate([hi, lo], axis=1)

    def cumsum(hi_lo):
        return jnp.dot(hi_lo, u2, preferred_element_type=F32)

    def weigh(h, ls_pos, first_col, excl):
        carry = carry_ref[h]
        att = jnp.exp2(ls_pos + excl + carry)
        if causal is not None:
            att = jnp.where(causal, att, 0.0)
        vp = v_ref[:, (h // 2) * LANES:(h // 2 + 1) * LANES]
        vp = jnp.where(half == h % 2, vp, jnp.zeros_like(vp))
        carry_ref[h] = carry + (excl[:, :1] + first_col)
        return jnp.dot(att.astype(BF16), vp, preferred_element_type=F32)

    stage_a, stage_b, outs = {}, {}, {}
    for step in range(H_C + 2):
        if step - 2 >= 0:
            h = step - 2
            outs[h] = weigh(h, stage_a[h][0], stage_a[h][1], stage_b.pop(h))
            del stage_a[h]
            if h % 2 == 1:
                p = h // 2
                acc_ref[:, p * LANES:(p + 1) * LANES] += outs.pop(h - 1) + outs.pop(h)
        if 0 <= step - 1 < H_C:
            stage_b[step - 1] = cumsum(stage_a[step - 1][2])
        if step < H_C:
            stage_a[step] = scores(step)


def _sb_kernel(q_ref, k_ref, v_ref, u2_ref, o_ref, qm_ref, acc_ref, carry_ref, *, t):
    i = pl.program_id(1)
    acc_ref[...] = jnp.zeros_like(acc_ref)
    carry_ref[...] = jnp.zeros_like(carry_ref)
    half = lax.broadcasted_iota(jnp.int32, (t, LANES), 1) // HEAD_DIM
    for h in range(H_C):
        qp = q_ref[:, (h // 2) * LANES:(h // 2 + 1) * LANES] * SB_QSCALE
        qm_ref[h] = jnp.where(half == h % 2, qp, 0.0).astype(BF16)

    def block(j, causal):
        k0 = pl.multiple_of((i - j) * t, t)
        _sb_block(qm_ref, k_ref.at[pl.ds(k0, t), :], v_ref.at[pl.ds(k0, t), :], u2_ref, acc_ref, carry_ref, causal)

    block(0, lax.broadcasted_iota(jnp.int32, (t, t), 1) < lax.broadcasted_iota(jnp.int32, (t, t), 0))

    def key_block(state):
        j, _ = state
        block(j, None)
        dead = jnp.max(carry_ref[...]) < SB_DEAD
        return j + 1, dead.astype(jnp.int32)

    lax.while_loop(lambda state: jnp.logical_and(state[0] <= i, state[1] == 0),
                   key_block, (jnp.int32(1), jnp.int32(0)))
    o_ref[...] = acc_ref[...]


def _stick_breaking(q, k, v, bsz, seq, t):
    nq = seq // t
    tri = np.tril(np.ones((t, t), np.float32), -1)
    u2 = jnp.asarray(np.concatenate([tri, tri], axis=0), dtype=BF16)
    resident = dict(pipeline_mode=pl.Buffered(1))
    return pl.pallas_call(
        functools.partial(_sb_kernel, t=t),
        grid=(bsz, nq),
        in_specs=[pl.BlockSpec((t, W_C), lambda b, i: (b * nq + i, 0)),
                  pl.BlockSpec((seq, W_C), lambda b, i: (b, 0), **resident),
                  pl.BlockSpec((seq, W_C), lambda b, i: (b, 0), **resident),
                  pl.BlockSpec((2 * t, t), lambda b, i: (0, 0), **resident)],
        out_specs=pl.BlockSpec((t, W_C), lambda b, i: (b * nq + i, 0)),
        out_shape=jax.ShapeDtypeStruct((bsz * seq, W_C), F32),
        scratch_shapes=[pltpu.VMEM((H_C, t, LANES), BF16),
                        pltpu.VMEM((t, W_C), F32),
                        pltpu.VMEM((H_C, t, 1), F32)],
        compiler_params=_params("parallel", "arbitrary"),
        name="stick_breaking",
    )(q, k, v, u2)


def _sb_scores(z, causal=None):
    neg_abs = pltpu.bitcast(pltpu.bitcast(z, jnp.uint32) | jnp.uint32(0x80000000), F32)
    ls_pos = jnp.minimum(z, 0.0) - jnp.log(1.0 + jnp.exp2(neg_abs)) * INV_LN2
    log_stay = ls_pos - z
    if causal is not None:
        log_stay = jnp.where(causal, log_stay, 0.0)
    hi = log_stay.astype(BF16)
    lo = (log_stay - hi.astype(F32)).astype(BF16)
    return ls_pos, log_stay, jnp.concatenate([hi, lo], axis=1)


def _sb_step_kernel(q_ref, kt_ref, vt_ref, kn_ref, vn_ref, u2_ref, un_ref, o_ref, acc_ref, carry_ref, *, tk):
    seq = q_ref.shape[0]
    past = kt_ref.shape[-1]
    qs = (q_ref[...] * SB_QSCALE).astype(BF16)
    row = lax.broadcasted_iota(jnp.int32, (seq, seq), 0)
    col = lax.broadcasted_iota(jnp.int32, (seq, seq), 1)
    causal = col < row
    un = un_ref[...]
    for h in range(H_C):
        sl = slice(h * HEAD_DIM, (h + 1) * HEAD_DIM)
        z = lax.dot_general(qs[:, sl], kn_ref[:, sl], NT_DIMS, preferred_element_type=F32)
        ls_pos, log_stay, hi_lo = _sb_scores(z, causal)
        excl = jnp.dot(hi_lo, un, preferred_element_type=F32)
        att = jnp.where(causal, jnp.exp2(ls_pos + excl), 0.0)
        acc_ref[h] = jnp.dot(att.astype(BF16), vn_ref[:, sl], preferred_element_type=F32)
        carry_ref[h] = excl[:, :1] + log_stay[:, :1]

    u2 = u2_ref[...]

    def key_block(state):
        j, _ = state
        k0 = pl.multiple_of(past - (j + 1) * tk, tk)
        for h in range(H_C):
            kt = kt_ref[0, 0, h, :, pl.ds(k0, tk)].astype(BF16)
            vt = vt_ref[0, 0, h, :, pl.ds(k0, tk)].astype(BF16)
            z = jnp.dot(qs[:, h * HEAD_DIM:(h + 1) * HEAD_DIM], kt, preferred_element_type=F32)
            ls_pos, log_stay, hi_lo = _sb_scores(z)
            excl = jnp.dot(hi_lo, u2, preferred_element_type=F32)
            carry = carry_ref[h]
            att = jnp.exp2(ls_pos + excl + carry)
            acc_ref[h] += lax.dot_general(att.astype(BF16), vt, NT_DIMS, preferred_element_type=F32)
            carry_ref[h] = carry + (excl[:, :1] + log_stay[:, :1])
        dead = jnp.max(carry_ref[...]) < SB_DEAD
        return j + 1, dead.astype(jnp.int32)

    lax.while_loop(lambda state: jnp.logical_and(state[0] < past // tk, state[1] == 0),
                   key_block, (jnp.int32(0), jnp.int32(0)))
    for h in range(H_C):
        o_ref[:, h * HEAD_DIM:(h + 1) * HEAD_DIM] = acc_ref[h]


def _stick_breaking_step(q, k_new, v_new, cache_kt, cache_vt, layer, bsz, seq, tk):
    past = cache_kt.shape[-1]
    assert past % tk == 0
    tri = lambda n: np.tril(np.ones((n, n), np.float32), -1)
    stacked = lambda n: jnp.asarray(np.concatenate([tri(n), tri(n)], axis=0), dtype=BF16)
    rows = pl.BlockSpec((seq, W_C), lambda b: (b, 0))
    cache = pl.BlockSpec((1, 1, H_C, HEAD_DIM, past), lambda b: (layer, b, 0, 0, 0))
    return pl.pallas_call(
        functools.partial(_sb_step_kernel, tk=tk),
        grid=(bsz,),
        in_specs=[rows, cache, cache, rows, rows,
                  pl.BlockSpec((2 * tk, tk), lambda b: (0, 0)), pl.BlockSpec((2 * seq, seq), lambda b: (0, 0))],
        out_specs=rows,
        out_shape=jax.ShapeDtypeStruct((bsz * seq, W_C), F32),
        scratch_shapes=[pltpu.VMEM((H_C, seq, HEAD_DIM), F32), pltpu.VMEM((H_C, seq, 1), F32)],
        compiler_params=_params("parallel"),
        name="stick_breaking_step",
    )(q, cache_kt, cache_vt, k_new, v_new, stacked(tk), stacked(seq))


def _route(sel, s):
    g_scores = []
    for g in range(N_GROUPS):
        a, b, c, d = sel[EXPERTS_PER_GROUP * g:EXPERTS_PER_GROUP * (g + 1)]
        ab_hi, ab_lo = jnp.maximum(a, b), jnp.minimum(a, b)
        cd_hi, cd_lo = jnp.maximum(c, d), jnp.minimum(c, d)
        top1 = jnp.maximum(ab_hi, cd_hi)
        top2 = jnp.maximum(jnp.minimum(ab_hi, cd_hi), jnp.maximum(ab_lo, cd_lo))
        g_scores.append(top1 + top2)
    best = g_scores[0]
    gi = jnp.zeros(best.shape, jnp.int32)
    for g in range(1, N_GROUPS):
        upd = g_scores[g] > best
        gi = jnp.where(upd, g, gi)
        best = jnp.where(upd, g_scores[g], best)

    def pick_group(rows, l):
        out = rows[(N_GROUPS - 1) * EXPERTS_PER_GROUP + l]
        for g in range(N_GROUPS - 2, -1, -1):
            out = jnp.where(gi == g, rows[g * EXPERTS_PER_GROUP + l], out)
        return out

    ig = [pick_group(sel, l) for l in range(EXPERTS_PER_GROUP)]
    sg = [pick_group(s, l) for l in range(EXPERTS_PER_GROUP)]
    b1 = ig[0]
    i1 = jnp.zeros(best.shape, jnp.int32)
    for l in range(1, EXPERTS_PER_GROUP):
        upd = ig[l] > b1
        i1 = jnp.where(upd, l, i1)
        b1 = jnp.where(upd, ig[l], b1)
    b2 = jnp.full(best.shape, -jnp.inf, F32)
    i2 = jnp.zeros(best.shape, jnp.int32)
    for l in range(EXPERTS_PER_GROUP):
        upd = jnp.logical_and(i1 != l, ig[l] > b2)
        i2 = jnp.where(upd, l, i2)
        b2 = jnp.where(upd, ig[l], b2)

    def pick_local(idx):
        out = sg[EXPERTS_PER_GROUP - 1]
        for l in range(EXPERTS_PER_GROUP - 2, -1, -1):
            out = jnp.where(idx == l, sg[l], out)
        return out

    w1 = pick_local(i1)
    w2 = pick_local(i2)
    tot = w1 + w2
    return gi * EXPERTS_PER_GROUP + i1, gi * EXPERTS_PER_GROUP + i2, w1 / tot, w2 / tot


def _pair_class(e1, e2):
    lo = jnp.minimum(e1, e2)
    hi = jnp.maximum(e1, e2)
    g = lo // EXPERTS_PER_GROUP
    llo = lo - g * EXPERTS_PER_GROUP
    lhi = hi - g * EXPERTS_PER_GROUP
    return g * PAIRS_PER_GROUP + ((llo * (2 * EXPERTS_PER_GROUP - 1 - llo)) >> 1) + (lhi - llo - 1)


def _post_kernel(a_ref, b_ref, c_ref, x_ref, wo_ref, gate_ref, g1_ref, b1_ref, sh2_ref, sc2_ref,
                 wrt_ref, br_ref, x1_ref, *rest, tm, with_hx):
    if with_hx:
        h2_ref = None
        comb_ref, hx_ref, wob_ref = rest
    else:
        hx_ref = None
        h2_ref, comb_ref, wob_ref = rest
    @pl.when(pl.program_id(0) == 0)
    def _():
        wob_ref[...] = wo_ref[0].astype(BF16)

    sub = tm // POST_SUB if tm % (POST_SUB * LANES) == 0 else tm
    tiles = [slice(r0, r0 + sub) for r0 in range(0, tm, sub)]
    per_row = lambda ref, rows: ref[rows, :] if ref.shape[0] == tm else ref[...]
    proj = [jnp.dot(a_ref[rows, :].astype(BF16), wob_ref[:W_A], preferred_element_type=F32)
            + jnp.dot(b_ref[rows, :].astype(BF16), wob_ref[W_A:W_A + W_B], preferred_element_type=F32)
            + jnp.dot(c_ref[rows, :].astype(BF16), wob_ref[W_A + W_B:], preferred_element_type=F32)
            for rows in tiles]
    x1 = [_ln(ALPHA * x_ref[rows, :] + per_row(gate_ref, rows) * pr) * g1_ref[...] + b1_ref[...]
          for rows, pr in zip(tiles, proj)]
    for rows, v in zip(tiles, x1):
        x1_ref[rows, :] = v
    h2 = [_ln(v) * (1.0 + per_row(sc2_ref, rows)) + per_row(sh2_ref, rows) for rows, v in zip(tiles, x1)]
    logits = [lax.dot_general(wrt_ref[...], v, NT_DIMS, preferred_element_type=F32, precision=HIGHEST) for v in h2]
    for rows, v, logits_t in zip(tiles, h2, logits):
        if h2_ref is not None:
            h2_ref[rows, :] = v.astype(BF16)
        s_t = jax.nn.sigmoid(logits_t)
        sel_t = s_t + br_ref[...]
        s = [s_t[e:e + 1, :] for e in range(N_EXPERTS)]
        sel = [sel_t[e:e + 1, :] for e in range(N_EXPERTS)]
        e1, e2, w1, w2 = _route(sel, s)
        expert = lax.broadcasted_iota(jnp.int32, (LANES, sub), 0)
        comb_t = jnp.where(expert == e1, w1, jnp.where(expert == e2, w2, 0.0))
        comb_t = jnp.where(expert == N_EXPERTS, _pair_class(e1, e2).astype(F32), comb_t)
        comb_ref[rows, :] = comb_t.T
        if hx_ref is not None:
            hx_ref[rows, :D_MODEL] = v
            hx_ref[rows, D_MODEL:] = comb_t.T


def _post(a, b, c, x, w_out, layer, gate1, ln_g, ln_b, shift2, scale2, wr_t, br_col, tm, with_hx):
    rows = x.shape[0]
    if with_hx:
        tail_spec = [pl.BlockSpec((tm, LANES), lambda i: (i, 0)), pl.BlockSpec((tm, D_MODEL + LANES), lambda i: (i, 0))]
        tail_shape = [jax.ShapeDtypeStruct((rows, LANES), F32), jax.ShapeDtypeStruct((rows, D_MODEL + LANES), F32)]
    else:
        tail_spec = [pl.BlockSpec((tm, D_MODEL), lambda i: (i, 0)), pl.BlockSpec((tm, LANES), lambda i: (i, 0))]
        tail_shape = [jax.ShapeDtypeStruct((rows, D_MODEL), BF16), jax.ShapeDtypeStruct((rows, LANES), F32)]
    row = lambda w: pl.BlockSpec((tm, w), lambda i: (i, 0))
    const = lambda r, w: pl.BlockSpec((r, w), lambda i: (0, 0))
    return pl.pallas_call(
        functools.partial(_post_kernel, tm=tm, with_hx=with_hx),
        grid=(rows // tm,),
        in_specs=[row(W_A), row(W_B), row(W_C), row(D_MODEL),
                  pl.BlockSpec((1, D_MODEL, D_MODEL), lambda i: (layer, 0, 0), pipeline_mode=pl.Buffered(1)),
                  _row_spec(gate1.shape[0], tm, D_MODEL, rows),
                  const(1, D_MODEL), const(1, D_MODEL),
                  _row_spec(shift2.shape[0], tm, D_MODEL, rows),
                  _row_spec(scale2.shape[0], tm, D_MODEL, rows),
                  const(N_EXPERTS, D_MODEL), const(N_EXPERTS, 1)],
        out_specs=[row(D_MODEL)] + tail_spec,
        out_shape=[jax.ShapeDtypeStruct((rows, D_MODEL), F32)] + tail_shape,
        scratch_shapes=[pltpu.VMEM((D_MODEL, D_MODEL), BF16)],
        compiler_params=_params("arbitrary"),
        name="post_mix",
    )(a, b, c, x, w_out, gate1, ln_g.reshape(1, D_MODEL), ln_b.reshape(1, D_MODEL),
      shift2, scale2, wr_t, br_col)


def _moe_kernel(h_ref, comb_ref, x_ref, gate_ref, g2_ref, b2_ref, w1_ref, w3_ref, w2_ref,
                o_ref, acc_ref, *, tm):
    e = pl.program_id(1)

    @pl.when(e == 0)
    def _():
        acc_ref[...] = jnp.zeros_like(acc_ref)

    h = h_ref[...]
    a = jnp.dot(h, w1_ref[0], preferred_element_type=F32)
    g = jnp.dot(h, w3_ref[0], preferred_element_type=F32)
    lane = lax.broadcasted_iota(jnp.int32, (tm, LANES), 1)
    ce = jnp.sum(jnp.where(lane == e, comb_ref[...], 0.0), axis=-1, keepdims=True)
    act = _silu(a) * g * ce
    acc_ref[...] += jnp.dot(act.astype(BF16), w2_ref[0], preferred_element_type=F32)

    @pl.when(e == N_EXPERTS - 1)
    def _():
        y = ALPHA * x_ref[...] + gate_ref[...] * acc_ref[...]
        o_ref[...] = _ln(y) * g2_ref[...] + b2_ref[...]


def _moe(h2, comb, x1, gate2, ln_g, ln_b, w1, w3, w2, tm):
    rows = x1.shape[0]
    row = lambda w: pl.BlockSpec((tm, w), lambda i, e: (i, 0))
    const = pl.BlockSpec((1, D_MODEL), lambda i, e: (0, 0))
    gate_spec = (pl.BlockSpec((1, D_MODEL), lambda i, e: (0, 0)) if gate2.shape[0] == 1
                 else row(D_MODEL))
    return pl.pallas_call(
        functools.partial(_moe_kernel, tm=tm),
        grid=(rows // tm, N_EXPERTS),
        in_specs=[row(D_MODEL), row(LANES), row(D_MODEL), gate_spec, const, const,
                  pl.BlockSpec((1, D_MODEL, D_EXPERT), lambda i, e: (e, 0, 0)),
                  pl.BlockSpec((1, D_MODEL, D_EXPERT), lambda i, e: (e, 0, 0)),
                  pl.BlockSpec((1, D_EXPERT, D_MODEL), lambda i, e: (e, 0, 0))],
        out_specs=row(D_MODEL),
        out_shape=jax.ShapeDtypeStruct((rows, D_MODEL), F32),
        scratch_shapes=[pltpu.VMEM((tm, D_MODEL), F32)],
        compiler_params=_params("parallel", "arbitrary"),
        name="experts",
    )(h2, comb, x1, gate2, ln_g.reshape(1, D_MODEL), ln_b.reshape(1, D_MODEL), w1, w3, w2)


def _pair_experts():
    lo, hi = [], []
    for g in range(N_GROUPS):
        for a in range(EXPERTS_PER_GROUP):
            for b in range(a + 1, EXPERTS_PER_GROUP):
                lo.append(g * EXPERTS_PER_GROUP + a)
                hi.append(g * EXPERTS_PER_GROUP + b)
    return np.asarray(lo, np.int32), np.asarray(hi, np.int32)


def _slot_plan(cls, tm):
    rows = cls.shape[0]
    n_tiles = rows // tm + N_PAIRS
    onehot = (cls[:, None] == jnp.arange(N_PAIRS, dtype=jnp.int32)[None, :]).astype(jnp.int32)
    rank = jnp.cumsum(onehot, axis=0) - onehot
    tiles_per = (jnp.sum(onehot, axis=0) + tm - 1) // tm
    tile_start = jnp.cumsum(tiles_per) - tiles_per
    pos = jnp.sum(onehot * (tile_start[None, :] * tm + rank), axis=1).astype(jnp.int32)
    src = jnp.zeros((n_tiles * tm,), jnp.int32).at[pos].set(jnp.arange(rows, dtype=jnp.int32))
    tile = jnp.arange(n_tiles, dtype=jnp.int32)
    tile_class = jnp.clip(jnp.sum((tile[:, None] >= tile_start[None, :]).astype(jnp.int32), axis=1) - 1,
                          0, N_PAIRS - 1)
    pair_lo, pair_hi = _pair_experts()
    n_used = jnp.sum(tiles_per).astype(jnp.int32).reshape(1)
    return pos, src, jnp.asarray(pair_lo)[tile_class], jnp.asarray(pair_hi)[tile_class], n_used


def _start_rows(idx_ref, src_hbm, dst_ref, sem, n):
    for r in range(n):
        pltpu.make_async_copy(src_hbm.at[pl.ds(idx_ref[0, 0, r], 1)], dst_ref.at[pl.ds(r, 1)], sem).start(priority=r % 2)


def _wait_rows(src_hbm, dst_ref, sem, n):
    pltpu.make_async_copy(src_hbm.at[pl.ds(0, n)], dst_ref, sem).wait()


def _moe_pair_kernel(lo_ref, hi_ref, nused_ref, src_ref, src_next_ref, hx_hbm,
                     w1a_ref, w3a_ref, w2a_ref, w1b_ref, w3b_ref, w2b_ref, y_ref, xbuf, sems, *, tm):
    i = pl.program_id(0)
    n_used = nused_ref[0]
    slot = i % 2

    @pl.when(jnp.logical_and(i == 0, n_used > 0))
    def _():
        _start_rows(src_ref, hx_hbm, xbuf.at[0], sems.at[0], tm)

    @pl.when(i + 1 < n_used)
    def _():
        _start_rows(src_next_ref, hx_hbm, xbuf.at[1 - slot], sems.at[1 - slot], tm)

    @pl.when(i < n_used)
    def _():
        _wait_rows(hx_hbm, xbuf.at[slot], sems.at[slot], tm)
        x = xbuf[slot]
        h = x[:, :D_MODEL].astype(BF16)
        comb = x[:, D_MODEL:]
        lane = lax.broadcasted_iota(jnp.int32, (tm, LANES), 1)
        acc = jnp.zeros((tm, D_MODEL), F32)
        for e, w1_ref, w3_ref, w2_ref in ((lo_ref[i], w1a_ref, w3a_ref, w2a_ref), (hi_ref[i], w1b_ref, w3b_ref, w2b_ref)):
            a = jnp.dot(h, w1_ref[0], preferred_element_type=F32)
            g = jnp.dot(h, w3_ref[0], preferred_element_type=F32)
            ce = jnp.sum(jnp.where(lane == e, comb, 0.0), axis=-1, keepdims=True)
            act = _silu(a) * g * ce
            acc = acc + jnp.dot(act.astype(BF16), w2_ref[0], preferred_element_type=F32)
        y_ref[...] = acc

    @pl.when(i >= n_used)
    def _():
        y_ref[...] = jnp.zeros_like(y_ref)


def _moe_pairs(hx, src, tile_lo, tile_hi, n_used, w1, w3, w2, tm):
    n_tiles = tile_lo.shape[0]
    lo_spec = lambda k, n: pl.BlockSpec((1, k, n), lambda i, lo, hi, nu: (lo[i], 0, 0))
    hi_spec = lambda k, n: pl.BlockSpec((1, k, n), lambda i, lo, hi, nu: (hi[i], 0, 0))
    src3 = src.reshape(n_tiles, 1, tm)
    return pl.pallas_call(
        functools.partial(_moe_pair_kernel, tm=tm),
        grid_spec=pltpu.PrefetchScalarGridSpec(
            num_scalar_prefetch=3,
            grid=(n_tiles,),
            in_specs=[pl.BlockSpec((1, 1, tm), lambda i, lo, hi, nu: (i, 0, 0), memory_space=pltpu.SMEM),
                      pl.BlockSpec((1, 1, tm), lambda i, lo, hi, nu: (jnp.minimum(i + 1, n_tiles - 1), 0, 0),
                                   memory_space=pltpu.SMEM),
                      pl.BlockSpec(memory_space=pl.ANY),
                      lo_spec(D_MODEL, D_EXPERT), lo_spec(D_MODEL, D_EXPERT), lo_spec(D_EXPERT, D_MODEL),
                      hi_spec(D_MODEL, D_EXPERT), hi_spec(D_MODEL, D_EXPERT), hi_spec(D_EXPERT, D_MODEL)],
            out_specs=pl.BlockSpec((tm, D_MODEL), lambda i, lo, hi, nu: (i, 0)),
            scratch_shapes=[pltpu.VMEM((2, tm, D_MODEL + LANES), F32), pltpu.SemaphoreType.DMA((2,))]),
        out_shape=jax.ShapeDtypeStruct((n_tiles * tm, D_MODEL), F32),
        compiler_params=_params("arbitrary"),
        name="experts_paired",
    )(tile_lo, tile_hi, n_used, src3, src3, hx, w1, w3, w2, w1, w3, w2)


def _combine_kernel(pos_ref, pos_next_ref, y_hbm, x_ref, gate_ref, g2_ref, b2_ref, o_ref, ybuf, sems, *, tm, n_tiles):
    i = pl.program_id(0)
    slot = i % 2

    @pl.when(i == 0)
    def _():
        _start_rows(pos_ref, y_hbm, ybuf.at[0], sems.at[0], tm)

    @pl.when(i + 1 < n_tiles)
    def _():
        _start_rows(pos_next_ref, y_hbm, ybuf.at[1 - slot], sems.at[1 - slot], tm)

    _wait_rows(y_hbm, ybuf.at[slot], sems.at[slot], tm)
    y = ALPHA * x_ref[...] + gate_ref[...] * ybuf[slot]
    o_ref[...] = _ln(y) * g2_ref[...] + b2_ref[...]


def _combine(y_sorted, pos, x1, gate2, ln_g, ln_b, tm):
    rows = x1.shape[0]
    n_tiles = rows // tm
    row = pl.BlockSpec((tm, D_MODEL), lambda i: (i, 0))
    const = pl.BlockSpec((1, D_MODEL), lambda i: (0, 0))
    pos3 = pos.reshape(n_tiles, 1, tm)
    return pl.pallas_call(
        functools.partial(_combine_kernel, tm=tm, n_tiles=n_tiles),
        grid=(n_tiles,),
        in_specs=[pl.BlockSpec((1, 1, tm), lambda i: (i, 0, 0), memory_space=pltpu.SMEM),
                  pl.BlockSpec((1, 1, tm), lambda i: (jnp.minimum(i + 1, n_tiles - 1), 0, 0), memory_space=pltpu.SMEM),
                  pl.BlockSpec(memory_space=pl.ANY),
                  row, _row_spec(gate2.shape[0], tm, D_MODEL, rows), const, const],
        out_specs=row,
        out_shape=jax.ShapeDtypeStruct((rows, D_MODEL), F32),
        scratch_shapes=[pltpu.VMEM((2, tm, D_MODEL), F32), pltpu.SemaphoreType.DMA((2,))],
        compiler_params=_params("arbitrary"),
        name="combine",
    )(pos3, pos3, y_sorted, x1, gate2, ln_g.reshape(1, D_MODEL), ln_b.reshape(1, D_MODEL))


def _trunk_layer(x, mods, bsz, seq, pos0, s0, k_past, v_past, p, tiles):
    shift1, scale1, gate1, shift2, scale2, gate2 = mods
    uv, ret, q_c, k_c, v_c, k_bf, v_bf = _inproj(x, shift1, scale1, p["w_in"], p["layer"], tiles["tm_in"])
    a_out, v_rows = _gmlp(uv, p["w_sp"], p["b_sp"], p["ln_v_g"], p["ln_v_b"], min(seq, GMLP_CHUNK))
    b_out, s_new = _retention(ret, s0, pos0, bsz, seq, p["gn_g"], p["gn_b"])
    if k_past is None:
        c_out = _stick_breaking(q_c, k_bf, v_bf, bsz, seq, tiles["sb"])
    else:
        c_out = _stick_breaking_step(q_c, k_bf, v_bf, k_past, v_past, p["layer"], bsz, seq, tiles["sb"])
    grouped = "tm_group" in tiles
    post = _post(a_out, b_out, c_out, x, p["w_out"], p["layer"], gate1, p["ln1_g"], p["ln1_b"],
                 shift2, scale2, p["wr"], p["br"], tiles["tm_post"], grouped)
    if grouped:
        x1, comb, hx = post
        tm = tiles["tm_group"]
        pos, src, tile_lo, tile_hi, n_used = _slot_plan(comb[:, N_EXPERTS].astype(jnp.int32), tm)
        y_sorted = _moe_pairs(hx, src, tile_lo, tile_hi, n_used, p["w1"], p["w3"], p["w2"], tm)
        y = _combine(y_sorted, pos, x1, gate2, p["ln2_g"], p["ln2_b"], tm)
    else:
        x1, h2, comb = post
        y = _moe(h2, comb, x1, gate2, p["ln2_g"], p["ln2_b"], p["w1"], p["w3"], p["w2"], tiles["tm_moe"])
    return y, v_rows, s_new, k_c, v_c


def kernel(x_prompt, x_sample, cache_sb_k, cache_sb_v, state_ret, c_prompt, c_sample, w_ada, b_ada, w_in, w_out, ln_v_g, ln_v_b, w_spatial, b_spatial, gn_g, gn_b, ln1_g, ln1_b, ln2_g, ln2_b, w_router, b_router, w1, w3, w2):
    bp, tp, _ = x_prompt.shape
    bs, ts, _ = x_sample.shape
    past_len = cache_sb_k.shape[2]
    assert bp == 1

    n_c = bp + bs
    c_rows = -(-n_c // 8) * 8
    c_all = jnp.concatenate([c_prompt, c_sample, jnp.zeros((c_rows - n_c, D_MODEL), F32)], axis=0)
    mods = _adaln(c_all, w_ada, b_ada)

    wr_t = w_router.T
    cache_kt = jnp.transpose(cache_sb_k, (0, 1, 3, 4, 2))
    cache_vt = jnp.transpose(cache_sb_v, (0, 1, 3, 4, 2))
    br_col = b_router.reshape(N_EXPERTS, 1)

    tiles_p = dict(tm_in=256, sb=256, tm_post=512, tm_group=512)
    tiles_s = dict(tm_in=bs * ts, sb=256, tm_post=bs * ts, tm_moe=bs * ts)

    y_p = x_prompt.reshape(bp * tp, D_MODEL)
    y_s = x_sample.reshape(bs * ts, D_MODEL)
    zero_state = jnp.zeros((bp, H_B, HEAD_DIM, HEAD_DIM), F32)
    outs = [[] for _ in range(7)]
    for l in range(DEPTH):
        p = dict(w_in=w_in, w_out=w_out,
                 w_sp=w_spatial[l], b_sp=b_spatial[l], ln_v_g=ln_v_g[l], ln_v_b=ln_v_b[l],
                 gn_g=gn_g[l], gn_b=gn_b[l], ln1_g=ln1_g[l], ln1_b=ln1_b[l],
                 ln2_g=ln2_g[l], ln2_b=ln2_b[l], wr=wr_t, br=br_col,
                 w1=w1[l].astype(BF16), w3=w3[l].astype(BF16), w2=w2[l].astype(BF16), layer=l)
        m = mods[l]
        mods_p = [m[0:1, i * D_MODEL:(i + 1) * D_MODEL] for i in range(6)]
        mods_s = [jnp.repeat(m[bp:bp + bs, i * D_MODEL:(i + 1) * D_MODEL], ts, axis=0) for i in range(6)]
        y_p, _, s_p, k_p, v_p = _trunk_layer(y_p, mods_p, bp, tp, 0, zero_state, None, None, p, tiles_p)
        y_s, g_s, s_s, k_s, v_s = _trunk_layer(
            y_s, mods_s, bs, ts, past_len, state_ret[l],
            cache_kt, cache_vt, p, tiles_s)
        outs[0].append(s_p)
        outs[1].append(k_p.reshape(bp, tp, H_C, HEAD_DIM))
        outs[2].append(v_p.reshape(bp, tp, H_C, HEAD_DIM))
        outs[3].append(s_s)
        outs[4].append(k_s.reshape(bs, ts, H_C, HEAD_DIM))
        outs[5].append(v_s.reshape(bs, ts, H_C, HEAD_DIM))
        outs[6].append(g_s.reshape(bs, ts, W_A))
    return (y_p.reshape(bp, tp, D_MODEL), y_s.reshape(bs, ts, D_MODEL)) + tuple(jnp.stack(o) for o in outs)
```

```python
import functools

import numpy as np
import jax
import jax.numpy as jnp
from jax import lax
from jax.experimental import pallas as pl
from jax.experimental.pallas import tpu as pltpu

F32 = jnp.float32
BF16 = jnp.bfloat16
HIGHEST = lax.Precision.HIGHEST

D_MODEL = 1024
DEPTH = 2
HEAD_DIM = 64
W_A = D_MODEL // 4
W_B = 3 * D_MODEL // 8
W_C = D_MODEL - W_A - W_B
H_A = W_A // HEAD_DIM
H_B = W_B // HEAD_DIM
H_C = W_C // HEAD_DIM
GMLP_CHUNK = 128
ROPE_BASE = 10000.0
N_EXPERTS = 16
N_GROUPS = 4
EXPERTS_PER_GROUP = N_EXPERTS // N_GROUPS
D_EXPERT = D_MODEL // 2
PAIRS_PER_GROUP = EXPERTS_PER_GROUP * (EXPERTS_PER_GROUP - 1) // 2
N_PAIRS = N_GROUPS * PAIRS_PER_GROUP
ALPHA = (2 * DEPTH) ** 0.25
LN_EPS = 1e-5
D_IN = 2 * W_A + 4 * W_B + 3 * W_C
LANES = 128
VMEM_LIMIT = 48 * 1024 * 1024

NT_DIMS = (((1,), (1,)), ((), ()))
TN_DIMS = (((0,), (0,)), ((), ()))
INV_LN2 = 1.4426950408889634
SB_QSCALE = HEAD_DIM ** -0.5 * INV_LN2
SB_DEAD = -152.0
GMLP_SUB = 8
POST_SUB = 4
RET_BLOCK = 512


def _ln(x):
    mu = jnp.mean(x, axis=-1, keepdims=True)
    xc = x - mu
    var = jnp.mean(xc * xc, axis=-1, keepdims=True)
    return xc * lax.rsqrt(var + LN_EPS)


def _silu(x):
    return x * jax.nn.sigmoid(x)


def _params(*sem):
    return pltpu.CompilerParams(dimension_semantics=sem, vmem_limit_bytes=VMEM_LIMIT)


def _row_spec(rows, tm, width, total_rows):
    if rows == 1:
        return pl.BlockSpec((1, width), lambda i: (0, 0))
    assert rows == total_rows
    return pl.BlockSpec((tm, width), lambda i: (i, 0))


def _adaln_kernel(c_ref, w_ref, b_ref, o_ref):
    sc = _silu(c_ref[...])
    o_ref[0] = jnp.dot(sc, w_ref[0], preferred_element_type=F32, precision=HIGHEST) + b_ref[0]


def _adaln(c_all, w_ada, b_ada):
    rows = c_all.shape[0]
    tn = 1536
    return pl.pallas_call(
        _adaln_kernel,
        grid=(DEPTH, 6 * D_MODEL // tn),
        in_specs=[pl.BlockSpec((rows, D_MODEL), lambda l, j: (0, 0)),
                  pl.BlockSpec((1, D_MODEL, tn), lambda l, j: (l, 0, j)),
                  pl.BlockSpec((1, 1, tn), lambda l, j: (l, 0, j))],
        out_specs=pl.BlockSpec((1, rows, tn), lambda l, j: (l, 0, j)),
        out_shape=jax.ShapeDtypeStruct((DEPTH, rows, 6 * D_MODEL), F32),
        compiler_params=_params("parallel", "parallel"),
        name="adaln",
    )(c_all, w_ada, b_ada.reshape(DEPTH, 1, 6 * D_MODEL))


def _inproj_kernel(x_ref, shift_ref, scale_ref, w_ref, uv_ref, ret_ref, q_ref, k_ref, v_ref, kb_ref, vb_ref, wb_ref):
    @pl.when(pl.program_id(0) == 0)
    def _():
        wb_ref[...] = w_ref[0].astype(BF16)

    h = _ln(x_ref[...]) * (1.0 + scale_ref[...]) + shift_ref[...]
    r = jnp.dot(h.astype(BF16), wb_ref[...], preferred_element_type=F32)
    c0 = 2 * W_A
    c1 = c0 + 4 * W_B
    uv_ref[...] = r[:, :c0]
    ret_ref[...] = r[:, c0:c1]
    q_ref[...] = r[:, c1:c1 + W_C]
    k = r[:, c1 + W_C:c1 + 2 * W_C]
    v = r[:, c1 + 2 * W_C:]
    k_ref[...] = k
    v_ref[...] = v
    kb_ref[...] = k.astype(BF16)
    vb_ref[...] = v.astype(BF16)


def _inproj(x, shift, scale, w_in, layer, tm):
    rows = x.shape[0]
    widths = (2 * W_A, 4 * W_B, W_C, W_C, W_C, W_C, W_C)
    dtypes = (F32,) * 5 + (BF16,) * 2
    return pl.pallas_call(
        _inproj_kernel,
        grid=(rows // tm,),
        in_specs=[pl.BlockSpec((tm, D_MODEL), lambda i: (i, 0)),
                  _row_spec(shift.shape[0], tm, D_MODEL, rows),
                  _row_spec(scale.shape[0], tm, D_MODEL, rows),
                  pl.BlockSpec((1, D_MODEL, D_IN), lambda i: (layer, 0, 0), pipeline_mode=pl.Buffered(1))],
        out_specs=[pl.BlockSpec((tm, w), lambda i: (i, 0)) for w in widths],
        out_shape=[jax.ShapeDtypeStruct((rows, w), dt) for w, dt in zip(widths, dtypes)],
        scratch_shapes=[pltpu.VMEM((D_MODEL, D_IN), BF16)],
        compiler_params=_params("arbitrary"),
        name="inproj",
    )(x, shift, scale, w_in)


def _gmlp_kernel(uv_ref, wsp_ref, bias_ref, g_ref, b_ref, a_ref, vn_ref, *, chunk, n_sub):
    uv = uv_ref[...]
    u = jax.nn.gelu(uv[:, :W_A])
    v = _ln(jax.nn.gelu(uv[:, W_A:])) * g_ref[...] + b_ref[...]
    vn_ref[...] = v
    row = lax.broadcasted_iota(jnp.int32, (chunk, chunk), 0)
    col = lax.broadcasted_iota(jnp.int32, (chunk, chunk), 1)
    lane_head = lax.broadcasted_iota(jnp.int32, (chunk, W_A), 1) // HEAD_DIM
    w = [jnp.where(col <= row, wsp_ref[h], 0.0).astype(BF16) for h in range(H_A)]
    for c in range(n_sub):
        rows = slice(c * chunk, (c + 1) * chunk)
        vc = v[rows]
        mixed = bias_ref[...]
        for h in range(H_A):
            vh = jnp.where(lane_head == h, vc, 0.0).astype(BF16)
            mixed = mixed + jnp.dot(w[h], vh, preferred_element_type=F32)
        a_ref[rows, :] = u[rows] * mixed


def _gmlp(uv, w_sp, b_sp, ln_g, ln_b, chunk):
    rows = uv.shape[0]
    n_sub = min(GMLP_SUB, rows // chunk)
    tm = n_sub * chunk
    wsp = w_sp[:, :chunk, :chunk]
    bias = jnp.repeat(b_sp[:, :chunk].T, HEAD_DIM, axis=1)
    return pl.pallas_call(
        functools.partial(_gmlp_kernel, chunk=chunk, n_sub=n_sub),
        grid=(rows // tm,),
        in_specs=[pl.BlockSpec((tm, 2 * W_A), lambda i: (i, 0)),
                  pl.BlockSpec((H_A, chunk, chunk), lambda i: (0, 0, 0)),
                  pl.BlockSpec((chunk, W_A), lambda i: (0, 0)),
                  pl.BlockSpec((1, W_A), lambda i: (0, 0)),
                  pl.BlockSpec((1, W_A), lambda i: (0, 0))],
        out_specs=[pl.BlockSpec((tm, W_A), lambda i: (i, 0)),
                   pl.BlockSpec((tm, W_A), lambda i: (i, 0))],
        out_shape=[jax.ShapeDtypeStruct((rows, W_A), F32),
                   jax.ShapeDtypeStruct((rows, W_A), F32)],
        compiler_params=_params("parallel"),
        name="gmlp",
    )(uv, wsp, bias, ln_g.reshape(1, W_A), ln_b.reshape(1, W_A))


def _rope(x, cos, sin):
    lane = lax.broadcasted_iota(jnp.int32, (x.shape[0], LANES), 1)
    first_half = (lane & (HEAD_DIM // 2)) == 0
    parts = []
    for c in range(x.shape[1] // LANES):
        xc = x[:, c * LANES:(c + 1) * LANES]
        rot = jnp.where(first_half,
                        pltpu.roll(xc, LANES - HEAD_DIM // 2, 1),
                        pltpu.roll(xc, HEAD_DIM // 2, 1))
        parts.append(xc * cos + rot * sin)
    return jnp.concatenate(parts, axis=1)


def _ret_kernel(r_ref, cos_ref, sin_ref, qdec_ref, kdec_ref, dec_ref, blk_ref, s0_ref,
                gng_ref, gnb_ref, o_ref, sout_ref, s_scr, o_scr, *, n_blocks):
    n = pl.program_id(1)

    @pl.when(n == 0)
    def _():
        s_scr[...] = s0_ref[0]

    r = r_ref[...]
    cos = cos_ref[...]
    sin = sin_ref[...]
    qr = _rope(r[:, :W_B], cos, sin)
    kr = _rope(r[:, W_B:2 * W_B], cos, sin) * (HEAD_DIM ** -0.5)
    vb = r[:, 2 * W_B:3 * W_B].astype(BF16)
    gate = r[:, 3 * W_B:]
    qb = qr.astype(BF16)
    kb = kr.astype(BF16)
    qdb = (qr * qdec_ref[...]).astype(BF16)
    kdb = (kr * kdec_ref[...]).astype(BF16)
    for h in range(H_B):
        sl = slice(h * HEAD_DIM, (h + 1) * HEAD_DIM)
        scores = lax.dot_general(qb[:, sl], kb[:, sl], NT_DIMS, preferred_element_type=F32) * dec_ref[h]
        s_h = s_scr[h]
        o_h = (jnp.dot(scores.astype(BF16), vb[:, sl], preferred_element_type=F32)
               + jnp.dot(qdb[:, sl], s_h.astype(BF16), preferred_element_type=F32))
        s_scr[h] = s_h * blk_ref[h] + lax.dot_general(kdb[:, sl], vb[:, sl], TN_DIMS,
                                                      preferred_element_type=F32)
        o_scr[:, sl] = _ln(o_h)
    o_ref[...] = (o_scr[...] * gng_ref[...] + gnb_ref[...]) * _silu(gate)

    @pl.when(n == n_blocks - 1)
    def _():
        sout_ref[0] = s_scr[...]


def _retention(ret, s0, pos0, bsz, seq, gn_g, gn_b):
    L = min(seq, RET_BLOCK)
    n_blocks = seq // L
    half = HEAD_DIM // 2
    inv = ROPE_BASE ** (-jnp.arange(half, dtype=F32) / half)
    ang = (pos0 + jnp.arange(seq)).astype(F32)[:, None] * inv[None, :]
    cos, sin = jnp.cos(ang), jnp.sin(ang)
    cos_t = jnp.tile(jnp.concatenate([cos, cos], axis=1), (1, LANES // HEAD_DIM))
    sin_t = jnp.tile(jnp.concatenate([-sin, sin], axis=1), (1, LANES // HEAD_DIM))
    log_g = jnp.log1p(-jnp.exp2(-5.0 - jnp.arange(H_B, dtype=F32)))
    idx = jnp.arange(L, dtype=F32)
    diff = idx[:, None] - idx[None, :]
    decay = jnp.where(diff >= 0, jnp.exp(diff[None] * log_g[:, None, None]), 0.0)
    q_decay = jnp.exp((idx[None, :] + 1.0) * log_g[:, None])
    k_decay = jnp.exp((L - 1.0 - idx[None, :]) * log_g[:, None])
    blk_decay = jnp.exp(L * log_g)
    qdec = jnp.repeat(q_decay.T, HEAD_DIM, axis=1)
    kdec = jnp.repeat(k_decay.T, HEAD_DIM, axis=1)
    blk = jnp.broadcast_to(blk_decay[:, None, None], (H_B, HEAD_DIM, HEAD_DIM))
    const2 = lambda b, n: (0, 0)
    const3 = lambda b, n: (0, 0, 0)
    return pl.pallas_call(
        functools.partial(_ret_kernel, n_blocks=n_blocks),
        grid=(bsz, n_blocks),
        in_specs=[pl.BlockSpec((L, 4 * W_B), lambda b, n: (b * n_blocks + n, 0)),
                  pl.BlockSpec((L, LANES), lambda b, n: (n, 0)),
                  pl.BlockSpec((L, LANES), lambda b, n: (n, 0)),
                  pl.BlockSpec((L, W_B), const2),
                  pl.BlockSpec((L, W_B), const2),
                  pl.BlockSpec((H_B, L, L), const3),
                  pl.BlockSpec((H_B, HEAD_DIM, HEAD_DIM), const3),
                  pl.BlockSpec((1, H_B, HEAD_DIM, HEAD_DIM), lambda b, n: (b, 0, 0, 0)),
                  pl.BlockSpec((1, W_B), const2),
                  pl.BlockSpec((1, W_B), const2)],
        out_specs=[pl.BlockSpec((L, W_B), lambda b, n: (b * n_blocks + n, 0)),
                   pl.BlockSpec((1, H_B, HEAD_DIM, HEAD_DIM), lambda b, n: (b, 0, 0, 0))],
        out_shape=[jax.ShapeDtypeStruct((bsz * seq, W_B), F32),
                   jax.ShapeDtypeStruct((bsz, H_B, HEAD_DIM, HEAD_DIM), F32)],
        scratch_shapes=[pltpu.VMEM((H_B, HEAD_DIM, HEAD_DIM), F32),
                        pltpu.VMEM((L, W_B), F32)],
        compiler_params=_params("parallel", "arbitrary"),
        name="retention",
    )(ret, cos_t, sin_t, qdec, kdec, decay, blk, s0, gn_g.reshape(1, W_B), gn_b.reshape(1, W_B))


def _sb_block(qm_ref, k_ref, v_ref, u2_ref, acc_ref, carry_ref, causal):
    tk = k_ref.shape[0]
    half = lax.broadcasted_iota(jnp.int32, (tk, LANES), 1) // HEAD_DIM
    u2 = u2_ref[...]
    kp = [k_ref[:, p * LANES:(p + 1) * LANES] for p in range(H_C // 2)]

    def scores(h):
        z = lax.dot_general(qm_ref[h], kp[h // 2], NT_DIMS, preferred_element_type=F32)
        neg_abs = pltpu.bitcast(pltpu.bitcast(z, jnp.uint32) | jnp.uint32(0x80000000), F32)
        ls_pos = jnp.minimum(z, 0.0) - jnp.log(1.0 + jnp.exp2(neg_abs)) * INV_LN2
        log_stay = ls_pos - z
        if causal is not None:
            log_stay = jnp.where(causal, log_stay, 0.0)
        hi = log_stay.astype(BF16)
        lo = (log_stay - hi.astype(F32)).astype(BF16)
        return ls_pos, log_stay[:, :1], jnp.concatenate([hi, lo], axis=1)

    def cumsum(hi_lo):
        return jnp.dot(hi_lo, u2, preferred_element_type=F32)

    def weigh(h, ls_pos, first_col, excl):
        carry = carry_ref[h]
        att = jnp.exp2(ls_pos + excl + carry)
        if causal is not None:
            att = jnp.where(causal, att, 0.0)
        vp = v_ref[:, (h // 2) * LANES:(h // 2 + 1) * LANES]
        vp = jnp.where(half == h % 2, vp, jnp.zeros_like(vp))
        carry_ref[h] = carry + (excl[:, :1] + first_col)
        return jnp.dot(att.astype(BF16), vp, preferred_element_type=F32)

    stage_a, stage_b, outs = {}, {}, {}
    for step in range(H_C + 2):
        if step - 2 >= 0:
            h = step - 2
            outs[h] = weigh(h, stage_a[h][0], stage_a[h][1], stage_b.pop(h))
            del stage_a[h]
            if h % 2 == 1:
                p = h // 2
                acc_ref[:, p * LANES:(p + 1) * LANES] += outs.pop(h - 1) + outs.pop(h)
        if 0 <= step - 1 < H_C:
            stage_b[step - 1] = cumsum(stage_a[step - 1][2])
        if step < H_C:
            stage_a[step] = scores(step)


def _sb_kernel(q_ref, k_ref, v_ref, u2_ref, o_ref, qm_ref, acc_ref, carry_ref, *, t):
    i = pl.program_id(1)
    acc_ref[...] = jnp.zeros_like(acc_ref)
    carry_ref[...] = jnp.zeros_like(carry_ref)
    half = lax.broadcasted_iota(jnp.int32, (t, LANES), 1) // HEAD_DIM
    for h in range(H_C):
        qp = q_ref[:, (h // 2) * LANES:(h // 2 + 1) * LANES] * SB_QSCALE
        qm_ref[h] = jnp.where(half == h % 2, qp, 0.0).astype(BF16)

    def block(j, causal):
        k0 = pl.multiple_of((i - j) * t, t)
        _sb_block(qm_ref, k_ref.at[pl.ds(k0, t), :], v_ref.at[pl.ds(k0, t), :], u2_ref, acc_ref, carry_ref, causal)

    block(0, lax.broadcasted_iota(jnp.int32, (t, t), 1) < lax.broadcasted_iota(jnp.int32, (t, t), 0))

    def key_block(state):
        j, _ = state
        block(j, None)
        dead = jnp.max(carry_ref[...]) < SB_DEAD
        return j + 1, dead.astype(jnp.int32)

    lax.while_loop(lambda state: jnp.logical_and(state[0] <= i, state[1] == 0),
                   key_block, (jnp.int32(1), jnp.int32(0)))
    o_ref[...] = acc_ref[...]


def _stick_breaking(q, k, v, bsz, seq, t):
    nq = seq // t
    tri = np.tril(np.ones((t, t), np.float32), -1)
    u2 = jnp.asarray(np.concatenate([tri, tri], axis=0), dtype=BF16)
    resident = dict(pipeline_mode=pl.Buffered(1))
    return pl.pallas_call(
        functools.partial(_sb_kernel, t=t),
        grid=(bsz, nq),
        in_specs=[pl.BlockSpec((t, W_C), lambda b, i: (b * nq + i, 0)),
                  pl.BlockSpec((seq, W_C), lambda b, i: (b, 0), **resident),
                  pl.BlockSpec((seq, W_C), lambda b, i: (b, 0), **resident),
                  pl.BlockSpec((2 * t, t), lambda b, i: (0, 0), **resident)],
        out_specs=pl.BlockSpec((t, W_C), lambda b, i: (b * nq + i, 0)),
        out_shape=jax.ShapeDtypeStruct((bsz * seq, W_C), F32),
        scratch_shapes=[pltpu.VMEM((H_C, t, LANES), BF16),
                        pltpu.VMEM((t, W_C), F32),
                        pltpu.VMEM((H_C, t, 1), F32)],
        compiler_params=_params("parallel", "arbitrary"),
        name="stick_breaking",
    )(q, k, v, u2)


def _sb_scores(z, causal=None):
    neg_abs = pltpu.bitcast(pltpu.bitcast(z, jnp.uint32) | jnp.uint32(0x80000000), F32)
    ls_pos = jnp.minimum(z, 0.0) - jnp.log(1.0 + jnp.exp2(neg_abs)) * INV_LN2
    log_stay = ls_pos - z
    if causal is not None:
        log_stay = jnp.where(causal, log_stay, 0.0)
    hi = log_stay.astype(BF16)
    lo = (log_stay - hi.astype(F32)).astype(BF16)
    return ls_pos, log_stay, jnp.concatenate([hi, lo], axis=1)


def _sb_step_kernel(q_ref, kt_ref, vt_ref, kn_ref, vn_ref, u2_ref, un_ref, o_ref, acc_ref, carry_ref, *, tk):
    seq = q_ref.shape[0]
    past = kt_ref.shape[-1]
    qs = (q_ref[...] * SB_QSCALE).astype(BF16)
    row = lax.broadcasted_iota(jnp.int32, (seq, seq), 0)
    col = lax.broadcasted_iota(jnp.int32, (seq, seq), 1)
    causal = col < row
    un = un_ref[...]
    for h in range(H_C):
        sl = slice(h * HEAD_DIM, (h + 1) * HEAD_DIM)
        z = lax.dot_general(qs[:, sl], kn_ref[:, sl], NT_DIMS, preferred_element_type=F32)
        ls_pos, log_stay, hi_lo = _sb_scores(z, causal)
        excl = jnp.dot(hi_lo, un, preferred_element_type=F32)
        att = jnp.where(causal, jnp.exp2(ls_pos + excl), 0.0)
        acc_ref[h] = jnp.dot(att.astype(BF16), vn_ref[:, sl], preferred_element_type=F32)
        carry_ref[h] = excl[:, :1] + log_stay[:, :1]

    u2 = u2_ref[...]

    def key_block(state):
        j, _ = state
        k0 = pl.multiple_of(past - (j + 1) * tk, tk)
        for h in range(H_C):
            kt = kt_ref[0, 0, h, :, pl.ds(k0, tk)].astype(BF16)
            vt = vt_ref[0, 0, h, :, pl.ds(k0, tk)].astype(BF16)
            z = jnp.dot(qs[:, h * HEAD_DIM:(h + 1) * HEAD_DIM], kt, preferred_element_type=F32)
            ls_pos, log_stay, hi_lo = _sb_scores(z)
            excl = jnp.dot(hi_lo, u2, preferred_element_type=F32)
            carry = carry_ref[h]
            att = jnp.exp2(ls_pos + excl + carry)
            acc_ref[h] += lax.dot_general(att.astype(BF16), vt, NT_DIMS, preferred_element_type=F32)
            carry_ref[h] = carry + (excl[:, :1] + log_stay[:, :1])
        dead = jnp.max(carry_ref[...]) < SB_DEAD
        return j + 1, dead.astype(jnp.int32)

    lax.while_loop(lambda state: jnp.logical_and(state[0] < past // tk, state[1] == 0),
                   key_block, (jnp.int32(0), jnp.int32(0)))
    for h in range(H_C):
        o_ref[:, h * HEAD_DIM:(h + 1) * HEAD_DIM] = acc_ref[h]


def _stick_breaking_step(q, k_new, v_new, cache_kt, cache_vt, layer, bsz, seq, tk):
    past = cache_kt.shape[-1]
    assert past % tk == 0
    tri = lambda n: np.tril(np.ones((n, n), np.float32), -1)
    stacked = lambda n: jnp.asarray(np.concatenate([tri(n), tri(n)], axis=0), dtype=BF16)
    rows = pl.BlockSpec((seq, W_C), lambda b: (b, 0))
    cache = pl.BlockSpec((1, 1, H_C, HEAD_DIM, past), lambda b: (layer, b, 0, 0, 0))
    return pl.pallas_call(
        functools.partial(_sb_step_kernel, tk=tk),
        grid=(bsz,),
        in_specs=[rows, cache, cache, rows, rows,
                  pl.BlockSpec((2 * tk, tk), lambda b: (0, 0)), pl.BlockSpec((2 * seq, seq), lambda b: (0, 0))],
        out_specs=rows,
        out_shape=jax.ShapeDtypeStruct((bsz * seq, W_C), F32),
        scratch_shapes=[pltpu.VMEM((H_C, seq, HEAD_DIM), F32), pltpu.VMEM((H_C, seq, 1), F32)],
        compiler_params=_params("parallel"),
        name="stick_breaking_step",
    )(q, cache_kt, cache_vt, k_new, v_new, stacked(tk), stacked(seq))


def _route(sel, s):
    g_scores = []
    for g in range(N_GROUPS):
        a, b, c, d = sel[EXPERTS_PER_GROUP * g:EXPERTS_PER_GROUP * (g + 1)]
        ab_hi, ab_lo = jnp.maximum(a, b), jnp.minimum(a, b)
        cd_hi, cd_lo = jnp.maximum(c, d), jnp.minimum(c, d)
        top1 = jnp.maximum(ab_hi, cd_hi)
        top2 = jnp.maximum(jnp.minimum(ab_hi, cd_hi), jnp.maximum(ab_lo, cd_lo))
        g_scores.append(top1 + top2)
    best = g_scores[0]
    gi = jnp.zeros(best.shape, jnp.int32)
    for g in range(1, N_GROUPS):
        upd = g_scores[g] > best
        gi = jnp.where(upd, g, gi)
        best = jnp.where(upd, g_scores[g], best)

    def pick_group(rows, l):
        out = rows[(N_GROUPS - 1) * EXPERTS_PER_GROUP + l]
        for g in range(N_GROUPS - 2, -1, -1):
            out = jnp.where(gi == g, rows[g * EXPERTS_PER_GROUP + l], out)
        return out

    ig = [pick_group(sel, l) for l in range(EXPERTS_PER_GROUP)]
    sg = [pick_group(s, l) for l in range(EXPERTS_PER_GROUP)]
    b1 = ig[0]
    i1 = jnp.zeros(best.shape, jnp.int32)
    for l in range(1, EXPERTS_PER_GROUP):
        upd = ig[l] > b1
        i1 = jnp.where(upd, l, i1)
        b1 = jnp.where(upd, ig[l], b1)
    b2 = jnp.full(best.shape, -jnp.inf, F32)
    i2 = jnp.zeros(best.shape, jnp.int32)
    for l in range(EXPERTS_PER_GROUP):
        upd = jnp.logical_and(i1 != l, ig[l] > b2)
        i2 = jnp.where(upd, l, i2)
        b2 = jnp.where(upd, ig[l], b2)

    def pick_local(idx):
        out = sg[EXPERTS_PER_GROUP - 1]
        for l in range(EXPERTS_PER_GROUP - 2, -1, -1):
            out = jnp.where(idx == l, sg[l], out)
        return out

    w1 = pick_local(i1)
    w2 = pick_local(i2)
    tot = w1 + w2
    return gi * EXPERTS_PER_GROUP + i1, gi * EXPERTS_PER_GROUP + i2, w1 / tot, w2 / tot


def _pair_class(e1, e2):
    lo = jnp.minimum(e1, e2)
    hi = jnp.maximum(e1, e2)
    g = lo // EXPERTS_PER_GROUP
    llo = lo - g * EXPERTS_PER_GROUP
    lhi = hi - g * EXPERTS_PER_GROUP
    return g * PAIRS_PER_GROUP + ((llo * (2 * EXPERTS_PER_GROUP - 1 - llo)) >> 1) + (lhi - llo - 1)


def _post_kernel(a_ref, b_ref, c_ref, x_ref, wo_ref, gate_ref, g1_ref, b1_ref, sh2_ref, sc2_ref,
                 wrt_ref, br_ref, x1_ref, *rest, tm, with_hx):
    if with_hx:
        h2_ref = None
        comb_ref, hx_ref, wob_ref = rest
    else:
        hx_ref = None
        h2_ref, comb_ref, wob_ref = rest
    @pl.when(pl.program_id(0) == 0)
    def _():
        wob_ref[...] = wo_ref[0].astype(BF16)

    sub = tm // POST_SUB if tm % (POST_SUB * LANES) == 0 else tm
    tiles = [slice(r0, r0 + sub) for r0 in range(0, tm, sub)]
    per_row = lambda ref, rows: ref[rows, :] if ref.shape[0] == tm else ref[...]
    proj = [jnp.dot(a_ref[rows, :].astype(BF16), wob_ref[:W_A], preferred_element_type=F32)
            + jnp.dot(b_ref[rows, :].astype(BF16), wob_ref[W_A:W_A + W_B], preferred_element_type=F32)
            + jnp.dot(c_ref[rows, :].astype(BF16), wob_ref[W_A + W_B:], preferred_element_type=F32)
            for rows in tiles]
    x1 = [_ln(ALPHA * x_ref[rows, :] + per_row(gate_ref, rows) * pr) * g1_ref[...] + b1_ref[...]
          for rows, pr in zip(tiles, proj)]
    for rows, v in zip(tiles, x1):
        x1_ref[rows, :] = v
    h2 = [_ln(v) * (1.0 + per_row(sc2_ref, rows)) + per_row(sh2_ref, rows) for rows, v in zip(tiles, x1)]
    logits = [lax.dot_general(wrt_ref[...], v, NT_DIMS, preferred_element_type=F32, precision=HIGHEST) for v in h2]
    for rows, v, logits_t in zip(tiles, h2, logits):
        if h2_ref is not None:
            h2_ref[rows, :] = v.astype(BF16)
        s_t = jax.nn.sigmoid(logits_t)
        sel_t = s_t + br_ref[...]
        s = [s_t[e:e + 1, :] for e in range(N_EXPERTS)]
        sel = [sel_t[e:e + 1, :] for e in range(N_EXPERTS)]
        e1, e2, w1, w2 = _route(sel, s)
        expert = lax.broadcasted_iota(jnp.int32, (LANES, sub), 0)
        comb_t = jnp.where(expert == e1, w1, jnp.where(expert == e2, w2, 0.0))
        comb_t = jnp.where(expert == N_EXPERTS, _pair_class(e1, e2).astype(F32), comb_t)
        comb_ref[rows, :] = comb_t.T
        if hx_ref is not None:
            hx_ref[rows, :D_MODEL] = v
            hx_ref[rows, D_MODEL:] = comb_t.T


def _post(a, b, c, x, w_out, layer, gate1, ln_g, ln_b, shift2, scale2, wr_t, br_col, tm, with_hx):
    rows = x.shape[0]
    if with_hx:
        tail_spec = [pl.BlockSpec((tm, LANES), lambda i: (i, 0)), pl.BlockSpec((tm, D_MODEL + LANES), lambda i: (i, 0))]
        tail_shape = [jax.ShapeDtypeStruct((rows, LANES), F32), jax.ShapeDtypeStruct((rows, D_MODEL + LANES), F32)]
    else:
        tail_spec = [pl.BlockSpec((tm, D_MODEL), lambda i: (i, 0)), pl.BlockSpec((tm, LANES), lambda i: (i, 0))]
        tail_shape = [jax.ShapeDtypeStruct((rows, D_MODEL), BF16), jax.ShapeDtypeStruct((rows, LANES), F32)]
    row = lambda w: pl.BlockSpec((tm, w), lambda i: (i, 0))
    const = lambda r, w: pl.BlockSpec((r, w), lambda i: (0, 0))
    return pl.pallas_call(
        functools.partial(_post_kernel, tm=tm, with_hx=with_hx),
        grid=(rows // tm,),
        in_specs=[row(W_A), row(W_B), row(W_C), row(D_MODEL),
                  pl.BlockSpec((1, D_MODEL, D_MODEL), lambda i: (layer, 0, 0), pipeline_mode=pl.Buffered(1)),
                  _row_spec(gate1.shape[0], tm, D_MODEL, rows),
                  const(1, D_MODEL), const(1, D_MODEL),
                  _row_spec(shift2.shape[0], tm, D_MODEL, rows),
                  _row_spec(scale2.shape[0], tm, D_MODEL, rows),
                  const(N_EXPERTS, D_MODEL), const(N_EXPERTS, 1)],
        out_specs=[row(D_MODEL)] + tail_spec,
        out_shape=[jax.ShapeDtypeStruct((rows, D_MODEL), F32)] + tail_shape,
        scratch_shapes=[pltpu.VMEM((D_MODEL, D_MODEL), BF16)],
        compiler_params=_params("arbitrary"),
        name="post_mix",
    )(a, b, c, x, w_out, gate1, ln_g.reshape(1, D_MODEL), ln_b.reshape(1, D_MODEL),
      shift2, scale2, wr_t, br_col)


def _moe_kernel(h_ref, comb_ref, x_ref, gate_ref, g2_ref, b2_ref, w1_ref, w3_ref, w2_ref,
                o_ref, acc_ref, *, tm):
    e = pl.program_id(1)

    @pl.when(e == 0)
    def _():
        acc_ref[...] = jnp.zeros_like(acc_ref)

    h = h_ref[...]
    a = jnp.dot(h, w1_ref[0], preferred_element_type=F32)
    g = jnp.dot(h, w3_ref[0], preferred_element_type=F32)
    lane = lax.broadcasted_iota(jnp.int32, (tm, LANES), 1)
    ce = jnp.sum(jnp.where(lane == e, comb_ref[...], 0.0), axis=-1, keepdims=True)
    act = _silu(a) * g * ce
    acc_ref[...] += jnp.dot(act.astype(BF16), w2_ref[0], preferred_element_type=F32)

    @pl.when(e == N_EXPERTS - 1)
    def _():
        y = ALPHA * x_ref[...] + gate_ref[...] * acc_ref[...]
        o_ref[...] = _ln(y) * g2_ref[...] + b2_ref[...]


def _moe(h2, comb, x1, gate2, ln_g, ln_b, w1, w3, w2, tm):
    rows = x1.shape[0]
    row = lambda w: pl.BlockSpec((tm, w), lambda i, e: (i, 0))
    const = pl.BlockSpec((1, D_MODEL), lambda i, e: (0, 0))
    gate_spec = (pl.BlockSpec((1, D_MODEL), lambda i, e: (0, 0)) if gate2.shape[0] == 1
                 else row(D_MODEL))
    return pl.pallas_call(
        functools.partial(_moe_kernel, tm=tm),
        grid=(rows // tm, N_EXPERTS),
        in_specs=[row(D_MODEL), row(LANES), row(D_MODEL), gate_spec, const, const,
                  pl.BlockSpec((1, D_MODEL, D_EXPERT), lambda i, e: (e, 0, 0)),
                  pl.BlockSpec((1, D_MODEL, D_EXPERT), lambda i, e: (e, 0, 0)),
                  pl.BlockSpec((1, D_EXPERT, D_MODEL), lambda i, e: (e, 0, 0))],
        out_specs=row(D_MODEL),
        out_shape=jax.ShapeDtypeStruct((rows, D_MODEL), F32),
        scratch_shapes=[pltpu.VMEM((tm, D_MODEL), F32)],
        compiler_params=_params("parallel", "arbitrary"),
        name="experts",
    )(h2, comb, x1, gate2, ln_g.reshape(1, D_MODEL), ln_b.reshape(1, D_MODEL), w1, w3, w2)


def _pair_experts():
    lo, hi = [], []
    for g in range(N_GROUPS):
        for a in range(EXPERTS_PER_GROUP):
            for b in range(a + 1, EXPERTS_PER_GROUP):
                lo.append(g * EXPERTS_PER_GROUP + a)
                hi.append(g * EXPERTS_PER_GROUP + b)
    return np.asarray(lo, np.int32), np.asarray(hi, np.int32)


def _slot_plan(cls, tm):
    rows = cls.shape[0]
    n_tiles = rows // tm + N_PAIRS
    onehot = (cls[:, None] == jnp.arange(N_PAIRS, dtype=jnp.int32)[None, :]).astype(jnp.int32)
    rank = jnp.cumsum(onehot, axis=0) - onehot
    tiles_per = (jnp.sum(onehot, axis=0) + tm - 1) // tm
    tile_start = jnp.cumsum(tiles_per) - tiles_per
    pos = jnp.sum(onehot * (tile_start[None, :] * tm + rank), axis=1).astype(jnp.int32)
    src = jnp.zeros((n_tiles * tm,), jnp.int32).at[pos].set(jnp.arange(rows, dtype=jnp.int32))
    tile = jnp.arange(n_tiles, dtype=jnp.int32)
    tile_class = jnp.clip(jnp.sum((tile[:, None] >= tile_start[None, :]).astype(jnp.int32), axis=1) - 1,
                          0, N_PAIRS - 1)
    pair_lo, pair_hi = _pair_experts()
    n_used = jnp.sum(tiles_per).astype(jnp.int32).reshape(1)
    return pos, src, jnp.asarray(pair_lo)[tile_class], jnp.asarray(pair_hi)[tile_class], n_used


def _start_rows(idx_ref, src_hbm, dst_ref, sem, n):
    for r in range(n):
        pltpu.make_async_copy(src_hbm.at[pl.ds(idx_ref[0, 0, r], 1)], dst_ref.at[pl.ds(r, 1)], sem).start(priority=r % 2)


def _wait_rows(src_hbm, dst_ref, sem, n):
    pltpu.make_async_copy(src_hbm.at[pl.ds(0, n)], dst_ref, sem).wait()


def _moe_pair_kernel(lo_ref, hi_ref, nused_ref, src_ref, src_next_ref, hx_hbm,
                     w1a_ref, w3a_ref, w2a_ref, w1b_ref, w3b_ref, w2b_ref, y_ref, xbuf, sems, *, tm):
    i = pl.program_id(0)
    n_used = nused_ref[0]
    slot = i % 2

    @pl.when(jnp.logical_and(i == 0, n_used > 0))
    def _():
        _start_rows(src_ref, hx_hbm, xbuf.at[0], sems.at[0], tm)

    @pl.when(i + 1 < n_used)
    def _():
        _start_rows(src_next_ref, hx_hbm, xbuf.at[1 - slot], sems.at[1 - slot], tm)

    @pl.when(i < n_used)
    def _():
        _wait_rows(hx_hbm, xbuf.at[slot], sems.at[slot], tm)
        x = xbuf[slot]
        h = x[:, :D_MODEL].astype(BF16)
        comb = x[:, D_MODEL:]
        lane = lax.broadcasted_iota(jnp.int32, (tm, LANES), 1)
        acc = jnp.zeros((tm, D_MODEL), F32)
        for e, w1_ref, w3_ref, w2_ref in ((lo_ref[i], w1a_ref, w3a_ref, w2a_ref), (hi_ref[i], w1b_ref, w3b_ref, w2b_ref)):
            a = jnp.dot(h, w1_ref[0], preferred_element_type=F32)
            g = jnp.dot(h, w3_ref[0], preferred_element_type=F32)
            ce = jnp.sum(jnp.where(lane == e, comb, 0.0), axis=-1, keepdims=True)
            act = _silu(a) * g * ce
            acc = acc + jnp.dot(act.astype(BF16), w2_ref[0], preferred_element_type=F32)
        y_ref[...] = acc

    @pl.when(i >= n_used)
    def _():
        y_ref[...] = jnp.zeros_like(y_ref)


def _moe_pairs(hx, src, tile_lo, tile_hi, n_used, w1, w3, w2, tm):
    n_tiles = tile_lo.shape[0]
    lo_spec = lambda k, n: pl.BlockSpec((1, k, n), lambda i, lo, hi, nu: (lo[i], 0, 0))
    hi_spec = lambda k, n: pl.BlockSpec((1, k, n), lambda i, lo, hi, nu: (hi[i], 0, 0))
    src3 = src.reshape(n_tiles, 1, tm)
    return pl.pallas_call(
        functools.partial(_moe_pair_kernel, tm=tm),
        grid_spec=pltpu.PrefetchScalarGridSpec(
            num_scalar_prefetch=3,
            grid=(n_tiles,),
            in_specs=[pl.BlockSpec((1, 1, tm), lambda i, lo, hi, nu: (i, 0, 0), memory_space=pltpu.SMEM),
                      pl.BlockSpec((1, 1, tm), lambda i, lo, hi, nu: (jnp.minimum(i + 1, n_tiles - 1), 0, 0),
                                   memory_space=pltpu.SMEM),
                      pl.BlockSpec(memory_space=pl.ANY),
                      lo_spec(D_MODEL, D_EXPERT), lo_spec(D_MODEL, D_EXPERT), lo_spec(D_EXPERT, D_MODEL),
                      hi_spec(D_MODEL, D_EXPERT), hi_spec(D_MODEL, D_EXPERT), hi_spec(D_EXPERT, D_MODEL)],
            out_specs=pl.BlockSpec((tm, D_MODEL), lambda i, lo, hi, nu: (i, 0)),
            scratch_shapes=[pltpu.VMEM((2, tm, D_MODEL + LANES), F32), pltpu.SemaphoreType.DMA((2,))]),
        out_shape=jax.ShapeDtypeStruct((n_tiles * tm, D_MODEL), F32),
        compiler_params=_params("arbitrary"),
        name="experts_paired",
    )(tile_lo, tile_hi, n_used, src3, src3, hx, w1, w3, w2, w1, w3, w2)


def _combine_kernel(pos_ref, pos_next_ref, y_hbm, x_ref, gate_ref, g2_ref, b2_ref, o_ref, ybuf, sems, *, tm, n_tiles):
    i = pl.program_id(0)
    slot = i % 2

    @pl.when(i == 0)
    def _():
        _start_rows(pos_ref, y_hbm, ybuf.at[0], sems.at[0], tm)

    @pl.when(i + 1 < n_tiles)
    def _():
        _start_rows(pos_next_ref, y_hbm, ybuf.at[1 - slot], sems.at[1 - slot], tm)

    _wait_rows(y_hbm, ybuf.at[slot], sems.at[slot], tm)
    y = ALPHA * x_ref[...] + gate_ref[...] * ybuf[slot]
    o_ref[...] = _ln(y) * g2_ref[...] + b2_ref[...]


def _combine(y_sorted, pos, x1, gate2, ln_g, ln_b, tm):
    rows = x1.shape[0]
    n_tiles = rows // tm
    row = pl.BlockSpec((tm, D_MODEL), lambda i: (i, 0))
    const = pl.BlockSpec((1, D_MODEL), lambda i: (0, 0))
    pos3 = pos.reshape(n_tiles, 1, tm)
    return pl.pallas_call(
        functools.partial(_combine_kernel, tm=tm, n_tiles=n_tiles),
        grid=(n_tiles,),
        in_specs=[pl.BlockSpec((1, 1, tm), lambda i: (i, 0, 0), memory_space=pltpu.SMEM),
                  pl.BlockSpec((1, 1, tm), lambda i: (jnp.minimum(i + 1, n_tiles - 1), 0, 0), memory_space=pltpu.SMEM),
                  pl.BlockSpec(memory_space=pl.ANY),
                  row, _row_spec(gate2.shape[0], tm, D_MODEL, rows), const, const],
        out_specs=row,
        out_shape=jax.ShapeDtypeStruct((rows, D_MODEL), F32),
        scratch_shapes=[pltpu.VMEM((2, tm, D_MODEL), F32), pltpu.SemaphoreType.DMA((2,))],
        compiler_params=_params("arbitrary"),
        name="combine",
    )(pos3, pos3, y_sorted, x1, gate2, ln_g.reshape(1, D_MODEL), ln_b.reshape(1, D_MODEL))


def _trunk_layer(x, mods, bsz, seq, pos0, s0, k_past, v_past, p, tiles):
    shift1, scale1, gate1, shift2, scale2, gate2 = mods
    uv, ret, q_c, k_c, v_c, k_bf, v_bf = _inproj(x, shift1, scale1, p["w_in"], p["layer"], tiles["tm_in"])
    a_out, v_rows = _gmlp(uv, p["w_sp"], p["b_sp"], p["ln_v_g"], p["ln_v_b"], min(seq, GMLP_CHUNK))
    b_out, s_new = _retention(ret, s0, pos0, bsz, seq, p["gn_g"], p["gn_b"])
    if k_past is None:
        c_out = _stick_breaking(q_c, k_bf, v_bf, bsz, seq, tiles["sb"])
    else:
        c_out = _stick_breaking_step(q_c, k_bf, v_bf, k_past, v_past, p["layer"], bsz, seq, tiles["sb"])
    grouped = "tm_group" in tiles
    post = _post(a_out, b_out, c_out, x, p["w_out"], p["layer"], gate1, p["ln1_g"], p["ln1_b"],
                 shift2, scale2, p["wr"], p["br"], tiles["tm_post"], grouped)
    if grouped:
        x1, comb, hx = post
        tm = tiles["tm_group"]
        pos, src, tile_lo, tile_hi, n_used = _slot_plan(comb[:, N_EXPERTS].astype(jnp.int32), tm)
        y_sorted = _moe_pairs(hx, src, tile_lo, tile_hi, n_used, p["w1"], p["w3"], p["w2"], tm)
        y = _combine(y_sorted, pos, x1, gate2, p["ln2_g"], p["ln2_b"], tiles["tm_combine"])
    else:
        x1, h2, comb = post
        y = _moe(h2, comb, x1, gate2, p["ln2_g"], p["ln2_b"], p["w1"], p["w3"], p["w2"], tiles["tm_moe"])
    return y, v_rows, s_new, k_c, v_c


def kernel(x_prompt, x_sample, cache_sb_k, cache_sb_v, state_ret, c_prompt, c_sample, w_ada, b_ada, w_in, w_out, ln_v_g, ln_v_b, w_spatial, b_spatial, gn_g, gn_b, ln1_g, ln1_b, ln2_g, ln2_b, w_router, b_router, w1, w3, w2):
    bp, tp, _ = x_prompt.shape
    bs, ts, _ = x_sample.shape
    past_len = cache_sb_k.shape[2]
    assert bp == 1

    n_c = bp + bs
    c_rows = -(-n_c // 8) * 8
    c_all = jnp.concatenate([c_prompt, c_sample, jnp.zeros((c_rows - n_c, D_MODEL), F32)], axis=0)
    mods = _adaln(c_all, w_ada, b_ada)

    wr_t = w_router.T
    cache_kt = jnp.transpose(cache_sb_k, (0, 1, 3, 4, 2))
    cache_vt = jnp.transpose(cache_sb_v, (0, 1, 3, 4, 2))
    br_col = b_router.reshape(N_EXPERTS, 1)

    tiles_p = dict(tm_in=256, sb=256, tm_post=512, tm_group=256, tm_combine=512)
    tiles_s = dict(tm_in=bs * ts, sb=256, tm_post=bs * ts, tm_moe=bs * ts)

    y_p = x_prompt.reshape(bp * tp, D_MODEL)
    y_s = x_sample.reshape(bs * ts, D_MODEL)
    zero_state = jnp.zeros((bp, H_B, HEAD_DIM, HEAD_DIM), F32)
    outs = [[] for _ in range(7)]
    for l in range(DEPTH):
        p = dict(w_in=w_in, w_out=w_out,
                 w_sp=w_spatial[l], b_sp=b_spatial[l], ln_v_g=ln_v_g[l], ln_v_b=ln_v_b[l],
                 gn_g=gn_g[l], gn_b=gn_b[l], ln1_g=ln1_g[l], ln1_b=ln1_b[l],
                 ln2_g=ln2_g[l], ln2_b=ln2_b[l], wr=wr_t, br=br_col,
                 w1=w1[l].astype(BF16), w3=w3[l].astype(BF16), w2=w2[l].astype(BF16), layer=l)
        m = mods[l]
        mods_p = [m[0:1, i * D_MODEL:(i + 1) * D_MODEL] for i in range(6)]
        mods_s = [jnp.repeat(m[bp:bp + bs, i * D_MODEL:(i + 1) * D_MODEL], ts, axis=0) for i in range(6)]
        y_p, _, s_p, k_p, v_p = _trunk_layer(y_p, mods_p, bp, tp, 0, zero_state, None, None, p, tiles_p)
        y_s, g_s, s_s, k_s, v_s = _trunk_layer(
            y_s, mods_s, bs, ts, past_len, state_ret[l],
            cache_kt, cache_vt, p, tiles_s)
        outs[0].append(s_p)
        outs[1].append(k_p.reshape(bp, tp, H_C, HEAD_DIM))
        outs[2].append(v_p.reshape(bp, tp, H_C, HEAD_DIM))
        outs[3].append(s_s)
        outs[4].append(k_s.reshape(bs, ts, H_C, HEAD_DIM))
        outs[5].append(v_s.reshape(bs, ts, H_C, HEAD_DIM))
        outs[6].append(g_s.reshape(bs, ts, W_A))
    return (y_p.reshape(bp, tp, D_MODEL), y_s.reshape(bs, ts, D_MODEL)) + tuple(jnp.stack(o) for o in outs)
```

```python
import functools

import numpy as np
import jax
import jax.numpy as jnp
from jax import lax
from jax.experimental import pallas as pl
from jax.experimental.pallas import tpu as pltpu

F32 = jnp.float32
BF16 = jnp.bfloat16
HIGHEST = lax.Precision.HIGHEST

D_MODEL = 1024
DEPTH = 2
HEAD_DIM = 64
W_A = D_MODEL // 4
W_B = 3 * D_MODEL // 8
W_C = D_MODEL - W_A - W_B
H_A = W_A // HEAD_DIM
H_B = W_B // HEAD_DIM
H_C = W_C // HEAD_DIM
GMLP_CHUNK = 128
ROPE_BASE = 10000.0
N_EXPERTS = 16
N_GROUPS = 4
EXPERTS_PER_GROUP = N_EXPERTS // N_GROUPS
D_EXPERT = D_MODEL // 2
PAIRS_PER_GROUP = EXPERTS_PER_GROUP * (EXPERTS_PER_GROUP - 1) // 2
N_PAIRS = N_GROUPS * PAIRS_PER_GROUP
HX_WORDS = D_MODEL // 2
ALPHA = (2 * DEPTH) ** 0.25
LN_EPS = 1e-5
D_IN = 2 * W_A + 4 * W_B + 3 * W_C
LANES = 128
VMEM_LIMIT = 48 * 1024 * 1024

NT_DIMS = (((1,), (1,)), ((), ()))
TN_DIMS = (((0,), (0,)), ((), ()))
INV_LN2 = 1.4426950408889634
SB_QSCALE = HEAD_DIM ** -0.5 * INV_LN2
SB_DEAD = -152.0
GMLP_SUB = 8
POST_SUB = 4
RET_BLOCK = 512


def _ln(x):
    mu = jnp.mean(x, axis=-1, keepdims=True)
    xc = x - mu
    var = jnp.mean(xc * xc, axis=-1, keepdims=True)
    return xc * lax.rsqrt(var + LN_EPS)


def _silu(x):
    return x * jax.nn.sigmoid(x)


def _params(*sem):
    return pltpu.CompilerParams(dimension_semantics=sem, vmem_limit_bytes=VMEM_LIMIT)


def _row_spec(rows, tm, width, total_rows):
    if rows == 1:
        return pl.BlockSpec((1, width), lambda i: (0, 0))
    assert rows == total_rows
    return pl.BlockSpec((tm, width), lambda i: (i, 0))


def _adaln_kernel(c_ref, w_ref, b_ref, o_ref):
    sc = _silu(c_ref[...])
    o_ref[0] = jnp.dot(sc, w_ref[0], preferred_element_type=F32, precision=HIGHEST) + b_ref[0]


def _adaln(c_all, w_ada, b_ada):
    rows = c_all.shape[0]
    tn = 1536
    return pl.pallas_call(
        _adaln_kernel,
        grid=(DEPTH, 6 * D_MODEL // tn),
        in_specs=[pl.BlockSpec((rows, D_MODEL), lambda l, j: (0, 0)),
                  pl.BlockSpec((1, D_MODEL, tn), lambda l, j: (l, 0, j)),
                  pl.BlockSpec((1, 1, tn), lambda l, j: (l, 0, j))],
        out_specs=pl.BlockSpec((1, rows, tn), lambda l, j: (l, 0, j)),
        out_shape=jax.ShapeDtypeStruct((DEPTH, rows, 6 * D_MODEL), F32),
        compiler_params=_params("parallel", "parallel"),
        name="adaln",
    )(c_all, w_ada, b_ada.reshape(DEPTH, 1, 6 * D_MODEL))


def _inproj_kernel(x_ref, shift_ref, scale_ref, w_ref, uv_ref, ret_ref, q_ref, k_ref, v_ref, kb_ref, vb_ref, wb_ref):
    @pl.when(pl.program_id(0) == 0)
    def _():
        wb_ref[...] = w_ref[0].astype(BF16)

    h = _ln(x_ref[...]) * (1.0 + scale_ref[...]) + shift_ref[...]
    r = jnp.dot(h.astype(BF16), wb_ref[...], preferred_element_type=F32)
    c0 = 2 * W_A
    c1 = c0 + 4 * W_B
    uv_ref[...] = r[:, :c0]
    ret_ref[...] = r[:, c0:c1]
    q_ref[...] = r[:, c1:c1 + W_C]
    k = r[:, c1 + W_C:c1 + 2 * W_C]
    v = r[:, c1 + 2 * W_C:]
    k_ref[...] = k
    v_ref[...] = v
    kb_ref[...] = k.astype(BF16)
    vb_ref[...] = v.astype(BF16)


def _inproj(x, shift, scale, w_in, layer, tm):
    rows = x.shape[0]
    widths = (2 * W_A, 4 * W_B, W_C, W_C, W_C, W_C, W_C)
    dtypes = (F32,) * 5 + (BF16,) * 2
    return pl.pallas_call(
        _inproj_kernel,
        grid=(rows // tm,),
        in_specs=[pl.BlockSpec((tm, D_MODEL), lambda i: (i, 0)),
                  _row_spec(shift.shape[0], tm, D_MODEL, rows),
                  _row_spec(scale.shape[0], tm, D_MODEL, rows),
                  pl.BlockSpec((1, D_MODEL, D_IN), lambda i: (layer, 0, 0), pipeline_mode=pl.Buffered(1))],
        out_specs=[pl.BlockSpec((tm, w), lambda i: (i, 0)) for w in widths],
        out_shape=[jax.ShapeDtypeStruct((rows, w), dt) for w, dt in zip(widths, dtypes)],
        scratch_shapes=[pltpu.VMEM((D_MODEL, D_IN), BF16)],
        compiler_params=_params("arbitrary"),
        name="inproj",
    )(x, shift, scale, w_in)


def _gmlp_kernel(uv_ref, wsp_ref, bias_ref, g_ref, b_ref, a_ref, vn_ref, *, chunk, n_sub):
    uv = uv_ref[...]
    u = jax.nn.gelu(uv[:, :W_A])
    v = _ln(jax.nn.gelu(uv[:, W_A:])) * g_ref[...] + b_ref[...]
    vn_ref[...] = v
    row = lax.broadcasted_iota(jnp.int32, (chunk, chunk), 0)
    col = lax.broadcasted_iota(jnp.int32, (chunk, chunk), 1)
    lane_head = lax.broadcasted_iota(jnp.int32, (chunk, W_A), 1) // HEAD_DIM
    w = [jnp.where(col <= row, wsp_ref[h], 0.0).astype(BF16) for h in range(H_A)]
    for c in range(n_sub):
        rows = slice(c * chunk, (c + 1) * chunk)
        vc = v[rows]
        mixed = bias_ref[...]
        for h in range(H_A):
            vh = jnp.where(lane_head == h, vc, 0.0).astype(BF16)
            mixed = mixed + jnp.dot(w[h], vh, preferred_element_type=F32)
        a_ref[rows, :] = u[rows] * mixed


def _gmlp(uv, w_sp, b_sp, ln_g, ln_b, chunk):
    rows = uv.shape[0]
    n_sub = min(GMLP_SUB, rows // chunk)
    tm = n_sub * chunk
    wsp = w_sp[:, :chunk, :chunk]
    bias = jnp.repeat(b_sp[:, :chunk].T, HEAD_DIM, axis=1)
    return pl.pallas_call(
        functools.partial(_gmlp_kernel, chunk=chunk, n_sub=n_sub),
        grid=(rows // tm,),
        in_specs=[pl.BlockSpec((tm, 2 * W_A), lambda i: (i, 0)),
                  pl.BlockSpec((H_A, chunk, chunk), lambda i: (0, 0, 0)),
                  pl.BlockSpec((chunk, W_A), lambda i: (0, 0)),
                  pl.BlockSpec((1, W_A), lambda i: (0, 0)),
                  pl.BlockSpec((1, W_A), lambda i: (0, 0))],
        out_specs=[pl.BlockSpec((tm, W_A), lambda i: (i, 0)),
                   pl.BlockSpec((tm, W_A), lambda i: (i, 0))],
        out_shape=[jax.ShapeDtypeStruct((rows, W_A), F32),
                   jax.ShapeDtypeStruct((rows, W_A), F32)],
        compiler_params=_params("parallel"),
        name="gmlp",
    )(uv, wsp, bias, ln_g.reshape(1, W_A), ln_b.reshape(1, W_A))


def _rope(x, cos, sin):
    lane = lax.broadcasted_iota(jnp.int32, (x.shape[0], LANES), 1)
    first_half = (lane & (HEAD_DIM // 2)) == 0
    parts = []
    for c in range(x.shape[1] // LANES):
        xc = x[:, c * LANES:(c + 1) * LANES]
        rot = jnp.where(first_half,
                        pltpu.roll(xc, LANES - HEAD_DIM // 2, 1),
                        pltpu.roll(xc, HEAD_DIM // 2, 1))
        parts.append(xc * cos + rot * sin)
    return jnp.concatenate(parts, axis=1)


def _ret_kernel(r_ref, cos_ref, sin_ref, qdec_ref, kdec_ref, dec_ref, blk_ref, s0_ref,
                gng_ref, gnb_ref, o_ref, sout_ref, s_scr, o_scr, *, n_blocks):
    n = pl.program_id(1)

    @pl.when(n == 0)
    def _():
        s_scr[...] = s0_ref[0]

    r = r_ref[...]
    cos = cos_ref[...]
    sin = sin_ref[...]
    qr = _rope(r[:, :W_B], cos, sin)
    kr = _rope(r[:, W_B:2 * W_B], cos, sin) * (HEAD_DIM ** -0.5)
    vb = r[:, 2 * W_B:3 * W_B].astype(BF16)
    gate = r[:, 3 * W_B:]
    qb = qr.astype(BF16)
    kb = kr.astype(BF16)
    qdb = (qr * qdec_ref[...]).astype(BF16)
    kdb = (kr * kdec_ref[...]).astype(BF16)
    for h in range(H_B):
        sl = slice(h * HEAD_DIM, (h + 1) * HEAD_DIM)
        scores = lax.dot_general(qb[:, sl], kb[:, sl], NT_DIMS, preferred_element_type=F32) * dec_ref[h]
        s_h = s_scr[h]
        o_h = (jnp.dot(scores.astype(BF16), vb[:, sl], preferred_element_type=F32)
               + jnp.dot(qdb[:, sl], s_h.astype(BF16), preferred_element_type=F32))
        s_scr[h] = s_h * blk_ref[h] + lax.dot_general(kdb[:, sl], vb[:, sl], TN_DIMS,
                                                      preferred_element_type=F32)
        o_scr[:, sl] = _ln(o_h)
    o_ref[...] = (o_scr[...] * gng_ref[...] + gnb_ref[...]) * _silu(gate)

    @pl.when(n == n_blocks - 1)
    def _():
        sout_ref[0] = s_scr[...]


def _retention(ret, s0, pos0, bsz, seq, gn_g, gn_b):
    L = min(seq, RET_BLOCK)
    n_blocks = seq // L
    half = HEAD_DIM // 2
    inv = ROPE_BASE ** (-jnp.arange(half, dtype=F32) / half)
    ang = (pos0 + jnp.arange(seq)).astype(F32)[:, None] * inv[None, :]
    cos, sin = jnp.cos(ang), jnp.sin(ang)
    cos_t = jnp.tile(jnp.concatenate([cos, cos], axis=1), (1, LANES // HEAD_DIM))
    sin_t = jnp.tile(jnp.concatenate([-sin, sin], axis=1), (1, LANES // HEAD_DIM))
    log_g = jnp.log1p(-jnp.exp2(-5.0 - jnp.arange(H_B, dtype=F32)))
    idx = jnp.arange(L, dtype=F32)
    diff = idx[:, None] - idx[None, :]
    decay = jnp.where(diff >= 0, jnp.exp(diff[None] * log_g[:, None, None]), 0.0)
    q_decay = jnp.exp((idx[None, :] + 1.0) * log_g[:, None])
    k_decay = jnp.exp((L - 1.0 - idx[None, :]) * log_g[:, None])
    blk_decay = jnp.exp(L * log_g)
    qdec = jnp.repeat(q_decay.T, HEAD_DIM, axis=1)
    kdec = jnp.repeat(k_decay.T, HEAD_DIM, axis=1)
    blk = jnp.broadcast_to(blk_decay[:, None, None], (H_B, HEAD_DIM, HEAD_DIM))
    const2 = lambda b, n: (0, 0)
    const3 = lambda b, n: (0, 0, 0)
    return pl.pallas_call(
        functools.partial(_ret_kernel, n_blocks=n_blocks),
        grid=(bsz, n_blocks),
        in_specs=[pl.BlockSpec((L, 4 * W_B), lambda b, n: (b * n_blocks + n, 0)),
                  pl.BlockSpec((L, LANES), lambda b, n: (n, 0)),
                  pl.BlockSpec((L, LANES), lambda b, n: (n, 0)),
                  pl.BlockSpec((L, W_B), const2),
                  pl.BlockSpec((L, W_B), const2),
                  pl.BlockSpec((H_B, L, L), const3),
                  pl.BlockSpec((H_B, HEAD_DIM, HEAD_DIM), const3),
                  pl.BlockSpec((1, H_B, HEAD_DIM, HEAD_DIM), lambda b, n: (b, 0, 0, 0)),
                  pl.BlockSpec((1, W_B), const2),
                  pl.BlockSpec((1, W_B), const2)],
        out_specs=[pl.BlockSpec((L, W_B), lambda b, n: (b * n_blocks + n, 0)),
                   pl.BlockSpec((1, H_B, HEAD_DIM, HEAD_DIM), lambda b, n: (b, 0, 0, 0))],
        out_shape=[jax.ShapeDtypeStruct((bsz * seq, W_B), F32),
                   jax.ShapeDtypeStruct((bsz, H_B, HEAD_DIM, HEAD_DIM), F32)],
        scratch_shapes=[pltpu.VMEM((H_B, HEAD_DIM, HEAD_DIM), F32),
                        pltpu.VMEM((L, W_B), F32)],
        compiler_params=_params("parallel", "arbitrary"),
        name="retention",
    )(ret, cos_t, sin_t, qdec, kdec, decay, blk, s0, gn_g.reshape(1, W_B), gn_b.reshape(1, W_B))


def _sb_block(qm_ref, k_ref, v_ref, u2_ref, acc_ref, carry_ref, causal):
    tk = k_ref.shape[0]
    half = lax.broadcasted_iota(jnp.int32, (tk, LANES), 1) // HEAD_DIM
    u2 = u2_ref[...]
    kp = [k_ref[:, p * LANES:(p + 1) * LANES] for p in range(H_C // 2)]

    def scores(h):
        z = lax.dot_general(qm_ref[h], kp[h // 2], NT_DIMS, preferred_element_type=F32)
        neg_abs = pltpu.bitcast(pltpu.bitcast(z, jnp.uint32) | jnp.uint32(0x80000000), F32)
        ls_pos = jnp.minimum(z, 0.0) - jnp.log(1.0 + jnp.exp2(neg_abs)) * INV_LN2
        log_stay = ls_pos - z
        if causal is not None:
            log_stay = jnp.where(causal, log_stay, 0.0)
        hi = log_stay.astype(BF16)
        lo = (log_stay - hi.astype(F32)).astype(BF16)
        return ls_pos, log_stay[:, :1], jnp.concatenate([hi, lo], axis=1)

    def cumsum(hi_lo):
        return jnp.dot(hi_lo, u2, preferred_element_type=F32)

    def weigh(h, ls_pos, first_col, excl):
        carry = carry_ref[h]
        att = jnp.exp2(ls_pos + excl + carry)
        if causal is not None:
            att = jnp.where(causal, att, 0.0)
        vp = v_ref[:, (h // 2) * LANES:(h // 2 + 1) * LANES]
        vp = jnp.where(half == h % 2, vp, jnp.zeros_like(vp))
        carry_ref[h] = carry + (excl[:, :1] + first_col)
        return jnp.dot(att.astype(BF16), vp, preferred_element_type=F32)

    stage_a, stage_b, outs = {}, {}, {}
    for step in range(H_C + 2):
        if step - 2 >= 0:
            h = step - 2
            outs[h] = weigh(h, stage_a[h][0], stage_a[h][1], stage_b.pop(h))
            del stage_a[h]
            if h % 2 == 1:
                p = h // 2
                acc_ref[:, p * LANES:(p + 1) * LANES] += outs.pop(h - 1) + outs.pop(h)
        if 0 <= step - 1 < H_C:
            stage_b[step - 1] = cumsum(stage_a[step - 1][2])
        if step < H_C:
            stage_a[step] = scores(step)


def _sb_kernel(q_ref, k_ref, v_ref, u2_ref, o_ref, qm_ref, acc_ref, carry_ref, *, t):
    i = pl.program_id(1)
    acc_ref[...] = jnp.zeros_like(acc_ref)
    carry_ref[...] = jnp.zeros_like(carry_ref)
    half = lax.broadcasted_iota(jnp.int32, (t, LANES), 1) // HEAD_DIM
    for h in range(H_C):
        qp = q_ref[:, (h // 2) * LANES:(h // 2 + 1) * LANES] * SB_QSCALE
        qm_ref[h] = jnp.where(half == h % 2, qp, 0.0).astype(BF16)

    def block(j, causal):
        k0 = pl.multiple_of((i - j) * t, t)
        _sb_block(qm_ref, k_ref.at[pl.ds(k0, t), :], v_ref.at[pl.ds(k0, t), :], u2_ref, acc_ref, carry_ref, causal)

    block(0, lax.broadcasted_iota(jnp.int32, (t, t), 1) < lax.broadcasted_iota(jnp.int32, (t, t), 0))

    def key_block(state):
        j, _ = state
        block(j, None)
        dead = jnp.max(carry_ref[...]) < SB_DEAD
        return j + 1, dead.astype(jnp.int32)

    lax.while_loop(lambda state: jnp.logical_and(state[0] <= i, state[1] == 0),
                   key_block, (jnp.int32(1), jnp.int32(0)))
    o_ref[...] = acc_ref[...]


def _stick_breaking(q, k, v, bsz, seq, t):
    nq = seq // t
    tri = np.tril(np.ones((t, t), np.float32), -1)
    u2 = jnp.asarray(np.concatenate([tri, tri], axis=0), dtype=BF16)
    resident = dict(pipeline_mode=pl.Buffered(1))
    return pl.pallas_call(
        functools.partial(_sb_kernel, t=t),
        grid=(bsz, nq),
        in_specs=[pl.BlockSpec((t, W_C), lambda b, i: (b * nq + i, 0)),
                  pl.BlockSpec((seq, W_C), lambda b, i: (b, 0), **resident),
                  pl.BlockSpec((seq, W_C), lambda b, i: (b, 0), **resident),
                  pl.BlockSpec((2 * t, t), lambda b, i: (0, 0), **resident)],
        out_specs=pl.BlockSpec((t, W_C), lambda b, i: (b * nq + i, 0)),
        out_shape=jax.ShapeDtypeStruct((bsz * seq, W_C), F32),
        scratch_shapes=[pltpu.VMEM((H_C, t, LANES), BF16),
                        pltpu.VMEM((t, W_C), F32),
                        pltpu.VMEM((H_C, t, 1), F32)],
        compiler_params=_params("parallel", "arbitrary"),
        name="stick_breaking",
    )(q, k, v, u2)


def _sb_scores(z, causal=None):
    neg_abs = pltpu.bitcast(pltpu.bitcast(z, jnp.uint32) | jnp.uint32(0x80000000), F32)
    ls_pos = jnp.minimum(z, 0.0) - jnp.log(1.0 + jnp.exp2(neg_abs)) * INV_LN2
    log_stay = ls_pos - z
    if causal is not None:
        log_stay = jnp.where(causal, log_stay, 0.0)
    hi = log_stay.astype(BF16)
    lo = (log_stay - hi.astype(F32)).astype(BF16)
    return ls_pos, log_stay, jnp.concatenate([hi, lo], axis=1)


def _sb_step_kernel(q_ref, kt_ref, vt_ref, kn_ref, vn_ref, u2_ref, un_ref, o_ref, acc_ref, carry_ref, *, tk):
    seq = q_ref.shape[0]
    past = kt_ref.shape[-1]
    qs = (q_ref[...] * SB_QSCALE).astype(BF16)
    row = lax.broadcasted_iota(jnp.int32, (seq, seq), 0)
    col = lax.broadcasted_iota(jnp.int32, (seq, seq), 1)
    causal = col < row
    un = un_ref[...]
    for h in range(H_C):
        sl = slice(h * HEAD_DIM, (h + 1) * HEAD_DIM)
        z = lax.dot_general(qs[:, sl], kn_ref[:, sl], NT_DIMS, preferred_element_type=F32)
        ls_pos, log_stay, hi_lo = _sb_scores(z, causal)
        excl = jnp.dot(hi_lo, un, preferred_element_type=F32)
        att = jnp.where(causal, jnp.exp2(ls_pos + excl), 0.0)
        acc_ref[h] = jnp.dot(att.astype(BF16), vn_ref[:, sl], preferred_element_type=F32)
        carry_ref[h] = excl[:, :1] + log_stay[:, :1]

    u2 = u2_ref[...]

    def key_block(state):
        j, _ = state
        k0 = pl.multiple_of(past - (j + 1) * tk, tk)
        for h in range(H_C):
            kt = kt_ref[0, 0, h, :, pl.ds(k0, tk)].astype(BF16)
            vt = vt_ref[0, 0, h, :, pl.ds(k0, tk)].astype(BF16)
            z = jnp.dot(qs[:, h * HEAD_DIM:(h + 1) * HEAD_DIM], kt, preferred_element_type=F32)
            ls_pos, log_stay, hi_lo = _sb_scores(z)
            excl = jnp.dot(hi_lo, u2, preferred_element_type=F32)
            carry = carry_ref[h]
            att = jnp.exp2(ls_pos + excl + carry)
            acc_ref[h] += lax.dot_general(att.astype(BF16), vt, NT_DIMS, preferred_element_type=F32)
            carry_ref[h] = carry + (excl[:, :1] + log_stay[:, :1])
        dead = jnp.max(carry_ref[...]) < SB_DEAD
        return j + 1, dead.astype(jnp.int32)

    lax.while_loop(lambda state: jnp.logical_and(state[0] < past // tk, state[1] == 0),
                   key_block, (jnp.int32(0), jnp.int32(0)))
    for h in range(H_C):
        o_ref[:, h * HEAD_DIM:(h + 1) * HEAD_DIM] = acc_ref[h]


def _stick_breaking_step(q, k_new, v_new, cache_kt, cache_vt, layer, bsz, seq, tk):
    past = cache_kt.shape[-1]
    assert past % tk == 0
    tri = lambda n: np.tril(np.ones((n, n), np.float32), -1)
    stacked = lambda n: jnp.asarray(np.concatenate([tri(n), tri(n)], axis=0), dtype=BF16)
    rows = pl.BlockSpec((seq, W_C), lambda b: (b, 0))
    cache = pl.BlockSpec((1, 1, H_C, HEAD_DIM, past), lambda b: (layer, b, 0, 0, 0))
    return pl.pallas_call(
        functools.partial(_sb_step_kernel, tk=tk),
        grid=(bsz,),
        in_specs=[rows, cache, cache, rows, rows,
                  pl.BlockSpec((2 * tk, tk), lambda b: (0, 0)), pl.BlockSpec((2 * seq, seq), lambda b: (0, 0))],
        out_specs=rows,
        out_shape=jax.ShapeDtypeStruct((bsz * seq, W_C), F32),
        scratch_shapes=[pltpu.VMEM((H_C, seq, HEAD_DIM), F32), pltpu.VMEM((H_C, seq, 1), F32)],
        compiler_params=_params("parallel"),
        name="stick_breaking_step",
    )(q, cache_kt, cache_vt, k_new, v_new, stacked(tk), stacked(seq))


def _route(sel, s):
    g_scores = []
    for g in range(N_GROUPS):
        a, b, c, d = sel[EXPERTS_PER_GROUP * g:EXPERTS_PER_GROUP * (g + 1)]
        ab_hi, ab_lo = jnp.maximum(a, b), jnp.minimum(a, b)
        cd_hi, cd_lo = jnp.maximum(c, d), jnp.minimum(c, d)
        top1 = jnp.maximum(ab_hi, cd_hi)
        top2 = jnp.maximum(jnp.minimum(ab_hi, cd_hi), jnp.maximum(ab_lo, cd_lo))
        g_scores.append(top1 + top2)
    best = g_scores[0]
    gi = jnp.zeros(best.shape, jnp.int32)
    for g in range(1, N_GROUPS):
        upd = g_scores[g] > best
        gi = jnp.where(upd, g, gi)
        best = jnp.where(upd, g_scores[g], best)

    def pick_group(rows, l):
        out = rows[(N_GROUPS - 1) * EXPERTS_PER_GROUP + l]
        for g in range(N_GROUPS - 2, -1, -1):
            out = jnp.where(gi == g, rows[g * EXPERTS_PER_GROUP + l], out)
        return out

    ig = [pick_group(sel, l) for l in range(EXPERTS_PER_GROUP)]
    sg = [pick_group(s, l) for l in range(EXPERTS_PER_GROUP)]
    b1 = ig[0]
    i1 = jnp.zeros(best.shape, jnp.int32)
    for l in range(1, EXPERTS_PER_GROUP):
        upd = ig[l] > b1
        i1 = jnp.where(upd, l, i1)
        b1 = jnp.where(upd, ig[l], b1)
    b2 = jnp.full(best.shape, -jnp.inf, F32)
    i2 = jnp.zeros(best.shape, jnp.int32)
    for l in range(EXPERTS_PER_GROUP):
        upd = jnp.logical_and(i1 != l, ig[l] > b2)
        i2 = jnp.where(upd, l, i2)
        b2 = jnp.where(upd, ig[l], b2)

    def pick_local(idx):
        out = sg[EXPERTS_PER_GROUP - 1]
        for l in range(EXPERTS_PER_GROUP - 2, -1, -1):
            out = jnp.where(idx == l, sg[l], out)
        return out

    w1 = pick_local(i1)
    w2 = pick_local(i2)
    tot = w1 + w2
    return gi * EXPERTS_PER_GROUP + i1, gi * EXPERTS_PER_GROUP + i2, w1 / tot, w2 / tot


def _pack_halves(h):
    half = h.shape[1] // 2
    hb = h.astype(BF16).astype(F32)
    hi = pltpu.bitcast(hb[:, :half], jnp.uint32)
    lo = pltpu.bitcast(hb[:, half:], jnp.uint32) >> 16
    return pltpu.bitcast(hi | lo, F32)


def _unpack_halves(w):
    bits = pltpu.bitcast(w, jnp.uint32)
    first = pltpu.bitcast(bits & jnp.uint32(0xFFFF0000), F32).astype(BF16)
    second = pltpu.bitcast(bits << 16, F32).astype(BF16)
    return jnp.concatenate([first, second], axis=1)


def _pair_class(e1, e2):
    lo = jnp.minimum(e1, e2)
    hi = jnp.maximum(e1, e2)
    g = lo // EXPERTS_PER_GROUP
    llo = lo - g * EXPERTS_PER_GROUP
    lhi = hi - g * EXPERTS_PER_GROUP
    return g * PAIRS_PER_GROUP + ((llo * (2 * EXPERTS_PER_GROUP - 1 - llo)) >> 1) + (lhi - llo - 1)


def _post_kernel(a_ref, b_ref, c_ref, x_ref, wo_ref, gate_ref, g1_ref, b1_ref, sh2_ref, sc2_ref,
                 wrt_ref, br_ref, x1_ref, *rest, tm, with_hx):
    if with_hx:
        h2_ref = None
        comb_ref, hx_ref, wob_ref = rest
    else:
        hx_ref = None
        h2_ref, comb_ref, wob_ref = rest
    @pl.when(pl.program_id(0) == 0)
    def _():
        wob_ref[...] = wo_ref[0].astype(BF16)

    sub = tm // POST_SUB if tm % (POST_SUB * LANES) == 0 else tm
    tiles = [slice(r0, r0 + sub) for r0 in range(0, tm, sub)]
    per_row = lambda ref, rows: ref[rows, :] if ref.shape[0] == tm else ref[...]
    proj = [jnp.dot(a_ref[rows, :].astype(BF16), wob_ref[:W_A], preferred_element_type=F32)
            + jnp.dot(b_ref[rows, :].astype(BF16), wob_ref[W_A:W_A + W_B], preferred_element_type=F32)
            + jnp.dot(c_ref[rows, :].astype(BF16), wob_ref[W_A + W_B:], preferred_element_type=F32)
            for rows in tiles]
    x1 = [_ln(ALPHA * x_ref[rows, :] + per_row(gate_ref, rows) * pr) * g1_ref[...] + b1_ref[...]
          for rows, pr in zip(tiles, proj)]
    for rows, v in zip(tiles, x1):
        x1_ref[rows, :] = v
    h2 = [_ln(v) * (1.0 + per_row(sc2_ref, rows)) + per_row(sh2_ref, rows) for rows, v in zip(tiles, x1)]
    logits = [lax.dot_general(wrt_ref[...], v, NT_DIMS, preferred_element_type=F32, precision=HIGHEST) for v in h2]
    for rows, v, logits_t in zip(tiles, h2, logits):
        if h2_ref is not None:
            h2_ref[rows, :] = v.astype(BF16)
        s_t = jax.nn.sigmoid(logits_t)
        sel_t = s_t + br_ref[...]
        s = [s_t[e:e + 1, :] for e in range(N_EXPERTS)]
        sel = [sel_t[e:e + 1, :] for e in range(N_EXPERTS)]
        e1, e2, w1, w2 = _route(sel, s)
        expert = lax.broadcasted_iota(jnp.int32, (LANES, sub), 0)
        comb_t = jnp.where(expert == e1, w1, jnp.where(expert == e2, w2, 0.0))
        comb_t = jnp.where(expert == N_EXPERTS, _pair_class(e1, e2).astype(F32), comb_t)
        comb_ref[rows, :] = comb_t.T
        if hx_ref is not None:
            hx_ref[rows, :HX_WORDS] = _pack_halves(v)
            hx_ref[rows, HX_WORDS:] = comb_t.T


def _post(a, b, c, x, w_out, layer, gate1, ln_g, ln_b, shift2, scale2, wr_t, br_col, tm, with_hx):
    rows = x.shape[0]
    if with_hx:
        tail_spec = [pl.BlockSpec((tm, LANES), lambda i: (i, 0)), pl.BlockSpec((tm, HX_WORDS + LANES), lambda i: (i, 0))]
        tail_shape = [jax.ShapeDtypeStruct((rows, LANES), F32), jax.ShapeDtypeStruct((rows, HX_WORDS + LANES), F32)]
    else:
        tail_spec = [pl.BlockSpec((tm, D_MODEL), lambda i: (i, 0)), pl.BlockSpec((tm, LANES), lambda i: (i, 0))]
        tail_shape = [jax.ShapeDtypeStruct((rows, D_MODEL), BF16), jax.ShapeDtypeStruct((rows, LANES), F32)]
    row = lambda w: pl.BlockSpec((tm, w), lambda i: (i, 0))
    const = lambda r, w: pl.BlockSpec((r, w), lambda i: (0, 0))
    return pl.pallas_call(
        functools.partial(_post_kernel, tm=tm, with_hx=with_hx),
        grid=(rows // tm,),
        in_specs=[row(W_A), row(W_B), row(W_C), row(D_MODEL),
                  pl.BlockSpec((1, D_MODEL, D_MODEL), lambda i: (layer, 0, 0), pipeline_mode=pl.Buffered(1)),
                  _row_spec(gate1.shape[0], tm, D_MODEL, rows),
                  const(1, D_MODEL), const(1, D_MODEL),
                  _row_spec(shift2.shape[0], tm, D_MODEL, rows),
                  _row_spec(scale2.shape[0], tm, D_MODEL, rows),
                  const(N_EXPERTS, D_MODEL), const(N_EXPERTS, 1)],
        out_specs=[row(D_MODEL)] + tail_spec,
        out_shape=[jax.ShapeDtypeStruct((rows, D_MODEL), F32)] + tail_shape,
        scratch_shapes=[pltpu.VMEM((D_MODEL, D_MODEL), BF16)],
        compiler_params=_params("arbitrary"),
        name="post_mix",
    )(a, b, c, x, w_out, gate1, ln_g.reshape(1, D_MODEL), ln_b.reshape(1, D_MODEL),
      shift2, scale2, wr_t, br_col)


def _moe_kernel(h_ref, comb_ref, x_ref, gate_ref, g2_ref, b2_ref, w1_ref, w3_ref, w2_ref,
                o_ref, acc_ref, *, tm):
    e = pl.program_id(1)

    @pl.when(e == 0)
    def _():
        acc_ref[...] = jnp.zeros_like(acc_ref)

    h = h_ref[...]
    a = jnp.dot(h, w1_ref[0], preferred_element_type=F32)
    g = jnp.dot(h, w3_ref[0], preferred_element_type=F32)
    lane = lax.broadcasted_iota(jnp.int32, (tm, LANES), 1)
    ce = jnp.sum(jnp.where(lane == e, comb_ref[...], 0.0), axis=-1, keepdims=True)
    act = _silu(a) * g * ce
    acc_ref[...] += jnp.dot(act.astype(BF16), w2_ref[0], preferred_element_type=F32)

    @pl.when(e == N_EXPERTS - 1)
    def _():
        y = ALPHA * x_ref[...] + gate_ref[...] * acc_ref[...]
        o_ref[...] = _ln(y) * g2_ref[...] + b2_ref[...]


def _moe(h2, comb, x1, gate2, ln_g, ln_b, w1, w3, w2, tm):
    rows = x1.shape[0]
    row = lambda w: pl.BlockSpec((tm, w), lambda i, e: (i, 0))
    const = pl.BlockSpec((1, D_MODEL), lambda i, e: (0, 0))
    gate_spec = (pl.BlockSpec((1, D_MODEL), lambda i, e: (0, 0)) if gate2.shape[0] == 1
                 else row(D_MODEL))
    return pl.pallas_call(
        functools.partial(_moe_kernel, tm=tm),
        grid=(rows // tm, N_EXPERTS),
        in_specs=[row(D_MODEL), row(LANES), row(D_MODEL), gate_spec, const, const,
                  pl.BlockSpec((1, D_MODEL, D_EXPERT), lambda i, e: (e, 0, 0)),
                  pl.BlockSpec((1, D_MODEL, D_EXPERT), lambda i, e: (e, 0, 0)),
                  pl.BlockSpec((1, D_EXPERT, D_MODEL), lambda i, e: (e, 0, 0))],
        out_specs=row(D_MODEL),
        out_shape=jax.ShapeDtypeStruct((rows, D_MODEL), F32),
        scratch_shapes=[pltpu.VMEM((tm, D_MODEL), F32)],
        compiler_params=_params("parallel", "arbitrary"),
        name="experts",
    )(h2, comb, x1, gate2, ln_g.reshape(1, D_MODEL), ln_b.reshape(1, D_MODEL), w1, w3, w2)


def _pair_experts():
    lo, hi = [], []
    for g in range(N_GROUPS):
        for a in range(EXPERTS_PER_GROUP):
            for b in range(a + 1, EXPERTS_PER_GROUP):
                lo.append(g * EXPERTS_PER_GROUP + a)
                hi.append(g * EXPERTS_PER_GROUP + b)
    return np.asarray(lo, np.int32), np.asarray(hi, np.int32)


def _slot_plan(cls, tm):
    rows = cls.shape[0]
    n_tiles = rows // tm + N_PAIRS
    onehot = (cls[:, None] == jnp.arange(N_PAIRS, dtype=jnp.int32)[None, :]).astype(jnp.int32)
    rank = jnp.cumsum(onehot, axis=0) - onehot
    tiles_per = (jnp.sum(onehot, axis=0) + tm - 1) // tm
    tile_start = jnp.cumsum(tiles_per) - tiles_per
    pos = jnp.sum(onehot * (tile_start[None, :] * tm + rank), axis=1).astype(jnp.int32)
    src = jnp.zeros((n_tiles * tm,), jnp.int32).at[pos].set(jnp.arange(rows, dtype=jnp.int32))
    tile = jnp.arange(n_tiles, dtype=jnp.int32)
    tile_class = jnp.clip(jnp.sum((tile[:, None] >= tile_start[None, :]).astype(jnp.int32), axis=1) - 1,
                          0, N_PAIRS - 1)
    pair_lo, pair_hi = _pair_experts()
    n_used = jnp.sum(tiles_per).astype(jnp.int32).reshape(1)
    return pos, src, jnp.asarray(pair_lo)[tile_class], jnp.asarray(pair_hi)[tile_class], n_used


def _start_rows(idx_ref, src_hbm, dst_ref, sem, n):
    for r in range(n):
        pltpu.make_async_copy(src_hbm.at[pl.ds(idx_ref[0, 0, r], 1)], dst_ref.at[pl.ds(r, 1)], sem).start(priority=r % 2)


def _wait_rows(src_hbm, dst_ref, sem, n):
    pltpu.make_async_copy(src_hbm.at[pl.ds(0, n)], dst_ref, sem).wait()


def _moe_pair_kernel(lo_ref, hi_ref, nused_ref, src_ref, src_next_ref, hx_hbm,
                     w1a_ref, w3a_ref, w2a_ref, w1b_ref, w3b_ref, w2b_ref, y_ref, xbuf, sems, *, tm):
    i = pl.program_id(0)
    n_used = nused_ref[0]
    slot = i % 2

    @pl.when(jnp.logical_and(i == 0, n_used > 0))
    def _():
        _start_rows(src_ref, hx_hbm, xbuf.at[0], sems.at[0], tm)

    @pl.when(i + 1 < n_used)
    def _():
        _start_rows(src_next_ref, hx_hbm, xbuf.at[1 - slot], sems.at[1 - slot], tm)

    @pl.when(i < n_used)
    def _():
        _wait_rows(hx_hbm, xbuf.at[slot], sems.at[slot], tm)
        x = xbuf[slot]
        h = _unpack_halves(x[:, :HX_WORDS])
        comb = x[:, HX_WORDS:]
        lane = lax.broadcasted_iota(jnp.int32, (tm, LANES), 1)
        acc = jnp.zeros((tm, D_MODEL), F32)
        for e, w1_ref, w3_ref, w2_ref in ((lo_ref[i], w1a_ref, w3a_ref, w2a_ref), (hi_ref[i], w1b_ref, w3b_ref, w2b_ref)):
            a = jnp.dot(h, w1_ref[0], preferred_element_type=F32)
            g = jnp.dot(h, w3_ref[0], preferred_element_type=F32)
            ce = jnp.sum(jnp.where(lane == e, comb, 0.0), axis=-1, keepdims=True)
            act = _silu(a) * g * ce
            acc = acc + jnp.dot(act.astype(BF16), w2_ref[0], preferred_element_type=F32)
        y_ref[...] = acc

    @pl.when(i >= n_used)
    def _():
        y_ref[...] = jnp.zeros_like(y_ref)


def _moe_pairs(hx, src, tile_lo, tile_hi, n_used, w1, w3, w2, tm):
    n_tiles = tile_lo.shape[0]
    lo_spec = lambda k, n: pl.BlockSpec((1, k, n), lambda i, lo, hi, nu: (lo[i], 0, 0))
    hi_spec = lambda k, n: pl.BlockSpec((1, k, n), lambda i, lo, hi, nu: (hi[i], 0, 0))
    src3 = src.reshape(n_tiles, 1, tm)
    return pl.pallas_call(
        functools.partial(_moe_pair_kernel, tm=tm),
        grid_spec=pltpu.PrefetchScalarGridSpec(
            num_scalar_prefetch=3,
            grid=(n_tiles,),
            in_specs=[pl.BlockSpec((1, 1, tm), lambda i, lo, hi, nu: (i, 0, 0), memory_space=pltpu.SMEM),
                      pl.BlockSpec((1, 1, tm), lambda i, lo, hi, nu: (jnp.minimum(i + 1, n_tiles - 1), 0, 0),
                                   memory_space=pltpu.SMEM),
                      pl.BlockSpec(memory_space=pl.ANY),
                      lo_spec(D_MODEL, D_EXPERT), lo_spec(D_MODEL, D_EXPERT), lo_spec(D_EXPERT, D_MODEL),
                      hi_spec(D_MODEL, D_EXPERT), hi_spec(D_MODEL, D_EXPERT), hi_spec(D_EXPERT, D_MODEL)],
            out_specs=pl.BlockSpec((tm, D_MODEL), lambda i, lo, hi, nu: (i, 0)),
            scratch_shapes=[pltpu.VMEM((2, tm, HX_WORDS + LANES), F32), pltpu.SemaphoreType.DMA((2,))]),
        out_shape=jax.ShapeDtypeStruct((n_tiles * tm, D_MODEL), F32),
        compiler_params=_params("arbitrary"),
        name="experts_paired",
    )(tile_lo, tile_hi, n_used, src3, src3, hx, w1, w3, w2, w1, w3, w2)


def _combine_kernel(pos_ref, pos_next_ref, y_hbm, x_ref, gate_ref, g2_ref, b2_ref, o_ref, ybuf, sems, *, tm, n_tiles):
    i = pl.program_id(0)
    slot = i % 2

    @pl.when(i == 0)
    def _():
        _start_rows(pos_ref, y_hbm, ybuf.at[0], sems.at[0], tm)

    @pl.when(i + 1 < n_tiles)
    def _():
        _start_rows(pos_next_ref, y_hbm, ybuf.at[1 - slot], sems.at[1 - slot], tm)

    _wait_rows(y_hbm, ybuf.at[slot], sems.at[slot], tm)
    y = ALPHA * x_ref[...] + gate_ref[...] * ybuf[slot]
    o_ref[...] = _ln(y) * g2_ref[...] + b2_ref[...]


def _combine(y_sorted, pos, x1, gate2, ln_g, ln_b, tm):
    rows = x1.shape[0]
    n_tiles = rows // tm
    row = pl.BlockSpec((tm, D_MODEL), lambda i: (i, 0))
    const = pl.BlockSpec((1, D_MODEL), lambda i: (0, 0))
    pos3 = pos.reshape(n_tiles, 1, tm)
    return pl.pallas_call(
        functools.partial(_combine_kernel, tm=tm, n_tiles=n_tiles),
        grid=(n_tiles,),
        in_specs=[pl.BlockSpec((1, 1, tm), lambda i: (i, 0, 0), memory_space=pltpu.SMEM),
                  pl.BlockSpec((1, 1, tm), lambda i: (jnp.minimum(i + 1, n_tiles - 1), 0, 0), memory_space=pltpu.SMEM),
                  pl.BlockSpec(memory_space=pl.ANY),
                  row, _row_spec(gate2.shape[0], tm, D_MODEL, rows), const, const],
        out_specs=row,
        out_shape=jax.ShapeDtypeStruct((rows, D_MODEL), F32),
        scratch_shapes=[pltpu.VMEM((2, tm, D_MODEL), F32), pltpu.SemaphoreType.DMA((2,))],
        compiler_params=_params("arbitrary"),
        name="combine",
    )(pos3, pos3, y_sorted, x1, gate2, ln_g.reshape(1, D_MODEL), ln_b.reshape(1, D_MODEL))


def _trunk_layer(x, mods, bsz, seq, pos0, s0, k_past, v_past, p, tiles):
    shift1, scale1, gate1, shift2, scale2, gate2 = mods
    uv, ret, q_c, k_c, v_c, k_bf, v_bf = _inproj(x, shift1, scale1, p["w_in"], p["layer"], tiles["tm_in"])
    a_out, v_rows = _gmlp(uv, p["w_sp"], p["b_sp"], p["ln_v_g"], p["ln_v_b"], min(seq, GMLP_CHUNK))
    b_out, s_new = _retention(ret, s0, pos0, bsz, seq, p["gn_g"], p["gn_b"])
    if k_past is None:
        c_out = _stick_breaking(q_c, k_bf, v_bf, bsz, seq, tiles["sb"])
    else:
        c_out = _stick_breaking_step(q_c, k_bf, v_bf, k_past, v_past, p["layer"], bsz, seq, tiles["sb"])
    grouped = "tm_group" in tiles
    post = _post(a_out, b_out, c_out, x, p["w_out"], p["layer"], gate1, p["ln1_g"], p["ln1_b"],
                 shift2, scale2, p["wr"], p["br"], tiles["tm_post"], grouped)
    if grouped:
        x1, comb, hx = post
        tm = tiles["tm_group"]
        pos, src, tile_lo, tile_hi, n_used = _slot_plan(comb[:, N_EXPERTS].astype(jnp.int32), tm)
        y_sorted = _moe_pairs(hx, src, tile_lo, tile_hi, n_used, p["w1"], p["w3"], p["w2"], tm)
        y = _combine(y_sorted, pos, x1, gate2, p["ln2_g"], p["ln2_b"], tiles["tm_combine"])
    else:
        x1, h2, comb = post
        y = _moe(h2, comb, x1, gate2, p["ln2_g"], p["ln2_b"], p["w1"], p["w3"], p["w2"], tiles["tm_moe"])
    return y, v_rows, s_new, k_c, v_c


def kernel(x_prompt, x_sample, cache_sb_k, cache_sb_v, state_ret, c_prompt, c_sample, w_ada, b_ada, w_in, w_out, ln_v_g, ln_v_b, w_spatial, b_spatial, gn_g, gn_b, ln1_g, ln1_b, ln2_g, ln2_b, w_router, b_router, w1, w3, w2):
    bp, tp, _ = x_prompt.shape
    bs, ts, _ = x_sample.shape
    past_len = cache_sb_k.shape[2]
    assert bp == 1

    n_c = bp + bs
    c_rows = -(-n_c // 8) * 8
    c_all = jnp.concatenate([c_prompt, c_sample, jnp.zeros((c_rows - n_c, D_MODEL), F32)], axis=0)
    mods = _adaln(c_all, w_ada, b_ada)

    wr_t = w_router.T
    cache_kt = jnp.transpose(cache_sb_k, (0, 1, 3, 4, 2))
    cache_vt = jnp.transpose(cache_sb_v, (0, 1, 3, 4, 2))
    br_col = b_router.reshape(N_EXPERTS, 1)

    tiles_p = dict(tm_in=256, sb=256, tm_post=512, tm_group=256, tm_combine=512)
    tiles_s = dict(tm_in=bs * ts, sb=256, tm_post=bs * ts, tm_moe=bs * ts)

    y_p = x_prompt.reshape(bp * tp, D_MODEL)
    y_s = x_sample.reshape(bs * ts, D_MODEL)
    zero_state = jnp.zeros((bp, H_B, HEAD_DIM, HEAD_DIM), F32)
    outs = [[] for _ in range(7)]
    for l in range(DEPTH):
        p = dict(w_in=w_in, w_out=w_out,
                 w_sp=w_spatial[l], b_sp=b_spatial[l], ln_v_g=ln_v_g[l], ln_v_b=ln_v_b[l],
                 gn_g=gn_g[l], gn_b=gn_b[l], ln1_g=ln1_g[l], ln1_b=ln1_b[l],
                 ln2_g=ln2_g[l], ln2_b=ln2_b[l], wr=wr_t, br=br_col,
                 w1=w1[l].astype(BF16), w3=w3[l].astype(BF16), w2=w2[l].astype(BF16), layer=l)
        m = mods[l]
        mods_p = [m[0:1, i * D_MODEL:(i + 1) * D_MODEL] for i in range(6)]
        mods_s = [jnp.repeat(m[bp:bp + bs, i * D_MODEL:(i + 1) * D_MODEL], ts, axis=0) for i in range(6)]
        y_p, _, s_p, k_p, v_p = _trunk_layer(y_p, mods_p, bp, tp, 0, zero_state, None, None, p, tiles_p)
        y_s, g_s, s_s, k_s, v_s = _trunk_layer(
            y_s, mods_s, bs, ts, past_len, state_ret[l],
            cache_kt, cache_vt, p, tiles_s)
        outs[0].append(s_p)
        outs[1].append(k_p.reshape(bp, tp, H_C, HEAD_DIM))
        outs[2].append(v_p.reshape(bp, tp, H_C, HEAD_DIM))
        outs[3].append(s_s)
        outs[4].append(k_s.reshape(bs, ts, H_C, HEAD_DIM))
        outs[5].append(v_s.reshape(bs, ts, H_C, HEAD_DIM))
        outs[6].append(g_s.reshape(bs, ts, W_A))
    return (y_p.reshape(bp, tp, D_MODEL), y_s.reshape(bs, ts, D_MODEL)) + tuple(jnp.stack(o) for o in outs)
```

```python
import functools

import numpy as np
import jax
import jax.numpy as jnp
from jax import lax
from jax.experimental import pallas as pl
from jax.experimental.pallas import tpu as pltpu

F32 = jnp.float32
BF16 = jnp.bfloat16
HIGHEST = lax.Precision.HIGHEST

D_MODEL = 1024
DEPTH = 2
HEAD_DIM = 64
W_A = D_MODEL // 4
W_B = 3 * D_MODEL // 8
W_C = D_MODEL - W_A - W_B
H_A = W_A // HEAD_DIM
H_B = W_B // HEAD_DIM
H_C = W_C // HEAD_DIM
GMLP_CHUNK = 128
ROPE_BASE = 10000.0
N_EXPERTS = 16
N_GROUPS = 4
EXPERTS_PER_GROUP = N_EXPERTS // N_GROUPS
D_EXPERT = D_MODEL // 2
PAIRS_PER_GROUP = EXPERTS_PER_GROUP * (EXPERTS_PER_GROUP - 1) // 2
N_PAIRS = N_GROUPS * PAIRS_PER_GROUP
HX_WORDS = D_MODEL // 2
ALPHA = (2 * DEPTH) ** 0.25
LN_EPS = 1e-5
D_IN = 2 * W_A + 4 * W_B + 3 * W_C
LANES = 128
VMEM_LIMIT = 48 * 1024 * 1024

NT_DIMS = (((1,), (1,)), ((), ()))
TN_DIMS = (((0,), (0,)), ((), ()))
INV_LN2 = 1.4426950408889634
SB_QSCALE = HEAD_DIM ** -0.5 * INV_LN2
SB_DEAD = -152.0
GMLP_SUB = 8
POST_SUB = 4
RET_BLOCK = 512


def _ln(x):
    mu = jnp.mean(x, axis=-1, keepdims=True)
    xc = x - mu
    var = jnp.mean(xc * xc, axis=-1, keepdims=True)
    return xc * lax.rsqrt(var + LN_EPS)


def _silu(x):
    return x * jax.nn.sigmoid(x)


def _params(*sem):
    return pltpu.CompilerParams(dimension_semantics=sem, vmem_limit_bytes=VMEM_LIMIT)


def _row_spec(rows, tm, width, total_rows):
    if rows == 1:
        return pl.BlockSpec((1, width), lambda i: (0, 0))
    assert rows == total_rows
    return pl.BlockSpec((tm, width), lambda i: (i, 0))


def _adaln_kernel(c_ref, w_ref, b_ref, o_ref):
    sc = _silu(c_ref[...])
    o_ref[0] = jnp.dot(sc, w_ref[0], preferred_element_type=F32, precision=HIGHEST) + b_ref[0]


def _adaln(c_all, w_ada, b_ada):
    rows = c_all.shape[0]
    tn = 1536
    return pl.pallas_call(
        _adaln_kernel,
        grid=(DEPTH, 6 * D_MODEL // tn),
        in_specs=[pl.BlockSpec((rows, D_MODEL), lambda l, j: (0, 0)),
                  pl.BlockSpec((1, D_MODEL, tn), lambda l, j: (l, 0, j)),
                  pl.BlockSpec((1, 1, tn), lambda l, j: (l, 0, j))],
        out_specs=pl.BlockSpec((1, rows, tn), lambda l, j: (l, 0, j)),
        out_shape=jax.ShapeDtypeStruct((DEPTH, rows, 6 * D_MODEL), F32),
        compiler_params=_params("parallel", "parallel"),
        name="adaln",
    )(c_all, w_ada, b_ada.reshape(DEPTH, 1, 6 * D_MODEL))


def _inproj_kernel(x_ref, shift_ref, scale_ref, w_ref, uv_ref, ret_ref, q_ref, k_ref, v_ref, kb_ref, vb_ref, wb_ref):
    @pl.when(pl.program_id(0) == 0)
    def _():
        wb_ref[...] = w_ref[0].astype(BF16)

    h = _ln(x_ref[...]) * (1.0 + scale_ref[...]) + shift_ref[...]
    r = jnp.dot(h.astype(BF16), wb_ref[...], preferred_element_type=F32)
    c0 = 2 * W_A
    c1 = c0 + 4 * W_B
    uv_ref[...] = r[:, :c0]
    ret_ref[...] = r[:, c0:c1]
    q_ref[...] = r[:, c1:c1 + W_C]
    k = r[:, c1 + W_C:c1 + 2 * W_C]
    v = r[:, c1 + 2 * W_C:]
    k_ref[...] = k
    v_ref[...] = v
    kb_ref[...] = k.astype(BF16)
    vb_ref[...] = v.astype(BF16)


def _inproj(x, shift, scale, w_in, layer, tm):
    rows = x.shape[0]
    widths = (2 * W_A, 4 * W_B, W_C, W_C, W_C, W_C, W_C)
    dtypes = (F32,) * 5 + (BF16,) * 2
    return pl.pallas_call(
        _inproj_kernel,
        grid=(rows // tm,),
        in_specs=[pl.BlockSpec((tm, D_MODEL), lambda i: (i, 0)),
                  _row_spec(shift.shape[0], tm, D_MODEL, rows),
                  _row_spec(scale.shape[0], tm, D_MODEL, rows),
                  pl.BlockSpec((1, D_MODEL, D_IN), lambda i: (layer, 0, 0), pipeline_mode=pl.Buffered(1))],
        out_specs=[pl.BlockSpec((tm, w), lambda i: (i, 0)) for w in widths],
        out_shape=[jax.ShapeDtypeStruct((rows, w), dt) for w, dt in zip(widths, dtypes)],
        scratch_shapes=[pltpu.VMEM((D_MODEL, D_IN), BF16)],
        compiler_params=_params("arbitrary"),
        name="inproj",
    )(x, shift, scale, w_in)


def _gmlp_kernel(uv_ref, wsp_ref, bias_ref, g_ref, b_ref, a_ref, vn_ref, *, chunk, n_sub):
    uv = uv_ref[...]
    u = jax.nn.gelu(uv[:, :W_A])
    v = _ln(jax.nn.gelu(uv[:, W_A:])) * g_ref[...] + b_ref[...]
    vn_ref[...] = v
    row = lax.broadcasted_iota(jnp.int32, (chunk, chunk), 0)
    col = lax.broadcasted_iota(jnp.int32, (chunk, chunk), 1)
    lane_head = lax.broadcasted_iota(jnp.int32, (chunk, W_A), 1) // HEAD_DIM
    w = [jnp.where(col <= row, wsp_ref[h], 0.0).astype(BF16) for h in range(H_A)]
    for c in range(n_sub):
        rows = slice(c * chunk, (c + 1) * chunk)
        vc = v[rows]
        mixed = bias_ref[...]
        for h in range(H_A):
            vh = jnp.where(lane_head == h, vc, 0.0).astype(BF16)
            mixed = mixed + jnp.dot(w[h], vh, preferred_element_type=F32)
        a_ref[rows, :] = u[rows] * mixed


def _gmlp(uv, w_sp, b_sp, ln_g, ln_b, chunk):
    rows = uv.shape[0]
    n_sub = min(GMLP_SUB, rows // chunk)
    tm = n_sub * chunk
    wsp = w_sp[:, :chunk, :chunk]
    bias = jnp.repeat(b_sp[:, :chunk].T, HEAD_DIM, axis=1)
    return pl.pallas_call(
        functools.partial(_gmlp_kernel, chunk=chunk, n_sub=n_sub),
        grid=(rows // tm,),
        in_specs=[pl.BlockSpec((tm, 2 * W_A), lambda i: (i, 0)),
                  pl.BlockSpec((H_A, chunk, chunk), lambda i: (0, 0, 0)),
                  pl.BlockSpec((chunk, W_A), lambda i: (0, 0)),
                  pl.BlockSpec((1, W_A), lambda i: (0, 0)),
                  pl.BlockSpec((1, W_A), lambda i: (0, 0))],
        out_specs=[pl.BlockSpec((tm, W_A), lambda i: (i, 0)),
                   pl.BlockSpec((tm, W_A), lambda i: (i, 0))],
        out_shape=[jax.ShapeDtypeStruct((rows, W_A), F32),
                   jax.ShapeDtypeStruct((rows, W_A), F32)],
        compiler_params=_params("parallel"),
        name="gmlp",
    )(uv, wsp, bias, ln_g.reshape(1, W_A), ln_b.reshape(1, W_A))


def _rope(x, cos, sin):
    lane = lax.broadcasted_iota(jnp.int32, (x.shape[0], LANES), 1)
    first_half = (lane & (HEAD_DIM // 2)) == 0
    parts = []
    for c in range(x.shape[1] // LANES):
        xc = x[:, c * LANES:(c + 1) * LANES]
        rot = jnp.where(first_half,
                        pltpu.roll(xc, LANES - HEAD_DIM // 2, 1),
                        pltpu.roll(xc, HEAD_DIM // 2, 1))
        parts.append(xc * cos + rot * sin)
    return jnp.concatenate(parts, axis=1)


def _ret_kernel(r_ref, cos_ref, sin_ref, qdec_ref, kdec_ref, dec_ref, blk_ref, s0_ref,
                gng_ref, gnb_ref, o_ref, sout_ref, s_scr, o_scr, *, n_blocks):
    n = pl.program_id(1)

    @pl.when(n == 0)
    def _():
        s_scr[...] = s0_ref[0]

    r = r_ref[...]
    cos = cos_ref[...]
    sin = sin_ref[...]
    qr = _rope(r[:, :W_B], cos, sin)
    kr = _rope(r[:, W_B:2 * W_B], cos, sin) * (HEAD_DIM ** -0.5)
    vb = r[:, 2 * W_B:3 * W_B].astype(BF16)
    gate = r[:, 3 * W_B:]
    qb = qr.astype(BF16)
    kb = kr.astype(BF16)
    qdb = (qr * qdec_ref[...]).astype(BF16)
    kdb = (kr * kdec_ref[...]).astype(BF16)
    for h in range(H_B):
        sl = slice(h * HEAD_DIM, (h + 1) * HEAD_DIM)
        scores = lax.dot_general(qb[:, sl], kb[:, sl], NT_DIMS, preferred_element_type=F32) * dec_ref[h]
        s_h = s_scr[h]
        o_h = (jnp.dot(scores.astype(BF16), vb[:, sl], preferred_element_type=F32)
               + jnp.dot(qdb[:, sl], s_h.astype(BF16), preferred_element_type=F32))
        s_scr[h] = s_h * blk_ref[h] + lax.dot_general(kdb[:, sl], vb[:, sl], TN_DIMS,
                                                      preferred_element_type=F32)
        o_scr[:, sl] = _ln(o_h)
    o_ref[...] = (o_scr[...] * gng_ref[...] + gnb_ref[...]) * _silu(gate)

    @pl.when(n == n_blocks - 1)
    def _():
        sout_ref[0] = s_scr[...]


def _retention(ret, s0, pos0, bsz, seq, gn_g, gn_b):
    L = min(seq, RET_BLOCK)
    n_blocks = seq // L
    half = HEAD_DIM // 2
    inv = ROPE_BASE ** (-jnp.arange(half, dtype=F32) / half)
    ang = (pos0 + jnp.arange(seq)).astype(F32)[:, None] * inv[None, :]
    cos, sin = jnp.cos(ang), jnp.sin(ang)
    cos_t = jnp.tile(jnp.concatenate([cos, cos], axis=1), (1, LANES // HEAD_DIM))
    sin_t = jnp.tile(jnp.concatenate([-sin, sin], axis=1), (1, LANES // HEAD_DIM))
    log_g = jnp.log1p(-jnp.exp2(-5.0 - jnp.arange(H_B, dtype=F32)))
    idx = jnp.arange(L, dtype=F32)
    diff = idx[:, None] - idx[None, :]
    decay = jnp.where(diff >= 0, jnp.exp(diff[None] * log_g[:, None, None]), 0.0)
    q_decay = jnp.exp((idx[None, :] + 1.0) * log_g[:, None])
    k_decay = jnp.exp((L - 1.0 - idx[None, :]) * log_g[:, None])
    blk_decay = jnp.exp(L * log_g)
    qdec = jnp.repeat(q_decay.T, HEAD_DIM, axis=1)
    kdec = jnp.repeat(k_decay.T, HEAD_DIM, axis=1)
    blk = jnp.broadcast_to(blk_decay[:, None, None], (H_B, HEAD_DIM, HEAD_DIM))
    const2 = lambda b, n: (0, 0)
    const3 = lambda b, n: (0, 0, 0)
    return pl.pallas_call(
        functools.partial(_ret_kernel, n_blocks=n_blocks),
        grid=(bsz, n_blocks),
        in_specs=[pl.BlockSpec((L, 4 * W_B), lambda b, n: (b * n_blocks + n, 0)),
                  pl.BlockSpec((L, LANES), lambda b, n: (n, 0)),
                  pl.BlockSpec((L, LANES), lambda b, n: (n, 0)),
                  pl.BlockSpec((L, W_B), const2),
                  pl.BlockSpec((L, W_B), const2),
                  pl.BlockSpec((H_B, L, L), const3),
                  pl.BlockSpec((H_B, HEAD_DIM, HEAD_DIM), const3),
                  pl.BlockSpec((1, H_B, HEAD_DIM, HEAD_DIM), lambda b, n: (b, 0, 0, 0)),
                  pl.BlockSpec((1, W_B), const2),
                  pl.BlockSpec((1, W_B), const2)],
        out_specs=[pl.BlockSpec((L, W_B), lambda b, n: (b * n_blocks + n, 0)),
                   pl.BlockSpec((1, H_B, HEAD_DIM, HEAD_DIM), lambda b, n: (b, 0, 0, 0))],
        out_shape=[jax.ShapeDtypeStruct((bsz * seq, W_B), F32),
                   jax.ShapeDtypeStruct((bsz, H_B, HEAD_DIM, HEAD_DIM), F32)],
        scratch_shapes=[pltpu.VMEM((H_B, HEAD_DIM, HEAD_DIM), F32),
                        pltpu.VMEM((L, W_B), F32)],
        compiler_params=_params("parallel", "arbitrary"),
        name="retention",
    )(ret, cos_t, sin_t, qdec, kdec, decay, blk, s0, gn_g.reshape(1, W_B), gn_b.reshape(1, W_B))


def _sb_block(qm_ref, k_ref, v_ref, u2_ref, acc_ref, carry_ref, causal):
    tk = k_ref.shape[0]
    half = lax.broadcasted_iota(jnp.int32, (tk, LANES), 1) // HEAD_DIM
    u2 = u2_ref[...]
    kp = [k_ref[:, p * LANES:(p + 1) * LANES] for p in range(H_C // 2)]

    def scores(h):
        z = lax.dot_general(qm_ref[h], kp[h // 2], NT_DIMS, preferred_element_type=F32)
        neg_abs = pltpu.bitcast(pltpu.bitcast(z, jnp.uint32) | jnp.uint32(0x80000000), F32)
        ls_pos = jnp.minimum(z, 0.0) - jnp.log(1.0 + jnp.exp2(neg_abs)) * INV_LN2
        log_stay = ls_pos - z
        if causal is not None:
            log_stay = jnp.where(causal, log_stay, 0.0)
        hi = log_stay.astype(BF16)
        lo = (log_stay - hi.astype(F32)).astype(BF16)
        return ls_pos, log_stay[:, :1], jnp.concatenate([hi, lo], axis=1)

    def cumsum(hi_lo):
        return jnp.dot(hi_lo, u2, preferred_element_type=F32)

    def weigh(h, ls_pos, first_col, excl):
        carry = carry_ref[h]
        att = jnp.exp2(ls_pos + excl + carry)
        if causal is not None:
            att = jnp.where(causal, att, 0.0)
        vp = v_ref[:, (h // 2) * LANES:(h // 2 + 1) * LANES]
        vp = jnp.where(half == h % 2, vp, jnp.zeros_like(vp))
        carry_ref[h] = carry + (excl[:, :1] + first_col)
        return jnp.dot(att.astype(BF16), vp, preferred_element_type=F32)

    stage_a, stage_b, outs = {}, {}, {}
    for step in range(H_C + 2):
        if step - 2 >= 0:
            h = step - 2
            outs[h] = weigh(h, stage_a[h][0], stage_a[h][1], stage_b.pop(h))
            del stage_a[h]
            if h % 2 == 1:
                p = h // 2
                acc_ref[:, p * LANES:(p + 1) * LANES] += outs.pop(h - 1) + outs.pop(h)
        if 0 <= step - 1 < H_C:
            stage_b[step - 1] = cumsum(stage_a[step - 1][2])
        if step < H_C:
            stage_a[step] = scores(step)


def _sb_kernel(q_ref, k_ref, v_ref, u2_ref, o_ref, qm_ref, acc_ref, carry_ref, *, t):
    i = pl.program_id(1)
    acc_ref[...] = jnp.zeros_like(acc_ref)
    carry_ref[...] = jnp.zeros_like(carry_ref)
    half = lax.broadcasted_iota(jnp.int32, (t, LANES), 1) // HEAD_DIM
    for h in range(H_C):
        qp = q_ref[:, (h // 2) * LANES:(h // 2 + 1) * LANES] * SB_QSCALE
        qm_ref[h] = jnp.where(half == h % 2, qp, 0.0).astype(BF16)

    def block(j, causal):
        k0 = pl.multiple_of((i - j) * t, t)
        _sb_block(qm_ref, k_ref.at[pl.ds(k0, t), :], v_ref.at[pl.ds(k0, t), :], u2_ref, acc_ref, carry_ref, causal)

    block(0, lax.broadcasted_iota(jnp.int32, (t, t), 1) < lax.broadcasted_iota(jnp.int32, (t, t), 0))

    def key_block(state):
        j, _ = state
        block(j, None)
        dead = jnp.max(carry_ref[...]) < SB_DEAD
        return j + 1, dead.astype(jnp.int32)

    lax.while_loop(lambda state: jnp.logical_and(state[0] <= i, state[1] == 0),
                   key_block, (jnp.int32(1), jnp.int32(0)))
    o_ref[...] = acc_ref[...]


def _stick_breaking(q, k, v, bsz, seq, t):
    nq = seq // t
    tri = np.tril(np.ones((t, t), np.float32), -1)
    u2 = jnp.asarray(np.concatenate([tri, tri], axis=0), dtype=BF16)
    resident = dict(pipeline_mode=pl.Buffered(1))
    return pl.pallas_call(
        functools.partial(_sb_kernel, t=t),
        grid=(bsz, nq),
        in_specs=[pl.BlockSpec((t, W_C), lambda b, i: (b * nq + i, 0)),
                  pl.BlockSpec((seq, W_C), lambda b, i: (b, 0), **resident),
                  pl.BlockSpec((seq, W_C), lambda b, i: (b, 0), **resident),
                  pl.BlockSpec((2 * t, t), lambda b, i: (0, 0), **resident)],
        out_specs=pl.BlockSpec((t, W_C), lambda b, i: (b * nq + i, 0)),
        out_shape=jax.ShapeDtypeStruct((bsz * seq, W_C), F32),
        scratch_shapes=[pltpu.VMEM((H_C, t, LANES), BF16),
                        pltpu.VMEM((t, W_C), F32),
                        pltpu.VMEM((H_C, t, 1), F32)],
        compiler_params=_params("parallel", "arbitrary"),
        name="stick_breaking",
    )(q, k, v, u2)


def _sb_scores(z, causal=None):
    neg_abs = pltpu.bitcast(pltpu.bitcast(z, jnp.uint32) | jnp.uint32(0x80000000), F32)
    ls_pos = jnp.minimum(z, 0.0) - jnp.log(1.0 + jnp.exp2(neg_abs)) * INV_LN2
    log_stay = ls_pos - z
    if causal is not None:
        log_stay = jnp.where(causal, log_stay, 0.0)
    hi = log_stay.astype(BF16)
    lo = (log_stay - hi.astype(F32)).astype(BF16)
    return ls_pos, log_stay, jnp.concatenate([hi, lo], axis=1)


def _sb_step_kernel(q_ref, kt_ref, vt_ref, kn_ref, vn_ref, u2_ref, un_ref, o_ref, acc_ref, carry_ref, *, tk):
    seq = q_ref.shape[0]
    past = kt_ref.shape[-1]
    qs = (q_ref[...] * SB_QSCALE).astype(BF16)
    row = lax.broadcasted_iota(jnp.int32, (seq, seq), 0)
    col = lax.broadcasted_iota(jnp.int32, (seq, seq), 1)
    causal = col < row
    un = un_ref[...]
    for h in range(H_C):
        sl = slice(h * HEAD_DIM, (h + 1) * HEAD_DIM)
        z = lax.dot_general(qs[:, sl], kn_ref[:, sl], NT_DIMS, preferred_element_type=F32)
        ls_pos, log_stay, hi_lo = _sb_scores(z, causal)
        excl = jnp.dot(hi_lo, un, preferred_element_type=F32)
        att = jnp.where(causal, jnp.exp2(ls_pos + excl), 0.0)
        acc_ref[h] = jnp.dot(att.astype(BF16), vn_ref[:, sl], preferred_element_type=F32)
        carry_ref[h] = excl[:, :1] + log_stay[:, :1]

    u2 = u2_ref[...]

    def key_block(state):
        j, _ = state
        k0 = pl.multiple_of(past - (j + 1) * tk, tk)
        for h in range(H_C):
            kt = kt_ref[0, 0, h, :, pl.ds(k0, tk)].astype(BF16)
            vt = vt_ref[0, 0, h, :, pl.ds(k0, tk)].astype(BF16)
            z = jnp.dot(qs[:, h * HEAD_DIM:(h + 1) * HEAD_DIM], kt, preferred_element_type=F32)
            ls_pos, log_stay, hi_lo = _sb_scores(z)
            excl = jnp.dot(hi_lo, u2, preferred_element_type=F32)
            carry = carry_ref[h]
            att = jnp.exp2(ls_pos + excl + carry)
            acc_ref[h] += lax.dot_general(att.astype(BF16), vt, NT_DIMS, preferred_element_type=F32)
            carry_ref[h] = carry + (excl[:, :1] + log_stay[:, :1])
        dead = jnp.max(carry_ref[...]) < SB_DEAD
        return j + 1, dead.astype(jnp.int32)

    lax.while_loop(lambda state: jnp.logical_and(state[0] < past // tk, state[1] == 0),
                   key_block, (jnp.int32(0), jnp.int32(0)))
    for h in range(H_C):
        o_ref[:, h * HEAD_DIM:(h + 1) * HEAD_DIM] = acc_ref[h]


def _stick_breaking_step(q, k_new, v_new, cache_kt, cache_vt, layer, bsz, seq, tk):
    past = cache_kt.shape[-1]
    assert past % tk == 0
    tri = lambda n: np.tril(np.ones((n, n), np.float32), -1)
    stacked = lambda n: jnp.asarray(np.concatenate([tri(n), tri(n)], axis=0), dtype=BF16)
    rows = pl.BlockSpec((seq, W_C), lambda b: (b, 0))
    cache = pl.BlockSpec((1, 1, H_C, HEAD_DIM, past), lambda b: (layer, b, 0, 0, 0))
    return pl.pallas_call(
        functools.partial(_sb_step_kernel, tk=tk),
        grid=(bsz,),
        in_specs=[rows, cache, cache, rows, rows,
                  pl.BlockSpec((2 * tk, tk), lambda b: (0, 0)), pl.BlockSpec((2 * seq, seq), lambda b: (0, 0))],
        out_specs=rows,
        out_shape=jax.ShapeDtypeStruct((bsz * seq, W_C), F32),
        scratch_shapes=[pltpu.VMEM((H_C, seq, HEAD_DIM), F32), pltpu.VMEM((H_C, seq, 1), F32)],
        compiler_params=_params("parallel"),
        name="stick_breaking_step",
    )(q, cache_kt, cache_vt, k_new, v_new, stacked(tk), stacked(seq))


def _route(sel, s):
    g_scores = []
    for g in range(N_GROUPS):
        a, b, c, d = sel[EXPERTS_PER_GROUP * g:EXPERTS_PER_GROUP * (g + 1)]
        ab_hi, ab_lo = jnp.maximum(a, b), jnp.minimum(a, b)
        cd_hi, cd_lo = jnp.maximum(c, d), jnp.minimum(c, d)
        top1 = jnp.maximum(ab_hi, cd_hi)
        top2 = jnp.maximum(jnp.minimum(ab_hi, cd_hi), jnp.maximum(ab_lo, cd_lo))
        g_scores.append(top1 + top2)
    best = g_scores[0]
    gi = jnp.zeros(best.shape, jnp.int32)
    for g in range(1, N_GROUPS):
        upd = g_scores[g] > best
        gi = jnp.where(upd, g, gi)
        best = jnp.where(upd, g_scores[g], best)

    def pick_group(rows, l):
        out = rows[(N_GROUPS - 1) * EXPERTS_PER_GROUP + l]
        for g in range(N_GROUPS - 2, -1, -1):
            out = jnp.where(gi == g, rows[g * EXPERTS_PER_GROUP + l], out)
        return out

    ig = [pick_group(sel, l) for l in range(EXPERTS_PER_GROUP)]
    sg = [pick_group(s, l) for l in range(EXPERTS_PER_GROUP)]
    b1 = ig[0]
    i1 = jnp.zeros(best.shape, jnp.int32)
    for l in range(1, EXPERTS_PER_GROUP):
        upd = ig[l] > b1
        i1 = jnp.where(upd, l, i1)
        b1 = jnp.where(upd, ig[l], b1)
    b2 = jnp.full(best.shape, -jnp.inf, F32)
    i2 = jnp.zeros(best.shape, jnp.int32)
    for l in range(EXPERTS_PER_GROUP):
        upd = jnp.logical_and(i1 != l, ig[l] > b2)
        i2 = jnp.where(upd, l, i2)
        b2 = jnp.where(upd, ig[l], b2)

    def pick_local(idx):
        out = sg[EXPERTS_PER_GROUP - 1]
        for l in range(EXPERTS_PER_GROUP - 2, -1, -1):
            out = jnp.where(idx == l, sg[l], out)
        return out

    w1 = pick_local(i1)
    w2 = pick_local(i2)
    tot = w1 + w2
    return gi * EXPERTS_PER_GROUP + i1, gi * EXPERTS_PER_GROUP + i2, w1 / tot, w2 / tot


def _pack_halves(h):
    half = h.shape[1] // 2
    hb = h.astype(BF16).astype(F32)
    hi = pltpu.bitcast(hb[:, :half], jnp.uint32)
    lo = pltpu.bitcast(hb[:, half:], jnp.uint32) >> 16
    return pltpu.bitcast(hi | lo, F32)


def _unpack_halves(w):
    bits = pltpu.bitcast(w, jnp.uint32)
    first = pltpu.bitcast(bits & jnp.uint32(0xFFFF0000), F32).astype(BF16)
    second = pltpu.bitcast(bits << 16, F32).astype(BF16)
    return jnp.concatenate([first, second], axis=1)


def _pair_class(e1, e2):
    lo = jnp.minimum(e1, e2)
    hi = jnp.maximum(e1, e2)
    g = lo // EXPERTS_PER_GROUP
    llo = lo - g * EXPERTS_PER_GROUP
    lhi = hi - g * EXPERTS_PER_GROUP
    return g * PAIRS_PER_GROUP + ((llo * (2 * EXPERTS_PER_GROUP - 1 - llo)) >> 1) + (lhi - llo - 1)


def _post_kernel(a_ref, b_ref, c_ref, x_ref, wo_ref, gate_ref, g1_ref, b1_ref, sh2_ref, sc2_ref,
                 wrt_ref, br_ref, x1_ref, *rest, tm, with_hx):
    if with_hx:
        h2_ref = None
        comb_ref, hx_ref, wob_ref = rest
    else:
        hx_ref = None
        h2_ref, comb_ref, wob_ref = rest
    @pl.when(pl.program_id(0) == 0)
    def _():
        wob_ref[...] = wo_ref[0].astype(BF16)

    sub = tm // POST_SUB if tm % (POST_SUB * LANES) == 0 else tm
    tiles = [slice(r0, r0 + sub) for r0 in range(0, tm, sub)]
    per_row = lambda ref, rows: ref[rows, :] if ref.shape[0] == tm else ref[...]
    proj = [jnp.dot(a_ref[rows, :].astype(BF16), wob_ref[:W_A], preferred_element_type=F32)
            + jnp.dot(b_ref[rows, :].astype(BF16), wob_ref[W_A:W_A + W_B], preferred_element_type=F32)
            + jnp.dot(c_ref[rows, :].astype(BF16), wob_ref[W_A + W_B:], preferred_element_type=F32)
            for rows in tiles]
    x1 = [_ln(ALPHA * x_ref[rows, :] + per_row(gate_ref, rows) * pr) * g1_ref[...] + b1_ref[...]
          for rows, pr in zip(tiles, proj)]
    for rows, v in zip(tiles, x1):
        x1_ref[rows, :] = v
    h2 = [_ln(v) * (1.0 + per_row(sc2_ref, rows)) + per_row(sh2_ref, rows) for rows, v in zip(tiles, x1)]
    logits = [lax.dot_general(wrt_ref[...], v, NT_DIMS, preferred_element_type=F32, precision=HIGHEST) for v in h2]
    for rows, v, logits_t in zip(tiles, h2, logits):
        if h2_ref is not None:
            h2_ref[rows, :] = v.astype(BF16)
        s_t = jax.nn.sigmoid(logits_t)
        sel_t = s_t + br_ref[...]
        s = [s_t[e:e + 1, :] for e in range(N_EXPERTS)]
        sel = [sel_t[e:e + 1, :] for e in range(N_EXPERTS)]
        e1, e2, w1, w2 = _route(sel, s)
        expert = lax.broadcasted_iota(jnp.int32, (LANES, sub), 0)
        comb_t = jnp.where(expert == e1, w1, jnp.where(expert == e2, w2, 0.0))
        comb_t = jnp.where(expert == N_EXPERTS, _pair_class(e1, e2).astype(F32), comb_t)
        comb_ref[rows, :] = comb_t.T
        if hx_ref is not None:
            hx_ref[rows, :HX_WORDS] = _pack_halves(v)
            hx_ref[rows, HX_WORDS:] = comb_t.T


def _post(a, b, c, x, w_out, layer, gate1, ln_g, ln_b, shift2, scale2, wr_t, br_col, tm, with_hx):
    rows = x.shape[0]
    if with_hx:
        tail_spec = [pl.BlockSpec((tm, LANES), lambda i: (i, 0)), pl.BlockSpec((tm, HX_WORDS + LANES), lambda i: (i, 0))]
        tail_shape = [jax.ShapeDtypeStruct((rows, LANES), F32), jax.ShapeDtypeStruct((rows, HX_WORDS + LANES), F32)]
    else:
        tail_spec = [pl.BlockSpec((tm, D_MODEL), lambda i: (i, 0)), pl.BlockSpec((tm, LANES), lambda i: (i, 0))]
        tail_shape = [jax.ShapeDtypeStruct((rows, D_MODEL), BF16), jax.ShapeDtypeStruct((rows, LANES), F32)]
    row = lambda w: pl.BlockSpec((tm, w), lambda i: (i, 0))
    const = lambda r, w: pl.BlockSpec((r, w), lambda i: (0, 0))
    return pl.pallas_call(
        functools.partial(_post_kernel, tm=tm, with_hx=with_hx),
        grid=(rows // tm,),
        in_specs=[row(W_A), row(W_B), row(W_C), row(D_MODEL),
                  pl.BlockSpec((1, D_MODEL, D_MODEL), lambda i: (layer, 0, 0), pipeline_mode=pl.Buffered(1)),
                  _row_spec(gate1.shape[0], tm, D_MODEL, rows),
                  const(1, D_MODEL), const(1, D_MODEL),
                  _row_spec(shift2.shape[0], tm, D_MODEL, rows),
                  _row_spec(scale2.shape[0], tm, D_MODEL, rows),
                  const(N_EXPERTS, D_MODEL), const(N_EXPERTS, 1)],
        out_specs=[row(D_MODEL)] + tail_spec,
        out_shape=[jax.ShapeDtypeStruct((rows, D_MODEL), F32)] + tail_shape,
        scratch_shapes=[pltpu.VMEM((D_MODEL, D_MODEL), BF16)],
        compiler_params=_params("arbitrary"),
        name="post_mix",
    )(a, b, c, x, w_out, gate1, ln_g.reshape(1, D_MODEL), ln_b.reshape(1, D_MODEL),
      shift2, scale2, wr_t, br_col)


def _moe_kernel(h_ref, comb_ref, x_ref, gate_ref, g2_ref, b2_ref, w1_ref, w3_ref, w2_ref,
                o_ref, acc_ref, *, tm):
    e = pl.program_id(1)

    @pl.when(e == 0)
    def _():
        acc_ref[...] = jnp.zeros_like(acc_ref)

    h = h_ref[...]
    a = jnp.dot(h, w1_ref[0], preferred_element_type=F32)
    g = jnp.dot(h, w3_ref[0], preferred_element_type=F32)
    lane = lax.broadcasted_iota(jnp.int32, (tm, LANES), 1)
    ce = jnp.sum(jnp.where(lane == e, comb_ref[...], 0.0), axis=-1, keepdims=True)
    act = _silu(a) * g * ce
    acc_ref[...] += jnp.dot(act.astype(BF16), w2_ref[0], preferred_element_type=F32)

    @pl.when(e == N_EXPERTS - 1)
    def _():
        y = ALPHA * x_ref[...] + gate_ref[...] * acc_ref[...]
        o_ref[...] = _ln(y) * g2_ref[...] + b2_ref[...]


def _moe(h2, comb, x1, gate2, ln_g, ln_b, w1, w3, w2, tm):
    rows = x1.shape[0]
    row = lambda w: pl.BlockSpec((tm, w), lambda i, e: (i, 0))
    const = pl.BlockSpec((1, D_MODEL), lambda i, e: (0, 0))
    gate_spec = (pl.BlockSpec((1, D_MODEL), lambda i, e: (0, 0)) if gate2.shape[0] == 1
                 else row(D_MODEL))
    return pl.pallas_call(
        functools.partial(_moe_kernel, tm=tm),
        grid=(rows // tm, N_EXPERTS),
        in_specs=[row(D_MODEL), row(LANES), row(D_MODEL), gate_spec, const, const,
                  pl.BlockSpec((1, D_MODEL, D_EXPERT), lambda i, e: (e, 0, 0)),
                  pl.BlockSpec((1, D_MODEL, D_EXPERT), lambda i, e: (e, 0, 0)),
                  pl.BlockSpec((1, D_EXPERT, D_MODEL), lambda i, e: (e, 0, 0))],
        out_specs=row(D_MODEL),
        out_shape=jax.ShapeDtypeStruct((rows, D_MODEL), F32),
        scratch_shapes=[pltpu.VMEM((tm, D_MODEL), F32)],
        compiler_params=_params("parallel", "arbitrary"),
        name="experts",
    )(h2, comb, x1, gate2, ln_g.reshape(1, D_MODEL), ln_b.reshape(1, D_MODEL), w1, w3, w2)


def _pair_experts():
    lo, hi = [], []
    for g in range(N_GROUPS):
        for a in range(EXPERTS_PER_GROUP):
            for b in range(a + 1, EXPERTS_PER_GROUP):
                lo.append(g * EXPERTS_PER_GROUP + a)
                hi.append(g * EXPERTS_PER_GROUP + b)
    return np.asarray(lo, np.int32), np.asarray(hi, np.int32)


def _slot_plan(cls, tm):
    rows = cls.shape[0]
    n_tiles = rows // tm + N_PAIRS
    onehot = (cls[:, None] == jnp.arange(N_PAIRS, dtype=jnp.int32)[None, :]).astype(jnp.int32)
    rank = jnp.cumsum(onehot, axis=0) - onehot
    tiles_per = (jnp.sum(onehot, axis=0) + tm - 1) // tm
    tile_start = jnp.cumsum(tiles_per) - tiles_per
    pos = jnp.sum(onehot * (tile_start[None, :] * tm + rank), axis=1).astype(jnp.int32)
    src = jnp.zeros((n_tiles * tm,), jnp.int32).at[pos].set(jnp.arange(rows, dtype=jnp.int32))
    tile = jnp.arange(n_tiles, dtype=jnp.int32)
    tile_class = jnp.clip(jnp.sum((tile[:, None] >= tile_start[None, :]).astype(jnp.int32), axis=1) - 1,
                          0, N_PAIRS - 1)
    pair_lo, pair_hi = _pair_experts()
    n_used = jnp.sum(tiles_per).astype(jnp.int32).reshape(1)
    return pos, src, jnp.asarray(pair_lo)[tile_class], jnp.asarray(pair_hi)[tile_class], n_used


def _start_rows(idx_ref, src_hbm, dst_ref, sem, n):
    for r in range(n):
        pltpu.make_async_copy(src_hbm.at[pl.ds(idx_ref[0, 0, r], 1)], dst_ref.at[pl.ds(r, 1)], sem).start(priority=r % 2)


def _wait_rows(src_hbm, dst_ref, sem, n):
    pltpu.make_async_copy(src_hbm.at[pl.ds(0, n)], dst_ref, sem).wait()


def _moe_pair_kernel(lo_ref, hi_ref, nused_ref, src_ref, src_next_ref, hx_hbm,
                     w1a_ref, w3a_ref, w2a_ref, w1b_ref, w3b_ref, w2b_ref, y_ref, xbuf, sems, *, tm):
    i = pl.program_id(0)
    n_used = nused_ref[0]
    slot = i % 2

    @pl.when(jnp.logical_and(i == 0, n_used > 0))
    def _():
        _start_rows(src_ref, hx_hbm, xbuf.at[0], sems.at[0], tm)

    @pl.when(i + 1 < n_used)
    def _():
        _start_rows(src_next_ref, hx_hbm, xbuf.at[1 - slot], sems.at[1 - slot], tm)

    @pl.when(i < n_used)
    def _():
        _wait_rows(hx_hbm, xbuf.at[slot], sems.at[slot], tm)
        x = xbuf[slot]
        h = _unpack_halves(x[:, :HX_WORDS])
        comb = x[:, HX_WORDS:]
        lane = lax.broadcasted_iota(jnp.int32, (tm, LANES), 1)
        acc = jnp.zeros((tm, D_MODEL), F32)
        for e, w1_ref, w3_ref, w2_ref in ((lo_ref[i], w1a_ref, w3a_ref, w2a_ref), (hi_ref[i], w1b_ref, w3b_ref, w2b_ref)):
            a = jnp.dot(h, w1_ref[0], preferred_element_type=F32)
            g = jnp.dot(h, w3_ref[0], preferred_element_type=F32)
            ce = jnp.sum(jnp.where(lane == e, comb, 0.0), axis=-1, keepdims=True)
            act = _silu(a) * g * ce
            acc = acc + jnp.dot(act.astype(BF16), w2_ref[0], preferred_element_type=F32)
        y_ref[...] = acc

    @pl.when(i >= n_used)
    def _():
        y_ref[...] = jnp.zeros_like(y_ref)


def _moe_pairs(hx, src, tile_lo, tile_hi, n_used, w1, w3, w2, tm):
    n_tiles = tile_lo.shape[0]
    lo_spec = lambda k, n: pl.BlockSpec((1, k, n), lambda i, lo, hi, nu: (lo[i], 0, 0))
    hi_spec = lambda k, n: pl.BlockSpec((1, k, n), lambda i, lo, hi, nu: (hi[i], 0, 0))
    src3 = src.reshape(n_tiles, 1, tm)
    return pl.pallas_call(
        functools.partial(_moe_pair_kernel, tm=tm),
        grid_spec=pltpu.PrefetchScalarGridSpec(
            num_scalar_prefetch=3,
            grid=(n_tiles,),
            in_specs=[pl.BlockSpec((1, 1, tm), lambda i, lo, hi, nu: (i, 0, 0), memory_space=pltpu.SMEM),
                      pl.BlockSpec((1, 1, tm), lambda i, lo, hi, nu: (jnp.minimum(i + 1, n_tiles - 1), 0, 0),
                                   memory_space=pltpu.SMEM),
                      pl.BlockSpec(memory_space=pl.ANY),
                      lo_spec(D_MODEL, D_EXPERT), lo_spec(D_MODEL, D_EXPERT), lo_spec(D_EXPERT, D_MODEL),
                      hi_spec(D_MODEL, D_EXPERT), hi_spec(D_MODEL, D_EXPERT), hi_spec(D_EXPERT, D_MODEL)],
            out_specs=pl.BlockSpec((tm, D_MODEL), lambda i, lo, hi, nu: (i, 0)),
            scratch_shapes=[pltpu.VMEM((2, tm, HX_WORDS + LANES), F32), pltpu.SemaphoreType.DMA((2,))]),
        out_shape=jax.ShapeDtypeStruct((n_tiles * tm, D_MODEL), F32),
        compiler_params=_params("arbitrary"),
        name="experts_paired",
    )(tile_lo, tile_hi, n_used, src3, src3, hx, w1, w3, w2, w1, w3, w2)


def _combine_kernel(pos_ref, pos_next_ref, y_hbm, x_ref, gate_ref, g2_ref, b2_ref, o_ref, ybuf, sems, *, tm, n_tiles):
    i = pl.program_id(0)
    slot = i % 2

    @pl.when(i == 0)
    def _():
        _start_rows(pos_ref, y_hbm, ybuf.at[0], sems.at[0], tm)

    @pl.when(i + 1 < n_tiles)
    def _():
        _start_rows(pos_next_ref, y_hbm, ybuf.at[1 - slot], sems.at[1 - slot], tm)

    _wait_rows(y_hbm, ybuf.at[slot], sems.at[slot], tm)
    y = ALPHA * x_ref[...] + gate_ref[...] * ybuf[slot]
    o_ref[...] = _ln(y) * g2_ref[...] + b2_ref[...]


def _combine(y_sorted, pos, x1, gate2, ln_g, ln_b, tm):
    rows = x1.shape[0]
    n_tiles = rows // tm
    row = pl.BlockSpec((tm, D_MODEL), lambda i: (i, 0))
    const = pl.BlockSpec((1, D_MODEL), lambda i: (0, 0))
    pos3 = pos.reshape(n_tiles, 1, tm)
    return pl.pallas_call(
        functools.partial(_combine_kernel, tm=tm, n_tiles=n_tiles),
        grid=(n_tiles,),
        in_specs=[pl.BlockSpec((1, 1, tm), lambda i: (i, 0, 0), memory_space=pltpu.SMEM),
                  pl.BlockSpec((1, 1, tm), lambda i: (jnp.minimum(i + 1, n_tiles - 1), 0, 0), memory_space=pltpu.SMEM),
                  pl.BlockSpec(memory_space=pl.ANY),
                  row, _row_spec(gate2.shape[0], tm, D_MODEL, rows), const, const],
        out_specs=row,
        out_shape=jax.ShapeDtypeStruct((rows, D_MODEL), F32),
        scratch_shapes=[pltpu.VMEM((2, tm, D_MODEL), F32), pltpu.SemaphoreType.DMA((2,))],
        compiler_params=_params("arbitrary"),
        name="combine",
    )(pos3, pos3, y_sorted, x1, gate2, ln_g.reshape(1, D_MODEL), ln_b.reshape(1, D_MODEL))


def _trunk_layer(x, mods, bsz, seq, pos0, s0, k_past, v_past, p, tiles):
    shift1, scale1, gate1, shift2, scale2, gate2 = mods
    uv, ret, q_c, k_c, v_c, k_bf, v_bf = _inproj(x, shift1, scale1, p["w_in"], p["layer"], tiles["tm_in"])
    a_out, v_rows = _gmlp(uv, p["w_sp"], p["b_sp"], p["ln_v_g"], p["ln_v_b"], min(seq, GMLP_CHUNK))
    b_out, s_new = _retention(ret, s0, pos0, bsz, seq, p["gn_g"], p["gn_b"])
    if k_past is None:
        c_out = _stick_breaking(q_c, k_bf, v_bf, bsz, seq, tiles["sb"])
    else:
        c_out = _stick_breaking_step(q_c, k_bf, v_bf, k_past, v_past, p["layer"], bsz, seq, tiles["sb"])
    grouped = "tm_group" in tiles
    post = _post(a_out, b_out, c_out, x, p["w_out"], p["layer"], gate1, p["ln1_g"], p["ln1_b"],
                 shift2, scale2, p["wr"], p["br"], tiles["tm_post"], grouped)
    if grouped:
        x1, comb, hx = post
        tm = tiles["tm_group"]
        pos, src, tile_lo, tile_hi, n_used = _slot_plan(comb[:, N_EXPERTS].astype(jnp.int32), tm)
        y_sorted = _moe_pairs(hx, src, tile_lo, tile_hi, n_used, p["w1"], p["w3"], p["w2"], tm)
        y = _combine(y_sorted, pos, x1, gate2, p["ln2_g"], p["ln2_b"], tiles["tm_combine"])
    else:
        x1, h2, comb = post
        y = _moe(h2, comb, x1, gate2, p["ln2_g"], p["ln2_b"], p["w1"], p["w3"], p["w2"], tiles["tm_moe"])
    return y, v_rows, s_new, k_c, v_c


def kernel(x_prompt, x_sample, cache_sb_k, cache_sb_v, state_ret, c_prompt, c_sample, w_ada, b_ada, w_in, w_out, ln_v_g, ln_v_b, w_spatial, b_spatial, gn_g, gn_b, ln1_g, ln1_b, ln2_g, ln2_b, w_router, b_router, w1, w3, w2):
    bp, tp, _ = x_prompt.shape
    bs, ts, _ = x_sample.shape
    past_len = cache_sb_k.shape[2]
    assert bp == 1

    n_c = bp + bs
    c_rows = -(-n_c // 8) * 8
    c_all = jnp.concatenate([c_prompt, c_sample, jnp.zeros((c_rows - n_c, D_MODEL), F32)], axis=0)
    mods = _adaln(c_all, w_ada, b_ada)

    wr_t = w_router.T
    cache_kt = jnp.transpose(cache_sb_k, (0, 1, 3, 4, 2))
    cache_vt = jnp.transpose(cache_sb_v, (0, 1, 3, 4, 2))
    br_col = b_router.reshape(N_EXPERTS, 1)

    tiles_p = dict(tm_in=256, sb=256, tm_post=512, tm_group=256, tm_combine=1024)
    tiles_s = dict(tm_in=bs * ts, sb=256, tm_post=bs * ts, tm_moe=bs * ts)

    y_p = x_prompt.reshape(bp * tp, D_MODEL)
    y_s = x_sample.reshape(bs * ts, D_MODEL)
    zero_state = jnp.zeros((bp, H_B, HEAD_DIM, HEAD_DIM), F32)
    outs = [[] for _ in range(7)]
    for l in range(DEPTH):
        p = dict(w_in=w_in, w_out=w_out,
                 w_sp=w_spatial[l], b_sp=b_spatial[l], ln_v_g=ln_v_g[l], ln_v_b=ln_v_b[l],
                 gn_g=gn_g[l], gn_b=gn_b[l], ln1_g=ln1_g[l], ln1_b=ln1_b[l],
                 ln2_g=ln2_g[l], ln2_b=ln2_b[l], wr=wr_t, br=br_col,
                 w1=w1[l].astype(BF16), w3=w3[l].astype(BF16), w2=w2[l].astype(BF16), layer=l)
        m = mods[l]
        mods_p = [m[0:1, i * D_MODEL:(i + 1) * D_MODEL] for i in range(6)]
        mods_s = [jnp.repeat(m[bp:bp + bs, i * D_MODEL:(i + 1) * D_MODEL], ts, axis=0) for i in range(6)]
        y_p, _, s_p, k_p, v_p = _trunk_layer(y_p, mods_p, bp, tp, 0, zero_state, None, None, p, tiles_p)
        y_s, g_s, s_s, k_s, v_s = _trunk_layer(
            y_s, mods_s, bs, ts, past_len, state_ret[l],
            cache_kt, cache_vt, p, tiles_s)
        outs[0].append(s_p)
        outs[1].append(k_p.reshape(bp, tp, H_C, HEAD_DIM))
        outs[2].append(v_p.reshape(bp, tp, H_C, HEAD_DIM))
        outs[3].append(s_s)
        outs[4].append(k_s.reshape(bs, ts, H_C, HEAD_DIM))
        outs[5].append(v_s.reshape(bs, ts, H_C, HEAD_DIM))
        outs[6].append(g_s.reshape(bs, ts, W_A))
    return (y_p.reshape(bp, tp, D_MODEL), y_s.reshape(bs, ts, D_MODEL)) + tuple(jnp.stack(o) for o in outs)
```

```python
import functools

import numpy as np
import jax
import jax.numpy as jnp
from jax import lax
from jax.experimental import pallas as pl
from jax.experimental.pallas import tpu as pltpu

F32 = jnp.float32
BF16 = jnp.bfloat16
HIGHEST = lax.Precision.HIGHEST

D_MODEL = 1024
DEPTH = 2
HEAD_DIM = 64
W_A = D_MODEL // 4
W_B = 3 * D_MODEL // 8
W_C = D_MODEL - W_A - W_B
H_A = W_A // HEAD_DIM
H_B = W_B // HEAD_DIM
H_C = W_C // HEAD_DIM
GMLP_CHUNK = 128
ROPE_BASE = 10000.0
N_EXPERTS = 16
N_GROUPS = 4
EXPERTS_PER_GROUP = N_EXPERTS // N_GROUPS
D_EXPERT = D_MODEL // 2
PAIRS_PER_GROUP = EXPERTS_PER_GROUP * (EXPERTS_PER_GROUP - 1) // 2
N_PAIRS = N_GROUPS * PAIRS_PER_GROUP
HX_WORDS = D_MODEL // 2
ALPHA = (2 * DEPTH) ** 0.25
LN_EPS = 1e-5
D_IN = 2 * W_A + 4 * W_B + 3 * W_C
LANES = 128
VMEM_LIMIT = 48 * 1024 * 1024

NT_DIMS = (((1,), (1,)), ((), ()))
TN_DIMS = (((0,), (0,)), ((), ()))
INV_LN2 = 1.4426950408889634
SB_QSCALE = HEAD_DIM ** -0.5 * INV_LN2
SB_DEAD = -152.0
GMLP_SUB = 8
POST_SUB = 4
RET_BLOCK = 512


def _ln(x):
    mu = jnp.mean(x, axis=-1, keepdims=True)
    xc = x - mu
    var = jnp.mean(xc * xc, axis=-1, keepdims=True)
    return xc * lax.rsqrt(var + LN_EPS)


def _silu(x):
    return x * jax.nn.sigmoid(x)


def _params(*sem):
    return pltpu.CompilerParams(dimension_semantics=sem, vmem_limit_bytes=VMEM_LIMIT)


def _row_spec(rows, tm, width, total_rows):
    if rows == 1:
        return pl.BlockSpec((1, width), lambda i: (0, 0))
    assert rows == total_rows
    return pl.BlockSpec((tm, width), lambda i: (i, 0))


def _adaln_kernel(c_ref, w_ref, b_ref, o_ref):
    sc = _silu(c_ref[...])
    o_ref[0] = jnp.dot(sc, w_ref[0], preferred_element_type=F32, precision=HIGHEST) + b_ref[0]


def _adaln(c_all, w_ada, b_ada):
    rows = c_all.shape[0]
    tn = 1536
    return pl.pallas_call(
        _adaln_kernel,
        grid=(DEPTH, 6 * D_MODEL // tn),
        in_specs=[pl.BlockSpec((rows, D_MODEL), lambda l, j: (0, 0)),
                  pl.BlockSpec((1, D_MODEL, tn), lambda l, j: (l, 0, j)),
                  pl.BlockSpec((1, 1, tn), lambda l, j: (l, 0, j))],
        out_specs=pl.BlockSpec((1, rows, tn), lambda l, j: (l, 0, j)),
        out_shape=jax.ShapeDtypeStruct((DEPTH, rows, 6 * D_MODEL), F32),
        compiler_params=_params("parallel", "parallel"),
        name="adaln",
    )(c_all, w_ada, b_ada.reshape(DEPTH, 1, 6 * D_MODEL))


def _inproj_kernel(x_ref, shift_ref, scale_ref, w_ref, uv_ref, ret_ref, q_ref, k_ref, v_ref, kb_ref, vb_ref, wb_ref):
    @pl.when(pl.program_id(0) == 0)
    def _():
        wb_ref[...] = w_ref[0].astype(BF16)

    h = _ln(x_ref[...]) * (1.0 + scale_ref[...]) + shift_ref[...]
    r = jnp.dot(h.astype(BF16), wb_ref[...], preferred_element_type=F32)
    c0 = 2 * W_A
    c1 = c0 + 4 * W_B
    uv_ref[...] = r[:, :c0]
    ret_ref[...] = r[:, c0:c1]
    q_ref[...] = r[:, c1:c1 + W_C]
    k = r[:, c1 + W_C:c1 + 2 * W_C]
    v = r[:, c1 + 2 * W_C:]
    k_ref[...] = k
    v_ref[...] = v
    kb_ref[...] = k.astype(BF16)
    vb_ref[...] = v.astype(BF16)


def _inproj(x, shift, scale, w_in, layer, tm):
    rows = x.shape[0]
    widths = (2 * W_A, 4 * W_B, W_C, W_C, W_C, W_C, W_C)
    dtypes = (F32,) * 5 + (BF16,) * 2
    return pl.pallas_call(
        _inproj_kernel,
        grid=(rows // tm,),
        in_specs=[pl.BlockSpec((tm, D_MODEL), lambda i: (i, 0)),
                  _row_spec(shift.shape[0], tm, D_MODEL, rows),
                  _row_spec(scale.shape[0], tm, D_MODEL, rows),
                  pl.BlockSpec((1, D_MODEL, D_IN), lambda i: (layer, 0, 0), pipeline_mode=pl.Buffered(1))],
        out_specs=[pl.BlockSpec((tm, w), lambda i: (i, 0)) for w in widths],
        out_shape=[jax.ShapeDtypeStruct((rows, w), dt) for w, dt in zip(widths, dtypes)],
        scratch_shapes=[pltpu.VMEM((D_MODEL, D_IN), BF16)],
        compiler_params=_params("arbitrary"),
        name="inproj",
    )(x, shift, scale, w_in)


def _gmlp_kernel(uv_ref, wsp_ref, bias_ref, g_ref, b_ref, a_ref, vn_ref, *, chunk, n_sub):
    uv = uv_ref[...]
    u = jax.nn.gelu(uv[:, :W_A])
    v = _ln(jax.nn.gelu(uv[:, W_A:])) * g_ref[...] + b_ref[...]
    vn_ref[...] = v
    row = lax.broadcasted_iota(jnp.int32, (chunk, chunk), 0)
    col = lax.broadcasted_iota(jnp.int32, (chunk, chunk), 1)
    lane_head = lax.broadcasted_iota(jnp.int32, (chunk, W_A), 1) // HEAD_DIM
    w = [jnp.where(col <= row, wsp_ref[h], 0.0).astype(BF16) for h in range(H_A)]
    for c in range(n_sub):
        rows = slice(c * chunk, (c + 1) * chunk)
        vc = v[rows]
        mixed = bias_ref[...]
        for h in range(H_A):
            vh = jnp.where(lane_head == h, vc, 0.0).astype(BF16)
            mixed = mixed + jnp.dot(w[h], vh, preferred_element_type=F32)
        a_ref[rows, :] = u[rows] * mixed


def _gmlp(uv, w_sp, b_sp, ln_g, ln_b, chunk):
    rows = uv.shape[0]
    n_sub = min(GMLP_SUB, rows // chunk)
    tm = n_sub * chunk
    wsp = w_sp[:, :chunk, :chunk]
    bias = jnp.repeat(b_sp[:, :chunk].T, HEAD_DIM, axis=1)
    return pl.pallas_call(
        functools.partial(_gmlp_kernel, chunk=chunk, n_sub=n_sub),
        grid=(rows // tm,),
        in_specs=[pl.BlockSpec((tm, 2 * W_A), lambda i: (i, 0)),
                  pl.BlockSpec((H_A, chunk, chunk), lambda i: (0, 0, 0)),
                  pl.BlockSpec((chunk, W_A), lambda i: (0, 0)),
                  pl.BlockSpec((1, W_A), lambda i: (0, 0)),
                  pl.BlockSpec((1, W_A), lambda i: (0, 0))],
        out_specs=[pl.BlockSpec((tm, W_A), lambda i: (i, 0)),
                   pl.BlockSpec((tm, W_A), lambda i: (i, 0))],
        out_shape=[jax.ShapeDtypeStruct((rows, W_A), F32),
                   jax.ShapeDtypeStruct((rows, W_A), F32)],
        compiler_params=_params("parallel"),
        name="gmlp",
    )(uv, wsp, bias, ln_g.reshape(1, W_A), ln_b.reshape(1, W_A))


def _rope(x, cos, sin):
    lane = lax.broadcasted_iota(jnp.int32, (x.shape[0], LANES), 1)
    first_half = (lane & (HEAD_DIM // 2)) == 0
    parts = []
    for c in range(x.shape[1] // LANES):
        xc = x[:, c * LANES:(c + 1) * LANES]
        rot = jnp.where(first_half,
                        pltpu.roll(xc, LANES - HEAD_DIM // 2, 1),
                        pltpu.roll(xc, HEAD_DIM // 2, 1))
        parts.append(xc * cos + rot * sin)
    return jnp.concatenate(parts, axis=1)


def _ret_kernel(r_ref, cos_ref, sin_ref, qdec_ref, kdec_ref, dec_ref, blk_ref, s0_ref,
                gng_ref, gnb_ref, o_ref, sout_ref, s_scr, o_scr, *, n_blocks):
    n = pl.program_id(1)

    @pl.when(n == 0)
    def _():
        s_scr[...] = s0_ref[0]

    r = r_ref[...]
    cos = cos_ref[...]
    sin = sin_ref[...]
    qr = _rope(r[:, :W_B], cos, sin)
    kr = _rope(r[:, W_B:2 * W_B], cos, sin) * (HEAD_DIM ** -0.5)
    vb = r[:, 2 * W_B:3 * W_B].astype(BF16)
    gate = r[:, 3 * W_B:]
    qb = qr.astype(BF16)
    kb = kr.astype(BF16)
    qdb = (qr * qdec_ref[...]).astype(BF16)
    kdb = (kr * kdec_ref[...]).astype(BF16)
    for h in range(H_B):
        sl = slice(h * HEAD_DIM, (h + 1) * HEAD_DIM)
        scores = lax.dot_general(qb[:, sl], kb[:, sl], NT_DIMS, preferred_element_type=F32) * dec_ref[h]
        s_h = s_scr[h]
        o_h = (jnp.dot(scores.astype(BF16), vb[:, sl], preferred_element_type=F32)
               + jnp.dot(qdb[:, sl], s_h.astype(BF16), preferred_element_type=F32))
        s_scr[h] = s_h * blk_ref[h] + lax.dot_general(kdb[:, sl], vb[:, sl], TN_DIMS,
                                                      preferred_element_type=F32)
        o_scr[:, sl] = _ln(o_h)
    o_ref[...] = (o_scr[...] * gng_ref[...] + gnb_ref[...]) * _silu(gate)

    @pl.when(n == n_blocks - 1)
    def _():
        sout_ref[0] = s_scr[...]


def _retention(ret, s0, pos0, bsz, seq, gn_g, gn_b):
    L = min(seq, RET_BLOCK)
    n_blocks = seq // L
    half = HEAD_DIM // 2
    inv = ROPE_BASE ** (-jnp.arange(half, dtype=F32) / half)
    ang = (pos0 + jnp.arange(seq)).astype(F32)[:, None] * inv[None, :]
    cos, sin = jnp.cos(ang), jnp.sin(ang)
    cos_t = jnp.tile(jnp.concatenate([cos, cos], axis=1), (1, LANES // HEAD_DIM))
    sin_t = jnp.tile(jnp.concatenate([-sin, sin], axis=1), (1, LANES // HEAD_DIM))
    log_g = jnp.log1p(-jnp.exp2(-5.0 - jnp.arange(H_B, dtype=F32)))
    idx = jnp.arange(L, dtype=F32)
    diff = idx[:, None] - idx[None, :]
    decay = jnp.where(diff >= 0, jnp.exp(diff[None] * log_g[:, None, None]), 0.0)
    q_decay = jnp.exp((idx[None, :] + 1.0) * log_g[:, None])
    k_decay = jnp.exp((L - 1.0 - idx[None, :]) * log_g[:, None])
    blk_decay = jnp.exp(L * log_g)
    qdec = jnp.repeat(q_decay.T, HEAD_DIM, axis=1)
    kdec = jnp.repeat(k_decay.T, HEAD_DIM, axis=1)
    blk = jnp.broadcast_to(blk_decay[:, None, None], (H_B, HEAD_DIM, HEAD_DIM))
    const2 = lambda b, n: (0, 0)
    const3 = lambda b, n: (0, 0, 0)
    return pl.pallas_call(
        functools.partial(_ret_kernel, n_blocks=n_blocks),
        grid=(bsz, n_blocks),
        in_specs=[pl.BlockSpec((L, 4 * W_B), lambda b, n: (b * n_blocks + n, 0)),
                  pl.BlockSpec((L, LANES), lambda b, n: (n, 0)),
                  pl.BlockSpec((L, LANES), lambda b, n: (n, 0)),
                  pl.BlockSpec((L, W_B), const2),
                  pl.BlockSpec((L, W_B), const2),
                  pl.BlockSpec((H_B, L, L), const3),
                  pl.BlockSpec((H_B, HEAD_DIM, HEAD_DIM), const3),
                  pl.BlockSpec((1, H_B, HEAD_DIM, HEAD_DIM), lambda b, n: (b, 0, 0, 0)),
                  pl.BlockSpec((1, W_B), const2),
                  pl.BlockSpec((1, W_B), const2)],
        out_specs=[pl.BlockSpec((L, W_B), lambda b, n: (b * n_blocks + n, 0)),
                   pl.BlockSpec((1, H_B, HEAD_DIM, HEAD_DIM), lambda b, n: (b, 0, 0, 0))],
        out_shape=[jax.ShapeDtypeStruct((bsz * seq, W_B), F32),
                   jax.ShapeDtypeStruct((bsz, H_B, HEAD_DIM, HEAD_DIM), F32)],
        scratch_shapes=[pltpu.VMEM((H_B, HEAD_DIM, HEAD_DIM), F32),
                        pltpu.VMEM((L, W_B), F32)],
        compiler_params=_params("parallel", "arbitrary"),
        name="retention",
    )(ret, cos_t, sin_t, qdec, kdec, decay, blk, s0, gn_g.reshape(1, W_B), gn_b.reshape(1, W_B))


def _sb_block(qm_ref, k_ref, v_ref, u2_ref, acc_ref, carry_ref, causal):
    tk = k_ref.shape[0]
    half = lax.broadcasted_iota(jnp.int32, (tk, LANES), 1) // HEAD_DIM
    u2 = u2_ref[...]
    kp = [k_ref[:, p * LANES:(p + 1) * LANES] for p in range(H_C // 2)]

    def scores(h):
        z = lax.dot_general(qm_ref[h], kp[h // 2], NT_DIMS, preferred_element_type=F32)
        neg_abs = pltpu.bitcast(pltpu.bitcast(z, jnp.uint32) | jnp.uint32(0x80000000), F32)
        ls_pos = jnp.minimum(z, 0.0) - jnp.log(1.0 + jnp.exp2(neg_abs)) * INV_LN2
        log_stay = ls_pos - z
        if causal is not None:
            log_stay = jnp.where(causal, log_stay, 0.0)
        hi = log_stay.astype(BF16)
        lo = (log_stay - hi.astype(F32)).astype(BF16)
        return ls_pos, log_stay[:, :1], jnp.concatenate([hi, lo], axis=1)

    def cumsum(hi_lo):
        return jnp.dot(hi_lo, u2, preferred_element_type=F32)

    def weigh(h, ls_pos, first_col, excl):
        carry = carry_ref[h]
        att = jnp.exp2(ls_pos + excl + carry)
        if causal is not None:
            att = jnp.where(causal, att, 0.0)
        vp = v_ref[:, (h // 2) * LANES:(h // 2 + 1) * LANES]
        vp = jnp.where(half == h % 2, vp, jnp.zeros_like(vp))
        carry_ref[h] = carry + (excl[:, :1] + first_col)
        return jnp.dot(att.astype(BF16), vp, preferred_element_type=F32)

    stage_a, stage_b, outs = {}, {}, {}
    for step in range(H_C + 2):
        if step - 2 >= 0:
            h = step - 2
            outs[h] = weigh(h, stage_a[h][0], stage_a[h][1], stage_b.pop(h))
            del stage_a[h]
            if h % 2 == 1:
                p = h // 2
                acc_ref[:, p * LANES:(p + 1) * LANES] += outs.pop(h - 1) + outs.pop(h)
        if 0 <= step - 1 < H_C:
            stage_b[step - 1] = cumsum(stage_a[step - 1][2])
        if step < H_C:
            stage_a[step] = scores(step)


def _sb_kernel(q_ref, k_ref, v_ref, u2_ref, o_ref, qm_ref, acc_ref, carry_ref, *, t):
    i = pl.program_id(1)
    acc_ref[...] = jnp.zeros_like(acc_ref)
    carry_ref[...] = jnp.zeros_like(carry_ref)
    half = lax.broadcasted_iota(jnp.int32, (t, LANES), 1) // HEAD_DIM
    for h in range(H_C):
        qp = q_ref[:, (h // 2) * LANES:(h // 2 + 1) * LANES] * SB_QSCALE
        qm_ref[h] = jnp.where(half == h % 2, qp, 0.0).astype(BF16)

    def block(j, causal):
        k0 = pl.multiple_of((i - j) * t, t)
        _sb_block(qm_ref, k_ref.at[pl.ds(k0, t), :], v_ref.at[pl.ds(k0, t), :], u2_ref, acc_ref, carry_ref, causal)

    block(0, lax.broadcasted_iota(jnp.int32, (t, t), 1) < lax.broadcasted_iota(jnp.int32, (t, t), 0))

    def key_block(state):
        j, _ = state
        block(j, None)
        dead = jnp.max(carry_ref[...]) < SB_DEAD
        return j + 1, dead.astype(jnp.int32)

    lax.while_loop(lambda state: jnp.logical_and(state[0] <= i, state[1] == 0),
                   key_block, (jnp.int32(1), jnp.int32(0)))
    o_ref[...] = acc_ref[...]


def _stick_breaking(q, k, v, bsz, seq, t):
    nq = seq // t
    tri = np.tril(np.ones((t, t), np.float32), -1)
    u2 = jnp.asarray(np.concatenate([tri, tri], axis=0), dtype=BF16)
    resident = dict(pipeline_mode=pl.Buffered(1))
    return pl.pallas_call(
        functools.partial(_sb_kernel, t=t),
        grid=(bsz, nq),
        in_specs=[pl.BlockSpec((t, W_C), lambda b, i: (b * nq + i, 0)),
                  pl.BlockSpec((seq, W_C), lambda b, i: (b, 0), **resident),
                  pl.BlockSpec((seq, W_C), lambda b, i: (b, 0), **resident),
                  pl.BlockSpec((2 * t, t), lambda b, i: (0, 0), **resident)],
        out_specs=pl.BlockSpec((t, W_C), lambda b, i: (b * nq + i, 0)),
        out_shape=jax.ShapeDtypeStruct((bsz * seq, W_C), F32),
        scratch_shapes=[pltpu.VMEM((H_C, t, LANES), BF16),
                        pltpu.VMEM((t, W_C), F32),
                        pltpu.VMEM((H_C, t, 1), F32)],
        compiler_params=_params("parallel", "arbitrary"),
        name="stick_breaking",
    )(q, k, v, u2)


def _sb_scores(z, causal=None):
    neg_abs = pltpu.bitcast(pltpu.bitcast(z, jnp.uint32) | jnp.uint32(0x80000000), F32)
    ls_pos = jnp.minimum(z, 0.0) - jnp.log(1.0 + jnp.exp2(neg_abs)) * INV_LN2
    log_stay = ls_pos - z
    if causal is not None:
        log_stay = jnp.where(causal, log_stay, 0.0)
    hi = log_stay.astype(BF16)
    lo = (log_stay - hi.astype(F32)).astype(BF16)
    return ls_pos, log_stay, jnp.concatenate([hi, lo], axis=1)


def _sb_step_kernel(q_ref, kt_ref, vt_ref, kn_ref, vn_ref, u2_ref, un_ref, o_ref, acc_ref, carry_ref, *, tk):
    seq = q_ref.shape[0]
    past = kt_ref.shape[-1]
    qs = (q_ref[...] * SB_QSCALE).astype(BF16)
    row = lax.broadcasted_iota(jnp.int32, (seq, seq), 0)
    col = lax.broadcasted_iota(jnp.int32, (seq, seq), 1)
    causal = col < row
    un = un_ref[...]
    for h in range(H_C):
        sl = slice(h * HEAD_DIM, (h + 1) * HEAD_DIM)
        z = lax.dot_general(qs[:, sl], kn_ref[:, sl], NT_DIMS, preferred_element_type=F32)
        ls_pos, log_stay, hi_lo = _sb_scores(z, causal)
        excl = jnp.dot(hi_lo, un, preferred_element_type=F32)
        att = jnp.where(causal, jnp.exp2(ls_pos + excl), 0.0)
        acc_ref[h] = jnp.dot(att.astype(BF16), vn_ref[:, sl], preferred_element_type=F32)
        carry_ref[h] = excl[:, :1] + log_stay[:, :1]

    u2 = u2_ref[...]

    def key_block(state):
        j, _ = state
        k0 = pl.multiple_of(past - (j + 1) * tk, tk)
        for h in range(H_C):
            kt = kt_ref[0, 0, h, :, pl.ds(k0, tk)].astype(BF16)
            vt = vt_ref[0, 0, h, :, pl.ds(k0, tk)].astype(BF16)
            z = jnp.dot(qs[:, h * HEAD_DIM:(h + 1) * HEAD_DIM], kt, preferred_element_type=F32)
            ls_pos, log_stay, hi_lo = _sb_scores(z)
            excl = jnp.dot(hi_lo, u2, preferred_element_type=F32)
            carry = carry_ref[h]
            att = jnp.exp2(ls_pos + excl + carry)
            acc_ref[h] += lax.dot_general(att.astype(BF16), vt, NT_DIMS, preferred_element_type=F32)
            carry_ref[h] = carry + (excl[:, :1] + log_stay[:, :1])
        dead = jnp.max(carry_ref[...]) < SB_DEAD
        return j + 1, dead.astype(jnp.int32)

    lax.while_loop(lambda state: jnp.logical_and(state[0] < past // tk, state[1] == 0),
                   key_block, (jnp.int32(0), jnp.int32(0)))
    for h in range(H_C):
        o_ref[:, h * HEAD_DIM:(h + 1) * HEAD_DIM] = acc_ref[h]


def _stick_breaking_step(q, k_new, v_new, cache_kt, cache_vt, layer, bsz, seq, tk):
    past = cache_kt.shape[-1]
    assert past % tk == 0
    tri = lambda n: np.tril(np.ones((n, n), np.float32), -1)
    stacked = lambda n: jnp.asarray(np.concatenate([tri(n), tri(n)], axis=0), dtype=BF16)
    rows = pl.BlockSpec((seq, W_C), lambda b: (b, 0))
    cache = pl.BlockSpec((1, 1, H_C, HEAD_DIM, past), lambda b: (layer, b, 0, 0, 0))
    return pl.pallas_call(
        functools.partial(_sb_step_kernel, tk=tk),
        grid=(bsz,),
        in_specs=[rows, cache, cache, rows, rows,
                  pl.BlockSpec((2 * tk, tk), lambda b: (0, 0)), pl.BlockSpec((2 * seq, seq), lambda b: (0, 0))],
        out_specs=rows,
        out_shape=jax.ShapeDtypeStruct((bsz * seq, W_C), F32),
        scratch_shapes=[pltpu.VMEM((H_C, seq, HEAD_DIM), F32), pltpu.VMEM((H_C, seq, 1), F32)],
        compiler_params=_params("parallel"),
        name="stick_breaking_step",
    )(q, cache_kt, cache_vt, k_new, v_new, stacked(tk), stacked(seq))


def _route(sel, s):
    g_scores = []
    for g in range(N_GROUPS):
        a, b, c, d = sel[EXPERTS_PER_GROUP * g:EXPERTS_PER_GROUP * (g + 1)]
        ab_hi, ab_lo = jnp.maximum(a, b), jnp.minimum(a, b)
        cd_hi, cd_lo = jnp.maximum(c, d), jnp.minimum(c, d)
        top1 = jnp.maximum(ab_hi, cd_hi)
        top2 = jnp.maximum(jnp.minimum(ab_hi, cd_hi), jnp.maximum(ab_lo, cd_lo))
        g_scores.append(top1 + top2)
    best = g_scores[0]
    gi = jnp.zeros(best.shape, jnp.int32)
    for g in range(1, N_GROUPS):
        upd = g_scores[g] > best
        gi = jnp.where(upd, g, gi)
        best = jnp.where(upd, g_scores[g], best)

    def pick_group(rows, l):
        out = rows[(N_GROUPS - 1) * EXPERTS_PER_GROUP + l]
        for g in range(N_GROUPS - 2, -1, -1):
            out = jnp.where(gi == g, rows[g * EXPERTS_PER_GROUP + l], out)
        return out

    ig = [pick_group(sel, l) for l in range(EXPERTS_PER_GROUP)]
    sg = [pick_group(s, l) for l in range(EXPERTS_PER_GROUP)]
    b1 = ig[0]
    i1 = jnp.zeros(best.shape, jnp.int32)
    for l in range(1, EXPERTS_PER_GROUP):
        upd = ig[l] > b1
        i1 = jnp.where(upd, l, i1)
        b1 = jnp.where(upd, ig[l], b1)
    b2 = jnp.full(best.shape, -jnp.inf, F32)
    i2 = jnp.zeros(best.shape, jnp.int32)
    for l in range(EXPERTS_PER_GROUP):
        upd = jnp.logical_and(i1 != l, ig[l] > b2)
        i2 = jnp.where(upd, l, i2)
        b2 = jnp.where(upd, ig[l], b2)

    def pick_local(idx):
        out = sg[EXPERTS_PER_GROUP - 1]
        for l in range(EXPERTS_PER_GROUP - 2, -1, -1):
            out = jnp.where(idx == l, sg[l], out)
        return out

    w1 = pick_local(i1)
    w2 = pick_local(i2)
    tot = w1 + w2
    return gi * EXPERTS_PER_GROUP + i1, gi * EXPERTS_PER_GROUP + i2, w1 / tot, w2 / tot


def _pack_halves(h):
    half = h.shape[1] // 2
    hb = h.astype(BF16).astype(F32)
    hi = pltpu.bitcast(hb[:, :half], jnp.uint32)
    lo = pltpu.bitcast(hb[:, half:], jnp.uint32) >> 16
    return pltpu.bitcast(hi | lo, F32)


def _unpack_halves(w):
    bits = pltpu.bitcast(w, jnp.uint32)
    first = pltpu.bitcast(bits & jnp.uint32(0xFFFF0000), F32).astype(BF16)
    second = pltpu.bitcast(bits << 16, F32).astype(BF16)
    return jnp.concatenate([first, second], axis=1)


def _pair_class(e1, e2):
    lo = jnp.minimum(e1, e2)
    hi = jnp.maximum(e1, e2)
    g = lo // EXPERTS_PER_GROUP
    llo = lo - g * EXPERTS_PER_GROUP
    lhi = hi - g * EXPERTS_PER_GROUP
    return g * PAIRS_PER_GROUP + ((llo * (2 * EXPERTS_PER_GROUP - 1 - llo)) >> 1) + (lhi - llo - 1)


def _post_kernel(a_ref, b_ref, c_ref, x_ref, wo_ref, gate_ref, g1_ref, b1_ref, sh2_ref, sc2_ref,
                 wrt_ref, br_ref, x1_ref, *rest, tm, with_hx):
    if with_hx:
        h2_ref = None
        comb_ref, hx_ref, wob_ref = rest
    else:
        hx_ref = None
        h2_ref, comb_ref, wob_ref = rest
    @pl.when(pl.program_id(0) == 0)
    def _():
        wob_ref[...] = wo_ref[0].astype(BF16)

    sub = tm // POST_SUB if tm % (POST_SUB * LANES) == 0 else tm
    tiles = [slice(r0, r0 + sub) for r0 in range(0, tm, sub)]
    per_row = lambda ref, rows: ref[rows, :] if ref.shape[0] == tm else ref[...]
    proj = [jnp.dot(a_ref[rows, :].astype(BF16), wob_ref[:W_A], preferred_element_type=F32)
            + jnp.dot(b_ref[rows, :].astype(BF16), wob_ref[W_A:W_A + W_B], preferred_element_type=F32)
            + jnp.dot(c_ref[rows, :].astype(BF16), wob_ref[W_A + W_B:], preferred_element_type=F32)
            for rows in tiles]
    x1 = [_ln(ALPHA * x_ref[rows, :] + per_row(gate_ref, rows) * pr) * g1_ref[...] + b1_ref[...]
          for rows, pr in zip(tiles, proj)]
    for rows, v in zip(tiles, x1):
        x1_ref[rows, :] = v
    h2 = [_ln(v) * (1.0 + per_row(sc2_ref, rows)) + per_row(sh2_ref, rows) for rows, v in zip(tiles, x1)]
    logits = [lax.dot_general(wrt_ref[...], v, NT_DIMS, preferred_element_type=F32, precision=HIGHEST) for v in h2]
    for rows, v, logits_t in zip(tiles, h2, logits):
        if h2_ref is not None:
            h2_ref[rows, :] = v.astype(BF16)
        s_t = jax.nn.sigmoid(logits_t)
        sel_t = s_t + br_ref[...]
        s = [s_t[e:e + 1, :] for e in range(N_EXPERTS)]
        sel = [sel_t[e:e + 1, :] for e in range(N_EXPERTS)]
        e1, e2, w1, w2 = _route(sel, s)
        expert = lax.broadcasted_iota(jnp.int32, (LANES, sub), 0)
        comb_t = jnp.where(expert == e1, w1, jnp.where(expert == e2, w2, 0.0))
        comb_t = jnp.where(expert == N_EXPERTS, _pair_class(e1, e2).astype(F32), comb_t)
        comb_ref[rows, :] = comb_t.T
        if hx_ref is not None:
            hx_ref[rows, :HX_WORDS] = _pack_halves(v)
            hx_ref[rows, HX_WORDS:] = comb_t.T


def _post(a, b, c, x, w_out, layer, gate1, ln_g, ln_b, shift2, scale2, wr_t, br_col, tm, with_hx):
    rows = x.shape[0]
    if with_hx:
        tail_spec = [pl.BlockSpec((tm, LANES), lambda i: (i, 0)), pl.BlockSpec((tm, HX_WORDS + LANES), lambda i: (i, 0))]
        tail_shape = [jax.ShapeDtypeStruct((rows, LANES), F32), jax.ShapeDtypeStruct((rows, HX_WORDS + LANES), F32)]
    else:
        tail_spec = [pl.BlockSpec((tm, D_MODEL), lambda i: (i, 0)), pl.BlockSpec((tm, LANES), lambda i: (i, 0))]
        tail_shape = [jax.ShapeDtypeStruct((rows, D_MODEL), BF16), jax.ShapeDtypeStruct((rows, LANES), F32)]
    row = lambda w: pl.BlockSpec((tm, w), lambda i: (i, 0))
    const = lambda r, w: pl.BlockSpec((r, w), lambda i: (0, 0))
    return pl.pallas_call(
        functools.partial(_post_kernel, tm=tm, with_hx=with_hx),
        grid=(rows // tm,),
        in_specs=[row(W_A), row(W_B), row(W_C), row(D_MODEL),
                  pl.BlockSpec((1, D_MODEL, D_MODEL), lambda i: (layer, 0, 0), pipeline_mode=pl.Buffered(1)),
                  _row_spec(gate1.shape[0], tm, D_MODEL, rows),
                  const(1, D_MODEL), const(1, D_MODEL),
                  _row_spec(shift2.shape[0], tm, D_MODEL, rows),
                  _row_spec(scale2.shape[0], tm, D_MODEL, rows),
                  const(N_EXPERTS, D_MODEL), const(N_EXPERTS, 1)],
        out_specs=[row(D_MODEL)] + tail_spec,
        out_shape=[jax.ShapeDtypeStruct((rows, D_MODEL), F32)] + tail_shape,
        scratch_shapes=[pltpu.VMEM((D_MODEL, D_MODEL), BF16)],
        compiler_params=_params("arbitrary"),
        name="post_mix",
    )(a, b, c, x, w_out, gate1, ln_g.reshape(1, D_MODEL), ln_b.reshape(1, D_MODEL),
      shift2, scale2, wr_t, br_col)


def _moe_kernel(h_ref, comb_ref, x_ref, gate_ref, g2_ref, b2_ref, w1_ref, w3_ref, w2_ref,
                o_ref, acc_ref, *, tm):
    e = pl.program_id(1)

    @pl.when(e == 0)
    def _():
        acc_ref[...] = jnp.zeros_like(acc_ref)

    h = h_ref[...]
    a = jnp.dot(h, w1_ref[0], preferred_element_type=F32)
    g = jnp.dot(h, w3_ref[0], preferred_element_type=F32)
    lane = lax.broadcasted_iota(jnp.int32, (tm, LANES), 1)
    ce = jnp.sum(jnp.where(lane == e, comb_ref[...], 0.0), axis=-1, keepdims=True)
    act = _silu(a) * g * ce
    acc_ref[...] += jnp.dot(act.astype(BF16), w2_ref[0], preferred_element_type=F32)

    @pl.when(e == N_EXPERTS - 1)
    def _():
        y = ALPHA * x_ref[...] + gate_ref[...] * acc_ref[...]
        o_ref[...] = _ln(y) * g2_ref[...] + b2_ref[...]


def _moe(h2, comb, x1, gate2, ln_g, ln_b, w1, w3, w2, tm):
    rows = x1.shape[0]
    row = lambda w: pl.BlockSpec((tm, w), lambda i, e: (i, 0))
    const = pl.BlockSpec((1, D_MODEL), lambda i, e: (0, 0))
    gate_spec = (pl.BlockSpec((1, D_MODEL), lambda i, e: (0, 0)) if gate2.shape[0] == 1
                 else row(D_MODEL))
    return pl.pallas_call(
        functools.partial(_moe_kernel, tm=tm),
        grid=(rows // tm, N_EXPERTS),
        in_specs=[row(D_MODEL), row(LANES), row(D_MODEL), gate_spec, const, const,
                  pl.BlockSpec((1, D_MODEL, D_EXPERT), lambda i, e: (e, 0, 0)),
                  pl.BlockSpec((1, D_MODEL, D_EXPERT), lambda i, e: (e, 0, 0)),
                  pl.BlockSpec((1, D_EXPERT, D_MODEL), lambda i, e: (e, 0, 0))],
        out_specs=row(D_MODEL),
        out_shape=jax.ShapeDtypeStruct((rows, D_MODEL), F32),
        scratch_shapes=[pltpu.VMEM((tm, D_MODEL), F32)],
        compiler_params=_params("parallel", "arbitrary"),
        name="experts",
    )(h2, comb, x1, gate2, ln_g.reshape(1, D_MODEL), ln_b.reshape(1, D_MODEL), w1, w3, w2)


def _pair_experts():
    lo, hi = [], []
    for g in range(N_GROUPS):
        for a in range(EXPERTS_PER_GROUP):
            for b in range(a + 1, EXPERTS_PER_GROUP):
                lo.append(g * EXPERTS_PER_GROUP + a)
                hi.append(g * EXPERTS_PER_GROUP + b)
    return np.asarray(lo, np.int32), np.asarray(hi, np.int32)


def _invert_kernel(pos_ref, src_ref):
    def clear(i, carry):
        src_ref[i] = 0
        return carry

    def put(t, carry):
        src_ref[pos_ref[t]] = t
        return carry

    lax.fori_loop(0, src_ref.shape[0], clear, 0, unroll=16)
    lax.fori_loop(0, pos_ref.shape[0], put, 0, unroll=16)


def _invert_slots(pos, n_slots):
    return pl.pallas_call(
        _invert_kernel,
        in_specs=[pl.BlockSpec(memory_space=pltpu.SMEM)],
        out_specs=pl.BlockSpec(memory_space=pltpu.SMEM),
        out_shape=jax.ShapeDtypeStruct((n_slots,), jnp.int32),
        name="invert_slots",
    )(pos)


def _slot_plan(cls, tm):
    rows = cls.shape[0]
    n_tiles = rows // tm + N_PAIRS
    onehot = (cls[:, None] == jnp.arange(N_PAIRS, dtype=jnp.int32)[None, :]).astype(jnp.int32)
    rank = jnp.cumsum(onehot, axis=0) - onehot
    tiles_per = (jnp.sum(onehot, axis=0) + tm - 1) // tm
    tile_start = jnp.cumsum(tiles_per) - tiles_per
    pos = jnp.sum(onehot * (tile_start[None, :] * tm + rank), axis=1).astype(jnp.int32)
    src = _invert_slots(pos, n_tiles * tm)
    tile = jnp.arange(n_tiles, dtype=jnp.int32)
    tile_class = jnp.clip(jnp.sum((tile[:, None] >= tile_start[None, :]).astype(jnp.int32), axis=1) - 1,
                          0, N_PAIRS - 1)
    pair_lo, pair_hi = _pair_experts()
    n_used = jnp.sum(tiles_per).astype(jnp.int32).reshape(1)
    return pos, src, jnp.asarray(pair_lo)[tile_class], jnp.asarray(pair_hi)[tile_class], n_used


def _start_rows(idx_ref, src_hbm, dst_ref, sem, n):
    for r in range(n):
        pltpu.make_async_copy(src_hbm.at[pl.ds(idx_ref[0, 0, r], 1)], dst_ref.at[pl.ds(r, 1)], sem).start(priority=r % 2)


def _wait_rows(src_hbm, dst_ref, sem, n):
    pltpu.make_async_copy(src_hbm.at[pl.ds(0, n)], dst_ref, sem).wait()


def _moe_pair_kernel(lo_ref, hi_ref, nused_ref, src_ref, src_next_ref, hx_hbm,
                     w1a_ref, w3a_ref, w2a_ref, w1b_ref, w3b_ref, w2b_ref, y_ref, xbuf, sems, *, tm):
    i = pl.program_id(0)
    n_used = nused_ref[0]
    slot = i % 2

    @pl.when(jnp.logical_and(i == 0, n_used > 0))
    def _():
        _start_rows(src_ref, hx_hbm, xbuf.at[0], sems.at[0], tm)

    @pl.when(i + 1 < n_used)
    def _():
        _start_rows(src_next_ref, hx_hbm, xbuf.at[1 - slot], sems.at[1 - slot], tm)

    @pl.when(i < n_used)
    def _():
        _wait_rows(hx_hbm, xbuf.at[slot], sems.at[slot], tm)
        x = xbuf[slot]
        h = _unpack_halves(x[:, :HX_WORDS])
        comb = x[:, HX_WORDS:]
        lane = lax.broadcasted_iota(jnp.int32, (tm, LANES), 1)
        acc = jnp.zeros((tm, D_MODEL), F32)
        for e, w1_ref, w3_ref, w2_ref in ((lo_ref[i], w1a_ref, w3a_ref, w2a_ref), (hi_ref[i], w1b_ref, w3b_ref, w2b_ref)):
            a = jnp.dot(h, w1_ref[0], preferred_element_type=F32)
            g = jnp.dot(h, w3_ref[0], preferred_element_type=F32)
            ce = jnp.sum(jnp.where(lane == e, comb, 0.0), axis=-1, keepdims=True)
            act = _silu(a) * g * ce
            acc = acc + jnp.dot(act.astype(BF16), w2_ref[0], preferred_element_type=F32)
        y_ref[...] = acc

    @pl.when(i >= n_used)
    def _():
        y_ref[...] = jnp.zeros_like(y_ref)


def _moe_pairs(hx, src, tile_lo, tile_hi, n_used, w1, w3, w2, tm):
    n_tiles = tile_lo.shape[0]
    lo_spec = lambda k, n: pl.BlockSpec((1, k, n), lambda i, lo, hi, nu: (lo[i], 0, 0))
    hi_spec = lambda k, n: pl.BlockSpec((1, k, n), lambda i, lo, hi, nu: (hi[i], 0, 0))
    src3 = src.reshape(n_tiles, 1, tm)
    return pl.pallas_call(
        functools.partial(_moe_pair_kernel, tm=tm),
        grid_spec=pltpu.PrefetchScalarGridSpec(
            num_scalar_prefetch=3,
            grid=(n_tiles,),
            in_specs=[pl.BlockSpec((1, 1, tm), lambda i, lo, hi, nu: (i, 0, 0), memory_space=pltpu.SMEM),
                      pl.BlockSpec((1, 1, tm), lambda i, lo, hi, nu: (jnp.minimum(i + 1, n_tiles - 1), 0, 0),
                                   memory_space=pltpu.SMEM),
                      pl.BlockSpec(memory_space=pl.ANY),
                      lo_spec(D_MODEL, D_EXPERT), lo_spec(D_MODEL, D_EXPERT), lo_spec(D_EXPERT, D_MODEL),
                      hi_spec(D_MODEL, D_EXPERT), hi_spec(D_MODEL, D_EXPERT), hi_spec(D_EXPERT, D_MODEL)],
            out_specs=pl.BlockSpec((tm, D_MODEL), lambda i, lo, hi, nu: (i, 0)),
            scratch_shapes=[pltpu.VMEM((2, tm, HX_WORDS + LANES), F32), pltpu.SemaphoreType.DMA((2,))]),
        out_shape=jax.ShapeDtypeStruct((n_tiles * tm, D_MODEL), F32),
        compiler_params=_params("arbitrary"),
        name="experts_paired",
    )(tile_lo, tile_hi, n_used, src3, src3, hx, w1, w3, w2, w1, w3, w2)


def _combine_kernel(pos_ref, pos_next_ref, y_hbm, x_ref, gate_ref, g2_ref, b2_ref, o_ref, ybuf, sems, *, tm, n_tiles):
    i = pl.program_id(0)
    slot = i % 2

    @pl.when(i == 0)
    def _():
        _start_rows(pos_ref, y_hbm, ybuf.at[0], sems.at[0], tm)

    @pl.when(i + 1 < n_tiles)
    def _():
        _start_rows(pos_next_ref, y_hbm, ybuf.at[1 - slot], sems.at[1 - slot], tm)

    _wait_rows(y_hbm, ybuf.at[slot], sems.at[slot], tm)
    y = ALPHA * x_ref[...] + gate_ref[...] * ybuf[slot]
    o_ref[...] = _ln(y) * g2_ref[...] + b2_ref[...]


def _combine(y_sorted, pos, x1, gate2, ln_g, ln_b, tm):
    rows = x1.shape[0]
    n_tiles = rows // tm
    row = pl.BlockSpec((tm, D_MODEL), lambda i: (i, 0))
    const = pl.BlockSpec((1, D_MODEL), lambda i: (0, 0))
    pos3 = pos.reshape(n_tiles, 1, tm)
    return pl.pallas_call(
        functools.partial(_combine_kernel, tm=tm, n_tiles=n_tiles),
        grid=(n_tiles,),
        in_specs=[pl.BlockSpec((1, 1, tm), lambda i: (i, 0, 0), memory_space=pltpu.SMEM),
                  pl.BlockSpec((1, 1, tm), lambda i: (jnp.minimum(i + 1, n_tiles - 1), 0, 0), memory_space=pltpu.SMEM),
                  pl.BlockSpec(memory_space=pl.ANY),
                  row, _row_spec(gate2.shape[0], tm, D_MODEL, rows), const, const],
        out_specs=row,
        out_shape=jax.ShapeDtypeStruct((rows, D_MODEL), F32),
        scratch_shapes=[pltpu.VMEM((2, tm, D_MODEL), F32), pltpu.SemaphoreType.DMA((2,))],
        compiler_params=_params("arbitrary"),
        name="combine",
    )(pos3, pos3, y_sorted, x1, gate2, ln_g.reshape(1, D_MODEL), ln_b.reshape(1, D_MODEL))


def _trunk_layer(x, mods, bsz, seq, pos0, s0, k_past, v_past, p, tiles):
    shift1, scale1, gate1, shift2, scale2, gate2 = mods
    uv, ret, q_c, k_c, v_c, k_bf, v_bf = _inproj(x, shift1, scale1, p["w_in"], p["layer"], tiles["tm_in"])
    a_out, v_rows = _gmlp(uv, p["w_sp"], p["b_sp"], p["ln_v_g"], p["ln_v_b"], min(seq, GMLP_CHUNK))
    b_out, s_new = _retention(ret, s0, pos0, bsz, seq, p["gn_g"], p["gn_b"])
    if k_past is None:
        c_out = _stick_breaking(q_c, k_bf, v_bf, bsz, seq, tiles["sb"])
    else:
        c_out = _stick_breaking_step(q_c, k_bf, v_bf, k_past, v_past, p["layer"], bsz, seq, tiles["sb"])
    grouped = "tm_group" in tiles
    post = _post(a_out, b_out, c_out, x, p["w_out"], p["layer"], gate1, p["ln1_g"], p["ln1_b"],
                 shift2, scale2, p["wr"], p["br"], tiles["tm_post"], grouped)
    if grouped:
        x1, comb, hx = post
        tm = tiles["tm_group"]
        pos, src, tile_lo, tile_hi, n_used = _slot_plan(comb[:, N_EXPERTS].astype(jnp.int32), tm)
        y_sorted = _moe_pairs(hx, src, tile_lo, tile_hi, n_used, p["w1"], p["w3"], p["w2"], tm)
        y = _combine(y_sorted, pos, x1, gate2, p["ln2_g"], p["ln2_b"], tiles["tm_combine"])
    else:
        x1, h2, comb = post
        y = _moe(h2, comb, x1, gate2, p["ln2_g"], p["ln2_b"], p["w1"], p["w3"], p["w2"], tiles["tm_moe"])
    return y, v_rows, s_new, k_c, v_c


def kernel(x_prompt, x_sample, cache_sb_k, cache_sb_v, state_ret, c_prompt, c_sample, w_ada, b_ada, w_in, w_out, ln_v_g, ln_v_b, w_spatial, b_spatial, gn_g, gn_b, ln1_g, ln1_b, ln2_g, ln2_b, w_router, b_router, w1, w3, w2):
    bp, tp, _ = x_prompt.shape
    bs, ts, _ = x_sample.shape
    past_len = cache_sb_k.shape[2]
    assert bp == 1

    n_c = bp + bs
    c_rows = -(-n_c // 8) * 8
    c_all = jnp.concatenate([c_prompt, c_sample, jnp.zeros((c_rows - n_c, D_MODEL), F32)], axis=0)
    mods = _adaln(c_all, w_ada, b_ada)

    wr_t = w_router.T
    cache_kt = jnp.transpose(cache_sb_k, (0, 1, 3, 4, 2))
    cache_vt = jnp.transpose(cache_sb_v, (0, 1, 3, 4, 2))
    br_col = b_router.reshape(N_EXPERTS, 1)

    tiles_p = dict(tm_in=256, sb=256, tm_post=512, tm_group=256, tm_combine=512)
    tiles_s = dict(tm_in=bs * ts, sb=256, tm_post=bs * ts, tm_moe=bs * ts)

    y_p = x_prompt.reshape(bp * tp, D_MODEL)
    y_s = x_sample.reshape(bs * ts, D_MODEL)
    zero_state = jnp.zeros((bp, H_B, HEAD_DIM, HEAD_DIM), F32)
    outs = [[] for _ in range(7)]
    for l in range(DEPTH):
        p = dict(w_in=w_in, w_out=w_out,
                 w_sp=w_spatial[l], b_sp=b_spatial[l], ln_v_g=ln_v_g[l], ln_v_b=ln_v_b[l],
                 gn_g=gn_g[l], gn_b=gn_b[l], ln1_g=ln1_g[l], ln1_b=ln1_b[l],
                 ln2_g=ln2_g[l], ln2_b=ln2_b[l], wr=wr_t, br=br_col,
                 w1=w1[l].astype(BF16), w3=w3[l].astype(BF16), w2=w2[l].astype(BF16), layer=l)
        m = mods[l]
        mods_p = [m[0:1, i * D_MODEL:(i + 1) * D_MODEL] for i in range(6)]
        mods_s = [jnp.repeat(m[bp:bp + bs, i * D_MODEL:(i + 1) * D_MODEL], ts, axis=0) for i in range(6)]
        y_p, _, s_p, k_p, v_p = _trunk_layer(y_p, mods_p, bp, tp, 0, zero_state, None, None, p, tiles_p)
        y_s, g_s, s_s, k_s, v_s = _trunk_layer(
            y_s, mods_s, bs, ts, past_len, state_ret[l],
            cache_kt, cache_vt, p, tiles_s)
        outs[0].append(s_p)
        outs[1].append(k_p.reshape(bp, tp, H_C, HEAD_DIM))
        outs[2].append(v_p.reshape(bp, tp, H_C, HEAD_DIM))
        outs[3].append(s_s)
        outs[4].append(k_s.reshape(bs, ts, H_C, HEAD_DIM))
        outs[5].append(v_s.reshape(bs, ts, H_C, HEAD_DIM))
        outs[6].append(g_s.reshape(bs, ts, W_A))
    return (y_p.reshape(bp, tp, D_MODEL), y_s.reshape(bs, ts, D_MODEL)) + tuple(jnp.stack(o) for o in outs)
```

```python
import functools

import numpy as np
import jax
import jax.numpy as jnp
from jax import lax
from jax.experimental import pallas as pl
from jax.experimental.pallas import tpu as pltpu

F32 = jnp.float32
BF16 = jnp.bfloat16
HIGHEST = lax.Precision.HIGHEST

D_MODEL = 1024
DEPTH = 2
HEAD_DIM = 64
W_A = D_MODEL // 4
W_B = 3 * D_MODEL // 8
W_C = D_MODEL - W_A - W_B
H_A = W_A // HEAD_DIM
H_B = W_B // HEAD_DIM
H_C = W_C // HEAD_DIM
GMLP_CHUNK = 128
ROPE_BASE = 10000.0
N_EXPERTS = 16
N_GROUPS = 4
EXPERTS_PER_GROUP = N_EXPERTS // N_GROUPS
D_EXPERT = D_MODEL // 2
PAIRS_PER_GROUP = EXPERTS_PER_GROUP * (EXPERTS_PER_GROUP - 1) // 2
N_PAIRS = N_GROUPS * PAIRS_PER_GROUP
HX_WORDS = D_MODEL // 2
ALPHA = (2 * DEPTH) ** 0.25
LN_EPS = 1e-5
D_IN = 2 * W_A + 4 * W_B + 3 * W_C
LANES = 128
VMEM_LIMIT = 48 * 1024 * 1024

NT_DIMS = (((1,), (1,)), ((), ()))
TN_DIMS = (((0,), (0,)), ((), ()))
INV_LN2 = 1.4426950408889634
SB_QSCALE = HEAD_DIM ** -0.5 * INV_LN2
SB_DEAD = -152.0
GMLP_SUB = 8
POST_SUB = 4
RET_BLOCK = 512


def _ln(x):
    mu = jnp.mean(x, axis=-1, keepdims=True)
    xc = x - mu
    var = jnp.mean(xc * xc, axis=-1, keepdims=True)
    return xc * lax.rsqrt(var + LN_EPS)


def _silu(x):
    return x * jax.nn.sigmoid(x)


def _params(*sem):
    return pltpu.CompilerParams(dimension_semantics=sem, vmem_limit_bytes=VMEM_LIMIT)


def _row_spec(rows, tm, width, total_rows):
    if rows == 1:
        return pl.BlockSpec((1, width), lambda i: (0, 0))
    assert rows == total_rows
    return pl.BlockSpec((tm, width), lambda i: (i, 0))


def _adaln_kernel(c_ref, w_ref, b_ref, o_ref):
    sc = _silu(c_ref[...])
    o_ref[0] = jnp.dot(sc, w_ref[0], preferred_element_type=F32, precision=HIGHEST) + b_ref[0]


def _adaln(c_all, w_ada, b_ada):
    rows = c_all.shape[0]
    tn = 1536
    return pl.pallas_call(
        _adaln_kernel,
        grid=(DEPTH, 6 * D_MODEL // tn),
        in_specs=[pl.BlockSpec((rows, D_MODEL), lambda l, j: (0, 0)),
                  pl.BlockSpec((1, D_MODEL, tn), lambda l, j: (l, 0, j)),
                  pl.BlockSpec((1, 1, tn), lambda l, j: (l, 0, j))],
        out_specs=pl.BlockSpec((1, rows, tn), lambda l, j: (l, 0, j)),
        out_shape=jax.ShapeDtypeStruct((DEPTH, rows, 6 * D_MODEL), F32),
        compiler_params=_params("parallel", "parallel"),
        name="adaln",
    )(c_all, w_ada, b_ada.reshape(DEPTH, 1, 6 * D_MODEL))


def _inproj_kernel(x_ref, shift_ref, scale_ref, w_ref, uv_ref, ret_ref, q_ref, k_ref, v_ref, kb_ref, vb_ref, wb_ref):
    @pl.when(pl.program_id(0) == 0)
    def _():
        wb_ref[...] = w_ref[0].astype(BF16)

    h = _ln(x_ref[...]) * (1.0 + scale_ref[...]) + shift_ref[...]
    r = jnp.dot(h.astype(BF16), wb_ref[...], preferred_element_type=F32)
    c0 = 2 * W_A
    c1 = c0 + 4 * W_B
    uv_ref[...] = r[:, :c0]
    ret_ref[...] = r[:, c0:c1]
    q_ref[...] = r[:, c1:c1 + W_C]
    k = r[:, c1 + W_C:c1 + 2 * W_C]
    v = r[:, c1 + 2 * W_C:]
    k_ref[...] = k
    v_ref[...] = v
    kb_ref[...] = k.astype(BF16)
    vb_ref[...] = v.astype(BF16)


def _inproj(x, shift, scale, w_in, layer, tm):
    rows = x.shape[0]
    widths = (2 * W_A, 4 * W_B, W_C, W_C, W_C, W_C, W_C)
    dtypes = (F32,) * 5 + (BF16,) * 2
    return pl.pallas_call(
        _inproj_kernel,
        grid=(rows // tm,),
        in_specs=[pl.BlockSpec((tm, D_MODEL), lambda i: (i, 0)),
                  _row_spec(shift.shape[0], tm, D_MODEL, rows),
                  _row_spec(scale.shape[0], tm, D_MODEL, rows),
                  pl.BlockSpec((1, D_MODEL, D_IN), lambda i: (layer, 0, 0), pipeline_mode=pl.Buffered(1))],
        out_specs=[pl.BlockSpec((tm, w), lambda i: (i, 0)) for w in widths],
        out_shape=[jax.ShapeDtypeStruct((rows, w), dt) for w, dt in zip(widths, dtypes)],
        scratch_shapes=[pltpu.VMEM((D_MODEL, D_IN), BF16)],
        compiler_params=_params("arbitrary"),
        name="inproj",
    )(x, shift, scale, w_in)


def _gmlp_kernel(uv_ref, wsp_ref, bias_ref, g_ref, b_ref, a_ref, vn_ref, *, chunk, n_sub):
    uv = uv_ref[...]
    u = jax.nn.gelu(uv[:, :W_A])
    v = _ln(jax.nn.gelu(uv[:, W_A:])) * g_ref[...] + b_ref[...]
    vn_ref[...] = v
    row = lax.broadcasted_iota(jnp.int32, (chunk, chunk), 0)
    col = lax.broadcasted_iota(jnp.int32, (chunk, chunk), 1)
    lane_head = lax.broadcasted_iota(jnp.int32, (chunk, W_A), 1) // HEAD_DIM
    w = [jnp.where(col <= row, wsp_ref[h], 0.0).astype(BF16) for h in range(H_A)]
    for c in range(n_sub):
        rows = slice(c * chunk, (c + 1) * chunk)
        vc = v[rows]
        mixed = bias_ref[...]
        for h in range(H_A):
            vh = jnp.where(lane_head == h, vc, 0.0).astype(BF16)
            mixed = mixed + jnp.dot(w[h], vh, preferred_element_type=F32)
        a_ref[rows, :] = u[rows] * mixed


def _gmlp(uv, w_sp, b_sp, ln_g, ln_b, chunk):
    rows = uv.shape[0]
    n_sub = min(GMLP_SUB, rows // chunk)
    tm = n_sub * chunk
    wsp = w_sp[:, :chunk, :chunk]
    bias = jnp.repeat(b_sp[:, :chunk].T, HEAD_DIM, axis=1)
    return pl.pallas_call(
        functools.partial(_gmlp_kernel, chunk=chunk, n_sub=n_sub),
        grid=(rows // tm,),
        in_specs=[pl.BlockSpec((tm, 2 * W_A), lambda i: (i, 0)),
                  pl.BlockSpec((H_A, chunk, chunk), lambda i: (0, 0, 0)),
                  pl.BlockSpec((chunk, W_A), lambda i: (0, 0)),
                  pl.BlockSpec((1, W_A), lambda i: (0, 0)),
                  pl.BlockSpec((1, W_A), lambda i: (0, 0))],
        out_specs=[pl.BlockSpec((tm, W_A), lambda i: (i, 0)),
                   pl.BlockSpec((tm, W_A), lambda i: (i, 0))],
        out_shape=[jax.ShapeDtypeStruct((rows, W_A), F32),
                   jax.ShapeDtypeStruct((rows, W_A), F32)],
        compiler_params=_params("parallel"),
        name="gmlp",
    )(uv, wsp, bias, ln_g.reshape(1, W_A), ln_b.reshape(1, W_A))


def _rope(x, cos, sin):
    lane = lax.broadcasted_iota(jnp.int32, (x.shape[0], LANES), 1)
    first_half = (lane & (HEAD_DIM // 2)) == 0
    parts = []
    for c in range(x.shape[1] // LANES):
        xc = x[:, c * LANES:(c + 1) * LANES]
        rot = jnp.where(first_half,
                        pltpu.roll(xc, LANES - HEAD_DIM // 2, 1),
                        pltpu.roll(xc, HEAD_DIM // 2, 1))
        parts.append(xc * cos + rot * sin)
    return jnp.concatenate(parts, axis=1)


def _ret_kernel(r_ref, cos_ref, sin_ref, qdec_ref, kdec_ref, dec_ref, blk_ref, s0_ref,
                gng_ref, gnb_ref, o_ref, sout_ref, s_scr, o_scr, *, n_blocks):
    n = pl.program_id(1)

    @pl.when(n == 0)
    def _():
        s_scr[...] = s0_ref[0]

    r = r_ref[...]
    cos = cos_ref[...]
    sin = sin_ref[...]
    qr = _rope(r[:, :W_B], cos, sin)
    kr = _rope(r[:, W_B:2 * W_B], cos, sin) * (HEAD_DIM ** -0.5)
    vb = r[:, 2 * W_B:3 * W_B].astype(BF16)
    gate = r[:, 3 * W_B:]
    qb = qr.astype(BF16)
    kb = kr.astype(BF16)
    qdb = (qr * qdec_ref[...]).astype(BF16)
    kdb = (kr * kdec_ref[...]).astype(BF16)
    for h in range(H_B):
        sl = slice(h * HEAD_DIM, (h + 1) * HEAD_DIM)
        scores = lax.dot_general(qb[:, sl], kb[:, sl], NT_DIMS, preferred_element_type=F32) * dec_ref[h]
        s_h = s_scr[h]
        o_h = (jnp.dot(scores.astype(BF16), vb[:, sl], preferred_element_type=F32)
               + jnp.dot(qdb[:, sl], s_h.astype(BF16), preferred_element_type=F32))
        s_scr[h] = s_h * blk_ref[h] + lax.dot_general(kdb[:, sl], vb[:, sl], TN_DIMS,
                                                      preferred_element_type=F32)
        o_scr[:, sl] = _ln(o_h)
    o_ref[...] = (o_scr[...] * gng_ref[...] + gnb_ref[...]) * _silu(gate)

    @pl.when(n == n_blocks - 1)
    def _():
        sout_ref[0] = s_scr[...]


def _retention(ret, s0, pos0, bsz, seq, gn_g, gn_b):
    L = min(seq, RET_BLOCK)
    n_blocks = seq // L
    half = HEAD_DIM // 2
    inv = ROPE_BASE ** (-jnp.arange(half, dtype=F32) / half)
    ang = (pos0 + jnp.arange(seq)).astype(F32)[:, None] * inv[None, :]
    cos, sin = jnp.cos(ang), jnp.sin(ang)
    cos_t = jnp.tile(jnp.concatenate([cos, cos], axis=1), (1, LANES // HEAD_DIM))
    sin_t = jnp.tile(jnp.concatenate([-sin, sin], axis=1), (1, LANES // HEAD_DIM))
    log_g = jnp.log1p(-jnp.exp2(-5.0 - jnp.arange(H_B, dtype=F32)))
    idx = jnp.arange(L, dtype=F32)
    diff = idx[:, None] - idx[None, :]
    decay = jnp.where(diff >= 0, jnp.exp(diff[None] * log_g[:, None, None]), 0.0)
    q_decay = jnp.exp((idx[None, :] + 1.0) * log_g[:, None])
    k_decay = jnp.exp((L - 1.0 - idx[None, :]) * log_g[:, None])
    blk_decay = jnp.exp(L * log_g)
    qdec = jnp.repeat(q_decay.T, HEAD_DIM, axis=1)
    kdec = jnp.repeat(k_decay.T, HEAD_DIM, axis=1)
    blk = jnp.broadcast_to(blk_decay[:, None, None], (H_B, HEAD_DIM, HEAD_DIM))
    const2 = lambda b, n: (0, 0)
    const3 = lambda b, n: (0, 0, 0)
    return pl.pallas_call(
        functools.partial(_ret_kernel, n_blocks=n_blocks),
        grid=(bsz, n_blocks),
        in_specs=[pl.BlockSpec((L, 4 * W_B), lambda b, n: (b * n_blocks + n, 0)),
                  pl.BlockSpec((L, LANES), lambda b, n: (n, 0)),
                  pl.BlockSpec((L, LANES), lambda b, n: (n, 0)),
                  pl.BlockSpec((L, W_B), const2),
                  pl.BlockSpec((L, W_B), const2),
                  pl.BlockSpec((H_B, L, L), const3),
                  pl.BlockSpec((H_B, HEAD_DIM, HEAD_DIM), const3),
                  pl.BlockSpec((1, H_B, HEAD_DIM, HEAD_DIM), lambda b, n: (b, 0, 0, 0)),
                  pl.BlockSpec((1, W_B), const2),
                  pl.BlockSpec((1, W_B), const2)],
        out_specs=[pl.BlockSpec((L, W_B), lambda b, n: (b * n_blocks + n, 0)),
                   pl.BlockSpec((1, H_B, HEAD_DIM, HEAD_DIM), lambda b, n: (b, 0, 0, 0))],
        out_shape=[jax.ShapeDtypeStruct((bsz * seq, W_B), F32),
                   jax.ShapeDtypeStruct((bsz, H_B, HEAD_DIM, HEAD_DIM), F32)],
        scratch_shapes=[pltpu.VMEM((H_B, HEAD_DIM, HEAD_DIM), F32),
                        pltpu.VMEM((L, W_B), F32)],
        compiler_params=_params("parallel", "arbitrary"),
        name="retention",
    )(ret, cos_t, sin_t, qdec, kdec, decay, blk, s0, gn_g.reshape(1, W_B), gn_b.reshape(1, W_B))


def _sb_block(qm_ref, k_ref, v_ref, u2_ref, acc_ref, carry_ref, causal):
    tk = k_ref.shape[0]
    half = lax.broadcasted_iota(jnp.int32, (tk, LANES), 1) // HEAD_DIM
    u2 = u2_ref[...]
    kp = [k_ref[:, p * LANES:(p + 1) * LANES] for p in range(H_C // 2)]

    def scores(h):
        z = lax.dot_general(qm_ref[h], kp[h // 2], NT_DIMS, preferred_element_type=F32)
        neg_abs = pltpu.bitcast(pltpu.bitcast(z, jnp.uint32) | jnp.uint32(0x80000000), F32)
        ls_pos = jnp.minimum(z, 0.0) - jnp.log(1.0 + jnp.exp2(neg_abs)) * INV_LN2
        log_stay = ls_pos - z
        if causal is not None:
            log_stay = jnp.where(causal, log_stay, 0.0)
        hi = log_stay.astype(BF16)
        lo = (log_stay - hi.astype(F32)).astype(BF16)
        return ls_pos, log_stay[:, :1], jnp.concatenate([hi, lo], axis=1)

    def cumsum(hi_lo):
        return jnp.dot(hi_lo, u2, preferred_element_type=F32)

    def weigh(h, ls_pos, first_col, excl):
        carry = carry_ref[h]
        att = jnp.exp2(ls_pos + excl + carry)
        if causal is not None:
            att = jnp.where(causal, att, 0.0)
        vp = v_ref[:, (h // 2) * LANES:(h // 2 + 1) * LANES]
        vp = jnp.where(half == h % 2, vp, jnp.zeros_like(vp))
        carry_ref[h] = carry + (excl[:, :1] + first_col)
        return jnp.dot(att.astype(BF16), vp, preferred_element_type=F32)

    stage_a, stage_b, outs = {}, {}, {}
    for step in range(H_C + 2):
        if step - 2 >= 0:
            h = step - 2
            outs[h] = weigh(h, stage_a[h][0], stage_a[h][1], stage_b.pop(h))
            del stage_a[h]
            if h % 2 == 1:
                p = h // 2
                acc_ref[:, p * LANES:(p + 1) * LANES] += outs.pop(h - 1) + outs.pop(h)
        if 0 <= step - 1 < H_C:
            stage_b[step - 1] = cumsum(stage_a[step - 1][2])
        if step < H_C:
            stage_a[step] = scores(step)


def _sb_kernel(q_ref, k_ref, v_ref, u2_ref, o_ref, qm_ref, acc_ref, carry_ref, *, t):
    i = pl.program_id(1)
    acc_ref[...] = jnp.zeros_like(acc_ref)
    carry_ref[...] = jnp.zeros_like(carry_ref)
    half = lax.broadcasted_iota(jnp.int32, (t, LANES), 1) // HEAD_DIM
    for h in range(H_C):
        qp = q_ref[:, (h // 2) * LANES:(h // 2 + 1) * LANES] * SB_QSCALE
        qm_ref[h] = jnp.where(half == h % 2, qp, 0.0).astype(BF16)

    def block(j, causal):
        k0 = pl.multiple_of((i - j) * t, t)
        _sb_block(qm_ref, k_ref.at[pl.ds(k0, t), :], v_ref.at[pl.ds(k0, t), :], u2_ref, acc_ref, carry_ref, causal)

    block(0, lax.broadcasted_iota(jnp.int32, (t, t), 1) < lax.broadcasted_iota(jnp.int32, (t, t), 0))

    def key_block(state):
        j, _ = state
        block(j, None)
        dead = jnp.max(carry_ref[...]) < SB_DEAD
        return j + 1, dead.astype(jnp.int32)

    lax.while_loop(lambda state: jnp.logical_and(state[0] <= i, state[1] == 0),
                   key_block, (jnp.int32(1), jnp.int32(0)))
    o_ref[...] = acc_ref[...]


def _stick_breaking(q, k, v, bsz, seq, t):
    nq = seq // t
    tri = np.tril(np.ones((t, t), np.float32), -1)
    u2 = jnp.asarray(np.concatenate([tri, tri], axis=0), dtype=BF16)
    resident = dict(pipeline_mode=pl.Buffered(1))
    return pl.pallas_call(
        functools.partial(_sb_kernel, t=t),
        grid=(bsz, nq),
        in_specs=[pl.BlockSpec((t, W_C), lambda b, i: (b * nq + i, 0)),
                  pl.BlockSpec((seq, W_C), lambda b, i: (b, 0), **resident),
                  pl.BlockSpec((seq, W_C), lambda b, i: (b, 0), **resident),
                  pl.BlockSpec((2 * t, t), lambda b, i: (0, 0), **resident)],
        out_specs=pl.BlockSpec((t, W_C), lambda b, i: (b * nq + i, 0)),
        out_shape=jax.ShapeDtypeStruct((bsz * seq, W_C), F32),
        scratch_shapes=[pltpu.VMEM((H_C, t, LANES), BF16),
                        pltpu.VMEM((t, W_C), F32),
                        pltpu.VMEM((H_C, t, 1), F32)],
        compiler_params=_params("parallel", "arbitrary"),
        name="stick_breaking",
    )(q, k, v, u2)


def _sb_scores(z, causal=None):
    neg_abs = pltpu.bitcast(pltpu.bitcast(z, jnp.uint32) | jnp.uint32(0x80000000), F32)
    ls_pos = jnp.minimum(z, 0.0) - jnp.log(1.0 + jnp.exp2(neg_abs)) * INV_LN2
    log_stay = ls_pos - z
    if causal is not None:
        log_stay = jnp.where(causal, log_stay, 0.0)
    hi = log_stay.astype(BF16)
    lo = (log_stay - hi.astype(F32)).astype(BF16)
    return ls_pos, log_stay, jnp.concatenate([hi, lo], axis=1)


def _sb_step_kernel(q_ref, kt_ref, vt_ref, kn_ref, vn_ref, u2_ref, un_ref, o_ref, acc_ref, carry_ref, *, tk):
    seq = q_ref.shape[0]
    past = kt_ref.shape[-1]
    qs = (q_ref[...] * SB_QSCALE).astype(BF16)
    row = lax.broadcasted_iota(jnp.int32, (seq, seq), 0)
    col = lax.broadcasted_iota(jnp.int32, (seq, seq), 1)
    causal = col < row
    un = un_ref[...]
    for h in range(H_C):
        sl = slice(h * HEAD_DIM, (h + 1) * HEAD_DIM)
        z = lax.dot_general(qs[:, sl], kn_ref[:, sl], NT_DIMS, preferred_element_type=F32)
        ls_pos, log_stay, hi_lo = _sb_scores(z, causal)
        excl = jnp.dot(hi_lo, un, preferred_element_type=F32)
        att = jnp.where(causal, jnp.exp2(ls_pos + excl), 0.0)
        acc_ref[h] = jnp.dot(att.astype(BF16), vn_ref[:, sl], preferred_element_type=F32)
        carry_ref[h] = excl[:, :1] + log_stay[:, :1]

    u2 = u2_ref[...]

    def key_block(state):
        j, _ = state
        k0 = pl.multiple_of(past - (j + 1) * tk, tk)
        for h in range(H_C):
            kt = kt_ref[0, 0, h, :, pl.ds(k0, tk)].astype(BF16)
            vt = vt_ref[0, 0, h, :, pl.ds(k0, tk)].astype(BF16)
            z = jnp.dot(qs[:, h * HEAD_DIM:(h + 1) * HEAD_DIM], kt, preferred_element_type=F32)
            ls_pos, log_stay, hi_lo = _sb_scores(z)
            excl = jnp.dot(hi_lo, u2, preferred_element_type=F32)
            carry = carry_ref[h]
            att = jnp.exp2(ls_pos + excl + carry)
            acc_ref[h] += lax.dot_general(att.astype(BF16), vt, NT_DIMS, preferred_element_type=F32)
            carry_ref[h] = carry + (excl[:, :1] + log_stay[:, :1])
        dead = jnp.max(carry_ref[...]) < SB_DEAD
        return j + 1, dead.astype(jnp.int32)

    lax.while_loop(lambda state: jnp.logical_and(state[0] < past // tk, state[1] == 0),
                   key_block, (jnp.int32(0), jnp.int32(0)))
    for h in range(H_C):
        o_ref[:, h * HEAD_DIM:(h + 1) * HEAD_DIM] = acc_ref[h]


def _stick_breaking_step(q, k_new, v_new, cache_kt, cache_vt, layer, bsz, seq, tk):
    past = cache_kt.shape[-1]
    assert past % tk == 0
    tri = lambda n: np.tril(np.ones((n, n), np.float32), -1)
    stacked = lambda n: jnp.asarray(np.concatenate([tri(n), tri(n)], axis=0), dtype=BF16)
    rows = pl.BlockSpec((seq, W_C), lambda b: (b, 0))
    cache = pl.BlockSpec((1, 1, H_C, HEAD_DIM, past), lambda b: (layer, b, 0, 0, 0))
    return pl.pallas_call(
        functools.partial(_sb_step_kernel, tk=tk),
        grid=(bsz,),
        in_specs=[rows, cache, cache, rows, rows,
                  pl.BlockSpec((2 * tk, tk), lambda b: (0, 0)), pl.BlockSpec((2 * seq, seq), lambda b: (0, 0))],
        out_specs=rows,
        out_shape=jax.ShapeDtypeStruct((bsz * seq, W_C), F32),
        scratch_shapes=[pltpu.VMEM((H_C, seq, HEAD_DIM), F32), pltpu.VMEM((H_C, seq, 1), F32)],
        compiler_params=_params("parallel"),
        name="stick_breaking_step",
    )(q, cache_kt, cache_vt, k_new, v_new, stacked(tk), stacked(seq))


def _route(sel, s):
    g_scores = []
    for g in range(N_GROUPS):
        a, b, c, d = sel[EXPERTS_PER_GROUP * g:EXPERTS_PER_GROUP * (g + 1)]
        ab_hi, ab_lo = jnp.maximum(a, b), jnp.minimum(a, b)
        cd_hi, cd_lo = jnp.maximum(c, d), jnp.minimum(c, d)
        top1 = jnp.maximum(ab_hi, cd_hi)
        top2 = jnp.maximum(jnp.minimum(ab_hi, cd_hi), jnp.maximum(ab_lo, cd_lo))
        g_scores.append(top1 + top2)
    best = g_scores[0]
    gi = jnp.zeros(best.shape, jnp.int32)
    for g in range(1, N_GROUPS):
        upd = g_scores[g] > best
        gi = jnp.where(upd, g, gi)
        best = jnp.where(upd, g_scores[g], best)

    def pick_group(rows, l):
        out = rows[(N_GROUPS - 1) * EXPERTS_PER_GROUP + l]
        for g in range(N_GROUPS - 2, -1, -1):
            out = jnp.where(gi == g, rows[g * EXPERTS_PER_GROUP + l], out)
        return out

    ig = [pick_group(sel, l) for l in range(EXPERTS_PER_GROUP)]
    sg = [pick_group(s, l) for l in range(EXPERTS_PER_GROUP)]
    b1 = ig[0]
    i1 = jnp.zeros(best.shape, jnp.int32)
    for l in range(1, EXPERTS_PER_GROUP):
        upd = ig[l] > b1
        i1 = jnp.where(upd, l, i1)
        b1 = jnp.where(upd, ig[l], b1)
    b2 = jnp.full(best.shape, -jnp.inf, F32)
    i2 = jnp.zeros(best.shape, jnp.int32)
    for l in range(EXPERTS_PER_GROUP):
        upd = jnp.logical_and(i1 != l, ig[l] > b2)
        i2 = jnp.where(upd, l, i2)
        b2 = jnp.where(upd, ig[l], b2)

    def pick_local(idx):
        out = sg[EXPERTS_PER_GROUP - 1]
        for l in range(EXPERTS_PER_GROUP - 2, -1, -1):
            out = jnp.where(idx == l, sg[l], out)
        return out

    w1 = pick_local(i1)
    w2 = pick_local(i2)
    tot = w1 + w2
    return gi * EXPERTS_PER_GROUP + i1, gi * EXPERTS_PER_GROUP + i2, w1 / tot, w2 / tot


def _pack_halves(h):
    half = h.shape[1] // 2
    hb = h.astype(BF16).astype(F32)
    hi = pltpu.bitcast(hb[:, :half], jnp.uint32)
    lo = pltpu.bitcast(hb[:, half:], jnp.uint32) >> 16
    return pltpu.bitcast(hi | lo, F32)


def _unpack_halves(w):
    bits = pltpu.bitcast(w, jnp.uint32)
    first = pltpu.bitcast(bits & jnp.uint32(0xFFFF0000), F32).astype(BF16)
    second = pltpu.bitcast(bits << 16, F32).astype(BF16)
    return jnp.concatenate([first, second], axis=1)


def _pair_class(e1, e2):
    lo = jnp.minimum(e1, e2)
    hi = jnp.maximum(e1, e2)
    g = lo // EXPERTS_PER_GROUP
    llo = lo - g * EXPERTS_PER_GROUP
    lhi = hi - g * EXPERTS_PER_GROUP
    return g * PAIRS_PER_GROUP + ((llo * (2 * EXPERTS_PER_GROUP - 1 - llo)) >> 1) + (lhi - llo - 1)


def _post_kernel(a_ref, b_ref, c_ref, x_ref, wo_ref, gate_ref, g1_ref, b1_ref, sh2_ref, sc2_ref,
                 wrt_ref, br_ref, x1_ref, *rest, tm, with_hx):
    if with_hx:
        h2_ref = None
        comb_ref, hx_ref, wob_ref = rest
    else:
        hx_ref = None
        h2_ref, comb_ref, wob_ref = rest
    @pl.when(pl.program_id(0) == 0)
    def _():
        wob_ref[...] = wo_ref[0].astype(BF16)

    sub = tm // POST_SUB if tm % (POST_SUB * LANES) == 0 else tm
    tiles = [slice(r0, r0 + sub) for r0 in range(0, tm, sub)]
    per_row = lambda ref, rows: ref[rows, :] if ref.shape[0] == tm else ref[...]
    proj = [jnp.dot(a_ref[rows, :].astype(BF16), wob_ref[:W_A], preferred_element_type=F32)
            + jnp.dot(b_ref[rows, :].astype(BF16), wob_ref[W_A:W_A + W_B], preferred_element_type=F32)
            + jnp.dot(c_ref[rows, :].astype(BF16), wob_ref[W_A + W_B:], preferred_element_type=F32)
            for rows in tiles]
    x1 = [_ln(ALPHA * x_ref[rows, :] + per_row(gate_ref, rows) * pr) * g1_ref[...] + b1_ref[...]
          for rows, pr in zip(tiles, proj)]
    for rows, v in zip(tiles, x1):
        x1_ref[rows, :] = v
    h2 = [_ln(v) * (1.0 + per_row(sc2_ref, rows)) + per_row(sh2_ref, rows) for rows, v in zip(tiles, x1)]
    logits = [lax.dot_general(wrt_ref[...], v, NT_DIMS, preferred_element_type=F32, precision=HIGHEST) for v in h2]
    for rows, v, logits_t in zip(tiles, h2, logits):
        if h2_ref is not None:
            h2_ref[rows, :] = v.astype(BF16)
        s_t = jax.nn.sigmoid(logits_t)
        sel_t = s_t + br_ref[...]
        s = [s_t[e:e + 1, :] for e in range(N_EXPERTS)]
        sel = [sel_t[e:e + 1, :] for e in range(N_EXPERTS)]
        e1, e2, w1, w2 = _route(sel, s)
        expert = lax.broadcasted_iota(jnp.int32, (LANES, sub), 0)
        comb_t = jnp.where(expert == e1, w1, jnp.where(expert == e2, w2, 0.0))
        comb_t = jnp.where(expert == N_EXPERTS, _pair_class(e1, e2).astype(F32), comb_t)
        comb_ref[rows, :] = comb_t.T
        if hx_ref is not None:
            hx_ref[rows, :HX_WORDS] = _pack_halves(v)
            hx_ref[rows, HX_WORDS:] = comb_t.T


def _post(a, b, c, x, w_out, layer, gate1, ln_g, ln_b, shift2, scale2, wr_t, br_col, tm, with_hx):
    rows = x.shape[0]
    if with_hx:
        tail_spec = [pl.BlockSpec((tm, LANES), lambda i: (i, 0)), pl.BlockSpec((tm, HX_WORDS + LANES), lambda i: (i, 0))]
        tail_shape = [jax.ShapeDtypeStruct((rows, LANES), F32), jax.ShapeDtypeStruct((rows, HX_WORDS + LANES), F32)]
    else:
        tail_spec = [pl.BlockSpec((tm, D_MODEL), lambda i: (i, 0)), pl.BlockSpec((tm, LANES), lambda i: (i, 0))]
        tail_shape = [jax.ShapeDtypeStruct((rows, D_MODEL), BF16), jax.ShapeDtypeStruct((rows, LANES), F32)]
    row = lambda w: pl.BlockSpec((tm, w), lambda i: (i, 0))
    const = lambda r, w: pl.BlockSpec((r, w), lambda i: (0, 0))
    return pl.pallas_call(
        functools.partial(_post_kernel, tm=tm, with_hx=with_hx),
        grid=(rows // tm,),
        in_specs=[row(W_A), row(W_B), row(W_C), row(D_MODEL),
                  pl.BlockSpec((1, D_MODEL, D_MODEL), lambda i: (layer, 0, 0), pipeline_mode=pl.Buffered(1)),
                  _row_spec(gate1.shape[0], tm, D_MODEL, rows),
                  const(1, D_MODEL), const(1, D_MODEL),
                  _row_spec(shift2.shape[0], tm, D_MODEL, rows),
                  _row_spec(scale2.shape[0], tm, D_MODEL, rows),
                  const(N_EXPERTS, D_MODEL), const(N_EXPERTS, 1)],
        out_specs=[row(D_MODEL)] + tail_spec,
        out_shape=[jax.ShapeDtypeStruct((rows, D_MODEL), F32)] + tail_shape,
        scratch_shapes=[pltpu.VMEM((D_MODEL, D_MODEL), BF16)],
        compiler_params=_params("arbitrary"),
        name="post_mix",
    )(a, b, c, x, w_out, gate1, ln_g.reshape(1, D_MODEL), ln_b.reshape(1, D_MODEL),
      shift2, scale2, wr_t, br_col)


def _moe_kernel(h_ref, comb_ref, x_ref, gate_ref, g2_ref, b2_ref, w1_ref, w3_ref, w2_ref,
                o_ref, acc_ref, *, tm):
    e = pl.program_id(1)

    @pl.when(e == 0)
    def _():
        acc_ref[...] = jnp.zeros_like(acc_ref)

    h = h_ref[...]
    a = jnp.dot(h, w1_ref[0], preferred_element_type=F32)
    g = jnp.dot(h, w3_ref[0], preferred_element_type=F32)
    lane = lax.broadcasted_iota(jnp.int32, (tm, LANES), 1)
    ce = jnp.sum(jnp.where(lane == e, comb_ref[...], 0.0), axis=-1, keepdims=True)
    act = _silu(a) * g * ce
    acc_ref[...] += jnp.dot(act.astype(BF16), w2_ref[0], preferred_element_type=F32)

    @pl.when(e == N_EXPERTS - 1)
    def _():
        y = ALPHA * x_ref[...] + gate_ref[...] * acc_ref[...]
        o_ref[...] = _ln(y) * g2_ref[...] + b2_ref[...]


def _moe(h2, comb, x1, gate2, ln_g, ln_b, w1, w3, w2, tm):
    rows = x1.shape[0]
    row = lambda w: pl.BlockSpec((tm, w), lambda i, e: (i, 0))
    const = pl.BlockSpec((1, D_MODEL), lambda i, e: (0, 0))
    gate_spec = (pl.BlockSpec((1, D_MODEL), lambda i, e: (0, 0)) if gate2.shape[0] == 1
                 else row(D_MODEL))
    return pl.pallas_call(
        functools.partial(_moe_kernel, tm=tm),
        grid=(rows // tm, N_EXPERTS),
        in_specs=[row(D_MODEL), row(LANES), row(D_MODEL), gate_spec, const, const,
                  pl.BlockSpec((1, D_MODEL, D_EXPERT), lambda i, e: (e, 0, 0)),
                  pl.BlockSpec((1, D_MODEL, D_EXPERT), lambda i, e: (e, 0, 0)),
                  pl.BlockSpec((1, D_EXPERT, D_MODEL), lambda i, e: (e, 0, 0))],
        out_specs=row(D_MODEL),
        out_shape=jax.ShapeDtypeStruct((rows, D_MODEL), F32),
        scratch_shapes=[pltpu.VMEM((tm, D_MODEL), F32)],
        compiler_params=_params("parallel", "arbitrary"),
        name="experts",
    )(h2, comb, x1, gate2, ln_g.reshape(1, D_MODEL), ln_b.reshape(1, D_MODEL), w1, w3, w2)


def _pair_experts():
    lo, hi = [], []
    for g in range(N_GROUPS):
        for a in range(EXPERTS_PER_GROUP):
            for b in range(a + 1, EXPERTS_PER_GROUP):
                lo.append(g * EXPERTS_PER_GROUP + a)
                hi.append(g * EXPERTS_PER_GROUP + b)
    return np.asarray(lo, np.int32), np.asarray(hi, np.int32)


def _invert_kernel(pos_ref, zeros_hbm, src_ref, sem):
    fill = pltpu.make_async_copy(zeros_hbm, src_ref, sem)
    fill.start()
    fill.wait()

    def put(t, carry):
        src_ref[pos_ref[t]] = t
        return carry

    lax.fori_loop(0, pos_ref.shape[0], put, 0, unroll=16)


def _invert_slots(pos, n_slots):
    return pl.pallas_call(
        _invert_kernel,
        in_specs=[pl.BlockSpec(memory_space=pltpu.SMEM), pl.BlockSpec(memory_space=pl.ANY)],
        out_specs=pl.BlockSpec(memory_space=pltpu.SMEM),
        out_shape=jax.ShapeDtypeStruct((n_slots,), jnp.int32),
        scratch_shapes=[pltpu.SemaphoreType.DMA(())],
        name="invert_slots",
    )(pos, jnp.zeros((n_slots,), jnp.int32))


def _slot_plan(cls, tm):
    rows = cls.shape[0]
    n_tiles = rows // tm + N_PAIRS
    onehot = (cls[:, None] == jnp.arange(N_PAIRS, dtype=jnp.int32)[None, :]).astype(jnp.int32)
    rank = jnp.cumsum(onehot, axis=0) - onehot
    tiles_per = (jnp.sum(onehot, axis=0) + tm - 1) // tm
    tile_start = jnp.cumsum(tiles_per) - tiles_per
    pos = jnp.sum(onehot * (tile_start[None, :] * tm + rank), axis=1).astype(jnp.int32)
    src = _invert_slots(pos, n_tiles * tm)
    tile = jnp.arange(n_tiles, dtype=jnp.int32)
    tile_class = jnp.clip(jnp.sum((tile[:, None] >= tile_start[None, :]).astype(jnp.int32), axis=1) - 1,
                          0, N_PAIRS - 1)
    pair_lo, pair_hi = _pair_experts()
    n_used = jnp.sum(tiles_per).astype(jnp.int32).reshape(1)
    return pos, src, jnp.asarray(pair_lo)[tile_class], jnp.asarray(pair_hi)[tile_class], n_used


def _start_rows(idx_ref, src_hbm, dst_ref, sem, n):
    for r in range(n):
        pltpu.make_async_copy(src_hbm.at[pl.ds(idx_ref[0, 0, r], 1)], dst_ref.at[pl.ds(r, 1)], sem).start(priority=r % 2)


def _wait_rows(src_hbm, dst_ref, sem, n):
    pltpu.make_async_copy(src_hbm.at[pl.ds(0, n)], dst_ref, sem).wait()


def _moe_pair_kernel(lo_ref, hi_ref, nused_ref, src_ref, src_next_ref, hx_hbm,
                     w1a_ref, w3a_ref, w2a_ref, w1b_ref, w3b_ref, w2b_ref, y_ref, xbuf, sems, *, tm):
    i = pl.program_id(0)
    n_used = nused_ref[0]
    slot = i % 2

    @pl.when(jnp.logical_and(i == 0, n_used > 0))
    def _():
        _start_rows(src_ref, hx_hbm, xbuf.at[0], sems.at[0], tm)

    @pl.when(i + 1 < n_used)
    def _():
        _start_rows(src_next_ref, hx_hbm, xbuf.at[1 - slot], sems.at[1 - slot], tm)

    @pl.when(i < n_used)
    def _():
        _wait_rows(hx_hbm, xbuf.at[slot], sems.at[slot], tm)
        x = xbuf[slot]
        h = _unpack_halves(x[:, :HX_WORDS])
        comb = x[:, HX_WORDS:]
        lane = lax.broadcasted_iota(jnp.int32, (tm, LANES), 1)
        acc = jnp.zeros((tm, D_MODEL), F32)
        for e, w1_ref, w3_ref, w2_ref in ((lo_ref[i], w1a_ref, w3a_ref, w2a_ref), (hi_ref[i], w1b_ref, w3b_ref, w2b_ref)):
            a = jnp.dot(h, w1_ref[0], preferred_element_type=F32)
            g = jnp.dot(h, w3_ref[0], preferred_element_type=F32)
            ce = jnp.sum(jnp.where(lane == e, comb, 0.0), axis=-1, keepdims=True)
            act = _silu(a) * g * ce
            acc = acc + jnp.dot(act.astype(BF16), w2_ref[0], preferred_element_type=F32)
        y_ref[...] = acc

    @pl.when(i >= n_used)
    def _():
        y_ref[...] = jnp.zeros_like(y_ref)


def _moe_pairs(hx, src, tile_lo, tile_hi, n_used, w1, w3, w2, tm):
    n_tiles = tile_lo.shape[0]
    lo_spec = lambda k, n: pl.BlockSpec((1, k, n), lambda i, lo, hi, nu: (lo[i], 0, 0))
    hi_spec = lambda k, n: pl.BlockSpec((1, k, n), lambda i, lo, hi, nu: (hi[i], 0, 0))
    src3 = src.reshape(n_tiles, 1, tm)
    return pl.pallas_call(
        functools.partial(_moe_pair_kernel, tm=tm),
        grid_spec=pltpu.PrefetchScalarGridSpec(
            num_scalar_prefetch=3,
            grid=(n_tiles,),
            in_specs=[pl.BlockSpec((1, 1, tm), lambda i, lo, hi, nu: (i, 0, 0), memory_space=pltpu.SMEM),
                      pl.BlockSpec((1, 1, tm), lambda i, lo, hi, nu: (jnp.minimum(i + 1, n_tiles - 1), 0, 0),
                                   memory_space=pltpu.SMEM),
                      pl.BlockSpec(memory_space=pl.ANY),
                      lo_spec(D_MODEL, D_EXPERT), lo_spec(D_MODEL, D_EXPERT), lo_spec(D_EXPERT, D_MODEL),
                      hi_spec(D_MODEL, D_EXPERT), hi_spec(D_MODEL, D_EXPERT), hi_spec(D_EXPERT, D_MODEL)],
            out_specs=pl.BlockSpec((tm, D_MODEL), lambda i, lo, hi, nu: (i, 0)),
            scratch_shapes=[pltpu.VMEM((2, tm, HX_WORDS + LANES), F32), pltpu.SemaphoreType.DMA((2,))]),
        out_shape=jax.ShapeDtypeStruct((n_tiles * tm, D_MODEL), F32),
        compiler_params=_params("arbitrary"),
        name="experts_paired",
    )(tile_lo, tile_hi, n_used, src3, src3, hx, w1, w3, w2, w1, w3, w2)


def _combine_kernel(pos_ref, pos_next_ref, y_hbm, x_ref, gate_ref, g2_ref, b2_ref, o_ref, ybuf, sems, *, tm, n_tiles):
    i = pl.program_id(0)
    slot = i % 2

    @pl.when(i == 0)
    def _():
        _start_rows(pos_ref, y_hbm, ybuf.at[0], sems.at[0], tm)

    @pl.when(i + 1 < n_tiles)
    def _():
        _start_rows(pos_next_ref, y_hbm, ybuf.at[1 - slot], sems.at[1 - slot], tm)

    _wait_rows(y_hbm, ybuf.at[slot], sems.at[slot], tm)
    y = ALPHA * x_ref[...] + gate_ref[...] * ybuf[slot]
    o_ref[...] = _ln(y) * g2_ref[...] + b2_ref[...]


def _combine(y_sorted, pos, x1, gate2, ln_g, ln_b, tm):
    rows = x1.shape[0]
    n_tiles = rows // tm
    row = pl.BlockSpec((tm, D_MODEL), lambda i: (i, 0))
    const = pl.BlockSpec((1, D_MODEL), lambda i: (0, 0))
    pos3 = pos.reshape(n_tiles, 1, tm)
    return pl.pallas_call(
        functools.partial(_combine_kernel, tm=tm, n_tiles=n_tiles),
        grid=(n_tiles,),
        in_specs=[pl.BlockSpec((1, 1, tm), lambda i: (i, 0, 0), memory_space=pltpu.SMEM),
                  pl.BlockSpec((1, 1, tm), lambda i: (jnp.minimum(i + 1, n_tiles - 1), 0, 0), memory_space=pltpu.SMEM),
                  pl.BlockSpec(memory_space=pl.ANY),
                  row, _row_spec(gate2.shape[0], tm, D_MODEL, rows), const, const],
        out_specs=row,
        out_shape=jax.ShapeDtypeStruct((rows, D_MODEL), F32),
        scratch_shapes=[pltpu.VMEM((2, tm, D_MODEL), F32), pltpu.SemaphoreType.DMA((2,))],
        compiler_params=_params("arbitrary"),
        name="combine",
    )(pos3, pos3, y_sorted, x1, gate2, ln_g.reshape(1, D_MODEL), ln_b.reshape(1, D_MODEL))


def _trunk_layer(x, mods, bsz, seq, pos0, s0, k_past, v_past, p, tiles):
    shift1, scale1, gate1, shift2, scale2, gate2 = mods
    uv, ret, q_c, k_c, v_c, k_bf, v_bf = _inproj(x, shift1, scale1, p["w_in"], p["layer"], tiles["tm_in"])
    a_out, v_rows = _gmlp(uv, p["w_sp"], p["b_sp"], p["ln_v_g"], p["ln_v_b"], min(seq, GMLP_CHUNK))
    b_out, s_new = _retention(ret, s0, pos0, bsz, seq, p["gn_g"], p["gn_b"])
    if k_past is None:
        c_out = _stick_breaking(q_c, k_bf, v_bf, bsz, seq, tiles["sb"])
    else:
        c_out = _stick_breaking_step(q_c, k_bf, v_bf, k_past, v_past, p["layer"], bsz, seq, tiles["sb"])
    grouped = "tm_group" in tiles
    post = _post(a_out, b_out, c_out, x, p["w_out"], p["layer"], gate1, p["ln1_g"], p["ln1_b"],
                 shift2, scale2, p["wr"], p["br"], tiles["tm_post"], grouped)
    if grouped:
        x1, comb, hx = post
        tm = tiles["tm_group"]
        pos, src, tile_lo, tile_hi, n_used = _slot_plan(comb[:, N_EXPERTS].astype(jnp.int32), tm)
        y_sorted = _moe_pairs(hx, src, tile_lo, tile_hi, n_used, p["w1"], p["w3"], p["w2"], tm)
        y = _combine(y_sorted, pos, x1, gate2, p["ln2_g"], p["ln2_b"], tiles["tm_combine"])
    else:
        x1, h2, comb = post
        y = _moe(h2, comb, x1, gate2, p["ln2_g"], p["ln2_b"], p["w1"], p["w3"], p["w2"], tiles["tm_moe"])
    return y, v_rows, s_new, k_c, v_c


def kernel(x_prompt, x_sample, cache_sb_k, cache_sb_v, state_ret, c_prompt, c_sample, w_ada, b_ada, w_in, w_out, ln_v_g, ln_v_b, w_spatial, b_spatial, gn_g, gn_b, ln1_g, ln1_b, ln2_g, ln2_b, w_router, b_router, w1, w3, w2):
    bp, tp, _ = x_prompt.shape
    bs, ts, _ = x_sample.shape
    past_len = cache_sb_k.shape[2]
    assert bp == 1

    n_c = bp + bs
    c_rows = -(-n_c // 8) * 8
    c_all = jnp.concatenate([c_prompt, c_sample, jnp.zeros((c_rows - n_c, D_MODEL), F32)], axis=0)
    mods = _adaln(c_all, w_ada, b_ada)

    wr_t = w_router.T
    cache_kt = jnp.transpose(cache_sb_k, (0, 1, 3, 4, 2))
    cache_vt = jnp.transpose(cache_sb_v, (0, 1, 3, 4, 2))
    br_col = b_router.reshape(N_EXPERTS, 1)

    tiles_p = dict(tm_in=256, sb=256, tm_post=512, tm_group=256, tm_combine=512)
    tiles_s = dict(tm_in=bs * ts, sb=256, tm_post=bs * ts, tm_moe=bs * ts)

    y_p = x_prompt.reshape(bp * tp, D_MODEL)
    y_s = x_sample.reshape(bs * ts, D_MODEL)
    zero_state = jnp.zeros((bp, H_B, HEAD_DIM, HEAD_DIM), F32)
    outs = [[] for _ in range(7)]
    for l in range(DEPTH):
        p = dict(w_in=w_in, w_out=w_out,
                 w_sp=w_spatial[l], b_sp=b_spatial[l], ln_v_g=ln_v_g[l], ln_v_b=ln_v_b[l],
                 gn_g=gn_g[l], gn_b=gn_b[l], ln1_g=ln1_g[l], ln1_b=ln1_b[l],
                 ln2_g=ln2_g[l], ln2_b=ln2_b[l], wr=wr_t, br=br_col,
                 w1=w1[l].astype(BF16), w3=w3[l].astype(BF16), w2=w2[l].astype(BF16), layer=l)
        m = mods[l]
        mods_p = [m[0:1, i * D_MODEL:(i + 1) * D_MODEL] for i in range(6)]
        mods_s = [jnp.repeat(m[bp:bp + bs, i * D_MODEL:(i + 1) * D_MODEL], ts, axis=0) for i in range(6)]
        y_p, _, s_p, k_p, v_p = _trunk_layer(y_p, mods_p, bp, tp, 0, zero_state, None, None, p, tiles_p)
        y_s, g_s, s_s, k_s, v_s = _trunk_layer(
            y_s, mods_s, bs, ts, past_len, state_ret[l],
            cache_kt, cache_vt, p, tiles_s)
        outs[0].append(s_p)
        outs[1].append(k_p.reshape(bp, tp, H_C, HEAD_DIM))
        outs[2].append(v_p.reshape(bp, tp, H_C, HEAD_DIM))
        outs[3].append(s_s)
        outs[4].append(k_s.reshape(bs, ts, H_C, HEAD_DIM))
        outs[5].append(v_s.reshape(bs, ts, H_C, HEAD_DIM))
        outs[6].append(g_s.reshape(bs, ts, W_A))
    return (y_p.reshape(bp, tp, D_MODEL), y_s.reshape(bs, ts, D_MODEL)) + tuple(jnp.stack(o) for o in outs)
```
